```python
import jax, jax.numpy as jnp
from jax import lax
import numpy as np

D_MODEL = 1024
BATCH = 2
SEQ = 8192
DEPTH = 1
DEC_BATCH = 16
DEC_SEQ = 16
PAST_LEN = 1024

CHUNK = 64
HEAD_DIM = 64
ATTN_WIDTH = D_MODEL // 2
N_HEADS = ATTN_WIDTH // HEAD_DIM
N_KV_HEADS = 2
GQA_GROUP = N_HEADS // N_KV_HEADS
KV_WIDTH = N_KV_HEADS * HEAD_DIM
WINDOW = 128
BACK_CHUNKS = WINDOW // CHUNK
BAND = (BACK_CHUNKS + 1) * CHUNK
SGU_CHUNK = 128
SGU_WIDTH = D_MODEL - ATTN_WIDTH
SGU_GROUPS = 4
SGU_GROUP_DIM = SGU_WIDTH // SGU_GROUPS
MIX_WIDTH = ATTN_WIDTH + SGU_WIDTH
SPLITS = [ATTN_WIDTH, ATTN_WIDTH + KV_WIDTH, ATTN_WIDTH + 2 * KV_WIDTH, ATTN_WIDTH + 2 * KV_WIDTH + SGU_WIDTH]
IN_COLS = ATTN_WIDTH + 2 * KV_WIDTH + 2 * SGU_WIDTH
N_EXPERTS = 256
TOP_K = 8
N_EXPERT_GROUPS = 8
TOPK_GROUPS = 4
EXPERT_DIM = 256
SHARED_DIM = 256
ROUTED_SCALE = 2.5
EXPERT_BLOCK = 128
EPS = 1e-6

kernel_name = 'hybrid_swa_sink_gmlp_moe_adaln_stream_step'


def rms_norm(x, g):
    xf = x.astype(jnp.float32)
    y = xf * lax.rsqrt(jnp.mean(xf * xf, axis=-1, keepdims=True) + EPS)
    return (y * g.astype(jnp.float32)).astype(x.dtype)


def layer_norm(x, g, b):
    xf = x.astype(jnp.float32)
    mu = jnp.mean(xf, axis=-1, keepdims=True)
    xc = xf - mu
    var = jnp.mean(xc * xc, axis=-1, keepdims=True)
    return (xc * lax.rsqrt(var + EPS) * g.astype(jnp.float32) + b.astype(jnp.float32)).astype(x.dtype)


def alibi_slopes():
    sl = 2.0 ** (-8.0 * (np.arange(N_HEADS, dtype=np.float32) + 1.0) / N_HEADS)
    return jnp.asarray(sl.astype(np.float32)).reshape(N_KV_HEADS, GQA_GROUP, 1, 1)


def adaln(c, w_ada, b_ada):
    mod = jnp.dot(jax.nn.silu(c), w_ada) + b_ada
    return jnp.split(mod[:, None, :], 6, axis=-1)


def modulate(x, g, shift, scale):
    return rms_norm(x, g) * (1 + scale) + shift


def project(h, w_in, ln_g, ln_b):
    proj = jnp.dot(h, w_in)
    q, k, v, u, vg = jnp.split(proj, SPLITS, axis=-1)
    lead = h.shape[:-1]
    q = q.reshape(lead + (N_KV_HEADS, GQA_GROUP, HEAD_DIM))
    k = k.reshape(lead + (N_KV_HEADS, HEAD_DIM))
    v = v.reshape(lead + (N_KV_HEADS, HEAD_DIM))
    u = jax.nn.gelu(u)
    vn = layer_norm(jax.nn.gelu(vg), ln_g, ln_b)
    return q, k, v, u, vn


def sink_attention(q, k, v, dist, valid, sinks):
    logits = jnp.einsum('...qhgd,...khd->...hgqk', q, k).astype(jnp.float32) * (HEAD_DIM ** -0.5)
    logits = logits - alibi_slopes() * dist.astype(jnp.float32)
    logits = jnp.where(valid, logits, -jnp.inf)
    sink = jnp.broadcast_to(sinks.astype(jnp.float32).reshape(N_KV_HEADS, GQA_GROUP, 1, 1), logits.shape[:-1] + (1,))
    p = jax.nn.softmax(jnp.concatenate([logits, sink], axis=-1), axis=-1)[..., :-1]
    return jnp.einsum('...hgqk,...khd->...qhgd', p.astype(v.dtype), v)


def attention_prompt(q, k, v, sinks):
    b, s = q.shape[:2]
    nc = s // CHUNK
    qb = q.reshape(b, nc, CHUNK, N_KV_HEADS, GQA_GROUP, HEAD_DIM)

    def band(t):
        tc = t.reshape(b, nc, CHUNK, N_KV_HEADS, HEAD_DIM)
        tp = jnp.pad(tc, ((0, 0), (BACK_CHUNKS, 0), (0, 0), (0, 0), (0, 0)))
        return jnp.concatenate([tp[:, i:i + nc] for i in range(BACK_CHUNKS + 1)], axis=2)

    kpos = jnp.arange(BAND) - BACK_CHUNKS * CHUNK
    qpos = jnp.arange(CHUNK)
    dist = jnp.abs(qpos[:, None] - kpos[None, :])
    valid = (jnp.arange(nc)[:, None] * CHUNK + kpos[None, :]) >= 0
    valid = valid[:, None, None, None, :]
    o = sink_attention(qb, band(k), band(v), dist, valid, sinks)
    return o.reshape(b, s, ATTN_WIDTH)


def attention_sample(q, k, v, cache_k, cache_v, sinks):
    bd, n = q.shape[:2]
    w = cache_k.shape[1]
    kk = jnp.concatenate([cache_k, k], axis=1)
    vv = jnp.concatenate([cache_v, v], axis=1)
    kpos = jnp.concatenate([jnp.arange(w) - w, jnp.arange(n)])
    qpos = jnp.arange(n)
    dist = jnp.abs(qpos[:, None] - kpos[None, :])
    qc = qpos[:, None] // CHUNK
    kc = kpos[None, :] // CHUNK
    valid = (kc >= qc - BACK_CHUNKS) & (kc <= qc)
    o = sink_attention(q, kk, vv, dist, valid, sinks)
    return o.reshape(bd, n, ATTN_WIDTH)


def sgu_mask():
    pos = jnp.arange(SGU_CHUNK) // CHUNK
    return pos[None, :] <= pos[:, None]


def sgu_prompt(u, vn, w_s, b_s):
    b, s, _ = u.shape
    nc = s // SGU_CHUNK
    wm = jnp.where(sgu_mask(), w_s, jnp.zeros_like(w_s))
    vr = vn.reshape(b, nc, SGU_CHUNK, SGU_GROUPS, SGU_GROUP_DIM)
    sv = jnp.einsum('gij,bnjgc->bnigc', wm, vr) + b_s.T[:, :, None]
    return u * sv.reshape(b, s, SGU_WIDTH)


def sgu_sample(u, vn, w_s, b_s):
    bd, n, _ = u.shape
    wm = jnp.where(sgu_mask(), w_s, jnp.zeros_like(w_s))[:, :n, :n]
    vr = vn.reshape(bd, n, SGU_GROUPS, SGU_GROUP_DIM)
    sv = jnp.einsum('gij,bjgc->bigc', wm, vr) + b_s.T[:n, :, None]
    return u * sv.reshape(bd, n, SGU_WIDTH)


def merge_groups(attn, sgu, g_attn, g_sgu, w_out):
    return jnp.dot(jnp.concatenate([rms_norm(attn, g_attn), rms_norm(sgu, g_sgu)], axis=-1), w_out)


def swiglu(h, wg, wu, wd):
    return jnp.dot(jax.nn.silu(jnp.dot(h, wg)) * jnp.dot(h, wu), wd)


def routed_experts(h, w_router, router_bias, w_gate, w_up, w_down):
    t, d = h.shape
    scores = jax.nn.sigmoid(jnp.dot(h, w_router).astype(jnp.float32))
    sel = scores + router_bias.astype(jnp.float32)
    grp_score = lax.top_k(sel.reshape(t, N_EXPERT_GROUPS, -1), 2)[0].sum(-1)
    _, gidx = lax.top_k(grp_score, TOPK_GROUPS)
    gmask = jnp.any(gidx[..., None] == jnp.arange(N_EXPERT_GROUPS), axis=-2)
    emask = jnp.repeat(gmask, N_EXPERTS // N_EXPERT_GROUPS, axis=-1)
    _, eidx = lax.top_k(jnp.where(emask, sel, -jnp.inf), TOP_K)
    wts = jnp.take_along_axis(scores, eidx, axis=-1)
    wts = wts / jnp.sum(wts, axis=-1, keepdims=True) * ROUTED_SCALE
    tk = t * TOP_K
    e_flat = eidx.reshape(-1)
    tok_flat = jnp.repeat(jnp.arange(t, dtype=jnp.int32), TOP_K)
    w_flat = wts.reshape(-1)
    order = jnp.argsort(e_flat)
    e_s = e_flat[order]
    tok_s = tok_flat[order]
    w_s = w_flat[order]
    counts = jnp.bincount(e_flat, length=N_EXPERTS)
    starts = jnp.cumsum(counts) - counts
    padded = (counts + EXPERT_BLOCK - 1) // EXPERT_BLOCK * EXPERT_BLOCK
    pends = jnp.cumsum(padded)
    pstarts = pends - padded
    dest = pstarts[e_s] + jnp.arange(tk) - starts[e_s]
    nb = -(-tk // EXPERT_BLOCK) + N_EXPERTS
    row_tok = jnp.full((nb * EXPERT_BLOCK,), t, jnp.int32).at[dest].set(tok_s)
    row_w = jnp.zeros((nb * EXPERT_BLOCK,), jnp.float32).at[dest].set(w_s)
    block_e = jnp.minimum(jnp.searchsorted(pends, jnp.arange(nb) * EXPERT_BLOCK, side='right'), N_EXPERTS - 1)
    hp = jnp.concatenate([h, jnp.zeros((1, d), h.dtype)], axis=0)

    def body(acc, xs):
        rt, rw, e = xs
        xb = hp[rt]
        a = jax.nn.silu(jnp.dot(xb, w_gate[e])) * jnp.dot(xb, w_up[e])
        y = jnp.dot(a, w_down[e]) * rw[:, None].astype(h.dtype)
        return acc.at[rt].add(y), None

    acc, _ = lax.scan(body, jnp.zeros_like(hp),
                      (row_tok.reshape(nb, EXPERT_BLOCK), row_w.reshape(nb, EXPERT_BLOCK), block_e))
    return acc[:t]


def setup_inputs(seed: int = 0) -> dict:
    key = jax.random.key(seed)
    ks = iter(jax.random.split(key, 40))

    def nrm(shape, scale):
        return jax.random.normal(next(ks), shape, jnp.float32) * scale

    win = min(WINDOW, PAST_LEN)
    d = D_MODEL
    return {
        'x_prompt': nrm((BATCH, SEQ, d), 1.0),
        'x_sample': nrm((DEC_BATCH, DEC_SEQ, d), 1.0),
        'cache_k': nrm((DEPTH, DEC_BATCH, win, N_KV_HEADS, HEAD_DIM), 1.0),
        'cache_v': nrm((DEPTH, DEC_BATCH, win, N_KV_HEADS, HEAD_DIM), 1.0),
        'c_prompt': nrm((BATCH, d), 1.0),
        'c_sample': nrm((DEC_BATCH, d), 1.0),
        'norm1_g': 1.0 + nrm((DEPTH, d), 0.05),
        'w_ada': nrm((DEPTH, d, 6 * d), 0.5 * d ** -0.5),
        'b_ada': nrm((DEPTH, 6 * d), 0.02),
        'w_in': nrm((DEPTH, d, IN_COLS), d ** -0.5),
        'sgu_ln_g': 1.0 + nrm((DEPTH, SGU_WIDTH), 0.05),
        'sgu_ln_b': nrm((DEPTH, SGU_WIDTH), 0.02),
        'sgu_w': nrm((DEPTH, SGU_GROUPS, SGU_CHUNK, SGU_CHUNK), SGU_CHUNK ** -0.5),
        'sgu_b': 1.0 + nrm((DEPTH, SGU_GROUPS, SGU_CHUNK), 0.1),
        'attn_sinks': nrm((DEPTH, N_HEADS), 1.0),
        'attn_out_g': 1.0 + nrm((DEPTH, ATTN_WIDTH), 0.05),
        'sgu_out_g': 1.0 + nrm((DEPTH, SGU_WIDTH), 0.05),
        'w_out': nrm((DEPTH, MIX_WIDTH, d), MIX_WIDTH ** -0.5),
        'norm2_g': 1.0 + nrm((DEPTH, d), 0.05),
        'w_router': nrm((DEPTH, d, N_EXPERTS), d ** -0.5),
        'router_bias': nrm((DEPTH, N_EXPERTS), 0.01),
        'w_gate': nrm((DEPTH, N_EXPERTS, d, EXPERT_DIM), d ** -0.5),
        'w_up': nrm((DEPTH, N_EXPERTS, d, EXPERT_DIM), d ** -0.5),
        'w_down': nrm((DEPTH, N_EXPERTS, EXPERT_DIM, d), EXPERT_DIM ** -0.5),
        'ws_gate': nrm((DEPTH, d, SHARED_DIM), d ** -0.5),
        'ws_up': nrm((DEPTH, d, SHARED_DIM), d ** -0.5),
        'ws_down': nrm((DEPTH, SHARED_DIM, d), SHARED_DIM ** -0.5),
        'final_g': 1.0 + nrm((d,), 0.05),
    }


def reference(x_prompt, x_sample, cache_k, cache_v, c_prompt, c_sample, norm1_g, w_ada, b_ada, w_in,
              sgu_ln_g, sgu_ln_b, sgu_w, sgu_b, attn_sinks, attn_out_g, sgu_out_g, w_out, norm2_g,
              w_router, router_bias, w_gate, w_up, w_down, ws_gate, ws_up, ws_down, final_g):
    xp, xs = x_prompt, x_sample
    b, s, d = xp.shape
    bd, n, _ = xs.shape
    kp_rows, vp_rows, ks_rows, vs_rows, sgu_rows = [], [], [], [], []
    for l in range(DEPTH):
        mp = adaln(c_prompt, w_ada[l], b_ada[l])
        ms = adaln(c_sample, w_ada[l], b_ada[l])
        qp, kp, vp, up, vnp = project(modulate(xp, norm1_g[l], mp[0], mp[1]), w_in[l], sgu_ln_g[l], sgu_ln_b[l])
        qs, ks_, vs_, us, vns = project(modulate(xs, norm1_g[l], ms[0], ms[1]), w_in[l], sgu_ln_g[l], sgu_ln_b[l])
        attn_p = attention_prompt(qp, kp, vp, attn_sinks[l])
        attn_s = attention_sample(qs, ks_, vs_, cache_k[l], cache_v[l], attn_sinks[l])
        sgu_p = sgu_prompt(up, vnp, sgu_w[l], sgu_b[l])
        sgu_s = sgu_sample(us, vns, sgu_w[l], sgu_b[l])
        xp = xp + mp[2] * merge_groups(attn_p, sgu_p, attn_out_g[l], sgu_out_g[l], w_out[l])
        xs = xs + ms[2] * merge_groups(attn_s, sgu_s, attn_out_g[l], sgu_out_g[l], w_out[l])
        keep = min(WINDOW, s)
        kp_rows.append(kp[:, s - keep:])
        vp_rows.append(vp[:, s - keep:])
        ks_rows.append(ks_)
        vs_rows.append(vs_)
        sgu_rows.append(vns)
        hp = modulate(xp, norm2_g[l], mp[3], mp[4]).reshape(b * s, d)
        hs = modulate(xs, norm2_g[l], ms[3], ms[4]).reshape(bd * n, d)
        h = jnp.concatenate([hp, hs], axis=0)
        f = (routed_experts(h, w_router[l], router_bias[l], w_gate[l], w_up[l], w_down[l])
             + swiglu(h, ws_gate[l], ws_up[l], ws_down[l]))
        xp = xp + mp[5] * f[:b * s].reshape(b, s, d)
        xs = xs + ms[5] * f[b * s:].reshape(bd, n, d)
    y_prompt = rms_norm(xp, final_g)
    y_sample = rms_norm(xs, final_g)
    return (y_prompt, y_sample, jnp.stack(kp_rows), jnp.stack(vp_rows), jnp.stack(ks_rows), jnp.stack(vs_rows), jnp.stack(sgu_rows))
```

```python
import functools

import numpy as np
import jax
import jax.numpy as jnp
from jax import lax
from jax.experimental import pallas as pl
from jax.experimental.pallas import tpu as pltpu

F32 = jnp.float32
BF16 = jnp.bfloat16
I32 = jnp.int32

D_MODEL = 1024
CHUNK = 64
HEAD_DIM = 64
ATTN_WIDTH = 512
N_HEADS = 8
N_KV_HEADS = 2
GQA_GROUP = 4
KV_WIDTH = 128
WINDOW = 128
SGU_CHUNK = 128
SGU_WIDTH = 512
SGU_GROUPS = 4
IN_COLS = ATTN_WIDTH + 2 * KV_WIDTH + 2 * SGU_WIDTH
N_EXPERTS = 256
TOP_K = 8
N_EXPERT_GROUPS = 8
GROUP_SIZE = N_EXPERTS // N_EXPERT_GROUPS
TOPK_GROUPS = 4
EXPERT_DIM = 256
ROUTED_SCALE = 2.5
EPS = 1e-6

ROW_GROUP = 16
TOKEN_TILE = 256
GROUPS_PER_TILE = TOKEN_TILE // ROW_GROUP
ATTN_TILE = 128
EXPERT_BLOCK = 128
COMBINE_TILE = 128
VMEM_LIMIT_BYTES = 56 * 1024 * 1024


def _params(n_axes=1, vmem=None):
    return pltpu.CompilerParams(dimension_semantics=("arbitrary",) * n_axes, vmem_limit_bytes=vmem)


def _rms(x):
    return x * lax.rsqrt(jnp.mean(x * x, axis=-1, keepdims=True) + EPS)


def _expand_groups(ref):
    m = ref[...]
    g, _, d = m.shape
    return jnp.broadcast_to(m, (g, ROW_GROUP, d)).reshape(g * ROW_GROUP, d)


def _pick_tile(is_sample, prompt_ref, sample_ref):
    v = jnp.where(is_sample, sample_ref[...], prompt_ref[...])
    return v.reshape(TOKEN_TILE, v.shape[-1])


def _adaln_kernel(c_ref, w_ref, b_ref, o_ref):
    c = c_ref[...]
    s = c * jax.nn.sigmoid(c)
    o_ref[...] = jnp.dot(s, w_ref[...], precision=lax.Precision.HIGHEST,
                         preferred_element_type=F32) + b_ref[...]


def _adaln(c, w_ada, b_ada):
    n, d = c.shape
    cols = w_ada.shape[1]
    tn = 1536
    return pl.pallas_call(
        _adaln_kernel,
        grid=(cols // tn,),
        in_specs=[pl.BlockSpec((n, d), lambda j: (0, 0)),
                  pl.BlockSpec((d, tn), lambda j: (0, j)),
                  pl.BlockSpec((1, tn), lambda j: (0, j))],
        out_specs=pl.BlockSpec((n, tn), lambda j: (0, j)),
        out_shape=jax.ShapeDtypeStruct((n, cols), F32),
        compiler_params=_params(1, VMEM_LIMIT_BYTES),
        name="adaln",
    )(c, w_ada, b_ada.reshape(1, cols))


def _stage1_kernel(xp_ref, xs_ref, shift_ref, scale_ref, g1_ref, win_ref, lng_ref, lnb_ref,
                   wm_ref, wms_ref, bs_ref, bss_ref, gsgu_ref,
                   q_ref, k_ref, v_ref, sgu_ref, vn_ref, *, n_prompt_tiles):
    is_sample = pl.program_id(0) >= n_prompt_tiles
    x = _pick_tile(is_sample, xp_ref, xs_ref)
    h = _rms(x) * g1_ref[...] * (1.0 + _expand_groups(scale_ref)) + _expand_groups(shift_ref)
    proj = jnp.dot(h.astype(BF16), win_ref[...], preferred_element_type=F32)
    q_ref[...] = (proj[:, :ATTN_WIDTH] * (HEAD_DIM ** -0.5)).astype(BF16)
    k_ref[...] = proj[:, ATTN_WIDTH:ATTN_WIDTH + KV_WIDTH]
    v_ref[...] = proj[:, ATTN_WIDTH + KV_WIDTH:ATTN_WIDTH + 2 * KV_WIDTH]
    c0 = ATTN_WIDTH + 2 * KV_WIDTH
    u = jax.nn.gelu(proj[:, c0:c0 + SGU_WIDTH])
    vg = jax.nn.gelu(proj[:, c0 + SGU_WIDTH:])
    mu = jnp.mean(vg, axis=-1, keepdims=True)
    xc = vg - mu
    var = jnp.mean(xc * xc, axis=-1, keepdims=True)
    vn = xc * lax.rsqrt(var + EPS) * lng_ref[...] + lnb_ref[...]

    @pl.when(is_sample)
    def _():
        vn_ref[...] = vn

    vnb = vn.astype(BF16)
    wm = jnp.where(is_sample, wms_ref[...], wm_ref[...])
    bs = jnp.where(is_sample, bss_ref[...], bs_ref[...])
    rows = []
    for c in range(TOKEN_TILE // SGU_CHUNK):
        r0 = c * SGU_CHUNK
        cols = []
        for g in range(SGU_GROUPS):
            l0 = g * SGU_CHUNK
            sv = jnp.dot(wm[g], vnb[r0:r0 + SGU_CHUNK, l0:l0 + SGU_CHUNK],
                         preferred_element_type=F32) + bs[g]
            cols.append(sv)
        rows.append(jnp.concatenate(cols, axis=1))
    sgu = u * jnp.concatenate(rows, axis=0)
    sgu_ref[...] = (_rms(sgu) * gsgu_ref[...]).astype(BF16)


def _stage1(xp3, xs3, shift_g, scale_g, g1, win_b, lng, lnb, wm, wms, bs, bss, gsgu, n_prompt_tiles):
    t = (xp3.shape[0] + xs3.shape[0]) * ROW_GROUP
    n_tiles = t // TOKEN_TILE
    last_p = n_prompt_tiles - 1
    gt = GROUPS_PER_TILE
    x_blk = (gt, ROW_GROUP, D_MODEL)
    mod_blk = (gt, 1, D_MODEL)
    const2 = lambda i: (0, 0)
    const3 = lambda i: (0, 0, 0)
    tile2 = lambda i: (i, 0)
    sgu_w_spec = pl.BlockSpec((SGU_GROUPS, SGU_CHUNK, SGU_CHUNK), const3)
    return pl.pallas_call(
        functools.partial(_stage1_kernel, n_prompt_tiles=n_prompt_tiles),
        grid=(n_tiles,),
        in_specs=[pl.BlockSpec(x_blk, lambda i: (jnp.minimum(i, last_p), 0, 0)),
                  pl.BlockSpec(x_blk, const3),
                  pl.BlockSpec(mod_blk, lambda i: (i, 0, 0)),
                  pl.BlockSpec(mod_blk, lambda i: (i, 0, 0)),
                  pl.BlockSpec((1, D_MODEL), const2),
                  pl.BlockSpec((D_MODEL, IN_COLS), const2),
                  pl.BlockSpec((1, SGU_WIDTH), const2),
                  pl.BlockSpec((1, SGU_WIDTH), const2),
                  sgu_w_spec, sgu_w_spec, sgu_w_spec, sgu_w_spec,
                  pl.BlockSpec((1, SGU_WIDTH), const2)],
        out_specs=[pl.BlockSpec((TOKEN_TILE, ATTN_WIDTH), tile2),
                   pl.BlockSpec((TOKEN_TILE, KV_WIDTH), tile2),
                   pl.BlockSpec((TOKEN_TILE, KV_WIDTH), tile2),
                   pl.BlockSpec((TOKEN_TILE, SGU_WIDTH), tile2),
                   pl.BlockSpec((TOKEN_TILE, SGU_WIDTH), const2)],
        out_shape=[jax.ShapeDtypeStruct((t, ATTN_WIDTH), BF16),
                   jax.ShapeDtypeStruct((t, KV_WIDTH), F32),
                   jax.ShapeDtypeStruct((t, KV_WIDTH), F32),
                   jax.ShapeDtypeStruct((t, SGU_WIDTH), BF16),
                   jax.ShapeDtypeStruct((TOKEN_TILE, SGU_WIDTH), F32)],
        compiler_params=_params(1, VMEM_LIMIT_BYTES),
        name="stage1",
    )(xp3, xs3, shift_g, scale_g, g1, win_b, lng, lnb, wm, wms, bs, bss, gsgu)


def _alibi_slope(h):
    return float(np.float32(2.0) ** np.float32(-8.0 * (h + 1.0) / N_HEADS))


def _sink_attention(q, segments, sinks_ref, gout):
    outs = []
    for h in range(N_HEADS):
        kv0 = (h // GQA_GROUP) * HEAD_DIM
        qh = q[:, h * HEAD_DIM:(h + 1) * HEAD_DIM]
        sink = sinks_ref[h]
        slope = _alibi_slope(h)
        logits = []
        m = jnp.full((q.shape[0], 1), sink, F32)
        for (k, _, dist, valid) in segments:
            l = lax.dot_general(qh, k[:, kv0:kv0 + HEAD_DIM], (((1,), (1,)), ((), ())),
                                preferred_element_type=F32)
            l = l - slope * dist
            if valid is not None:
                l = jnp.where(valid, l, -jnp.inf)
            logits.append(l)
            m = jnp.maximum(m, jnp.max(l, axis=-1, keepdims=True))
        denom = jnp.exp(sink - m)
        acc = None
        for l, (_, v, _, _) in zip(logits, segments):
            e = jnp.exp(l - m)
            denom = denom + jnp.sum(e, axis=-1, keepdims=True)
            pv = jnp.dot(e.astype(BF16), v[:, kv0:kv0 + HEAD_DIM], preferred_element_type=F32)
            acc = pv if acc is None else acc + pv
        outs.append(acc / denom)
    o = jnp.concatenate(outs, axis=1)
    return (_rms(o) * gout).astype(BF16)


def _attn_prompt_kernel(sinks_ref, q_ref, kp_ref, kc_ref, vp_ref, vc_ref, g_ref, o_ref):
    i = pl.program_id(1)
    nq = ATTN_TILE
    r = lax.broadcasted_iota(I32, (nq, nq), 0)
    c = lax.broadcasted_iota(I32, (nq, nq), 1)
    rc = r // CHUNK
    cc = c // CHUNK
    dist_prev = jnp.abs(r - c + nq).astype(F32)
    dist_cur = jnp.abs(r - c).astype(F32)
    valid_prev = jnp.logical_and(i > 0, cc >= rc)
    valid_cur = cc <= rc
    segs = [(kp_ref[...].astype(BF16), vp_ref[...].astype(BF16), dist_prev, valid_prev),
            (kc_ref[...].astype(BF16), vc_ref[...].astype(BF16), dist_cur, valid_cur)]
    o_ref[...] = _sink_attention(q_ref[...], segs, sinks_ref, g_ref[...])


def _attn_prompt(q, k, v, sinks, gattn, batch, seq):
    nt = seq // ATTN_TILE
    cur = lambda b, i, s: (b * nt + i, 0)
    prev = lambda b, i, s: (b * nt + jnp.maximum(i - 1, 0), 0)
    kv_blk = (ATTN_TILE, KV_WIDTH)
    return pl.pallas_call(
        _attn_prompt_kernel,
        grid_spec=pltpu.PrefetchScalarGridSpec(
            num_scalar_prefetch=1,
            grid=(batch, nt),
            in_specs=[pl.BlockSpec((ATTN_TILE, ATTN_WIDTH), cur),
                      pl.BlockSpec(kv_blk, prev), pl.BlockSpec(kv_blk, cur),
                      pl.BlockSpec(kv_blk, prev), pl.BlockSpec(kv_blk, cur),
                      pl.BlockSpec((1, ATTN_WIDTH), lambda b, i, s: (0, 0))],
            out_specs=pl.BlockSpec((ATTN_TILE, ATTN_WIDTH), cur)),
        out_shape=jax.ShapeDtypeStruct((batch * seq, ATTN_WIDTH), BF16),
        compiler_params=_params(2, VMEM_LIMIT_BYTES),
        name="attn_prompt",
    )(sinks, q, k, k, v, v, gattn)


def _attn_sample_kernel(sinks_ref, q_ref, ck_ref, cv_ref, k_ref, v_ref, g_ref, o_ref):
    nq = q_ref.shape[0]
    w = ck_ref.shape[1]
    r = lax.broadcasted_iota(I32, (nq, w), 0)
    c = lax.broadcasted_iota(I32, (nq, w), 1)
    dist_cache = jnp.abs(r - c + w).astype(F32)
    r2 = lax.broadcasted_iota(I32, (nq, nq), 0)
    c2 = lax.broadcasted_iota(I32, (nq, nq), 1)
    dist_new = jnp.abs(r2 - c2).astype(F32)
    segs = [(ck_ref[0].astype(BF16), cv_ref[0].astype(BF16), dist_cache, None),
            (k_ref[...].astype(BF16), v_ref[...].astype(BF16), dist_new, None)]
    o_ref[...] = _sink_attention(q_ref[...], segs, sinks_ref, g_ref[...])


def _attn_sample(q, k, v, cache_k, cache_v, sinks, gattn, dec_batch, n, row0):
    w = cache_k.shape[1]
    blk0 = row0 // n
    new = lambda b, s: (blk0 + b, 0)
    return pl.pallas_call(
        _attn_sample_kernel,
        grid_spec=pltpu.PrefetchScalarGridSpec(
            num_scalar_prefetch=1,
            grid=(dec_batch,),
            in_specs=[pl.BlockSpec((n, ATTN_WIDTH), new),
                      pl.BlockSpec((1, w, KV_WIDTH), lambda b, s: (b, 0, 0)),
                      pl.BlockSpec((1, w, KV_WIDTH), lambda b, s: (b, 0, 0)),
                      pl.BlockSpec((n, KV_WIDTH), new),
                      pl.BlockSpec((n, KV_WIDTH), new),
                      pl.BlockSpec((1, ATTN_WIDTH), lambda b, s: (0, 0))],
            out_specs=pl.BlockSpec((n, ATTN_WIDTH), lambda b, s: (b, 0))),
        out_shape=jax.ShapeDtypeStruct((dec_batch * n, ATTN_WIDTH), BF16),
        compiler_params=_params(1, VMEM_LIMIT_BYTES),
        name="attn_sample",
    )(sinks, q, cache_k, cache_v, k, v, gattn)


def _merge_kernel(ap_ref, as_ref, sgu_ref, xp_ref, xs_ref, gate1_ref, shift2_ref, scale2_ref, gate2_ref,
                  woa_ref, wos_ref, g2_ref, wr_ref, wsgu_ref, wsd_ref,
                  xb_ref, h2_ref, sc_ref, *, n_prompt_tiles):
    is_sample = pl.program_id(0) >= n_prompt_tiles
    x = _pick_tile(is_sample, xp_ref, xs_ref)
    a = jnp.where(is_sample, as_ref[...], ap_ref[...])
    mix = (jnp.dot(a, woa_ref[...], preferred_element_type=F32)
           + jnp.dot(sgu_ref[...], wos_ref[...], preferred_element_type=F32))
    x1 = x + _expand_groups(gate1_ref) * mix
    h2 = _rms(x1) * g2_ref[...] * (1.0 + _expand_groups(scale2_ref)) + _expand_groups(shift2_ref)
    h2_ref[...] = h2
    logits = jnp.dot(h2, wr_ref[...], precision=lax.Precision.HIGHEST, preferred_element_type=F32)
    sc_ref[...] = jax.nn.sigmoid(logits)
    gu = jnp.dot(h2.astype(BF16), wsgu_ref[...], preferred_element_type=F32)
    g = gu[:, :EXPERT_DIM]
    act = (g * jax.nn.sigmoid(g)) * gu[:, EXPERT_DIM:]
    shared = jnp.dot(act.astype(BF16), wsd_ref[...], preferred_element_type=F32)
    xb_ref[...] = x1 + _expand_groups(gate2_ref) * shared


def _merge(attn_p, attn_s, sgu_n, xp3, xs3, gate1_g, shift2_g, scale2_g, gate2_g,
           woa, wos, g2, wr, wsgu, wsd, n_prompt_tiles):
    t = sgu_n.shape[0]
    n_tiles = t // TOKEN_TILE
    last_p = n_prompt_tiles - 1
    gt = GROUPS_PER_TILE
    x_blk = (gt, ROW_GROUP, D_MODEL)
    mod_blk = (gt, 1, D_MODEL)
    const2 = lambda i: (0, 0)
    const3 = lambda i: (0, 0, 0)
    tile2 = lambda i: (i, 0)
    mod_spec = pl.BlockSpec(mod_blk, lambda i: (i, 0, 0))
    return pl.pallas_call(
        functools.partial(_merge_kernel, n_prompt_tiles=n_prompt_tiles),
        grid=(n_tiles,),
        in_specs=[pl.BlockSpec((TOKEN_TILE, ATTN_WIDTH), lambda i: (jnp.minimum(i, last_p), 0)),
                  pl.BlockSpec((TOKEN_TILE, ATTN_WIDTH), const2),
                  pl.BlockSpec((TOKEN_TILE, SGU_WIDTH), tile2),
                  pl.BlockSpec(x_blk, lambda i: (jnp.minimum(i, last_p), 0, 0)),
                  pl.BlockSpec(x_blk, const3),
                  mod_spec, mod_spec, mod_spec, mod_spec,
                  pl.BlockSpec((ATTN_WIDTH, D_MODEL), const2),
                  pl.BlockSpec((SGU_WIDTH, D_MODEL), const2),
                  pl.BlockSpec((1, D_MODEL), const2),
                  pl.BlockSpec((D_MODEL, N_EXPERTS), const2),
                  pl.BlockSpec((D_MODEL, 2 * EXPERT_DIM), const2),
                  pl.BlockSpec((EXPERT_DIM, D_MODEL), const2)],
        out_specs=[pl.BlockSpec((TOKEN_TILE, D_MODEL), tile2),
                   pl.BlockSpec((TOKEN_TILE, D_MODEL), tile2),
                   pl.BlockSpec((TOKEN_TILE, N_EXPERTS), tile2)],
        out_shape=[jax.ShapeDtypeStruct((t, D_MODEL), F32),
                   jax.ShapeDtypeStruct((t, D_MODEL), F32),
                   jax.ShapeDtypeStruct((t, N_EXPERTS), F32)],
        compiler_params=_params(1, VMEM_LIMIT_BYTES),
        name="merge",
    )(attn_p, attn_s, sgu_n, xp3, xs3, gate1_g, shift2_g, scale2_g, gate2_g, woa, wos, g2, wr, wsgu, wsd)


def _first_argmax(x, iota, size, axis):
    m = jnp.max(x, axis=axis, keepdims=True)
    idx = jnp.min(jnp.where(x == m, iota, size), axis=axis, keepdims=True)
    return m, idx


def _route_kernel(sc_ref, bias_ref, tri_ref, eidx_ref, wts_ref, rank_ref, cnt_ref, base_ref):
    tt = sc_ref.shape[0]

    @pl.when(pl.program_id(0) == 0)
    def _():
        base_ref[...] = jnp.zeros_like(base_ref)

    s_t = sc_ref[...].T
    sel = s_t + bias_ref[...]
    sel3 = sel.reshape(N_EXPERT_GROUPS, GROUP_SIZE, tt)
    io3 = lax.broadcasted_iota(I32, sel3.shape, 1)
    m1, i1 = _first_argmax(sel3, io3, GROUP_SIZE, 1)
    m2 = jnp.max(jnp.where(io3 == i1, -jnp.inf, sel3), axis=1, keepdims=True)
    gs = (m1 + m2).reshape(N_EXPERT_GROUPS, tt)
    io8 = lax.broadcasted_iota(I32, gs.shape, 0)
    gmask = jnp.zeros(gs.shape, jnp.bool_)
    for _ in range(TOPK_GROUPS):
        _, gi = _first_argmax(gs, io8, N_EXPERT_GROUPS, 0)
        hit = io8 == gi
        gmask = jnp.logical_or(gmask, hit)
        gs = jnp.where(hit, -jnp.inf, gs)
    emask = jnp.broadcast_to(gmask.reshape(N_EXPERT_GROUPS, 1, tt), sel3.shape).reshape(N_EXPERTS, tt)
    cand = jnp.where(emask, sel, -jnp.inf)
    io = lax.broadcasted_iota(I32, cand.shape, 0)
    chosen = jnp.zeros(cand.shape, jnp.bool_)
    eidx, wts = [], []
    for _ in range(TOP_K):
        _, ei = _first_argmax(cand, io, N_EXPERTS, 0)
        hit = io == ei
        eidx.append(ei)
        wts.append(jnp.sum(jnp.where(hit, s_t, 0.0), axis=0, keepdims=True))
        chosen = jnp.logical_or(chosen, hit)
        cand = jnp.where(hit, -jnp.inf, cand)
    w = jnp.concatenate(wts, axis=0)
    wts_ref[...] = w / jnp.sum(w, axis=0, keepdims=True) * ROUTED_SCALE
    eidx_ref[...] = jnp.concatenate(eidx, axis=0)
    cf = jnp.where(chosen, 1.0, 0.0)
    ahead = jnp.dot(cf.astype(BF16), tri_ref[...], preferred_element_type=F32) + base_ref[...]
    ranks = [jnp.sum(jnp.where(io == ei, ahead, 0.0), axis=0, keepdims=True) for ei in eidx]
    rank_ref[...] = jnp.concatenate(ranks, axis=0).astype(I32)
    total = base_ref[...] + jnp.sum(cf, axis=1, keepdims=True)
    base_ref[...] = total
    cnt_ref[...] = total.astype(I32)


def _route(scores, bias_col, tri):
    t = scores.shape[0]
    n_tiles = t // TOKEN_TILE
    col = lambda i: (0, i)
    return pl.pallas_call(
        _route_kernel,
        grid=(n_tiles,),
        in_specs=[pl.BlockSpec((TOKEN_TILE, N_EXPERTS), lambda i: (i, 0)),
                  pl.BlockSpec((N_EXPERTS, 1), lambda i: (0, 0)),
                  pl.BlockSpec((TOKEN_TILE, TOKEN_TILE), lambda i: (0, 0))],
        out_specs=[pl.BlockSpec((TOP_K, TOKEN_TILE), col),
                   pl.BlockSpec((TOP_K, TOKEN_TILE), col),
                   pl.BlockSpec((TOP_K, TOKEN_TILE), col),
                   pl.BlockSpec((N_EXPERTS, 1), lambda i: (0, 0))],
        out_shape=[jax.ShapeDtypeStruct((TOP_K, t), I32),
                   jax.ShapeDtypeStruct((TOP_K, t), F32),
                   jax.ShapeDtypeStruct((TOP_K, t), I32),
                   jax.ShapeDtypeStruct((N_EXPERTS, 1), I32)],
        scratch_shapes=[pltpu.VMEM((N_EXPERTS, 1), F32)],
        compiler_params=_params(1, VMEM_LIMIT_BYTES),
        name="route",
    )(scores, bias_col, tri)


def _pos_kernel(eidx_ref, rank_ref, pstart_ref, pos_ref):
    e = eidx_ref[...]
    tt = e.shape[1]
    io = lax.broadcasted_iota(I32, (N_EXPERTS, tt), 0)
    ps = pstart_ref[...]
    rows = [jnp.sum(jnp.where(io == e[k:k + 1, :], ps, 0), axis=0, keepdims=True) for k in range(TOP_K)]
    pos_ref[...] = jnp.concatenate(rows, axis=0) + rank_ref[...]


def _positions(eidx, rank, pstart_col):
    t = eidx.shape[1]
    col = lambda i: (0, i)
    blk = pl.BlockSpec((TOP_K, TOKEN_TILE), col)
    return pl.pallas_call(
        _pos_kernel,
        grid=(t // TOKEN_TILE,),
        in_specs=[blk, blk, pl.BlockSpec((N_EXPERTS, 1), lambda i: (0, 0))],
        out_specs=blk,
        out_shape=jax.ShapeDtypeStruct((TOP_K, t), I32),
        compiler_params=_params(1, VMEM_LIMIT_BYTES),
        name="positions",
    )(eidx, rank, pstart_col)


def _row_copy(src_ref, src_row, dst_ref, dst_row, sem):
    return pltpu.make_async_copy(src_ref.at[pl.ds(src_row, 1)], dst_ref.at[pl.ds(dst_row, 1)], sem)


def _dispatch_kernel(pos_ref, h_ref, xs_ref, sem):
    tt = h_ref.shape[0]

    def body(t, carry):
        for k in range(TOP_K):
            _row_copy(h_ref, t, xs_ref, pos_ref[k, t], sem).start()
        return carry

    lax.fori_loop(0, tt, body, 0)
    for k in range(TOP_K):
        pltpu.make_async_copy(h_ref, xs_ref.at[pl.ds(0, tt)], sem).wait()


def _dispatch(pos, h2, n_rows):
    t = h2.shape[0]
    return pl.pallas_call(
        _dispatch_kernel,
        grid=(t // TOKEN_TILE,),
        in_specs=[pl.BlockSpec((TOP_K, TOKEN_TILE), lambda i: (0, i), memory_space=pltpu.SMEM),
                  pl.BlockSpec((TOKEN_TILE, D_MODEL), lambda i: (i, 0))],
        out_specs=pl.BlockSpec(memory_space=pl.ANY),
        out_shape=jax.ShapeDtypeStruct((n_rows, D_MODEL), F32),
        scratch_shapes=[pltpu.SemaphoreType.DMA],
        compiler_params=_params(1, VMEM_LIMIT_BYTES),
        name="dispatch",
    )(pos, h2)


def _expert_kernel(e_ref, valid_ref, src_ref, xs_ref, wg_ref, wu_ref, wd_ref, ys_ref, wgu_s, wd_s):
    b = pl.program_id(0)
    nvalid = valid_ref[b]

    @pl.when(nvalid > 0)
    def _():
        first = jnp.logical_or(b == 0, e_ref[b] != e_ref[jnp.maximum(b - 1, 0)])

        @pl.when(first)
        def _():
            wgu_s[:, :EXPERT_DIM] = wg_ref[0].astype(BF16)
            wgu_s[:, EXPERT_DIM:] = wu_ref[0].astype(BF16)
            wd_s[...] = wd_ref[0].astype(BF16)

        rows = lax.broadcasted_iota(I32, (EXPERT_BLOCK, 1), 0)
        x = jnp.where(rows < nvalid, xs_ref[...], 0.0).astype(BF16)
        gu = jnp.dot(x, wgu_s[...], preferred_element_type=F32)
        g = gu[:, :EXPERT_DIM]
        act = (g * jax.nn.sigmoid(g)) * gu[:, EXPERT_DIM:]
        ys_ref[...] = jnp.dot(act.astype(BF16), wd_s[...], preferred_element_type=F32)


def _experts(block_e, block_valid, block_src, xs, w_gate, w_up, w_down):
    n_rows = xs.shape[0]
    nb = n_rows // EXPERT_BLOCK
    rows_map = lambda b, e, v, s: (s[b], 0)
    w_map = lambda b, e, v, s: (e[b], 0, 0)
    return pl.pallas_call(
        _expert_kernel,
        grid_spec=pltpu.PrefetchScalarGridSpec(
            num_scalar_prefetch=3,
            grid=(nb,),
            in_specs=[pl.BlockSpec((EXPERT_BLOCK, D_MODEL), rows_map),
                      pl.BlockSpec((1, D_MODEL, EXPERT_DIM), w_map),
                      pl.BlockSpec((1, D_MODEL, EXPERT_DIM), w_map),
                      pl.BlockSpec((1, EXPERT_DIM, D_MODEL), w_map)],
            out_specs=pl.BlockSpec((EXPERT_BLOCK, D_MODEL), rows_map),
            scratch_shapes=[pltpu.VMEM((D_MODEL, 2 * EXPERT_DIM), BF16),
                            pltpu.VMEM((EXPERT_DIM, D_MODEL), BF16)]),
        out_shape=jax.ShapeDtypeStruct((n_rows, D_MODEL), F32),
        compiler_params=_params(1, VMEM_LIMIT_BYTES),
        name="experts",
    )(block_e, block_valid, block_src, xs, w_gate, w_up, w_down)


def _combine_kernel(pos_ref, ys_ref, xb_ref, wts_ref, gate2_ref, gf_ref, yp_ref, ysm_ref, buf, sem,
                    *, n_prompt_tiles):
    i = pl.program_id(0)
    tt = xb_ref.shape[0]

    def body(t, carry):
        for k in range(TOP_K):
            _row_copy(ys_ref, pos_ref[k, t], buf.at[k], t, sem).start()
        return carry

    lax.fori_loop(0, tt, body, 0)
    for k in range(TOP_K):
        pltpu.make_async_copy(ys_ref.at[pl.ds(0, tt)], buf.at[k], sem).wait()
    w = wts_ref[...]
    routed = w[:, 0:1] * buf[0]
    for k in range(1, TOP_K):
        routed = routed + w[:, k:k + 1] * buf[k]
    x = xb_ref[...] + _expand_groups(gate2_ref) * routed
    y = _rms(x) * gf_ref[...]

    @pl.when(i < n_prompt_tiles)
    def _():
        yp_ref[...] = y

    @pl.when(i >= n_prompt_tiles)
    def _():
        ysm_ref[...] = y


def _combine(pos, ys, xb, wts_t, gate2_g, gf, n_prompt_rows):
    t = xb.shape[0]
    tt = COMBINE_TILE
    n_tiles = t // tt
    n_p = n_prompt_rows // tt
    gt = tt // ROW_GROUP
    return pl.pallas_call(
        functools.partial(_combine_kernel, n_prompt_tiles=n_p),
        grid=(n_tiles,),
        in_specs=[pl.BlockSpec((TOP_K, tt), lambda i: (0, i), memory_space=pltpu.SMEM),
                  pl.BlockSpec(memory_space=pl.ANY),
                  pl.BlockSpec((tt, D_MODEL), lambda i: (i, 0)),
                  pl.BlockSpec((tt, TOP_K), lambda i: (i, 0)),
                  pl.BlockSpec((gt, 1, D_MODEL), lambda i: (i, 0, 0)),
                  pl.BlockSpec((1, D_MODEL), lambda i: (0, 0))],
        out_specs=[pl.BlockSpec((tt, D_MODEL), lambda i: (jnp.minimum(i, n_p - 1), 0)),
                   pl.BlockSpec((tt, D_MODEL), lambda i: (jnp.maximum(i - n_p, 0), 0))],
        out_shape=[jax.ShapeDtypeStruct((n_prompt_rows, D_MODEL), F32),
                   jax.ShapeDtypeStruct((t - n_prompt_rows, D_MODEL), F32)],
        scratch_shapes=[pltpu.VMEM((TOP_K, tt, D_MODEL), F32), pltpu.SemaphoreType.DMA],
        compiler_params=_params(1, VMEM_LIMIT_BYTES),
        name="combine",
    )(pos, ys, xb, wts_t, gate2_g, gf)


def _sgu_mask():
    pos = np.arange(SGU_CHUNK) // CHUNK
    return jnp.asarray(pos[None, :] <= pos[:, None])


def _layer(xp3, xs3, cache_k, cache_v, mod, batch, seq, dec_batch, n, p):
    tp = batch * seq
    ts = dec_batch * n
    t = tp + ts
    n_prompt_tiles = tp // TOKEN_TILE

    mod_g = jnp.concatenate([jnp.repeat(mod[:batch], seq // ROW_GROUP, axis=0),
                             jnp.repeat(mod[batch:], n // ROW_GROUP, axis=0)], axis=0)
    shift1, scale1, gate1, shift2, scale2, gate2 = [
        mod_g[:, j * D_MODEL:(j + 1) * D_MODEL].reshape(t // ROW_GROUP, 1, D_MODEL) for j in range(6)]

    wm = jnp.where(_sgu_mask(), p["sgu_w"], 0.0)
    reps = SGU_CHUNK // n
    eye = jnp.eye(reps, dtype=F32)
    wms = jnp.stack([jnp.kron(eye, wm[g, :n, :n]) for g in range(SGU_GROUPS)])
    bs = jnp.broadcast_to(p["sgu_b"][:, :, None], (SGU_GROUPS, SGU_CHUNK, SGU_CHUNK))
    bss = jnp.broadcast_to(jnp.tile(p["sgu_b"][:, :n], (1, reps))[:, :, None], (SGU_GROUPS, SGU_CHUNK, SGU_CHUNK))

    q, k, v, sgu_n, vn_s = _stage1(
        xp3, xs3, shift1, scale1, p["norm1_g"].reshape(1, -1), p["w_in"].astype(BF16),
        p["sgu_ln_g"].reshape(1, -1), p["sgu_ln_b"].reshape(1, -1),
        wm.astype(BF16), wms.astype(BF16), bs, bss, p["sgu_out_g"].reshape(1, -1), n_prompt_tiles)

    gattn = p["attn_out_g"].reshape(1, -1)
    attn_p = _attn_prompt(q, k, v, p["attn_sinks"], gattn, batch, seq)
    w = cache_k.shape[1]
    attn_s = _attn_sample(q, k, v, cache_k.reshape(dec_batch, w, KV_WIDTH), cache_v.reshape(dec_batch, w, KV_WIDTH),
                          p["attn_sinks"], gattn, dec_batch, n, tp)

    wo = p["w_out"].astype(BF16)
    wsgu = jnp.concatenate([p["ws_gate"], p["ws_up"]], axis=1).astype(BF16)
    xb, h2, scores = _merge(attn_p, attn_s, sgu_n, xp3, xs3, gate1, shift2, scale2, gate2,
                            wo[:ATTN_WIDTH], wo[ATTN_WIDTH:], p["norm2_g"].reshape(1, -1), p["w_router"],
                            wsgu, p["ws_down"].astype(BF16), n_prompt_tiles)

    ti = np.arange(TOKEN_TILE)
    tri = jnp.asarray(ti[:, None] < ti[None, :], dtype=BF16)
    eidx, wts, rank, counts = _route(scores, p["router_bias"].reshape(N_EXPERTS, 1), tri)

    counts = counts[:, 0]
    nblk = (counts + EXPERT_BLOCK - 1) // EXPERT_BLOCK
    blk_end = jnp.cumsum(nblk)
    blk_start = blk_end - nblk
    n_blocks = -(-t * TOP_K // EXPERT_BLOCK) + N_EXPERTS
    n_used = blk_end[-1]
    b = jnp.arange(n_blocks, dtype=I32)
    src = jnp.minimum(b, n_used - 1)
    block_e = jnp.minimum(jnp.sum(src[:, None] >= blk_end[None, :], axis=1), N_EXPERTS - 1).astype(I32)
    block_valid = jnp.where(b < n_used,
                            jnp.clip(counts[block_e] - (b - blk_start[block_e]) * EXPERT_BLOCK, 0, EXPERT_BLOCK),
                            0).astype(I32)
    pos = _positions(eidx, rank, (blk_start * EXPERT_BLOCK).astype(I32).reshape(N_EXPERTS, 1))

    xs_sorted = _dispatch(pos, h2, n_blocks * EXPERT_BLOCK)
    ys_sorted = _experts(block_e, block_valid, src.astype(I32), xs_sorted, p["w_gate"], p["w_up"], p["w_down"])
    return pos, ys_sorted, xb, wts.T, gate2, k, v, vn_s


def kernel(x_prompt, x_sample, cache_k, cache_v, c_prompt, c_sample, norm1_g, w_ada, b_ada, w_in, sgu_ln_g, sgu_ln_b, sgu_w, sgu_b, attn_sinks, attn_out_g, sgu_out_g, w_out, norm2_g, w_router, router_bias, w_gate, w_up, w_down, ws_gate, ws_up, ws_down, final_g):
    batch, seq, d = x_prompt.shape
    dec_batch, n, _ = x_sample.shape
    depth = norm1_g.shape[0]
    assert depth == 1 and d == D_MODEL
    assert (batch * seq) % TOKEN_TILE == 0 and dec_batch * n == TOKEN_TILE and n == ROW_GROUP
    assert seq % ATTN_TILE == 0 and seq >= WINDOW and cache_k.shape[2] == WINDOW
    tp = batch * seq

    xp3 = x_prompt.reshape(tp // ROW_GROUP, ROW_GROUP, d)
    xs3 = x_sample.reshape(dec_batch * n // ROW_GROUP, ROW_GROUP, d)
    c_all = jnp.concatenate([c_prompt, c_sample], axis=0)
    l = 0
    mod = _adaln(c_all, w_ada[l], b_ada[l])
    p = dict(norm1_g=norm1_g[l], w_in=w_in[l], sgu_ln_g=sgu_ln_g[l], sgu_ln_b=sgu_ln_b[l], sgu_w=sgu_w[l],
             sgu_b=sgu_b[l], attn_sinks=attn_sinks[l], attn_out_g=attn_out_g[l], sgu_out_g=sgu_out_g[l],
             w_out=w_out[l], norm2_g=norm2_g[l], w_router=w_router[l], router_bias=router_bias[l],
             w_gate=w_gate[l], w_up=w_up[l], w_down=w_down[l], ws_gate=ws_gate[l], ws_up=ws_up[l],
             ws_down=ws_down[l])
    pos, ys_sorted, xb, wts_t, gate2, k, v, vn_s = _layer(
        xp3, xs3, cache_k[l], cache_v[l], mod, batch, seq, dec_batch, n, p)
    y_p, y_s = _combine(pos, ys_sorted, xb, wts_t, gate2, final_g.reshape(1, -1), tp)

    keep = min(WINDOW, seq)
    k_p = k[:tp].reshape(batch, seq, N_KV_HEADS, HEAD_DIM)[:, seq - keep:]
    v_p = v[:tp].reshape(batch, seq, N_KV_HEADS, HEAD_DIM)[:, seq - keep:]
    k_s = k[tp:].reshape(dec_batch, n, N_KV_HEADS, HEAD_DIM)
    v_s = v[tp:].reshape(dec_batch, n, N_KV_HEADS, HEAD_DIM)
    return (y_p.reshape(batch, seq, d), y_s.reshape(dec_batch, n, d),
            k_p[None], v_p[None], k_s[None], v_s[None], vn_s.reshape(dec_batch, n, SGU_WIDTH)[None])
```

```python
import functools

import numpy as np
import jax
import jax.numpy as jnp
from jax import lax
from jax.experimental import pallas as pl
from jax.experimental.pallas import tpu as pltpu

F32 = jnp.float32
BF16 = jnp.bfloat16
I32 = jnp.int32

D_MODEL = 1024
CHUNK = 64
HEAD_DIM = 64
ATTN_WIDTH = 512
N_HEADS = 8
N_KV_HEADS = 2
GQA_GROUP = 4
KV_WIDTH = 128
WINDOW = 128
SGU_CHUNK = 128
SGU_WIDTH = 512
SGU_GROUPS = 4
IN_COLS = ATTN_WIDTH + 2 * KV_WIDTH + 2 * SGU_WIDTH
N_MOD = 6
N_EXPERTS = 256
TOP_K = 8
N_EXPERT_GROUPS = 8
GROUP_SIZE = N_EXPERTS // N_EXPERT_GROUPS
TOPK_GROUPS = 4
EXPERT_DIM = 256
ROUTED_SCALE = 2.5
EPS = 1e-6

LANES = 128
SUBLANES = 8
ROW_TILES = D_MODEL // LANES
ROW_GROUP = 16
TOKEN_TILE = 256
ATTN_TILE = 128
EXPERT_BLOCK = 256
COMBINE_TILE = 128
VMEM_LIMIT_BYTES = 56 * 1024 * 1024

assert ROW_TILES == SUBLANES


def _params(n_axes=1):
    return pltpu.CompilerParams(dimension_semantics=("arbitrary",) * n_axes,
                                vmem_limit_bytes=VMEM_LIMIT_BYTES)


def _rms(x):
    return x * lax.rsqrt(jnp.mean(x * x, axis=-1, keepdims=True) + EPS)


def _cond_rows(is_sample, mp_ref, ms_ref):
    ms = ms_ref[...]
    g, _, d = ms.shape
    s = jnp.broadcast_to(ms, (g, ROW_GROUP, d)).reshape(g * ROW_GROUP, d)
    return jnp.where(is_sample, s, mp_ref[0])


def _cond_specs(piece, tile_rows, tiles_per_batch, batch, n_prompt_tiles, n_sample_tiles):
    g = tile_rows // ROW_GROUP
    mp = pl.BlockSpec((1, 1, D_MODEL),
                      lambda i: (jnp.minimum(i // tiles_per_batch, batch - 1), 0, piece))
    ms = pl.BlockSpec((g, 1, D_MODEL),
                      lambda i: (jnp.clip(i - n_prompt_tiles, 0, n_sample_tiles - 1), 0, piece))
    return [mp, ms]


def _pick_tile(is_sample, prompt_ref, sample_ref):
    v = jnp.where(is_sample, sample_ref[...], prompt_ref[...])
    return v.reshape(TOKEN_TILE, v.shape[-1])


def _load_rows(ref, n_rows):
    return jnp.concatenate([ref[pl.ds(j, n_rows, stride=ROW_TILES), :] for j in range(ROW_TILES)], axis=1)


def _store_rows(ref, x):
    n_rows = x.shape[0]
    for j in range(ROW_TILES):
        ref[pl.ds(j, n_rows, stride=ROW_TILES), :] = x[:, j * LANES:(j + 1) * LANES]


def _tile_offset(row):
    off = row * ROW_TILES
    return off if isinstance(off, int) else pl.multiple_of(off, ROW_TILES)


def _rows_copy(src_ref, src_row, dst_ref, dst_row, n_rows, sem):
    s = _tile_offset(src_row)
    d = _tile_offset(dst_row)
    return pltpu.make_async_copy(src_ref.at[pl.ds(s, n_rows * ROW_TILES)],
                                 dst_ref.at[pl.ds(d, n_rows * ROW_TILES)], sem)


def _adaln_kernel(c_ref, w_ref, b_ref, o_ref):
    c = c_ref[...]
    s = c * jax.nn.sigmoid(c)
    o_ref[...] = jnp.dot(s, w_ref[...], precision=lax.Precision.HIGHEST,
                         preferred_element_type=F32) + b_ref[...]


def _adaln(c, w_ada, b_ada):
    n, d = c.shape
    cols = w_ada.shape[1]
    tn = 1536
    return pl.pallas_call(
        _adaln_kernel,
        grid=(cols // tn,),
        in_specs=[pl.BlockSpec((n, d), lambda j: (0, 0)),
                  pl.BlockSpec((d, tn), lambda j: (0, j)),
                  pl.BlockSpec((1, tn), lambda j: (0, j))],
        out_specs=pl.BlockSpec((n, tn), lambda j: (0, j)),
        out_shape=jax.ShapeDtypeStruct((n, cols), F32),
        compiler_params=_params(1),
        name="adaln",
    )(c, w_ada, b_ada.reshape(1, cols))


def _stage1_kernel(xp_ref, xs_ref, shp_ref, shs_ref, scp_ref, scs_ref, g1_ref, win_ref, lng_ref, lnb_ref,
                   wm_ref, wms_ref, bs_ref, bss_ref, gsgu_ref,
                   q_ref, k_ref, v_ref, sgu_ref, vn_ref, *, n_prompt_tiles):
    is_sample = pl.program_id(0) >= n_prompt_tiles
    x = _pick_tile(is_sample, xp_ref, xs_ref)
    h = (_rms(x) * g1_ref[...] * (1.0 + _cond_rows(is_sample, scp_ref, scs_ref))
         + _cond_rows(is_sample, shp_ref, shs_ref))
    proj = jnp.dot(h.astype(BF16), win_ref[...], preferred_element_type=F32)
    q_ref[...] = (proj[:, :ATTN_WIDTH] * (HEAD_DIM ** -0.5)).astype(BF16)
    k_ref[...] = proj[:, ATTN_WIDTH:ATTN_WIDTH + KV_WIDTH]
    v_ref[...] = proj[:, ATTN_WIDTH + KV_WIDTH:ATTN_WIDTH + 2 * KV_WIDTH]
    c0 = ATTN_WIDTH + 2 * KV_WIDTH
    u = jax.nn.gelu(proj[:, c0:c0 + SGU_WIDTH])
    vg = jax.nn.gelu(proj[:, c0 + SGU_WIDTH:])
    mu = jnp.mean(vg, axis=-1, keepdims=True)
    xc = vg - mu
    var = jnp.mean(xc * xc, axis=-1, keepdims=True)
    vn = xc * lax.rsqrt(var + EPS) * lng_ref[...] + lnb_ref[...]

    @pl.when(is_sample)
    def _():
        vn_ref[...] = vn

    vnb = vn.astype(BF16)
    wm = jnp.where(is_sample, wms_ref[...], wm_ref[...])
    bs = jnp.where(is_sample, bss_ref[...], bs_ref[...])
    rows = []
    for c in range(TOKEN_TILE // SGU_CHUNK):
        r0 = c * SGU_CHUNK
        cols = []
        for g in range(SGU_GROUPS):
            l0 = g * SGU_CHUNK
            sv = jnp.dot(wm[g], vnb[r0:r0 + SGU_CHUNK, l0:l0 + SGU_CHUNK],
                         preferred_element_type=F32) + bs[g]
            cols.append(sv)
        rows.append(jnp.concatenate(cols, axis=1))
    sgu = u * jnp.concatenate(rows, axis=0)
    sgu_ref[...] = (_rms(sgu) * gsgu_ref[...]).astype(BF16)


def _stage1(xp3, xs3, mod_p, mod_s, g1, win_b, lng, lnb, wm, wms, bs, bss, gsgu, batch, n_prompt_tiles):
    t = (xp3.shape[0] + xs3.shape[0]) * ROW_GROUP
    n_tiles = t // TOKEN_TILE
    last_p = n_prompt_tiles - 1
    x_blk = (TOKEN_TILE // ROW_GROUP, ROW_GROUP, D_MODEL)
    const2 = lambda i: (0, 0)
    const3 = lambda i: (0, 0, 0)
    tile2 = lambda i: (i, 0)
    cond = functools.partial(_cond_specs, tile_rows=TOKEN_TILE, tiles_per_batch=n_prompt_tiles // batch,
                             batch=batch, n_prompt_tiles=n_prompt_tiles, n_sample_tiles=n_tiles - n_prompt_tiles)
    sgu_w_spec = pl.BlockSpec((SGU_GROUPS, SGU_CHUNK, SGU_CHUNK), const3)
    return pl.pallas_call(
        functools.partial(_stage1_kernel, n_prompt_tiles=n_prompt_tiles),
        grid=(n_tiles,),
        in_specs=[pl.BlockSpec(x_blk, lambda i: (jnp.minimum(i, last_p), 0, 0)),
                  pl.BlockSpec(x_blk, const3)]
                 + cond(0) + cond(1)
                 + [pl.BlockSpec((1, D_MODEL), const2),
                    pl.BlockSpec((D_MODEL, IN_COLS), const2),
                    pl.BlockSpec((1, SGU_WIDTH), const2),
                    pl.BlockSpec((1, SGU_WIDTH), const2),
                    sgu_w_spec, sgu_w_spec, sgu_w_spec, sgu_w_spec,
                    pl.BlockSpec((1, SGU_WIDTH), const2)],
        out_specs=[pl.BlockSpec((TOKEN_TILE, ATTN_WIDTH), tile2),
                   pl.BlockSpec((TOKEN_TILE, KV_WIDTH), tile2),
                   pl.BlockSpec((TOKEN_TILE, KV_WIDTH), tile2),
                   pl.BlockSpec((TOKEN_TILE, SGU_WIDTH), tile2),
                   pl.BlockSpec((TOKEN_TILE, SGU_WIDTH), const2)],
        out_shape=[jax.ShapeDtypeStruct((t, ATTN_WIDTH), BF16),
                   jax.ShapeDtypeStruct((t, KV_WIDTH), F32),
                   jax.ShapeDtypeStruct((t, KV_WIDTH), F32),
                   jax.ShapeDtypeStruct((t, SGU_WIDTH), BF16),
                   jax.ShapeDtypeStruct((TOKEN_TILE, SGU_WIDTH), F32)],
        compiler_params=_params(1),
        name="stage1",
    )(xp3, xs3, mod_p, mod_s, mod_p, mod_s, g1, win_b, lng, lnb, wm, wms, bs, bss, gsgu)


def _alibi_slope(h):
    return float(np.float32(2.0) ** np.float32(-8.0 * (h + 1.0) / N_HEADS))


def _sink_attention(q, segments, sinks_ref, gout):
    outs = []
    for h in range(N_HEADS):
        kv0 = (h // GQA_GROUP) * HEAD_DIM
        qh = q[:, h * HEAD_DIM:(h + 1) * HEAD_DIM]
        sink = sinks_ref[h]
        slope = _alibi_slope(h)
        logits = []
        m = jnp.full((q.shape[0], 1), sink, F32)
        for (k, _, dist, valid) in segments:
            l = lax.dot_general(qh, k[:, kv0:kv0 + HEAD_DIM], (((1,), (1,)), ((), ())),
                                preferred_element_type=F32)
            l = l - slope * dist
            if valid is not None:
                l = jnp.where(valid, l, -jnp.inf)
            logits.append(l)
            m = jnp.maximum(m, jnp.max(l, axis=-1, keepdims=True))
        denom = jnp.exp(sink - m)
        acc = None
        for l, (_, v, _, _) in zip(logits, segments):
            e = jnp.exp(l - m)
            denom = denom + jnp.sum(e, axis=-1, keepdims=True)
            pv = jnp.dot(e.astype(BF16), v[:, kv0:kv0 + HEAD_DIM], preferred_element_type=F32)
            acc = pv if acc is None else acc + pv
        outs.append(acc / denom)
    o = jnp.concatenate(outs, axis=1)
    return (_rms(o) * gout).astype(BF16)


def _attn_prompt_kernel(sinks_ref, q_ref, kp_ref, kc_ref, vp_ref, vc_ref, g_ref, o_ref):
    i = pl.program_id(1)
    nq = ATTN_TILE
    r = lax.broadcasted_iota(I32, (nq, nq), 0)
    c = lax.broadcasted_iota(I32, (nq, nq), 1)
    rc = r // CHUNK
    cc = c // CHUNK
    dist_prev = jnp.abs(r - c + nq).astype(F32)
    dist_cur = jnp.abs(r - c).astype(F32)
    valid_prev = jnp.logical_and(i > 0, cc >= rc)
    valid_cur = cc <= rc
    segs = [(kp_ref[...].astype(BF16), vp_ref[...].astype(BF16), dist_prev, valid_prev),
            (kc_ref[...].astype(BF16), vc_ref[...].astype(BF16), dist_cur, valid_cur)]
    o_ref[...] = _sink_attention(q_ref[...], segs, sinks_ref, g_ref[...])


def _attn_prompt(q, k, v, sinks, gattn, batch, seq):
    nt = seq // ATTN_TILE
    cur = lambda b, i, s: (b * nt + i, 0)
    prev = lambda b, i, s: (b * nt + jnp.maximum(i - 1, 0), 0)
    kv_blk = (ATTN_TILE, KV_WIDTH)
    return pl.pallas_call(
        _attn_prompt_kernel,
        grid_spec=pltpu.PrefetchScalarGridSpec(
            num_scalar_prefetch=1,
            grid=(batch, nt),
            in_specs=[pl.BlockSpec((ATTN_TILE, ATTN_WIDTH), cur),
                      pl.BlockSpec(kv_blk, prev), pl.BlockSpec(kv_blk, cur),
                      pl.BlockSpec(kv_blk, prev), pl.BlockSpec(kv_blk, cur),
                      pl.BlockSpec((1, ATTN_WIDTH), lambda b, i, s: (0, 0))],
            out_specs=pl.BlockSpec((ATTN_TILE, ATTN_WIDTH), cur)),
        out_shape=jax.ShapeDtypeStruct((batch * seq, ATTN_WIDTH), BF16),
        compiler_params=_params(2),
        name="attn_prompt",
    )(sinks, q, k, k, v, v, gattn)


def _attn_sample_kernel(sinks_ref, q_ref, ck_ref, cv_ref, k_ref, v_ref, g_ref, o_ref):
    nq = q_ref.shape[0]
    w = ck_ref.shape[1]
    r = lax.broadcasted_iota(I32, (nq, w), 0)
    c = lax.broadcasted_iota(I32, (nq, w), 1)
    dist_cache = jnp.abs(r - c + w).astype(F32)
    r2 = lax.broadcasted_iota(I32, (nq, nq), 0)
    c2 = lax.broadcasted_iota(I32, (nq, nq), 1)
    dist_new = jnp.abs(r2 - c2).astype(F32)
    segs = [(ck_ref[0].astype(BF16), cv_ref[0].astype(BF16), dist_cache, None),
            (k_ref[...].astype(BF16), v_ref[...].astype(BF16), dist_new, None)]
    o_ref[...] = _sink_attention(q_ref[...], segs, sinks_ref, g_ref[...])


def _attn_sample(q, k, v, cache_k, cache_v, sinks, gattn, dec_batch, n, row0):
    w = cache_k.shape[1]
    blk0 = row0 // n
    new = lambda b, s: (blk0 + b, 0)
    return pl.pallas_call(
        _attn_sample_kernel,
        grid_spec=pltpu.PrefetchScalarGridSpec(
            num_scalar_prefetch=1,
            grid=(dec_batch,),
            in_specs=[pl.BlockSpec((n, ATTN_WIDTH), new),
                      pl.BlockSpec((1, w, KV_WIDTH), lambda b, s: (b, 0, 0)),
                      pl.BlockSpec((1, w, KV_WIDTH), lambda b, s: (b, 0, 0)),
                      pl.BlockSpec((n, KV_WIDTH), new),
                      pl.BlockSpec((n, KV_WIDTH), new),
                      pl.BlockSpec((1, ATTN_WIDTH), lambda b, s: (0, 0))],
            out_specs=pl.BlockSpec((n, ATTN_WIDTH), lambda b, s: (b, 0))),
        out_shape=jax.ShapeDtypeStruct((dec_batch * n, ATTN_WIDTH), BF16),
        compiler_params=_params(1),
        name="attn_sample",
    )(sinks, q, cache_k, cache_v, k, v, gattn)


def _merge_kernel(ap_ref, as_ref, sgu_ref, xp_ref, xs_ref,
                  g1p_ref, g1s_ref, sh2p_ref, sh2s_ref, sc2p_ref, sc2s_ref, g2p_ref, g2s_ref,
                  woa_ref, wos_ref, g2_ref, wrh_ref, wrl_ref, wsgu_ref, wsd_ref,
                  xb_ref, h2_ref, sc_ref, *, n_prompt_tiles):
    is_sample = pl.program_id(0) >= n_prompt_tiles
    x = _pick_tile(is_sample, xp_ref, xs_ref)
    a = jnp.where(is_sample, as_ref[...], ap_ref[...])
    mix = (jnp.dot(a, woa_ref[...], preferred_element_type=F32)
           + jnp.dot(sgu_ref[...], wos_ref[...], preferred_element_type=F32))
    x1 = x + _cond_rows(is_sample, g1p_ref, g1s_ref) * mix
    h2 = (_rms(x1) * g2_ref[...] * (1.0 + _cond_rows(is_sample, sc2p_ref, sc2s_ref))
          + _cond_rows(is_sample, sh2p_ref, sh2s_ref))
    _store_rows(h2_ref, h2)
    hh = h2.astype(BF16)
    hl = (h2 - hh.astype(F32)).astype(BF16)
    logits = (jnp.dot(hh, wrh_ref[...], preferred_element_type=F32)
              + (jnp.dot(hl, wrh_ref[...], preferred_element_type=F32)
                 + jnp.dot(hh, wrl_ref[...], preferred_element_type=F32)))
    sc_ref[...] = jax.nn.sigmoid(logits)
    gu = jnp.dot(hh, wsgu_ref[...], preferred_element_type=F32)
    g = gu[:, :EXPERT_DIM]
    act = (g * jax.nn.sigmoid(g)) * gu[:, EXPERT_DIM:]
    shared = jnp.dot(act.astype(BF16), wsd_ref[...], preferred_element_type=F32)
    xb_ref[...] = x1 + _cond_rows(is_sample, g2p_ref, g2s_ref) * shared


def _merge(attn_p, attn_s, sgu_n, xp3, xs3, mod_p, mod_s, woa, wos, g2, wrh, wrl, wsgu, wsd, batch, n_prompt_tiles):
    t = sgu_n.shape[0]
    n_tiles = t // TOKEN_TILE
    last_p = n_prompt_tiles - 1
    x_blk = (TOKEN_TILE // ROW_GROUP, ROW_GROUP, D_MODEL)
    const2 = lambda i: (0, 0)
    const3 = lambda i: (0, 0, 0)
    tile2 = lambda i: (i, 0)
    cond = functools.partial(_cond_specs, tile_rows=TOKEN_TILE, tiles_per_batch=n_prompt_tiles // batch,
                             batch=batch, n_prompt_tiles=n_prompt_tiles, n_sample_tiles=n_tiles - n_prompt_tiles)
    return pl.pallas_call(
        functools.partial(_merge_kernel, n_prompt_tiles=n_prompt_tiles),
        grid=(n_tiles,),
        in_specs=[pl.BlockSpec((TOKEN_TILE, ATTN_WIDTH), lambda i: (jnp.minimum(i, last_p), 0)),
                  pl.BlockSpec((TOKEN_TILE, ATTN_WIDTH), const2),
                  pl.BlockSpec((TOKEN_TILE, SGU_WIDTH), tile2),
                  pl.BlockSpec(x_blk, lambda i: (jnp.minimum(i, last_p), 0, 0)),
                  pl.BlockSpec(x_blk, const3)]
                 + cond(2) + cond(3) + cond(4) + cond(5)
                 + [pl.BlockSpec((ATTN_WIDTH, D_MODEL), const2),
                    pl.BlockSpec((SGU_WIDTH, D_MODEL), const2),
                    pl.BlockSpec((1, D_MODEL), const2),
                    pl.BlockSpec((D_MODEL, N_EXPERTS), const2),
                    pl.BlockSpec((D_MODEL, N_EXPERTS), const2),
                    pl.BlockSpec((D_MODEL, 2 * EXPERT_DIM), const2),
                    pl.BlockSpec((EXPERT_DIM, D_MODEL), const2)],
        out_specs=[pl.BlockSpec((TOKEN_TILE, D_MODEL), tile2),
                   pl.BlockSpec((TOKEN_TILE * ROW_TILES, LANES), tile2),
                   pl.BlockSpec((TOKEN_TILE, N_EXPERTS), tile2)],
        out_shape=[jax.ShapeDtypeStruct((t, D_MODEL), F32),
                   jax.ShapeDtypeStruct((t * ROW_TILES, LANES), F32),
                   jax.ShapeDtypeStruct((t, N_EXPERTS), F32)],
        compiler_params=_params(1),
        name="merge",
    )(attn_p, attn_s, sgu_n, xp3, xs3, *([mod_p, mod_s] * 4), woa, wos, g2, wrh, wrl, wsgu, wsd)


def _first_argmax(x, iota, size, axis):
    m = jnp.max(x, axis=axis, keepdims=True)
    idx = jnp.min(jnp.where(x == m, iota, size), axis=axis, keepdims=True)
    return m, idx


def _route_kernel(sc_ref, bias_ref, tri_ref, eidx_ref, wts_ref, rank_ref, cnt_ref, base_ref):
    tt = sc_ref.shape[0]

    @pl.when(pl.program_id(0) == 0)
    def _():
        base_ref[...] = jnp.zeros_like(base_ref)

    s_t = sc_ref[...].T
    sel = s_t + bias_ref[...]
    sel3 = sel.reshape(N_EXPERT_GROUPS, GROUP_SIZE, tt)
    io3 = lax.broadcasted_iota(I32, sel3.shape, 1)
    m1, i1 = _first_argmax(sel3, io3, GROUP_SIZE, 1)
    m2 = jnp.max(jnp.where(io3 == i1, -jnp.inf, sel3), axis=1, keepdims=True)
    gs = (m1 + m2).reshape(N_EXPERT_GROUPS, tt)
    io8 = lax.broadcasted_iota(I32, gs.shape, 0)
    gmask = jnp.zeros(gs.shape, jnp.bool_)
    for _ in range(TOPK_GROUPS):
        _, gi = _first_argmax(gs, io8, N_EXPERT_GROUPS, 0)
        hit = io8 == gi
        gmask = jnp.logical_or(gmask, hit)
        gs = jnp.where(hit, -jnp.inf, gs)
    emask = jnp.broadcast_to(gmask.reshape(N_EXPERT_GROUPS, 1, tt), sel3.shape).reshape(N_EXPERTS, tt)
    cand = jnp.where(emask, sel, -jnp.inf)
    io = lax.broadcasted_iota(I32, cand.shape, 0)
    chosen = jnp.zeros(cand.shape, jnp.bool_)
    eidx, wts = [], []
    for _ in range(TOP_K):
        _, ei = _first_argmax(cand, io, N_EXPERTS, 0)
        hit = io == ei
        eidx.append(ei)
        wts.append(jnp.sum(jnp.where(hit, s_t, 0.0), axis=0, keepdims=True))
        chosen = jnp.logical_or(chosen, hit)
        cand = jnp.where(hit, -jnp.inf, cand)
    w = jnp.concatenate(wts, axis=0)
    wts_ref[...] = w / jnp.sum(w, axis=0, keepdims=True) * ROUTED_SCALE
    eidx_ref[...] = jnp.concatenate(eidx, axis=0)
    cf = jnp.where(chosen, 1.0, 0.0)
    ahead = jnp.dot(cf.astype(BF16), tri_ref[...], preferred_element_type=F32) + base_ref[...]
    ranks = [jnp.sum(jnp.where(io == ei, ahead, 0.0), axis=0, keepdims=True) for ei in eidx]
    rank_ref[...] = jnp.concatenate(ranks, axis=0).astype(I32)
    total = base_ref[...] + jnp.sum(cf, axis=1, keepdims=True)
    base_ref[...] = total
    cnt_ref[...] = total.astype(I32)


def _route(scores, bias_col, tri):
    t = scores.shape[0]
    n_tiles = t // TOKEN_TILE
    col = lambda i: (0, i)
    return pl.pallas_call(
        _route_kernel,
        grid=(n_tiles,),
        in_specs=[pl.BlockSpec((TOKEN_TILE, N_EXPERTS), lambda i: (i, 0)),
                  pl.BlockSpec((N_EXPERTS, 1), lambda i: (0, 0)),
                  pl.BlockSpec((TOKEN_TILE, TOKEN_TILE), lambda i: (0, 0))],
        out_specs=[pl.BlockSpec((TOP_K, TOKEN_TILE), col),
                   pl.BlockSpec((TOP_K, TOKEN_TILE), col),
                   pl.BlockSpec((TOP_K, TOKEN_TILE), col),
                   pl.BlockSpec((N_EXPERTS, 1), lambda i: (0, 0))],
        out_shape=[jax.ShapeDtypeStruct((TOP_K, t), I32),
                   jax.ShapeDtypeStruct((TOP_K, t), F32),
                   jax.ShapeDtypeStruct((TOP_K, t), I32),
                   jax.ShapeDtypeStruct((N_EXPERTS, 1), I32)],
        scratch_shapes=[pltpu.VMEM((N_EXPERTS, 1), F32)],
        compiler_params=_params(1),
        name="route",
    )(scores, bias_col, tri)


def _pos_kernel(eidx_ref, rank_ref, start_ref, pos_ref):
    e = eidx_ref[...]
    tt = e.shape[1]
    io = lax.broadcasted_iota(I32, (N_EXPERTS, tt), 0)
    st = start_ref[...]
    rows = [jnp.sum(jnp.where(io == e[k:k + 1, :], st, 0), axis=0, keepdims=True) for k in range(TOP_K)]
    pos_ref[...] = jnp.concatenate(rows, axis=0) + rank_ref[...]


def _positions(eidx, rank, start_col):
    t = eidx.shape[1]
    col = lambda i: (0, i)
    blk = pl.BlockSpec((TOP_K, TOKEN_TILE), col)
    return pl.pallas_call(
        _pos_kernel,
        grid=(t // TOKEN_TILE,),
        in_specs=[blk, blk, pl.BlockSpec((N_EXPERTS, 1), lambda i: (0, 0))],
        out_specs=blk,
        out_shape=jax.ShapeDtypeStruct((TOP_K, t), I32),
        compiler_params=_params(1),
        name="positions",
    )(eidx, rank, start_col)


def _dispatch_kernel(pos_ref, h_ref, xs_ref, zero_ref, sem, zsem, *, n_rows):
    tt = h_ref.shape[0] // ROW_TILES
    first = pl.program_id(0) == 0

    @pl.when(first)
    def _():
        zero_ref[...] = jnp.zeros_like(zero_ref)
        _rows_copy(zero_ref, 0, xs_ref, n_rows, EXPERT_BLOCK, zsem).start()

    def body(t, carry):
        for k in range(TOP_K):
            _rows_copy(h_ref, t, xs_ref, pos_ref[k, t], 1, sem).start(priority=k % 2)
        return carry

    lax.fori_loop(0, tt, body, 0)
    for k in range(TOP_K):
        _rows_copy(h_ref, 0, xs_ref, 0, tt, sem).wait()

    @pl.when(first)
    def _():
        _rows_copy(zero_ref, 0, xs_ref, n_rows, EXPERT_BLOCK, zsem).wait()


def _dispatch(pos, h2_tiles, n_rows):
    t = h2_tiles.shape[0] // ROW_TILES
    return pl.pallas_call(
        functools.partial(_dispatch_kernel, n_rows=n_rows),
        grid=(t // TOKEN_TILE,),
        in_specs=[pl.BlockSpec((TOP_K, TOKEN_TILE), lambda i: (0, i), memory_space=pltpu.SMEM),
                  pl.BlockSpec((TOKEN_TILE * ROW_TILES, LANES), lambda i: (i, 0))],
        out_specs=pl.BlockSpec(memory_space=pl.ANY),
        out_shape=jax.ShapeDtypeStruct(((n_rows + EXPERT_BLOCK) * ROW_TILES, LANES), F32),
        scratch_shapes=[pltpu.VMEM((EXPERT_BLOCK * ROW_TILES, LANES), F32),
                        pltpu.SemaphoreType.DMA, pltpu.SemaphoreType.DMA],
        compiler_params=_params(1),
        name="dispatch",
    )(pos, h2_tiles)


def _piece_sizes():
    sizes, s = [], EXPERT_BLOCK // 2
    while s >= 1:
        sizes.append(s)
        s //= 2
    return sizes


def _out_pieces(nvalid, fn):
    @pl.when(nvalid == EXPERT_BLOCK)
    def _():
        fn(0, EXPERT_BLOCK)

    @pl.when(nvalid < EXPERT_BLOCK)
    def _():
        for s in _piece_sizes():
            @pl.when((nvalid & s) != 0)
            def _(s=s):
                fn(nvalid & ~(2 * s - 1), s)


def _expert_kernel(start_ref, count_ref, blk0_ref, xs_ref, wg_ref, wu_ref, wd_ref, ys_ref,
                   xbuf, ybuf, wgu_s, wd_s, prev_ref, xsem, ysem, *, n_rows):
    e = pl.program_id(0)
    cnt = count_ref[e]
    st = start_ref[e]
    g0 = blk0_ref[e]
    nblk = (cnt + EXPERT_BLOCK - 1) // EXPERT_BLOCK

    def x_copy(row, slot):
        return _rows_copy(xs_ref, row, xbuf.at[slot], 0, EXPERT_BLOCK, xsem.at[slot])

    def y_wait(nvalid, slot):
        def wait(off, size):
            _rows_copy(ybuf.at[slot], 0, ys_ref, 0, size, ysem.at[slot]).wait()
        _out_pieces(nvalid, wait)

    @pl.when(e == 0)
    def _():
        prev_ref[0] = 0
        x_copy(0, 0).start()

    @pl.when(cnt > 0)
    def _():
        wgu_s[:, :EXPERT_DIM] = wg_ref[0].astype(BF16)
        wgu_s[:, EXPERT_DIM:] = wu_ref[0].astype(BF16)
        wd_s[...] = wd_ref[0].astype(BF16)

    def block(j, carry):
        slot = (g0 + j) & 1
        row0 = st + j * EXPERT_BLOCK
        nvalid = jnp.minimum(cnt - j * EXPERT_BLOCK, EXPERT_BLOCK)
        x_copy(row0, slot).wait()
        nxt = row0 + nvalid

        @pl.when(nxt < n_rows)
        def _():
            x_copy(nxt, 1 - slot).start()

        x = _load_rows(xbuf.at[slot], EXPERT_BLOCK).astype(BF16)
        gu = jnp.dot(x, wgu_s[...], preferred_element_type=F32)
        g = gu[:, :EXPERT_DIM]
        act = (g * jax.nn.sigmoid(g)) * gu[:, EXPERT_DIM:]
        y = jnp.dot(act.astype(BF16), wd_s[...], preferred_element_type=F32)

        prev = prev_ref[0]
        y_wait(prev, 1 - slot)
        _store_rows(ybuf.at[slot], y)

        def put(off, size):
            _rows_copy(ybuf.at[slot], off, ys_ref, row0 + off, size, ysem.at[slot]).start()
        _out_pieces(nvalid, put)
        prev_ref[0] = nvalid
        return carry

    lax.fori_loop(0, nblk, block, 0)

    @pl.when(e == pl.num_programs(0) - 1)
    def _():
        y_wait(prev_ref[0], (g0 + nblk - 1) & 1)


def _experts(start, count, blk0, xs, w_gate, w_up, w_down, n_rows):
    w_map = lambda e, s, c, b: (e, 0, 0)
    blk_shape = (2, EXPERT_BLOCK * ROW_TILES, LANES)
    return pl.pallas_call(
        functools.partial(_expert_kernel, n_rows=n_rows),
        grid_spec=pltpu.PrefetchScalarGridSpec(
            num_scalar_prefetch=3,
            grid=(N_EXPERTS,),
            in_specs=[pl.BlockSpec(memory_space=pl.ANY),
                      pl.BlockSpec((1, D_MODEL, EXPERT_DIM), w_map),
                      pl.BlockSpec((1, D_MODEL, EXPERT_DIM), w_map),
                      pl.BlockSpec((1, EXPERT_DIM, D_MODEL), w_map)],
            out_specs=pl.BlockSpec(memory_space=pl.ANY),
            scratch_shapes=[pltpu.VMEM(blk_shape, F32),
                            pltpu.VMEM(blk_shape, F32),
                            pltpu.VMEM((D_MODEL, 2 * EXPERT_DIM), BF16),
                            pltpu.VMEM((EXPERT_DIM, D_MODEL), BF16),
                            pltpu.SMEM((1,), I32),
                            pltpu.SemaphoreType.DMA((2,)),
                            pltpu.SemaphoreType.DMA((2,))]),
        out_shape=jax.ShapeDtypeStruct((n_rows * ROW_TILES, LANES), F32),
        compiler_params=_params(1),
        name="experts",
    )(start, count, blk0, xs, w_gate, w_up, w_down)


def _combine_kernel(pos_ref, posn_ref, ys_ref, xb_ref, wts_ref, g2p_ref, g2s_ref, gf_ref, yp_ref, ysm_ref,
                    buf, sem, *, n_prompt_tiles):
    i = pl.program_id(0)
    n_tiles = pl.num_programs(0)
    tt = xb_ref.shape[0]
    slot = i & 1

    def gather(p_ref, s):
        def body(t, carry):
            for k in range(TOP_K):
                _rows_copy(ys_ref, p_ref[k, t], buf.at[s, k], t, 1, sem.at[s]).start(priority=k % 2)
            return carry
        lax.fori_loop(0, tt, body, 0)

    @pl.when(i == 0)
    def _():
        gather(pos_ref, 0)

    @pl.when(i + 1 < n_tiles)
    def _():
        gather(posn_ref, 1 - slot)

    for k in range(TOP_K):
        _rows_copy(ys_ref, 0, buf.at[slot, k], 0, tt, sem.at[slot]).wait()
    w = wts_ref[...]
    routed = w[:, 0:1] * _load_rows(buf.at[slot, 0], tt)
    for k in range(1, TOP_K):
        routed = routed + w[:, k:k + 1] * _load_rows(buf.at[slot, k], tt)
    is_sample = i >= n_prompt_tiles
    x = xb_ref[...] + _cond_rows(is_sample, g2p_ref, g2s_ref) * routed
    y = _rms(x) * gf_ref[...]

    @pl.when(i < n_prompt_tiles)
    def _():
        yp_ref[...] = y

    @pl.when(i >= n_prompt_tiles)
    def _():
        ysm_ref[...] = y


def _combine(pos, ys, xb, wts_t, mod_p, mod_s, gf, batch, n_prompt_rows):
    t = xb.shape[0]
    tt = COMBINE_TILE
    n_tiles = t // tt
    n_p = n_prompt_rows // tt
    pos_spec = lambda f: pl.BlockSpec((TOP_K, tt), f, memory_space=pltpu.SMEM)
    return pl.pallas_call(
        functools.partial(_combine_kernel, n_prompt_tiles=n_p),
        grid=(n_tiles,),
        in_specs=[pos_spec(lambda i: (0, i)),
                  pos_spec(lambda i: (0, jnp.minimum(i + 1, n_tiles - 1))),
                  pl.BlockSpec(memory_space=pl.ANY),
                  pl.BlockSpec((tt, D_MODEL), lambda i: (i, 0)),
                  pl.BlockSpec((tt, TOP_K), lambda i: (i, 0))]
                 + _cond_specs(5, tt, n_p // batch, batch, n_p, n_tiles - n_p)
                 + [pl.BlockSpec((1, D_MODEL), lambda i: (0, 0))],
        out_specs=[pl.BlockSpec((tt, D_MODEL), lambda i: (jnp.minimum(i, n_p - 1), 0)),
                   pl.BlockSpec((tt, D_MODEL), lambda i: (jnp.maximum(i - n_p, 0), 0))],
        out_shape=[jax.ShapeDtypeStruct((n_prompt_rows, D_MODEL), F32),
                   jax.ShapeDtypeStruct((t - n_prompt_rows, D_MODEL), F32)],
        scratch_shapes=[pltpu.VMEM((2, TOP_K, tt * ROW_TILES, LANES), F32), pltpu.SemaphoreType.DMA((2,))],
        compiler_params=_params(1),
        name="combine",
    )(pos, pos, ys, xb, wts_t, mod_p, mod_s, gf)


def _sgu_mask():
    pos = np.arange(SGU_CHUNK) // CHUNK
    return jnp.asarray(pos[None, :] <= pos[:, None])


def _layer(xp3, xs3, cache_k, cache_v, mod, final_g, batch, seq, dec_batch, n, p):
    tp = batch * seq
    t = tp + dec_batch * n
    n_prompt_tiles = tp // TOKEN_TILE
    mod_p = mod[:batch].reshape(batch, 1, N_MOD * D_MODEL)
    mod_s = mod[batch:].reshape(dec_batch, 1, N_MOD * D_MODEL)

    wm = jnp.where(_sgu_mask(), p["sgu_w"], 0.0)
    reps = SGU_CHUNK // n
    eye = jnp.eye(reps, dtype=F32)
    wms = jnp.stack([jnp.kron(eye, wm[g, :n, :n]) for g in range(SGU_GROUPS)])
    bs = jnp.broadcast_to(p["sgu_b"][:, :, None], (SGU_GROUPS, SGU_CHUNK, SGU_CHUNK))
    bss = jnp.broadcast_to(jnp.tile(p["sgu_b"][:, :n], (1, reps))[:, :, None], (SGU_GROUPS, SGU_CHUNK, SGU_CHUNK))

    q, k, v, sgu_n, vn_s = _stage1(
        xp3, xs3, mod_p, mod_s, p["norm1_g"].reshape(1, -1), p["w_in"].astype(BF16),
        p["sgu_ln_g"].reshape(1, -1), p["sgu_ln_b"].reshape(1, -1),
        wm.astype(BF16), wms.astype(BF16), bs, bss, p["sgu_out_g"].reshape(1, -1), batch, n_prompt_tiles)

    gattn = p["attn_out_g"].reshape(1, -1)
    attn_p = _attn_prompt(q, k, v, p["attn_sinks"], gattn, batch, seq)
    w = cache_k.shape[1]
    attn_s = _attn_sample(q, k, v, cache_k.reshape(dec_batch, w, KV_WIDTH), cache_v.reshape(dec_batch, w, KV_WIDTH),
                          p["attn_sinks"], gattn, dec_batch, n, tp)

    wo = p["w_out"].astype(BF16)
    wsgu = jnp.concatenate([p["ws_gate"], p["ws_up"]], axis=1).astype(BF16)
    wrh = p["w_router"].astype(BF16)
    wrl = (p["w_router"] - wrh.astype(F32)).astype(BF16)
    xb, h2_tiles, scores = _merge(attn_p, attn_s, sgu_n, xp3, xs3, mod_p, mod_s,
                                  wo[:ATTN_WIDTH], wo[ATTN_WIDTH:], p["norm2_g"].reshape(1, -1), wrh, wrl,
                                  wsgu, p["ws_down"].astype(BF16), batch, n_prompt_tiles)

    ti = np.arange(TOKEN_TILE)
    tri = jnp.asarray(ti[:, None] < ti[None, :], dtype=BF16)
    eidx, wts, rank, counts = _route(scores, p["router_bias"].reshape(N_EXPERTS, 1), tri)

    counts = counts[:, 0]
    ends = jnp.cumsum(counts)
    start = (ends - counts).astype(I32)
    nblk = (counts + EXPERT_BLOCK - 1) // EXPERT_BLOCK
    blk0 = (jnp.cumsum(nblk) - nblk).astype(I32)
    pos = _positions(eidx, rank, start.reshape(N_EXPERTS, 1))

    n_rows = t * TOP_K
    xs_sorted = _dispatch(pos, h2_tiles, n_rows)
    ys_sorted = _experts(start, counts.astype(I32), blk0, xs_sorted, p["w_gate"], p["w_up"], p["w_down"], n_rows)
    y_p, y_s = _combine(pos, ys_sorted, xb, wts.T, mod_p, mod_s, final_g.reshape(1, -1), batch, tp)
    return y_p, y_s, k, v, vn_s


def kernel(x_prompt, x_sample, cache_k, cache_v, c_prompt, c_sample, norm1_g, w_ada, b_ada, w_in, sgu_ln_g, sgu_ln_b, sgu_w, sgu_b, attn_sinks, attn_out_g, sgu_out_g, w_out, norm2_g, w_router, router_bias, w_gate, w_up, w_down, ws_gate, ws_up, ws_down, final_g):
    batch, seq, d = x_prompt.shape
    dec_batch, n, _ = x_sample.shape
    depth = norm1_g.shape[0]
    assert depth == 1 and d == D_MODEL
    assert seq % TOKEN_TILE == 0 and dec_batch * n == TOKEN_TILE and n == ROW_GROUP
    assert seq % ATTN_TILE == 0 and seq >= WINDOW and cache_k.shape[2] == WINDOW
    tp = batch * seq

    xp3 = x_prompt.reshape(tp // ROW_GROUP, ROW_GROUP, d)
    xs3 = x_sample.reshape(dec_batch * n // ROW_GROUP, ROW_GROUP, d)
    c_all = jnp.concatenate([c_prompt, c_sample], axis=0)
    l = 0
    mod = _adaln(c_all, w_ada[l], b_ada[l])
    p = dict(norm1_g=norm1_g[l], w_in=w_in[l], sgu_ln_g=sgu_ln_g[l], sgu_ln_b=sgu_ln_b[l], sgu_w=sgu_w[l],
             sgu_b=sgu_b[l], attn_sinks=attn_sinks[l], attn_out_g=attn_out_g[l], sgu_out_g=sgu_out_g[l],
             w_out=w_out[l], norm2_g=norm2_g[l], w_router=w_router[l], router_bias=router_bias[l],
             w_gate=w_gate[l], w_up=w_up[l], w_down=w_down[l], ws_gate=ws_gate[l], ws_up=ws_up[l],
             ws_down=ws_down[l])
    y_p, y_s, k, v, vn_s = _layer(xp3, xs3, cache_k[l], cache_v[l], mod, final_g, batch, seq, dec_batch, n, p)

    keep = min(WINDOW, seq)
    kv_shape = (batch, keep, N_KV_HEADS, HEAD_DIM)
    k_p = jnp.stack([k[(b + 1) * seq - keep:(b + 1) * seq] for b in range(batch)]).reshape(kv_shape)
    v_p = jnp.stack([v[(b + 1) * seq - keep:(b + 1) * seq] for b in range(batch)]).reshape(kv_shape)
    k_s = k[tp:].reshape(dec_batch, n, N_KV_HEADS, HEAD_DIM)
    v_s = v[tp:].reshape(dec_batch, n, N_KV_HEADS, HEAD_DIM)
    return (y_p.reshape(batch, seq, d), y_s.reshape(dec_batch, n, d),
            k_p[None], v_p[None], k_s[None], v_s[None], vn_s.reshape(dec_batch, n, SGU_WIDTH)[None])
```

```python
import functools

import numpy as np
import jax
import jax.numpy as jnp
from jax import lax
from jax.experimental import pallas as pl
from jax.experimental.pallas import tpu as pltpu

F32 = jnp.float32
BF16 = jnp.bfloat16
I32 = jnp.int32

D_MODEL = 1024
CHUNK = 64
HEAD_DIM = 64
ATTN_WIDTH = 512
N_HEADS = 8
N_KV_HEADS = 2
GQA_GROUP = 4
KV_WIDTH = 128
WINDOW = 128
SGU_CHUNK = 128
SGU_WIDTH = 512
SGU_GROUPS = 4
IN_COLS = ATTN_WIDTH + 2 * KV_WIDTH + 2 * SGU_WIDTH
N_MOD = 6
N_EXPERTS = 256
TOP_K = 8
N_EXPERT_GROUPS = 8
GROUP_SIZE = N_EXPERTS // N_EXPERT_GROUPS
TOPK_GROUPS = 4
EXPERT_DIM = 256
ROUTED_SCALE = 2.5
EPS = 1e-6

LANES = 128
ROW_TILES = D_MODEL // LANES
ROW_DTYPE = F32
ROW_GROUP = 16
TOKEN_TILE = 256
ATTN_TILE = 256
EXPERT_BLOCK = 256
BLOCK_SIZES = (32, 64, 128, EXPERT_BLOCK)
X_SLOTS = 3
COMBINE_TILE = 128
VMEM_LIMIT_BYTES = 56 * 1024 * 1024


def _params(n_axes=1):
    return pltpu.CompilerParams(dimension_semantics=("arbitrary",) * n_axes,
                                vmem_limit_bytes=VMEM_LIMIT_BYTES)


def _rms(x):
    return x * lax.rsqrt(jnp.mean(x * x, axis=-1, keepdims=True) + EPS)


def _cond_rows(is_sample, mp_ref, ms_ref):
    ms = ms_ref[...]
    g, _, d = ms.shape
    s = jnp.broadcast_to(ms, (g, ROW_GROUP, d)).reshape(g * ROW_GROUP, d)
    return jnp.where(is_sample, s, mp_ref[0])


def _cond_specs(piece, tile_rows, tiles_per_batch, batch, n_prompt_tiles, n_sample_tiles):
    g = tile_rows // ROW_GROUP
    mp = pl.BlockSpec((1, 1, D_MODEL),
                      lambda i: (jnp.minimum(i // tiles_per_batch, batch - 1), 0, piece))
    ms = pl.BlockSpec((g, 1, D_MODEL),
                      lambda i: (jnp.clip(i - n_prompt_tiles, 0, n_sample_tiles - 1), 0, piece))
    return [mp, ms]


def _pick_tile(is_sample, prompt_ref, sample_ref):
    v = jnp.where(is_sample, sample_ref[...], prompt_ref[...])
    return v.reshape(TOKEN_TILE, v.shape[-1])


def _load_rows(ref, n_rows):
    return jnp.concatenate([ref[pl.ds(j, n_rows, stride=ROW_TILES), :] for j in range(ROW_TILES)], axis=1)


def _store_rows(ref, x):
    n_rows = x.shape[0]
    for j in range(ROW_TILES):
        ref[pl.ds(j, n_rows, stride=ROW_TILES), :] = x[:, j * LANES:(j + 1) * LANES].astype(ROW_DTYPE)


def _tile_offset(row):
    off = row * ROW_TILES
    return off if isinstance(off, int) else pl.multiple_of(off, ROW_TILES)


def _rows_copy(src_ref, src_row, dst_ref, dst_row, n_rows, sem):
    s = _tile_offset(src_row)
    d = _tile_offset(dst_row)
    return pltpu.make_async_copy(src_ref.at[pl.ds(s, n_rows * ROW_TILES)],
                                 dst_ref.at[pl.ds(d, n_rows * ROW_TILES)], sem)


def _adaln_kernel(c_ref, w_ref, b_ref, o_ref):
    c = c_ref[...]
    s = c * jax.nn.sigmoid(c)
    o_ref[...] = jnp.dot(s, w_ref[...], precision=lax.Precision.HIGHEST,
                         preferred_element_type=F32) + b_ref[...]


def _adaln(c, w_ada, b_ada):
    n, d = c.shape
    cols = w_ada.shape[1]
    tn = 1536
    return pl.pallas_call(
        _adaln_kernel,
        grid=(cols // tn,),
        in_specs=[pl.BlockSpec((n, d), lambda j: (0, 0)),
                  pl.BlockSpec((d, tn), lambda j: (0, j)),
                  pl.BlockSpec((1, tn), lambda j: (0, j))],
        out_specs=pl.BlockSpec((n, tn), lambda j: (0, j)),
        out_shape=jax.ShapeDtypeStruct((n, cols), F32),
        compiler_params=_params(1),
        name="adaln",
    )(c, w_ada, b_ada.reshape(1, cols))


def _stage1_kernel(xp_ref, xs_ref, shp_ref, shs_ref, scp_ref, scs_ref, g1_ref, win_ref, lng_ref, lnb_ref,
                   wm_ref, wms_ref, bs_ref, bss_ref, gsgu_ref,
                   q_ref, k_ref, v_ref, sgu_ref, vn_ref, *, n_prompt_tiles):
    is_sample = pl.program_id(0) >= n_prompt_tiles
    x = _pick_tile(is_sample, xp_ref, xs_ref)
    h = (_rms(x) * g1_ref[...] * (1.0 + _cond_rows(is_sample, scp_ref, scs_ref))
         + _cond_rows(is_sample, shp_ref, shs_ref))
    proj = jnp.dot(h.astype(BF16), win_ref[...], preferred_element_type=F32)
    q_ref[...] = (proj[:, :ATTN_WIDTH] * (HEAD_DIM ** -0.5)).astype(BF16)
    k_ref[...] = proj[:, ATTN_WIDTH:ATTN_WIDTH + KV_WIDTH]
    v_ref[...] = proj[:, ATTN_WIDTH + KV_WIDTH:ATTN_WIDTH + 2 * KV_WIDTH]
    c0 = ATTN_WIDTH + 2 * KV_WIDTH
    u = jax.nn.gelu(proj[:, c0:c0 + SGU_WIDTH])
    vg = jax.nn.gelu(proj[:, c0 + SGU_WIDTH:])
    mu = jnp.mean(vg, axis=-1, keepdims=True)
    xc = vg - mu
    var = jnp.mean(xc * xc, axis=-1, keepdims=True)
    vn = xc * lax.rsqrt(var + EPS) * lng_ref[...] + lnb_ref[...]

    @pl.when(is_sample)
    def _():
        vn_ref[...] = vn

    vnb = vn.astype(BF16)
    wm = jnp.where(is_sample, wms_ref[...], wm_ref[...])
    bs = jnp.where(is_sample, bss_ref[...], bs_ref[...])
    rows = []
    for c in range(TOKEN_TILE // SGU_CHUNK):
        r0 = c * SGU_CHUNK
        cols = []
        for g in range(SGU_GROUPS):
            l0 = g * SGU_CHUNK
            sv = jnp.dot(wm[g], vnb[r0:r0 + SGU_CHUNK, l0:l0 + SGU_CHUNK],
                         preferred_element_type=F32) + bs[g]
            cols.append(sv)
        rows.append(jnp.concatenate(cols, axis=1))
    sgu = u * jnp.concatenate(rows, axis=0)
    sgu_ref[...] = (_rms(sgu) * gsgu_ref[...]).astype(BF16)


def _stage1(xp3, xs3, mod_p, mod_s, g1, win_b, lng, lnb, wm, wms, bs, bss, gsgu, batch, n_prompt_tiles):
    t = (xp3.shape[0] + xs3.shape[0]) * ROW_GROUP
    n_tiles = t // TOKEN_TILE
    last_p = n_prompt_tiles - 1
    x_blk = (TOKEN_TILE // ROW_GROUP, ROW_GROUP, D_MODEL)
    const2 = lambda i: (0, 0)
    const3 = lambda i: (0, 0, 0)
    tile2 = lambda i: (i, 0)
    cond = functools.partial(_cond_specs, tile_rows=TOKEN_TILE, tiles_per_batch=n_prompt_tiles // batch,
                             batch=batch, n_prompt_tiles=n_prompt_tiles, n_sample_tiles=n_tiles - n_prompt_tiles)
    sgu_w_spec = pl.BlockSpec((SGU_GROUPS, SGU_CHUNK, SGU_CHUNK), const3)
    return pl.pallas_call(
        functools.partial(_stage1_kernel, n_prompt_tiles=n_prompt_tiles),
        grid=(n_tiles,),
        in_specs=[pl.BlockSpec(x_blk, lambda i: (jnp.minimum(i, last_p), 0, 0)),
                  pl.BlockSpec(x_blk, const3)]
                 + cond(0) + cond(1)
                 + [pl.BlockSpec((1, D_MODEL), const2),
                    pl.BlockSpec((D_MODEL, IN_COLS), const2),
                    pl.BlockSpec((1, SGU_WIDTH), const2),
                    pl.BlockSpec((1, SGU_WIDTH), const2),
                    sgu_w_spec, sgu_w_spec, sgu_w_spec, sgu_w_spec,
                    pl.BlockSpec((1, SGU_WIDTH), const2)],
        out_specs=[pl.BlockSpec((TOKEN_TILE, ATTN_WIDTH), tile2),
                   pl.BlockSpec((TOKEN_TILE, KV_WIDTH), tile2),
                   pl.BlockSpec((TOKEN_TILE, KV_WIDTH), tile2),
                   pl.BlockSpec((TOKEN_TILE, SGU_WIDTH), tile2),
                   pl.BlockSpec((TOKEN_TILE, SGU_WIDTH), const2)],
        out_shape=[jax.ShapeDtypeStruct((t, ATTN_WIDTH), BF16),
                   jax.ShapeDtypeStruct((t, KV_WIDTH), F32),
                   jax.ShapeDtypeStruct((t, KV_WIDTH), F32),
                   jax.ShapeDtypeStruct((t, SGU_WIDTH), BF16),
                   jax.ShapeDtypeStruct((TOKEN_TILE, SGU_WIDTH), F32)],
        compiler_params=_params(1),
        name="stage1",
    )(xp3, xs3, mod_p, mod_s, mod_p, mod_s, g1, win_b, lng, lnb, wm, wms, bs, bss, gsgu)


def _alibi_slope(h):
    return float(np.float32(2.0) ** np.float32(-8.0 * (h + 1.0) / N_HEADS))


def _sink_attention(q, segments, sinks_ref, gout):
    outs = []
    for h in range(N_HEADS):
        kv0 = (h // GQA_GROUP) * HEAD_DIM
        qh = q[:, h * HEAD_DIM:(h + 1) * HEAD_DIM]
        sink = sinks_ref[h]
        slope = _alibi_slope(h)
        logits = []
        m = jnp.full((q.shape[0], 1), sink, F32)
        for (k, _, dist, valid) in segments:
            l = lax.dot_general(qh, k[:, kv0:kv0 + HEAD_DIM], (((1,), (1,)), ((), ())),
                                preferred_element_type=F32)
            l = l - slope * dist
            if valid is not None:
                l = jnp.where(valid, l, -jnp.inf)
            logits.append(l)
            m = jnp.maximum(m, jnp.max(l, axis=-1, keepdims=True))
        denom = jnp.exp(sink - m)
        acc = None
        for l, (_, v, _, _) in zip(logits, segments):
            e = jnp.exp(l - m)
            denom = denom + jnp.sum(e, axis=-1, keepdims=True)
            pv = jnp.dot(e.astype(BF16), v[:, kv0:kv0 + HEAD_DIM], preferred_element_type=F32)
            acc = pv if acc is None else acc + pv
        outs.append(acc / denom)
    o = jnp.concatenate(outs, axis=1)
    return (_rms(o) * gout).astype(BF16)


def _attn_prompt_kernel(sinks_ref, q_ref, kp_ref, kc_ref, vp_ref, vc_ref, g_ref, o_ref):
    i = pl.program_id(1)
    nq = q_ref.shape[0]
    back = kp_ref.shape[0]
    nk = back + nq
    q_t = q_ref[...].astype(F32).T.astype(BF16)
    k = jnp.concatenate([kp_ref[...], kc_ref[...]], axis=0).astype(BF16)
    v_t = jnp.concatenate([vp_ref[...], vc_ref[...]], axis=0).T.astype(BF16)
    c = lax.broadcasted_iota(I32, (nk, nq), 0)
    r = lax.broadcasted_iota(I32, (nk, nq), 1)
    dist = jnp.abs(r - c + back).astype(F32)
    kc = c // CHUNK
    qc = r // CHUNK + (back // CHUNK)
    valid = jnp.logical_and(jnp.logical_and(kc <= qc, kc >= qc - WINDOW // CHUNK),
                            jnp.logical_or(i > 0, kc >= back // CHUNK))
    outs = []
    for kvh in range(N_KV_HEADS):
        kv0 = kvh * HEAD_DIM
        heads = range(kvh * GQA_GROUP, (kvh + 1) * GQA_GROUP)
        qs = jnp.concatenate([q_t[h * HEAD_DIM:(h + 1) * HEAD_DIM, :] for h in heads], axis=1)
        l = jnp.dot(k[:, kv0:kv0 + HEAD_DIM], qs, preferred_element_type=F32)
        l = jnp.concatenate(
            [jnp.where(valid, l[:, g * nq:(g + 1) * nq] - _alibi_slope(h) * dist, -jnp.inf)
             for g, h in enumerate(heads)], axis=1)
        sink = jnp.concatenate([jnp.full((1, nq), sinks_ref[h], F32) for h in heads], axis=1)
        m = jnp.maximum(jnp.max(l, axis=0, keepdims=True), sink)
        e = jnp.exp(l - m)
        denom = jnp.sum(e, axis=0, keepdims=True) + jnp.exp(sink - m)
        pv = jnp.dot(v_t[kv0:kv0 + HEAD_DIM, :], e.astype(BF16), preferred_element_type=F32)
        o_t = pv / denom
        outs.extend(o_t[:, g * nq:(g + 1) * nq] for g in range(GQA_GROUP))
    o = jnp.concatenate(outs, axis=0).T
    o_ref[...] = (_rms(o) * g_ref[...]).astype(BF16)


def _attn_prompt(q, k, v, sinks, gattn, batch, seq):
    nt = seq // ATTN_TILE
    per_tile = ATTN_TILE // WINDOW
    cur = lambda b, i, s: (b * nt + i, 0)
    prev = lambda b, i, s: (jnp.maximum((b * nt + i) * per_tile - 1, 0), 0)
    kv_blk = (ATTN_TILE, KV_WIDTH)
    back_blk = (WINDOW, KV_WIDTH)
    return pl.pallas_call(
        _attn_prompt_kernel,
        grid_spec=pltpu.PrefetchScalarGridSpec(
            num_scalar_prefetch=1,
            grid=(batch, nt),
            in_specs=[pl.BlockSpec((ATTN_TILE, ATTN_WIDTH), cur),
                      pl.BlockSpec(back_blk, prev), pl.BlockSpec(kv_blk, cur),
                      pl.BlockSpec(back_blk, prev), pl.BlockSpec(kv_blk, cur),
                      pl.BlockSpec((1, ATTN_WIDTH), lambda b, i, s: (0, 0))],
            out_specs=pl.BlockSpec((ATTN_TILE, ATTN_WIDTH), cur)),
        out_shape=jax.ShapeDtypeStruct((batch * seq, ATTN_WIDTH), BF16),
        compiler_params=_params(2),
        name="attn_prompt",
    )(sinks, q, k, k, v, v, gattn)


def _attn_sample_kernel(sinks_ref, q_ref, ck_ref, cv_ref, k_ref, v_ref, g_ref, o_ref):
    nq = q_ref.shape[0]
    w = ck_ref.shape[1]
    r = lax.broadcasted_iota(I32, (nq, w), 0)
    c = lax.broadcasted_iota(I32, (nq, w), 1)
    dist_cache = jnp.abs(r - c + w).astype(F32)
    r2 = lax.broadcasted_iota(I32, (nq, nq), 0)
    c2 = lax.broadcasted_iota(I32, (nq, nq), 1)
    dist_new = jnp.abs(r2 - c2).astype(F32)
    segs = [(ck_ref[0].astype(BF16), cv_ref[0].astype(BF16), dist_cache, None),
            (k_ref[...].astype(BF16), v_ref[...].astype(BF16), dist_new, None)]
    o_ref[...] = _sink_attention(q_ref[...], segs, sinks_ref, g_ref[...])


def _attn_sample(q, k, v, cache_k, cache_v, sinks, gattn, dec_batch, n, row0):
    w = cache_k.shape[1]
    blk0 = row0 // n
    new = lambda b, s: (blk0 + b, 0)
    return pl.pallas_call(
        _attn_sample_kernel,
        grid_spec=pltpu.PrefetchScalarGridSpec(
            num_scalar_prefetch=1,
            grid=(dec_batch,),
            in_specs=[pl.BlockSpec((n, ATTN_WIDTH), new),
                      pl.BlockSpec((1, w, KV_WIDTH), lambda b, s: (b, 0, 0)),
                      pl.BlockSpec((1, w, KV_WIDTH), lambda b, s: (b, 0, 0)),
                      pl.BlockSpec((n, KV_WIDTH), new),
                      pl.BlockSpec((n, KV_WIDTH), new),
                      pl.BlockSpec((1, ATTN_WIDTH), lambda b, s: (0, 0))],
            out_specs=pl.BlockSpec((n, ATTN_WIDTH), lambda b, s: (b, 0))),
        out_shape=jax.ShapeDtypeStruct((dec_batch * n, ATTN_WIDTH), BF16),
        compiler_params=_params(1),
        name="attn_sample",
    )(sinks, q, cache_k, cache_v, k, v, gattn)


def _merge_kernel(ap_ref, as_ref, sgu_ref, xp_ref, xs_ref,
                  g1p_ref, g1s_ref, sh2p_ref, sh2s_ref, sc2p_ref, sc2s_ref, g2p_ref, g2s_ref,
                  woa_ref, wos_ref, g2_ref, wrh_ref, wrl_ref, wsgu_ref, wsd_ref,
                  xb_ref, h2_ref, sc_ref, *, n_prompt_tiles):
    is_sample = pl.program_id(0) >= n_prompt_tiles
    x = _pick_tile(is_sample, xp_ref, xs_ref)
    a = jnp.where(is_sample, as_ref[...], ap_ref[...])
    mix = (jnp.dot(a, woa_ref[...], preferred_element_type=F32)
           + jnp.dot(sgu_ref[...], wos_ref[...], preferred_element_type=F32))
    x1 = x + _cond_rows(is_sample, g1p_ref, g1s_ref) * mix
    h2 = (_rms(x1) * g2_ref[...] * (1.0 + _cond_rows(is_sample, sc2p_ref, sc2s_ref))
          + _cond_rows(is_sample, sh2p_ref, sh2s_ref))
    hh = h2.astype(BF16)
    _store_rows(h2_ref, hh)
    hl = (h2 - hh.astype(F32)).astype(BF16)
    logits = (jnp.dot(hh, wrh_ref[...], preferred_element_type=F32)
              + (jnp.dot(hl, wrh_ref[...], preferred_element_type=F32)
                 + jnp.dot(hh, wrl_ref[...], preferred_element_type=F32)))
    sc_ref[...] = jax.nn.sigmoid(logits)
    gu = jnp.dot(hh, wsgu_ref[...], preferred_element_type=F32)
    g = gu[:, :EXPERT_DIM]
    act = (g * jax.nn.sigmoid(g)) * gu[:, EXPERT_DIM:]
    shared = jnp.dot(act.astype(BF16), wsd_ref[...], preferred_element_type=F32)
    xb_ref[...] = x1 + _cond_rows(is_sample, g2p_ref, g2s_ref) * shared


def _merge(attn_p, attn_s, sgu_n, xp3, xs3, mod_p, mod_s, woa, wos, g2, wrh, wrl, wsgu, wsd, batch, n_prompt_tiles):
    t = sgu_n.shape[0]
    n_tiles = t // TOKEN_TILE
    last_p = n_prompt_tiles - 1
    x_blk = (TOKEN_TILE // ROW_GROUP, ROW_GROUP, D_MODEL)
    const2 = lambda i: (0, 0)
    const3 = lambda i: (0, 0, 0)
    tile2 = lambda i: (i, 0)
    cond = functools.partial(_cond_specs, tile_rows=TOKEN_TILE, tiles_per_batch=n_prompt_tiles // batch,
                             batch=batch, n_prompt_tiles=n_prompt_tiles, n_sample_tiles=n_tiles - n_prompt_tiles)
    return pl.pallas_call(
        functools.partial(_merge_kernel, n_prompt_tiles=n_prompt_tiles),
        grid=(n_tiles,),
        in_specs=[pl.BlockSpec((TOKEN_TILE, ATTN_WIDTH), lambda i: (jnp.minimum(i, last_p), 0)),
                  pl.BlockSpec((TOKEN_TILE, ATTN_WIDTH), const2),
                  pl.BlockSpec((TOKEN_TILE, SGU_WIDTH), tile2),
                  pl.BlockSpec(x_blk, lambda i: (jnp.minimum(i, last_p), 0, 0)),
                  pl.BlockSpec(x_blk, const3)]
                 + cond(2) + cond(3) + cond(4) + cond(5)
                 + [pl.BlockSpec((ATTN_WIDTH, D_MODEL), const2),
                    pl.BlockSpec((SGU_WIDTH, D_MODEL), const2),
                    pl.BlockSpec((1, D_MODEL), const2),
                    pl.BlockSpec((D_MODEL, N_EXPERTS), const2),
                    pl.BlockSpec((D_MODEL, N_EXPERTS), const2),
                    pl.BlockSpec((D_MODEL, 2 * EXPERT_DIM), const2),
                    pl.BlockSpec((EXPERT_DIM, D_MODEL), const2)],
        out_specs=[pl.BlockSpec((TOKEN_TILE, D_MODEL), tile2),
                   pl.BlockSpec((TOKEN_TILE * ROW_TILES, LANES), tile2),
                   pl.BlockSpec((TOKEN_TILE, N_EXPERTS), tile2)],
        out_shape=[jax.ShapeDtypeStruct((t, D_MODEL), F32),
                   jax.ShapeDtypeStruct((t * ROW_TILES, LANES), ROW_DTYPE),
                   jax.ShapeDtypeStruct((t, N_EXPERTS), F32)],
        compiler_params=_params(1),
        name="merge",
    )(attn_p, attn_s, sgu_n, xp3, xs3, *([mod_p, mod_s] * 4), woa, wos, g2, wrh, wrl, wsgu, wsd)


def _first_argmax(x, iota, size, axis):
    m = jnp.max(x, axis=axis, keepdims=True)
    idx = jnp.min(jnp.where(x == m, iota, size), axis=axis, keepdims=True)
    return m, idx


def _route_kernel(sc_ref, bias_ref, tri_ref, eidx_ref, wts_ref, rank_ref, cnt_ref, base_ref):
    tt = sc_ref.shape[0]

    @pl.when(pl.program_id(0) == 0)
    def _():
        base_ref[...] = jnp.zeros_like(base_ref)

    s_t = sc_ref[...].T
    sel = s_t + bias_ref[...]
    sel3 = sel.reshape(N_EXPERT_GROUPS, GROUP_SIZE, tt)
    io3 = lax.broadcasted_iota(I32, sel3.shape, 1)
    m1, i1 = _first_argmax(sel3, io3, GROUP_SIZE, 1)
    m2 = jnp.max(jnp.where(io3 == i1, -jnp.inf, sel3), axis=1, keepdims=True)
    gs = (m1 + m2).reshape(N_EXPERT_GROUPS, tt)
    io8 = lax.broadcasted_iota(I32, gs.shape, 0)
    gmask = jnp.zeros(gs.shape, jnp.bool_)
    for _ in range(TOPK_GROUPS):
        _, gi = _first_argmax(gs, io8, N_EXPERT_GROUPS, 0)
        hit = io8 == gi
        gmask = jnp.logical_or(gmask, hit)
        gs = jnp.where(hit, -jnp.inf, gs)
    emask = jnp.broadcast_to(gmask.reshape(N_EXPERT_GROUPS, 1, tt), sel3.shape).reshape(N_EXPERTS, tt)
    cand = jnp.where(emask, sel, -jnp.inf)
    io = lax.broadcasted_iota(I32, cand.shape, 0)
    chosen = jnp.zeros(cand.shape, jnp.bool_)
    eidx, wts = [], []
    for _ in range(TOP_K):
        _, ei = _first_argmax(cand, io, N_EXPERTS, 0)
        hit = io == ei
        eidx.append(ei)
        wts.append(jnp.sum(jnp.where(hit, s_t, 0.0), axis=0, keepdims=True))
        chosen = jnp.logical_or(chosen, hit)
        cand = jnp.where(hit, -jnp.inf, cand)
    w = jnp.concatenate(wts, axis=0)
    wts_ref[...] = w / jnp.sum(w, axis=0, keepdims=True) * ROUTED_SCALE
    eidx_ref[...] = jnp.concatenate(eidx, axis=0)
    cf = jnp.where(chosen, 1.0, 0.0)
    ahead = jnp.dot(cf.astype(BF16), tri_ref[...], preferred_element_type=F32) + base_ref[...]
    ranks = [jnp.sum(jnp.where(io == ei, ahead, 0.0), axis=0, keepdims=True) for ei in eidx]
    rank_ref[...] = jnp.concatenate(ranks, axis=0).astype(I32)
    total = base_ref[...] + jnp.sum(cf, axis=1, keepdims=True)
    base_ref[...] = total
    cnt_ref[...] = total.astype(I32)


def _route(scores, bias_col, tri):
    t = scores.shape[0]
    n_tiles = t // TOKEN_TILE
    col = lambda i: (0, i)
    return pl.pallas_call(
        _route_kernel,
        grid=(n_tiles,),
        in_specs=[pl.BlockSpec((TOKEN_TILE, N_EXPERTS), lambda i: (i, 0)),
                  pl.BlockSpec((N_EXPERTS, 1), lambda i: (0, 0)),
                  pl.BlockSpec((TOKEN_TILE, TOKEN_TILE), lambda i: (0, 0))],
        out_specs=[pl.BlockSpec((TOP_K, TOKEN_TILE), col),
                   pl.BlockSpec((TOP_K, TOKEN_TILE), col),
                   pl.BlockSpec((TOP_K, TOKEN_TILE), col),
                   pl.BlockSpec((N_EXPERTS, 1), lambda i: (0, 0))],
        out_shape=[jax.ShapeDtypeStruct((TOP_K, t), I32),
                   jax.ShapeDtypeStruct((TOP_K, t), F32),
                   jax.ShapeDtypeStruct((TOP_K, t), I32),
                   jax.ShapeDtypeStruct((N_EXPERTS, 1), I32)],
        scratch_shapes=[pltpu.VMEM((N_EXPERTS, 1), F32)],
        compiler_params=_params(1),
        name="route",
    )(scores, bias_col, tri)


def _pos_kernel(eidx_ref, rank_ref, start_ref, pos_ref):
    e = eidx_ref[...]
    tt = e.shape[1]
    io = lax.broadcasted_iota(I32, (N_EXPERTS, tt), 0)
    st = start_ref[...]
    rows = [jnp.sum(jnp.where(io == e[k:k + 1, :], st, 0), axis=0, keepdims=True) for k in range(TOP_K)]
    pos_ref[...] = jnp.concatenate(rows, axis=0) + rank_ref[...]


def _positions(eidx, rank, start_col):
    t = eidx.shape[1]
    col = lambda i: (0, i)
    blk = pl.BlockSpec((TOP_K, TOKEN_TILE), col)
    return pl.pallas_call(
        _pos_kernel,
        grid=(t // TOKEN_TILE,),
        in_specs=[blk, blk, pl.BlockSpec((N_EXPERTS, 1), lambda i: (0, 0))],
        out_specs=blk,
        out_shape=jax.ShapeDtypeStruct((TOP_K, t), I32),
        compiler_params=_params(1),
        name="positions",
    )(eidx, rank, start_col)


def _dispatch_kernel(pos_ref, h_ref, xs_ref, zero_ref, sem, zsem, *, n_rows):
    tt = h_ref.shape[0] // ROW_TILES
    first = pl.program_id(0) == 0

    @pl.when(first)
    def _():
        zero_ref[...] = jnp.zeros_like(zero_ref)
        _rows_copy(zero_ref, 0, xs_ref, n_rows, EXPERT_BLOCK, zsem).start()

    def body(t, carry):
        for k in range(TOP_K):
            _rows_copy(h_ref, t, xs_ref, pos_ref[k, t], 1, sem).start(priority=k % 2)
        return carry

    lax.fori_loop(0, tt, body, 0)
    for k in range(TOP_K):
        _rows_copy(h_ref, 0, xs_ref, 0, tt, sem).wait()

    @pl.when(first)
    def _():
        _rows_copy(zero_ref, 0, xs_ref, n_rows, EXPERT_BLOCK, zsem).wait()


def _dispatch(pos, h2_tiles, n_rows):
    t = h2_tiles.shape[0] // ROW_TILES
    return pl.pallas_call(
        functools.partial(_dispatch_kernel, n_rows=n_rows),
        grid=(t // TOKEN_TILE,),
        in_specs=[pl.BlockSpec((TOP_K, TOKEN_TILE), lambda i: (0, i), memory_space=pltpu.SMEM),
                  pl.BlockSpec((TOKEN_TILE * ROW_TILES, LANES), lambda i: (i, 0))],
        out_specs=pl.BlockSpec(memory_space=pl.ANY),
        out_shape=jax.ShapeDtypeStruct(((n_rows + EXPERT_BLOCK) * ROW_TILES, LANES), ROW_DTYPE),
        scratch_shapes=[pltpu.VMEM((EXPERT_BLOCK * ROW_TILES, LANES), ROW_DTYPE),
                        pltpu.SemaphoreType.DMA, pltpu.SemaphoreType.DMA],
        compiler_params=_params(1),
        name="dispatch",
    )(pos, h2_tiles)


def _piece_sizes():
    sizes, s = [], EXPERT_BLOCK // 2
    while s >= 1:
        sizes.append(s)
        s //= 2
    return sizes


def _out_pieces(nvalid, fn):
    @pl.when(nvalid == EXPERT_BLOCK)
    def _():
        fn(0, EXPERT_BLOCK)

    @pl.when(nvalid < EXPERT_BLOCK)
    def _():
        for s in _piece_sizes():
            @pl.when((nvalid & s) != 0)
            def _(s=s):
                fn(nvalid & ~(2 * s - 1), s)


def _expert_kernel(count_ref, blk0_ref, row_ref, xs_ref, wg_ref, wu_ref, wd_ref, ys_ref,
                   xbuf, ybuf, wgu_s, wd_s, state, xsem, ysem, *, n_rows):
    e = pl.program_id(0)
    cnt = count_ref[e]
    g0 = blk0_ref[e]
    nblk = (cnt + EXPERT_BLOCK - 1) // EXPERT_BLOCK

    def x_copy(row, slot):
        return _rows_copy(xs_ref, row, xbuf.at[slot], 0, EXPERT_BLOCK, xsem.at[slot])

    def x_fetch(g, slot):
        row = row_ref[g]

        @pl.when(row < n_rows)
        def _():
            x_copy(row, slot).start()

    def y_wait(nvalid, slot):
        def wait(off, size):
            _rows_copy(ybuf.at[slot], 0, ys_ref, 0, size, ysem.at[slot]).wait()
        _out_pieces(nvalid, wait)

    @pl.when(e == 0)
    def _():
        state[0] = 0
        state[1] = 0
        for g in range(X_SLOTS - 1):
            x_fetch(g, g)

    @pl.when(cnt > 0)
    def _():
        wgu_s[:, :EXPERT_DIM] = wg_ref[0].astype(BF16)
        wgu_s[:, EXPERT_DIM:] = wu_ref[0].astype(BF16)
        wd_s[...] = wd_ref[0].astype(BF16)

    def compute(rows, xslot, yslot, row0, nvalid):
        x = _load_rows(xbuf.at[xslot], rows).astype(BF16)
        gu = jnp.dot(x, wgu_s[...], preferred_element_type=F32)
        g = gu[:, :EXPERT_DIM]
        act = (g * jax.nn.sigmoid(g)) * gu[:, EXPERT_DIM:]
        y = jnp.dot(act.astype(BF16), wd_s[...], preferred_element_type=F32)
        y_wait(state[0], 1 - yslot)
        _store_rows(ybuf.at[yslot], y)

        def put(off, size):
            _rows_copy(ybuf.at[yslot], off, ys_ref, row0 + off, size, ysem.at[yslot]).start()
        _out_pieces(nvalid, put)

    def block(j, carry):
        g = g0 + j
        xslot = state[1]
        yslot = g & 1
        row0 = row_ref[g]
        nvalid = jnp.minimum(cnt - j * EXPERT_BLOCK, EXPERT_BLOCK)
        x_copy(row0, xslot).wait()
        x_fetch(g + X_SLOTS - 1, jnp.where(xslot == 0, X_SLOTS - 1, xslot - 1))
        lo = 0
        for rows in BLOCK_SIZES:
            @pl.when(jnp.logical_and(nvalid > lo, nvalid <= rows))
            def _(rows=rows):
                compute(rows, xslot, yslot, row0, nvalid)
            lo = rows
        state[0] = nvalid
        state[1] = jnp.where(xslot == X_SLOTS - 1, 0, xslot + 1)
        return carry

    lax.fori_loop(0, nblk, block, 0)

    @pl.when(e == pl.num_programs(0) - 1)
    def _():
        y_wait(state[0], (g0 + nblk - 1) & 1)


def _experts(count, blk0, blk_row, xs, w_gate, w_up, w_down, n_rows):
    w_map = lambda e, c, b, r: (e, 0, 0)
    blk_rows = EXPERT_BLOCK * ROW_TILES
    return pl.pallas_call(
        functools.partial(_expert_kernel, n_rows=n_rows),
        grid_spec=pltpu.PrefetchScalarGridSpec(
            num_scalar_prefetch=3,
            grid=(N_EXPERTS,),
            in_specs=[pl.BlockSpec(memory_space=pl.ANY),
                      pl.BlockSpec((1, D_MODEL, EXPERT_DIM), w_map),
                      pl.BlockSpec((1, D_MODEL, EXPERT_DIM), w_map),
                      pl.BlockSpec((1, EXPERT_DIM, D_MODEL), w_map)],
            out_specs=pl.BlockSpec(memory_space=pl.ANY),
            scratch_shapes=[pltpu.VMEM((X_SLOTS, blk_rows, LANES), ROW_DTYPE),
                            pltpu.VMEM((2, blk_rows, LANES), ROW_DTYPE),
                            pltpu.VMEM((D_MODEL, 2 * EXPERT_DIM), BF16),
                            pltpu.VMEM((EXPERT_DIM, D_MODEL), BF16),
                            pltpu.SMEM((2,), I32),
                            pltpu.SemaphoreType.DMA((X_SLOTS,)),
                            pltpu.SemaphoreType.DMA((2,))]),
        out_shape=jax.ShapeDtypeStruct((n_rows * ROW_TILES, LANES), ROW_DTYPE),
        compiler_params=_params(1),
        name="experts",
    )(count, blk0, blk_row, xs, w_gate, w_up, w_down)


def _combine_kernel(pos_ref, posn_ref, ys_ref, xb_ref, wts_ref, g2p_ref, g2s_ref, gf_ref, yp_ref, ysm_ref,
                    buf, sem, *, n_prompt_tiles):
    i = pl.program_id(0)
    n_tiles = pl.num_programs(0)
    tt = xb_ref.shape[0]
    slot = i & 1

    def gather(p_ref, s):
        def body(t, carry):
            for k in range(TOP_K):
                _rows_copy(ys_ref, p_ref[k, t], buf.at[s, k], t, 1, sem.at[s]).start(priority=k % 2)
            return carry
        lax.fori_loop(0, tt, body, 0)

    @pl.when(i == 0)
    def _():
        gather(pos_ref, 0)

    @pl.when(i + 1 < n_tiles)
    def _():
        gather(posn_ref, 1 - slot)

    for k in range(TOP_K):
        _rows_copy(ys_ref, 0, buf.at[slot, k], 0, tt, sem.at[slot]).wait()
    w = wts_ref[...]
    routed = w[:, 0:1] * _load_rows(buf.at[slot, 0], tt)
    for k in range(1, TOP_K):
        routed = routed + w[:, k:k + 1] * _load_rows(buf.at[slot, k], tt)
    is_sample = i >= n_prompt_tiles
    x = xb_ref[...] + _cond_rows(is_sample, g2p_ref, g2s_ref) * routed
    y = _rms(x) * gf_ref[...]

    @pl.when(i < n_prompt_tiles)
    def _():
        yp_ref[...] = y

    @pl.when(i >= n_prompt_tiles)
    def _():
        ysm_ref[...] = y


def _combine(pos, ys, xb, wts_t, mod_p, mod_s, gf, batch, n_prompt_rows):
    t = xb.shape[0]
    tt = COMBINE_TILE
    n_tiles = t // tt
    n_p = n_prompt_rows // tt
    pos_spec = lambda f: pl.BlockSpec((TOP_K, tt), f, memory_space=pltpu.SMEM)
    return pl.pallas_call(
        functools.partial(_combine_kernel, n_prompt_tiles=n_p),
        grid=(n_tiles,),
        in_specs=[pos_spec(lambda i: (0, i)),
                  pos_spec(lambda i: (0, jnp.minimum(i + 1, n_tiles - 1))),
                  pl.BlockSpec(memory_space=pl.ANY),
                  pl.BlockSpec((tt, D_MODEL), lambda i: (i, 0)),
                  pl.BlockSpec((tt, TOP_K), lambda i: (i, 0))]
                 + _cond_specs(5, tt, n_p // batch, batch, n_p, n_tiles - n_p)
                 + [pl.BlockSpec((1, D_MODEL), lambda i: (0, 0))],
        out_specs=[pl.BlockSpec((tt, D_MODEL), lambda i: (jnp.minimum(i, n_p - 1), 0)),
                   pl.BlockSpec((tt, D_MODEL), lambda i: (jnp.maximum(i - n_p, 0), 0))],
        out_shape=[jax.ShapeDtypeStruct((n_prompt_rows, D_MODEL), F32),
                   jax.ShapeDtypeStruct((t - n_prompt_rows, D_MODEL), F32)],
        scratch_shapes=[pltpu.VMEM((2, TOP_K, tt * ROW_TILES, LANES), ROW_DTYPE), pltpu.SemaphoreType.DMA((2,))],
        compiler_params=_params(1),
        name="combine",
    )(pos, pos, ys, xb, wts_t, mod_p, mod_s, gf)


def _sgu_mask():
    pos = np.arange(SGU_CHUNK) // CHUNK
    return jnp.asarray(pos[None, :] <= pos[:, None])


def _layer(xp3, xs3, cache_k, cache_v, mod, final_g, batch, seq, dec_batch, n, p):
    tp = batch * seq
    t = tp + dec_batch * n
    n_prompt_tiles = tp // TOKEN_TILE
    mod_p = mod[:batch].reshape(batch, 1, N_MOD * D_MODEL)
    mod_s = mod[batch:].reshape(dec_batch, 1, N_MOD * D_MODEL)

    wm = jnp.where(_sgu_mask(), p["sgu_w"], 0.0)
    reps = SGU_CHUNK // n
    eye = jnp.eye(reps, dtype=F32)
    wms = jnp.stack([jnp.kron(eye, wm[g, :n, :n]) for g in range(SGU_GROUPS)])
    bs = jnp.broadcast_to(p["sgu_b"][:, :, None], (SGU_GROUPS, SGU_CHUNK, SGU_CHUNK))
    bss = jnp.broadcast_to(jnp.tile(p["sgu_b"][:, :n], (1, reps))[:, :, None], (SGU_GROUPS, SGU_CHUNK, SGU_CHUNK))

    q, k, v, sgu_n, vn_s = _stage1(
        xp3, xs3, mod_p, mod_s, p["norm1_g"].reshape(1, -1), p["w_in"].astype(BF16),
        p["sgu_ln_g"].reshape(1, -1), p["sgu_ln_b"].reshape(1, -1),
        wm.astype(BF16), wms.astype(BF16), bs, bss, p["sgu_out_g"].reshape(1, -1), batch, n_prompt_tiles)

    gattn = p["attn_out_g"].reshape(1, -1)
    attn_p = _attn_prompt(q, k, v, p["attn_sinks"], gattn, batch, seq)
    w = cache_k.shape[1]
    attn_s = _attn_sample(q, k, v, cache_k.reshape(dec_batch, w, KV_WIDTH), cache_v.reshape(dec_batch, w, KV_WIDTH),
                          p["attn_sinks"], gattn, dec_batch, n, tp)

    wo = p["w_out"].astype(BF16)
    wsgu = jnp.concatenate([p["ws_gate"], p["ws_up"]], axis=1).astype(BF16)
    wrh = p["w_router"].astype(BF16)
    wrl = (p["w_router"] - wrh.astype(F32)).astype(BF16)
    xb, h2_tiles, scores = _merge(attn_p, attn_s, sgu_n, xp3, xs3, mod_p, mod_s,
                                  wo[:ATTN_WIDTH], wo[ATTN_WIDTH:], p["norm2_g"].reshape(1, -1), wrh, wrl,
                                  wsgu, p["ws_down"].astype(BF16), batch, n_prompt_tiles)

    ti = np.arange(TOKEN_TILE)
    tri = jnp.asarray(ti[:, None] < ti[None, :], dtype=BF16)
    eidx, wts, rank, counts = _route(scores, p["router_bias"].reshape(N_EXPERTS, 1), tri)

    counts = counts[:, 0]
    ends = jnp.cumsum(counts)
    start = (ends - counts).astype(I32)
    nblk = (counts + EXPERT_BLOCK - 1) // EXPERT_BLOCK
    blk_end = jnp.cumsum(nblk)
    blk0 = (blk_end - nblk).astype(I32)
    pos = _positions(eidx, rank, start.reshape(N_EXPERTS, 1))

    n_rows = t * TOP_K
    g = jnp.arange(n_rows // EXPERT_BLOCK + N_EXPERTS + X_SLOTS, dtype=I32)[:, None]
    mine = jnp.logical_and(g >= blk0[None, :], g < blk_end[None, :])
    blk_row = jnp.sum(jnp.where(mine, start[None, :] + (g - blk0[None, :]) * EXPERT_BLOCK, 0), axis=1)
    blk_row = jnp.where(g[:, 0] < blk_end[-1], blk_row, n_rows).astype(I32)

    xs_sorted = _dispatch(pos, h2_tiles, n_rows)
    ys_sorted = _experts(counts.astype(I32), blk0, blk_row, xs_sorted, p["w_gate"], p["w_up"], p["w_down"], n_rows)
    y_p, y_s = _combine(pos, ys_sorted, xb, wts.T, mod_p, mod_s, final_g.reshape(1, -1), batch, tp)
    return y_p, y_s, k, v, vn_s


def kernel(x_prompt, x_sample, cache_k, cache_v, c_prompt, c_sample, norm1_g, w_ada, b_ada, w_in, sgu_ln_g, sgu_ln_b, sgu_w, sgu_b, attn_sinks, attn_out_g, sgu_out_g, w_out, norm2_g, w_router, router_bias, w_gate, w_up, w_down, ws_gate, ws_up, ws_down, final_g):
    batch, seq, d = x_prompt.shape
    dec_batch, n, _ = x_sample.shape
    depth = norm1_g.shape[0]
    assert depth == 1 and d == D_MODEL
    assert seq % TOKEN_TILE == 0 and dec_batch * n == TOKEN_TILE and n == ROW_GROUP
    assert seq % ATTN_TILE == 0 and seq >= WINDOW and cache_k.shape[2] == WINDOW
    tp = batch * seq

    xp3 = x_prompt.reshape(tp // ROW_GROUP, ROW_GROUP, d)
    xs3 = x_sample.reshape(dec_batch * n // ROW_GROUP, ROW_GROUP, d)
    c_all = jnp.concatenate([c_prompt, c_sample], axis=0)
    l = 0
    mod = _adaln(c_all, w_ada[l], b_ada[l])
    p = dict(norm1_g=norm1_g[l], w_in=w_in[l], sgu_ln_g=sgu_ln_g[l], sgu_ln_b=sgu_ln_b[l], sgu_w=sgu_w[l],
             sgu_b=sgu_b[l], attn_sinks=attn_sinks[l], attn_out_g=attn_out_g[l], sgu_out_g=sgu_out_g[l],
             w_out=w_out[l], norm2_g=norm2_g[l], w_router=w_router[l], router_bias=router_bias[l],
             w_gate=w_gate[l], w_up=w_up[l], w_down=w_down[l], ws_gate=ws_gate[l], ws_up=ws_up[l],
             ws_down=ws_down[l])
    y_p, y_s, k, v, vn_s = _layer(xp3, xs3, cache_k[l], cache_v[l], mod, final_g, batch, seq, dec_batch, n, p)

    keep = min(WINDOW, seq)
    kv_shape = (batch, keep, N_KV_HEADS, HEAD_DIM)
    k_p = jnp.stack([k[(b + 1) * seq - keep:(b + 1) * seq] for b in range(batch)]).reshape(kv_shape)
    v_p = jnp.stack([v[(b + 1) * seq - keep:(b + 1) * seq] for b in range(batch)]).reshape(kv_shape)
    k_s = k[tp:].reshape(dec_batch, n, N_KV_HEADS, HEAD_DIM)
    v_s = v[tp:].reshape(dec_batch, n, N_KV_HEADS, HEAD_DIM)
    return (y_p.reshape(batch, seq, d), y_s.reshape(dec_batch, n, d),
            k_p[None], v_p[None], k_s[None], v_s[None], vn_s.reshape(dec_batch, n, SGU_WIDTH)[None])
```

```python
import functools

import numpy as np
import jax
import jax.numpy as jnp
from jax import lax
from jax.experimental import pallas as pl
from jax.experimental.pallas import tpu as pltpu

F32 = jnp.float32
BF16 = jnp.bfloat16
I32 = jnp.int32

D_MODEL = 1024
CHUNK = 64
HEAD_DIM = 64
ATTN_WIDTH = 512
N_HEADS = 8
N_KV_HEADS = 2
GQA_GROUP = 4
KV_WIDTH = 128
WINDOW = 128
SGU_CHUNK = 128
SGU_WIDTH = 512
SGU_GROUPS = 4
IN_COLS = ATTN_WIDTH + 2 * KV_WIDTH + 2 * SGU_WIDTH
N_MOD = 6
N_EXPERTS = 256
TOP_K = 8
N_EXPERT_GROUPS = 8
GROUP_SIZE = N_EXPERTS // N_EXPERT_GROUPS
TOPK_GROUPS = 4
EXPERT_DIM = 256
ROUTED_SCALE = 2.5
EPS = 1e-6

LANES = 128
ROW_TILES = D_MODEL // LANES
ROW_DTYPE = F32
ROW_GROUP = 16
TOKEN_TILE = 256
ATTN_TILE = 256
EXPERT_BLOCK = 640
BLOCK_SIZES = (128, 256, 384, 512, 576, EXPERT_BLOCK)
X_SLOTS = 3
COMBINE_TILE = 128
VMEM_LIMIT_BYTES = 56 * 1024 * 1024


def _params(n_axes=1):
    return pltpu.CompilerParams(dimension_semantics=("arbitrary",) * n_axes,
                                vmem_limit_bytes=VMEM_LIMIT_BYTES)


def _rms(x):
    return x * lax.rsqrt(jnp.mean(x * x, axis=-1, keepdims=True) + EPS)


def _cond_rows(is_sample, mp_ref, ms_ref):
    ms = ms_ref[...]
    g, _, d = ms.shape
    s = jnp.broadcast_to(ms, (g, ROW_GROUP, d)).reshape(g * ROW_GROUP, d)
    return jnp.where(is_sample, s, mp_ref[0])


def _cond_specs(piece, tile_rows, tiles_per_batch, batch, n_prompt_tiles, n_sample_tiles):
    g = tile_rows // ROW_GROUP
    mp = pl.BlockSpec((1, 1, D_MODEL),
                      lambda i: (jnp.minimum(i // tiles_per_batch, batch - 1), 0, piece))
    ms = pl.BlockSpec((g, 1, D_MODEL),
                      lambda i: (jnp.clip(i - n_prompt_tiles, 0, n_sample_tiles - 1), 0, piece))
    return [mp, ms]


def _pick_tile(is_sample, prompt_ref, sample_ref):
    v = jnp.where(is_sample, sample_ref[...], prompt_ref[...])
    return v.reshape(TOKEN_TILE, v.shape[-1])


def _load_rows(ref, n_rows):
    return jnp.concatenate([ref[pl.ds(j, n_rows, stride=ROW_TILES), :] for j in range(ROW_TILES)], axis=1)


def _store_rows(ref, x):
    n_rows = x.shape[0]
    for j in range(ROW_TILES):
        ref[pl.ds(j, n_rows, stride=ROW_TILES), :] = x[:, j * LANES:(j + 1) * LANES].astype(ROW_DTYPE)


def _tile_offset(row):
    off = row * ROW_TILES
    return off if isinstance(off, int) else pl.multiple_of(off, ROW_TILES)


def _rows_copy(src_ref, src_row, dst_ref, dst_row, n_rows, sem):
    s = _tile_offset(src_row)
    d = _tile_offset(dst_row)
    return pltpu.make_async_copy(src_ref.at[pl.ds(s, n_rows * ROW_TILES)],
                                 dst_ref.at[pl.ds(d, n_rows * ROW_TILES)], sem)


def _adaln_kernel(c_ref, w_ref, b_ref, o_ref):
    c = c_ref[...]
    s = c * jax.nn.sigmoid(c)
    o_ref[...] = jnp.dot(s, w_ref[...], precision=lax.Precision.HIGHEST,
                         preferred_element_type=F32) + b_ref[...]


def _adaln(c, w_ada, b_ada):
    n, d = c.shape
    cols = w_ada.shape[1]
    tn = 1536
    return pl.pallas_call(
        _adaln_kernel,
        grid=(cols // tn,),
        in_specs=[pl.BlockSpec((n, d), lambda j: (0, 0)),
                  pl.BlockSpec((d, tn), lambda j: (0, j)),
                  pl.BlockSpec((1, tn), lambda j: (0, j))],
        out_specs=pl.BlockSpec((n, tn), lambda j: (0, j)),
        out_shape=jax.ShapeDtypeStruct((n, cols), F32),
        compiler_params=_params(1),
        name="adaln",
    )(c, w_ada, b_ada.reshape(1, cols))


def _stage1_kernel(xp_ref, xs_ref, shp_ref, shs_ref, scp_ref, scs_ref, g1_ref, win_ref, lng_ref, lnb_ref,
                   wm_ref, wms_ref, bs_ref, bss_ref, gsgu_ref,
                   q_ref, k_ref, v_ref, sgu_ref, vn_ref, *, n_prompt_tiles):
    is_sample = pl.program_id(0) >= n_prompt_tiles
    x = _pick_tile(is_sample, xp_ref, xs_ref)
    h = (_rms(x) * g1_ref[...] * (1.0 + _cond_rows(is_sample, scp_ref, scs_ref))
         + _cond_rows(is_sample, shp_ref, shs_ref))
    proj = jnp.dot(h.astype(BF16), win_ref[...], preferred_element_type=F32)
    q_ref[...] = (proj[:, :ATTN_WIDTH] * (HEAD_DIM ** -0.5)).astype(BF16)
    k_ref[...] = proj[:, ATTN_WIDTH:ATTN_WIDTH + KV_WIDTH]
    v_ref[...] = proj[:, ATTN_WIDTH + KV_WIDTH:ATTN_WIDTH + 2 * KV_WIDTH]
    c0 = ATTN_WIDTH + 2 * KV_WIDTH
    u = jax.nn.gelu(proj[:, c0:c0 + SGU_WIDTH])
    vg = jax.nn.gelu(proj[:, c0 + SGU_WIDTH:])
    mu = jnp.mean(vg, axis=-1, keepdims=True)
    xc = vg - mu
    var = jnp.mean(xc * xc, axis=-1, keepdims=True)
    vn = xc * lax.rsqrt(var + EPS) * lng_ref[...] + lnb_ref[...]

    @pl.when(is_sample)
    def _():
        vn_ref[...] = vn

    vnb = vn.astype(BF16)
    wm = jnp.where(is_sample, wms_ref[...], wm_ref[...])
    bs = jnp.where(is_sample, bss_ref[...], bs_ref[...])
    rows = []
    for c in range(TOKEN_TILE // SGU_CHUNK):
        r0 = c * SGU_CHUNK
        cols = []
        for g in range(SGU_GROUPS):
            l0 = g * SGU_CHUNK
            sv = jnp.dot(wm[g], vnb[r0:r0 + SGU_CHUNK, l0:l0 + SGU_CHUNK],
                         preferred_element_type=F32) + bs[g]
            cols.append(sv)
        rows.append(jnp.concatenate(cols, axis=1))
    sgu = u * jnp.concatenate(rows, axis=0)
    sgu_ref[...] = (_rms(sgu) * gsgu_ref[...]).astype(BF16)


def _stage1(xp3, xs3, mod_p, mod_s, g1, win_b, lng, lnb, wm, wms, bs, bss, gsgu, batch, n_prompt_tiles):
    t = (xp3.shape[0] + xs3.shape[0]) * ROW_GROUP
    n_tiles = t // TOKEN_TILE
    last_p = n_prompt_tiles - 1
    x_blk = (TOKEN_TILE // ROW_GROUP, ROW_GROUP, D_MODEL)
    const2 = lambda i: (0, 0)
    const3 = lambda i: (0, 0, 0)
    tile2 = lambda i: (i, 0)
    cond = functools.partial(_cond_specs, tile_rows=TOKEN_TILE, tiles_per_batch=n_prompt_tiles // batch,
                             batch=batch, n_prompt_tiles=n_prompt_tiles, n_sample_tiles=n_tiles - n_prompt_tiles)
    sgu_w_spec = pl.BlockSpec((SGU_GROUPS, SGU_CHUNK, SGU_CHUNK), const3)
    return pl.pallas_call(
        functools.partial(_stage1_kernel, n_prompt_tiles=n_prompt_tiles),
        grid=(n_tiles,),
        in_specs=[pl.BlockSpec(x_blk, lambda i: (jnp.minimum(i, last_p), 0, 0)),
                  pl.BlockSpec(x_blk, const3)]
                 + cond(0) + cond(1)
                 + [pl.BlockSpec((1, D_MODEL), const2),
                    pl.BlockSpec((D_MODEL, IN_COLS), const2),
                    pl.BlockSpec((1, SGU_WIDTH), const2),
                    pl.BlockSpec((1, SGU_WIDTH), const2),
                    sgu_w_spec, sgu_w_spec, sgu_w_spec, sgu_w_spec,
                    pl.BlockSpec((1, SGU_WIDTH), const2)],
        out_specs=[pl.BlockSpec((TOKEN_TILE, ATTN_WIDTH), tile2),
                   pl.BlockSpec((TOKEN_TILE, KV_WIDTH), tile2),
                   pl.BlockSpec((TOKEN_TILE, KV_WIDTH), tile2),
                   pl.BlockSpec((TOKEN_TILE, SGU_WIDTH), tile2),
                   pl.BlockSpec((TOKEN_TILE, SGU_WIDTH), const2)],
        out_shape=[jax.ShapeDtypeStruct((t, ATTN_WIDTH), BF16),
                   jax.ShapeDtypeStruct((t, KV_WIDTH), F32),
                   jax.ShapeDtypeStruct((t, KV_WIDTH), F32),
                   jax.ShapeDtypeStruct((t, SGU_WIDTH), BF16),
                   jax.ShapeDtypeStruct((TOKEN_TILE, SGU_WIDTH), F32)],
        compiler_params=_params(1),
        name="stage1",
    )(xp3, xs3, mod_p, mod_s, mod_p, mod_s, g1, win_b, lng, lnb, wm, wms, bs, bss, gsgu)


def _alibi_slope(h):
    return float(np.float32(2.0) ** np.float32(-8.0 * (h + 1.0) / N_HEADS))


def _sink_attention(q, segments, sinks_ref, gout):
    outs = []
    for h in range(N_HEADS):
        kv0 = (h // GQA_GROUP) * HEAD_DIM
        qh = q[:, h * HEAD_DIM:(h + 1) * HEAD_DIM]
        sink = sinks_ref[h]
        slope = _alibi_slope(h)
        logits = []
        m = jnp.full((q.shape[0], 1), sink, F32)
        for (k, _, dist, valid) in segments:
            l = lax.dot_general(qh, k[:, kv0:kv0 + HEAD_DIM], (((1,), (1,)), ((), ())),
                                preferred_element_type=F32)
            l = l - slope * dist
            if valid is not None:
                l = jnp.where(valid, l, -jnp.inf)
            logits.append(l)
            m = jnp.maximum(m, jnp.max(l, axis=-1, keepdims=True))
        denom = jnp.exp(sink - m)
        acc = None
        for l, (_, v, _, _) in zip(logits, segments):
            e = jnp.exp(l - m)
            denom = denom + jnp.sum(e, axis=-1, keepdims=True)
            pv = jnp.dot(e.astype(BF16), v[:, kv0:kv0 + HEAD_DIM], preferred_element_type=F32)
            acc = pv if acc is None else acc + pv
        outs.append(acc / denom)
    o = jnp.concatenate(outs, axis=1)
    return (_rms(o) * gout).astype(BF16)


def _attn_prompt_kernel(sinks_ref, q_ref, kp_ref, kc_ref, vp_ref, vc_ref, g_ref, o_ref):
    i = pl.program_id(1)
    nq = q_ref.shape[0]
    back = kp_ref.shape[0]
    nk = back + nq
    q_t = q_ref[...].astype(F32).T.astype(BF16)
    k = jnp.concatenate([kp_ref[...], kc_ref[...]], axis=0).astype(BF16)
    v_t = jnp.concatenate([vp_ref[...], vc_ref[...]], axis=0).T.astype(BF16)
    c = lax.broadcasted_iota(I32, (nk, nq), 0)
    r = lax.broadcasted_iota(I32, (nk, nq), 1)
    dist = jnp.abs(r - c + back).astype(F32)
    kc = c // CHUNK
    qc = r // CHUNK + (back // CHUNK)
    valid = jnp.logical_and(jnp.logical_and(kc <= qc, kc >= qc - WINDOW // CHUNK),
                            jnp.logical_or(i > 0, kc >= back // CHUNK))
    outs = []
    for kvh in range(N_KV_HEADS):
        kv0 = kvh * HEAD_DIM
        heads = range(kvh * GQA_GROUP, (kvh + 1) * GQA_GROUP)
        qs = jnp.concatenate([q_t[h * HEAD_DIM:(h + 1) * HEAD_DIM, :] for h in heads], axis=1)
        l = jnp.dot(k[:, kv0:kv0 + HEAD_DIM], qs, preferred_element_type=F32)
        l = jnp.concatenate(
            [jnp.where(valid, l[:, g * nq:(g + 1) * nq] - _alibi_slope(h) * dist, -jnp.inf)
             for g, h in enumerate(heads)], axis=1)
        sink = jnp.concatenate([jnp.full((1, nq), sinks_ref[h], F32) for h in heads], axis=1)
        m = jnp.maximum(jnp.max(l, axis=0, keepdims=True), sink)
        e = jnp.exp(l - m)
        denom = jnp.sum(e, axis=0, keepdims=True) + jnp.exp(sink - m)
        pv = jnp.dot(v_t[kv0:kv0 + HEAD_DIM, :], e.astype(BF16), preferred_element_type=F32)
        o_t = pv / denom
        outs.extend(o_t[:, g * nq:(g + 1) * nq] for g in range(GQA_GROUP))
    o = jnp.concatenate(outs, axis=0).T
    o_ref[...] = (_rms(o) * g_ref[...]).astype(BF16)


def _attn_prompt(q, k, v, sinks, gattn, batch, seq):
    nt = seq // ATTN_TILE
    per_tile = ATTN_TILE // WINDOW
    cur = lambda b, i, s: (b * nt + i, 0)
    prev = lambda b, i, s: (jnp.maximum((b * nt + i) * per_tile - 1, 0), 0)
    kv_blk = (ATTN_TILE, KV_WIDTH)
    back_blk = (WINDOW, KV_WIDTH)
    return pl.pallas_call(
        _attn_prompt_kernel,
        grid_spec=pltpu.PrefetchScalarGridSpec(
            num_scalar_prefetch=1,
            grid=(batch, nt),
            in_specs=[pl.BlockSpec((ATTN_TILE, ATTN_WIDTH), cur),
                      pl.BlockSpec(back_blk, prev), pl.BlockSpec(kv_blk, cur),
                      pl.BlockSpec(back_blk, prev), pl.BlockSpec(kv_blk, cur),
                      pl.BlockSpec((1, ATTN_WIDTH), lambda b, i, s: (0, 0))],
            out_specs=pl.BlockSpec((ATTN_TILE, ATTN_WIDTH), cur)),
        out_shape=jax.ShapeDtypeStruct((batch * seq, ATTN_WIDTH), BF16),
        compiler_params=_params(2),
        name="attn_prompt",
    )(sinks, q, k, k, v, v, gattn)


def _attn_sample_kernel(sinks_ref, q_ref, ck_ref, cv_ref, k_ref, v_ref, g_ref, o_ref):
    nq = q_ref.shape[0]
    w = ck_ref.shape[1]
    r = lax.broadcasted_iota(I32, (nq, w), 0)
    c = lax.broadcasted_iota(I32, (nq, w), 1)
    dist_cache = jnp.abs(r - c + w).astype(F32)
    r2 = lax.broadcasted_iota(I32, (nq, nq), 0)
    c2 = lax.broadcasted_iota(I32, (nq, nq), 1)
    dist_new = jnp.abs(r2 - c2).astype(F32)
    segs = [(ck_ref[0].astype(BF16), cv_ref[0].astype(BF16), dist_cache, None),
            (k_ref[...].astype(BF16), v_ref[...].astype(BF16), dist_new, None)]
    o_ref[...] = _sink_attention(q_ref[...], segs, sinks_ref, g_ref[...])


def _attn_sample(q, k, v, cache_k, cache_v, sinks, gattn, dec_batch, n, row0):
    w = cache_k.shape[1]
    blk0 = row0 // n
    new = lambda b, s: (blk0 + b, 0)
    return pl.pallas_call(
        _attn_sample_kernel,
        grid_spec=pltpu.PrefetchScalarGridSpec(
            num_scalar_prefetch=1,
            grid=(dec_batch,),
            in_specs=[pl.BlockSpec((n, ATTN_WIDTH), new),
                      pl.BlockSpec((1, w, KV_WIDTH), lambda b, s: (b, 0, 0)),
                      pl.BlockSpec((1, w, KV_WIDTH), lambda b, s: (b, 0, 0)),
                      pl.BlockSpec((n, KV_WIDTH), new),
                      pl.BlockSpec((n, KV_WIDTH), new),
                      pl.BlockSpec((1, ATTN_WIDTH), lambda b, s: (0, 0))],
            out_specs=pl.BlockSpec((n, ATTN_WIDTH), lambda b, s: (b, 0))),
        out_shape=jax.ShapeDtypeStruct((dec_batch * n, ATTN_WIDTH), BF16),
        compiler_params=_params(1),
        name="attn_sample",
    )(sinks, q, cache_k, cache_v, k, v, gattn)


def _merge_kernel(ap_ref, as_ref, sgu_ref, xp_ref, xs_ref,
                  g1p_ref, g1s_ref, sh2p_ref, sh2s_ref, sc2p_ref, sc2s_ref, g2p_ref, g2s_ref,
                  woa_ref, wos_ref, g2_ref, wrh_ref, wrl_ref, wsgu_ref, wsd_ref,
                  xb_ref, h2_ref, sc_ref, *, n_prompt_tiles):
    is_sample = pl.program_id(0) >= n_prompt_tiles
    x = _pick_tile(is_sample, xp_ref, xs_ref)
    a = jnp.where(is_sample, as_ref[...], ap_ref[...])
    mix = (jnp.dot(a, woa_ref[...], preferred_element_type=F32)
           + jnp.dot(sgu_ref[...], wos_ref[...], preferred_element_type=F32))
    x1 = x + _cond_rows(is_sample, g1p_ref, g1s_ref) * mix
    h2 = (_rms(x1) * g2_ref[...] * (1.0 + _cond_rows(is_sample, sc2p_ref, sc2s_ref))
          + _cond_rows(is_sample, sh2p_ref, sh2s_ref))
    hh = h2.astype(BF16)
    _store_rows(h2_ref, hh)
    hl = (h2 - hh.astype(F32)).astype(BF16)
    logits = (jnp.dot(hh, wrh_ref[...], preferred_element_type=F32)
              + (jnp.dot(hl, wrh_ref[...], preferred_element_type=F32)
                 + jnp.dot(hh, wrl_ref[...], preferred_element_type=F32)))
    sc_ref[...] = jax.nn.sigmoid(logits)
    gu = jnp.dot(hh, wsgu_ref[...], preferred_element_type=F32)
    g = gu[:, :EXPERT_DIM]
    act = (g * jax.nn.sigmoid(g)) * gu[:, EXPERT_DIM:]
    shared = jnp.dot(act.astype(BF16), wsd_ref[...], preferred_element_type=F32)
    xb_ref[...] = x1 + _cond_rows(is_sample, g2p_ref, g2s_ref) * shared


def _merge(attn_p, attn_s, sgu_n, xp3, xs3, mod_p, mod_s, woa, wos, g2, wrh, wrl, wsgu, wsd, batch, n_prompt_tiles):
    t = sgu_n.shape[0]
    n_tiles = t // TOKEN_TILE
    last_p = n_prompt_tiles - 1
    x_blk = (TOKEN_TILE // ROW_GROUP, ROW_GROUP, D_MODEL)
    const2 = lambda i: (0, 0)
    const3 = lambda i: (0, 0, 0)
    tile2 = lambda i: (i, 0)
    cond = functools.partial(_cond_specs, tile_rows=TOKEN_TILE, tiles_per_batch=n_prompt_tiles // batch,
                             batch=batch, n_prompt_tiles=n_prompt_tiles, n_sample_tiles=n_tiles - n_prompt_tiles)
    return pl.pallas_call(
        functools.partial(_merge_kernel, n_prompt_tiles=n_prompt_tiles),
        grid=(n_tiles,),
        in_specs=[pl.BlockSpec((TOKEN_TILE, ATTN_WIDTH), lambda i: (jnp.minimum(i, last_p), 0)),
                  pl.BlockSpec((TOKEN_TILE, ATTN_WIDTH), const2),
                  pl.BlockSpec((TOKEN_TILE, SGU_WIDTH), tile2),
                  pl.BlockSpec(x_blk, lambda i: (jnp.minimum(i, last_p), 0, 0)),
                  pl.BlockSpec(x_blk, const3)]
                 + cond(2) + cond(3) + cond(4) + cond(5)
                 + [pl.BlockSpec((ATTN_WIDTH, D_MODEL), const2),
                    pl.BlockSpec((SGU_WIDTH, D_MODEL), const2),
                    pl.BlockSpec((1, D_MODEL), const2),
                    pl.BlockSpec((D_MODEL, N_EXPERTS), const2),
                    pl.BlockSpec((D_MODEL, N_EXPERTS), const2),
                    pl.BlockSpec((D_MODEL, 2 * EXPERT_DIM), const2),
                    pl.BlockSpec((EXPERT_DIM, D_MODEL), const2)],
        out_specs=[pl.BlockSpec((TOKEN_TILE, D_MODEL), tile2),
                   pl.BlockSpec((TOKEN_TILE * ROW_TILES, LANES), tile2),
                   pl.BlockSpec((TOKEN_TILE, N_EXPERTS), tile2)],
        out_shape=[jax.ShapeDtypeStruct((t, D_MODEL), F32),
                   jax.ShapeDtypeStruct((t * ROW_TILES, LANES), ROW_DTYPE),
                   jax.ShapeDtypeStruct((t, N_EXPERTS), F32)],
        compiler_params=_params(1),
        name="merge",
    )(attn_p, attn_s, sgu_n, xp3, xs3, *([mod_p, mod_s] * 4), woa, wos, g2, wrh, wrl, wsgu, wsd)


def _first_argmax(x, iota, size, axis):
    m = jnp.max(x, axis=axis, keepdims=True)
    idx = jnp.min(jnp.where(x == m, iota, size), axis=axis, keepdims=True)
    return m, idx


def _route_kernel(sc_ref, bias_ref, tri_ref, eidx_ref, wts_ref, rank_ref, cnt_ref, base_ref):
    tt = sc_ref.shape[0]

    @pl.when(pl.program_id(0) == 0)
    def _():
        base_ref[...] = jnp.zeros_like(base_ref)

    s_t = sc_ref[...].T
    sel = s_t + bias_ref[...]
    sel3 = sel.reshape(N_EXPERT_GROUPS, GROUP_SIZE, tt)
    io3 = lax.broadcasted_iota(I32, sel3.shape, 1)
    m1, i1 = _first_argmax(sel3, io3, GROUP_SIZE, 1)
    m2 = jnp.max(jnp.where(io3 == i1, -jnp.inf, sel3), axis=1, keepdims=True)
    gs = (m1 + m2).reshape(N_EXPERT_GROUPS, tt)
    io8 = lax.broadcasted_iota(I32, gs.shape, 0)
    gmask = jnp.zeros(gs.shape, jnp.bool_)
    for _ in range(TOPK_GROUPS):
        _, gi = _first_argmax(gs, io8, N_EXPERT_GROUPS, 0)
        hit = io8 == gi
        gmask = jnp.logical_or(gmask, hit)
        gs = jnp.where(hit, -jnp.inf, gs)
    emask = jnp.broadcast_to(gmask.reshape(N_EXPERT_GROUPS, 1, tt), sel3.shape).reshape(N_EXPERTS, tt)
    cand = jnp.where(emask, sel, -jnp.inf)
    io = lax.broadcasted_iota(I32, cand.shape, 0)
    chosen = jnp.zeros(cand.shape, jnp.bool_)
    eidx, wts = [], []
    for _ in range(TOP_K):
        _, ei = _first_argmax(cand, io, N_EXPERTS, 0)
        hit = io == ei
        eidx.append(ei)
        wts.append(jnp.sum(jnp.where(hit, s_t, 0.0), axis=0, keepdims=True))
        chosen = jnp.logical_or(chosen, hit)
        cand = jnp.where(hit, -jnp.inf, cand)
    w = jnp.concatenate(wts, axis=0)
    wts_ref[...] = w / jnp.sum(w, axis=0, keepdims=True) * ROUTED_SCALE
    eidx_ref[...] = jnp.concatenate(eidx, axis=0)
    cf = jnp.where(chosen, 1.0, 0.0)
    ahead = jnp.dot(cf.astype(BF16), tri_ref[...], preferred_element_type=F32) + base_ref[...]
    ranks = [jnp.sum(jnp.where(io == ei, ahead, 0.0), axis=0, keepdims=True) for ei in eidx]
    rank_ref[...] = jnp.concatenate(ranks, axis=0).astype(I32)
    total = base_ref[...] + jnp.sum(cf, axis=1, keepdims=True)
    base_ref[...] = total
    cnt_ref[...] = total.astype(I32)


def _route(scores, bias_col, tri):
    t = scores.shape[0]
    n_tiles = t // TOKEN_TILE
    col = lambda i: (0, i)
    return pl.pallas_call(
        _route_kernel,
        grid=(n_tiles,),
        in_specs=[pl.BlockSpec((TOKEN_TILE, N_EXPERTS), lambda i: (i, 0)),
                  pl.BlockSpec((N_EXPERTS, 1), lambda i: (0, 0)),
                  pl.BlockSpec((TOKEN_TILE, TOKEN_TILE), lambda i: (0, 0))],
        out_specs=[pl.BlockSpec((TOP_K, TOKEN_TILE), col),
                   pl.BlockSpec((TOP_K, TOKEN_TILE), col),
                   pl.BlockSpec((TOP_K, TOKEN_TILE), col),
                   pl.BlockSpec((N_EXPERTS, 1), lambda i: (0, 0))],
        out_shape=[jax.ShapeDtypeStruct((TOP_K, t), I32),
                   jax.ShapeDtypeStruct((TOP_K, t), F32),
                   jax.ShapeDtypeStruct((TOP_K, t), I32),
                   jax.ShapeDtypeStruct((N_EXPERTS, 1), I32)],
        scratch_shapes=[pltpu.VMEM((N_EXPERTS, 1), F32)],
        compiler_params=_params(1),
        name="route",
    )(scores, bias_col, tri)


def _pos_kernel(eidx_ref, rank_ref, start_ref, pos_ref):
    e = eidx_ref[...]
    tt = e.shape[1]
    io = lax.broadcasted_iota(I32, (N_EXPERTS, tt), 0)
    st = start_ref[...]
    rows = [jnp.sum(jnp.where(io == e[k:k + 1, :], st, 0), axis=0, keepdims=True) for k in range(TOP_K)]
    pos_ref[...] = jnp.concatenate(rows, axis=0) + rank_ref[...]


def _positions(eidx, rank, start_col):
    t = eidx.shape[1]
    col = lambda i: (0, i)
    blk = pl.BlockSpec((TOP_K, TOKEN_TILE), col)
    return pl.pallas_call(
        _pos_kernel,
        grid=(t // TOKEN_TILE,),
        in_specs=[blk, blk, pl.BlockSpec((N_EXPERTS, 1), lambda i: (0, 0))],
        out_specs=blk,
        out_shape=jax.ShapeDtypeStruct((TOP_K, t), I32),
        compiler_params=_params(1),
        name="positions",
    )(eidx, rank, start_col)


def _dispatch_kernel(pos_ref, h_ref, xs_ref, zero_ref, sem, zsem, *, n_rows):
    tt = h_ref.shape[0] // ROW_TILES
    first = pl.program_id(0) == 0

    @pl.when(first)
    def _():
        zero_ref[...] = jnp.zeros_like(zero_ref)
        _rows_copy(zero_ref, 0, xs_ref, n_rows, EXPERT_BLOCK, zsem).start()

    def body(t, carry):
        for k in range(TOP_K):
            _rows_copy(h_ref, t, xs_ref, pos_ref[k, t], 1, sem).start(priority=k % 2)
        return carry

    lax.fori_loop(0, tt, body, 0)
    for k in range(TOP_K):
        _rows_copy(h_ref, 0, xs_ref, 0, tt, sem).wait()

    @pl.when(first)
    def _():
        _rows_copy(zero_ref, 0, xs_ref, n_rows, EXPERT_BLOCK, zsem).wait()


def _dispatch(pos, h2_tiles, n_rows):
    t = h2_tiles.shape[0] // ROW_TILES
    return pl.pallas_call(
        functools.partial(_dispatch_kernel, n_rows=n_rows),
        grid=(t // TOKEN_TILE,),
        in_specs=[pl.BlockSpec((TOP_K, TOKEN_TILE), lambda i: (0, i), memory_space=pltpu.SMEM),
                  pl.BlockSpec((TOKEN_TILE * ROW_TILES, LANES), lambda i: (i, 0))],
        out_specs=pl.BlockSpec(memory_space=pl.ANY),
        out_shape=jax.ShapeDtypeStruct(((n_rows + EXPERT_BLOCK) * ROW_TILES, LANES), ROW_DTYPE),
        scratch_shapes=[pltpu.VMEM((EXPERT_BLOCK * ROW_TILES, LANES), ROW_DTYPE),
                        pltpu.SemaphoreType.DMA, pltpu.SemaphoreType.DMA],
        compiler_params=_params(1),
        name="dispatch",
    )(pos, h2_tiles)


def _out_pieces(nvalid, fn):
    s = 1 << (EXPERT_BLOCK.bit_length() - 1)
    while s >= 1:
        @pl.when((nvalid & s) != 0)
        def _(s=s):
            fn(nvalid & ~(2 * s - 1), s)
        s //= 2


def _for_block_size(nvalid, fn):
    lo = 0
    for rows in BLOCK_SIZES:
        @pl.when(jnp.logical_and(nvalid > lo, nvalid <= rows))
        def _(rows=rows):
            fn(rows)
        lo = rows


def _expert_kernel(count_ref, blk0_ref, row_ref, cnt_ref, xs_ref, wg_ref, wu_ref, wd_ref, ys_ref,
                   xbuf, ybuf, wgu_s, wd_s, state, xsem, ysem):
    e = pl.program_id(0)
    cnt = count_ref[e]
    g0 = blk0_ref[e]
    nblk = (cnt + EXPERT_BLOCK - 1) // EXPERT_BLOCK

    def x_copy(row, slot, rows):
        return _rows_copy(xs_ref, row, xbuf.at[slot], 0, rows, xsem.at[slot])

    def x_fetch(g, slot):
        _for_block_size(cnt_ref[g], lambda rows: x_copy(row_ref[g], slot, rows).start())

    def y_wait(nvalid, slot):
        def wait(off, size):
            _rows_copy(ybuf.at[slot], 0, ys_ref, 0, size, ysem.at[slot]).wait()
        _out_pieces(nvalid, wait)

    @pl.when(e == 0)
    def _():
        state[0] = 0
        state[1] = 0
        for g in range(X_SLOTS - 1):
            x_fetch(g, g)

    @pl.when(cnt > 0)
    def _():
        wgu_s[:, :EXPERT_DIM] = wg_ref[0].astype(BF16)
        wgu_s[:, EXPERT_DIM:] = wu_ref[0].astype(BF16)
        wd_s[...] = wd_ref[0].astype(BF16)

    def compute(rows, xslot, yslot, row0, nvalid):
        x_copy(row0, xslot, rows).wait()
        x = _load_rows(xbuf.at[xslot], rows).astype(BF16)
        gu = jnp.dot(x, wgu_s[...], preferred_element_type=F32)
        g = gu[:, :EXPERT_DIM]
        act = (g * jax.nn.sigmoid(g)) * gu[:, EXPERT_DIM:]
        y = jnp.dot(act.astype(BF16), wd_s[...], preferred_element_type=F32)
        y_wait(state[0], 1 - yslot)
        _store_rows(ybuf.at[yslot], y)

        def put(off, size):
            _rows_copy(ybuf.at[yslot], off, ys_ref, row0 + off, size, ysem.at[yslot]).start()
        _out_pieces(nvalid, put)

    def block(j, carry):
        g = g0 + j
        xslot = state[1]
        yslot = g & 1
        row0 = row_ref[g]
        nvalid = cnt_ref[g]
        x_fetch(g + X_SLOTS - 1, jnp.where(xslot == 0, X_SLOTS - 1, xslot - 1))
        _for_block_size(nvalid, lambda rows: compute(rows, xslot, yslot, row0, nvalid))
        state[0] = nvalid
        state[1] = jnp.where(xslot == X_SLOTS - 1, 0, xslot + 1)
        return carry

    lax.fori_loop(0, nblk, block, 0)

    @pl.when(e == pl.num_programs(0) - 1)
    def _():
        y_wait(state[0], (g0 + nblk - 1) & 1)


def _experts(count, blk0, blk_row, blk_cnt, xs, w_gate, w_up, w_down, n_rows):
    w_map = lambda e, c, b, r, n: (e, 0, 0)
    blk_rows = EXPERT_BLOCK * ROW_TILES
    return pl.pallas_call(
        _expert_kernel,
        grid_spec=pltpu.PrefetchScalarGridSpec(
            num_scalar_prefetch=4,
            grid=(N_EXPERTS,),
            in_specs=[pl.BlockSpec(memory_space=pl.ANY),
                      pl.BlockSpec((1, D_MODEL, EXPERT_DIM), w_map),
                      pl.BlockSpec((1, D_MODEL, EXPERT_DIM), w_map),
                      pl.BlockSpec((1, EXPERT_DIM, D_MODEL), w_map)],
            out_specs=pl.BlockSpec(memory_space=pl.ANY),
            scratch_shapes=[pltpu.VMEM((X_SLOTS, blk_rows, LANES), ROW_DTYPE),
                            pltpu.VMEM((2, blk_rows, LANES), ROW_DTYPE),
                            pltpu.VMEM((D_MODEL, 2 * EXPERT_DIM), BF16),
                            pltpu.VMEM((EXPERT_DIM, D_MODEL), BF16),
                            pltpu.SMEM((2,), I32),
                            pltpu.SemaphoreType.DMA((X_SLOTS,)),
                            pltpu.SemaphoreType.DMA((2,))]),
        out_shape=jax.ShapeDtypeStruct((n_rows * ROW_TILES, LANES), ROW_DTYPE),
        compiler_params=_params(1),
        name="experts",
    )(count, blk0, blk_row, blk_cnt, xs, w_gate, w_up, w_down)


def _combine_kernel(pos_ref, posn_ref, ys_ref, xb_ref, wts_ref, g2p_ref, g2s_ref, gf_ref, yp_ref, ysm_ref,
                    buf, sem, *, n_prompt_tiles):
    i = pl.program_id(0)
    n_tiles = pl.num_programs(0)
    tt = xb_ref.shape[0]
    slot = i & 1

    def gather(p_ref, s):
        def body(t, carry):
            for k in range(TOP_K):
                _rows_copy(ys_ref, p_ref[k, t], buf.at[s, k], t, 1, sem.at[s]).start(priority=k % 2)
            return carry
        lax.fori_loop(0, tt, body, 0)

    @pl.when(i == 0)
    def _():
        gather(pos_ref, 0)

    @pl.when(i + 1 < n_tiles)
    def _():
        gather(posn_ref, 1 - slot)

    for k in range(TOP_K):
        _rows_copy(ys_ref, 0, buf.at[slot, k], 0, tt, sem.at[slot]).wait()
    w = wts_ref[...]
    routed = w[:, 0:1] * _load_rows(buf.at[slot, 0], tt)
    for k in range(1, TOP_K):
        routed = routed + w[:, k:k + 1] * _load_rows(buf.at[slot, k], tt)
    is_sample = i >= n_prompt_tiles
    x = xb_ref[...] + _cond_rows(is_sample, g2p_ref, g2s_ref) * routed
    y = _rms(x) * gf_ref[...]

    @pl.when(i < n_prompt_tiles)
    def _():
        yp_ref[...] = y

    @pl.when(i >= n_prompt_tiles)
    def _():
        ysm_ref[...] = y


def _combine(pos, ys, xb, wts_t, mod_p, mod_s, gf, batch, n_prompt_rows):
    t = xb.shape[0]
    tt = COMBINE_TILE
    n_tiles = t // tt
    n_p = n_prompt_rows // tt
    pos_spec = lambda f: pl.BlockSpec((TOP_K, tt), f, memory_space=pltpu.SMEM)
    return pl.pallas_call(
        functools.partial(_combine_kernel, n_prompt_tiles=n_p),
        grid=(n_tiles,),
        in_specs=[pos_spec(lambda i: (0, i)),
                  pos_spec(lambda i: (0, jnp.minimum(i + 1, n_tiles - 1))),
                  pl.BlockSpec(memory_space=pl.ANY),
                  pl.BlockSpec((tt, D_MODEL), lambda i: (i, 0)),
                  pl.BlockSpec((tt, TOP_K), lambda i: (i, 0))]
                 + _cond_specs(5, tt, n_p // batch, batch, n_p, n_tiles - n_p)
                 + [pl.BlockSpec((1, D_MODEL), lambda i: (0, 0))],
        out_specs=[pl.BlockSpec((tt, D_MODEL), lambda i: (jnp.minimum(i, n_p - 1), 0)),
                   pl.BlockSpec((tt, D_MODEL), lambda i: (jnp.maximum(i - n_p, 0), 0))],
        out_shape=[jax.ShapeDtypeStruct((n_prompt_rows, D_MODEL), F32),
                   jax.ShapeDtypeStruct((t - n_prompt_rows, D_MODEL), F32)],
        scratch_shapes=[pltpu.VMEM((2, TOP_K, tt * ROW_TILES, LANES), ROW_DTYPE), pltpu.SemaphoreType.DMA((2,))],
        compiler_params=_params(1),
        name="combine",
    )(pos, pos, ys, xb, wts_t, mod_p, mod_s, gf)


def _sgu_mask():
    pos = np.arange(SGU_CHUNK) // CHUNK
    return jnp.asarray(pos[None, :] <= pos[:, None])


def _layer(xp3, xs3, cache_k, cache_v, mod, final_g, batch, seq, dec_batch, n, p):
    tp = batch * seq
    t = tp + dec_batch * n
    n_prompt_tiles = tp // TOKEN_TILE
    mod_p = mod[:batch].reshape(batch, 1, N_MOD * D_MODEL)
    mod_s = mod[batch:].reshape(dec_batch, 1, N_MOD * D_MODEL)

    wm = jnp.where(_sgu_mask(), p["sgu_w"], 0.0)
    reps = SGU_CHUNK // n
    eye = jnp.eye(reps, dtype=F32)
    wms = jnp.stack([jnp.kron(eye, wm[g, :n, :n]) for g in range(SGU_GROUPS)])
    bs = jnp.broadcast_to(p["sgu_b"][:, :, None], (SGU_GROUPS, SGU_CHUNK, SGU_CHUNK))
    bss = jnp.broadcast_to(jnp.tile(p["sgu_b"][:, :n], (1, reps))[:, :, None], (SGU_GROUPS, SGU_CHUNK, SGU_CHUNK))

    q, k, v, sgu_n, vn_s = _stage1(
        xp3, xs3, mod_p, mod_s, p["norm1_g"].reshape(1, -1), p["w_in"].astype(BF16),
        p["sgu_ln_g"].reshape(1, -1), p["sgu_ln_b"].reshape(1, -1),
        wm.astype(BF16), wms.astype(BF16), bs, bss, p["sgu_out_g"].reshape(1, -1), batch, n_prompt_tiles)

    gattn = p["attn_out_g"].reshape(1, -1)
    attn_p = _attn_prompt(q, k, v, p["attn_sinks"], gattn, batch, seq)
    w = cache_k.shape[1]
    attn_s = _attn_sample(q, k, v, cache_k.reshape(dec_batch, w, KV_WIDTH), cache_v.reshape(dec_batch, w, KV_WIDTH),
                          p["attn_sinks"], gattn, dec_batch, n, tp)

    wo = p["w_out"].astype(BF16)
    wsgu = jnp.concatenate([p["ws_gate"], p["ws_up"]], axis=1).astype(BF16)
    wrh = p["w_router"].astype(BF16)
    wrl = (p["w_router"] - wrh.astype(F32)).astype(BF16)
    xb, h2_tiles, scores = _merge(attn_p, attn_s, sgu_n, xp3, xs3, mod_p, mod_s,
                                  wo[:ATTN_WIDTH], wo[ATTN_WIDTH:], p["norm2_g"].reshape(1, -1), wrh, wrl,
                                  wsgu, p["ws_down"].astype(BF16), batch, n_prompt_tiles)

    ti = np.arange(TOKEN_TILE)
    tri = jnp.asarray(ti[:, None] < ti[None, :], dtype=BF16)
    eidx, wts, rank, counts = _route(scores, p["router_bias"].reshape(N_EXPERTS, 1), tri)

    counts = counts[:, 0]
    ends = jnp.cumsum(counts)
    start = (ends - counts).astype(I32)
    nblk = (counts + EXPERT_BLOCK - 1) // EXPERT_BLOCK
    blk_end = jnp.cumsum(nblk)
    blk0 = (blk_end - nblk).astype(I32)
    pos = _positions(eidx, rank, start.reshape(N_EXPERTS, 1))

    n_rows = t * TOP_K
    g = jnp.arange(n_rows // EXPERT_BLOCK + N_EXPERTS + X_SLOTS, dtype=I32)[:, None]
    mine = jnp.logical_and(g >= blk0[None, :], g < blk_end[None, :])
    off = (g - blk0[None, :]) * EXPERT_BLOCK
    blk_row = jnp.sum(jnp.where(mine, start[None, :] + off, 0), axis=1).astype(I32)
    blk_cnt = jnp.sum(jnp.where(mine, jnp.minimum(counts[None, :] - off, EXPERT_BLOCK), 0), axis=1).astype(I32)

    xs_sorted = _dispatch(pos, h2_tiles, n_rows)
    ys_sorted = _experts(counts.astype(I32), blk0, blk_row, blk_cnt, xs_sorted,
                         p["w_gate"], p["w_up"], p["w_down"], n_rows)
    y_p, y_s = _combine(pos, ys_sorted, xb, wts.T, mod_p, mod_s, final_g.reshape(1, -1), batch, tp)
    return y_p, y_s, k, v, vn_s


def kernel(x_prompt, x_sample, cache_k, cache_v, c_prompt, c_sample, norm1_g, w_ada, b_ada, w_in, sgu_ln_g, sgu_ln_b, sgu_w, sgu_b, attn_sinks, attn_out_g, sgu_out_g, w_out, norm2_g, w_router, router_bias, w_gate, w_up, w_down, ws_gate, ws_up, ws_down, final_g):
    batch, seq, d = x_prompt.shape
    dec_batch, n, _ = x_sample.shape
    depth = norm1_g.shape[0]
    assert depth == 1 and d == D_MODEL
    assert seq % TOKEN_TILE == 0 and dec_batch * n == TOKEN_TILE and n == ROW_GROUP
    assert seq % ATTN_TILE == 0 and seq >= WINDOW and cache_k.shape[2] == WINDOW
    tp = batch * seq

    xp3 = x_prompt.reshape(tp // ROW_GROUP, ROW_GROUP, d)
    xs3 = x_sample.reshape(dec_batch * n // ROW_GROUP, ROW_GROUP, d)
    c_all = jnp.concatenate([c_prompt, c_sample], axis=0)
    l = 0
    mod = _adaln(c_all, w_ada[l], b_ada[l])
    p = dict(norm1_g=norm1_g[l], w_in=w_in[l], sgu_ln_g=sgu_ln_g[l], sgu_ln_b=sgu_ln_b[l], sgu_w=sgu_w[l],
             sgu_b=sgu_b[l], attn_sinks=attn_sinks[l], attn_out_g=attn_out_g[l], sgu_out_g=sgu_out_g[l],
             w_out=w_out[l], norm2_g=norm2_g[l], w_router=w_router[l], router_bias=router_bias[l],
             w_gate=w_gate[l], w_up=w_up[l], w_down=w_down[l], ws_gate=ws_gate[l], ws_up=ws_up[l],
             ws_down=ws_down[l])
    y_p, y_s, k, v, vn_s = _layer(xp3, xs3, cache_k[l], cache_v[l], mod, final_g, batch, seq, dec_batch, n, p)

    keep = min(WINDOW, seq)
    kv_shape = (batch, keep, N_KV_HEADS, HEAD_DIM)
    k_p = jnp.stack([k[(b + 1) * seq - keep:(b + 1) * seq] for b in range(batch)]).reshape(kv_shape)
    v_p = jnp.stack([v[(b + 1) * seq - keep:(b + 1) * seq] for b in range(batch)]).reshape(kv_shape)
    k_s = k[tp:].reshape(dec_batch, n, N_KV_HEADS, HEAD_DIM)
    v_s = v[tp:].reshape(dec_batch, n, N_KV_HEADS, HEAD_DIM)
    return (y_p.reshape(batch, seq, d), y_s.reshape(dec_batch, n, d),
            k_p[None], v_p[None], k_s[None], v_s[None], vn_s.reshape(dec_batch, n, SGU_WIDTH)[None])
```

```python
import functools

import numpy as np
import jax
import jax.numpy as jnp
from jax import lax
from jax.experimental import pallas as pl
from jax.experimental.pallas import tpu as pltpu

F32 = jnp.float32
BF16 = jnp.bfloat16
I32 = jnp.int32

D_MODEL = 1024
CHUNK = 64
HEAD_DIM = 64
ATTN_WIDTH = 512
N_HEADS = 8
N_KV_HEADS = 2
GQA_GROUP = 4
KV_WIDTH = 128
WINDOW = 128
SGU_CHUNK = 128
SGU_WIDTH = 512
SGU_GROUPS = 4
IN_COLS = ATTN_WIDTH + 2 * KV_WIDTH + 2 * SGU_WIDTH
N_MOD = 6
N_EXPERTS = 256
TOP_K = 8
N_EXPERT_GROUPS = 8
GROUP_SIZE = N_EXPERTS // N_EXPERT_GROUPS
TOPK_GROUPS = 4
EXPERT_DIM = 256
ROUTED_SCALE = 2.5
EPS = 1e-6

LANES = 128
ROW_TILES = D_MODEL // LANES
ROW_DTYPE = F32
ROW_GROUP = 16
TOKEN_TILE = 256
ATTN_TILE = 256
EXPERT_BLOCK = 640
BLOCK_SIZES = (128, 256, 384, 512, 576, EXPERT_BLOCK)
X_SLOTS = 3
COMBINE_TILE = 128
COMBINE_GROUP = 8
VMEM_LIMIT_BYTES = 56 * 1024 * 1024


def _params(n_axes=1):
    return pltpu.CompilerParams(dimension_semantics=("arbitrary",) * n_axes,
                                vmem_limit_bytes=VMEM_LIMIT_BYTES)


def _rms(x):
    return x * lax.rsqrt(jnp.mean(x * x, axis=-1, keepdims=True) + EPS)


def _cond_rows(is_sample, mp_ref, ms_ref):
    ms = ms_ref[...]
    g, _, d = ms.shape
    s = jnp.broadcast_to(ms, (g, ROW_GROUP, d)).reshape(g * ROW_GROUP, d)
    return jnp.where(is_sample, s, mp_ref[0])


def _cond_specs(piece, tile_rows, tiles_per_batch, batch, n_prompt_tiles, n_sample_tiles):
    g = tile_rows // ROW_GROUP
    mp = pl.BlockSpec((1, 1, D_MODEL),
                      lambda i: (jnp.minimum(i // tiles_per_batch, batch - 1), 0, piece))
    ms = pl.BlockSpec((g, 1, D_MODEL),
                      lambda i: (jnp.clip(i - n_prompt_tiles, 0, n_sample_tiles - 1), 0, piece))
    return [mp, ms]


def _pick_tile(is_sample, prompt_ref, sample_ref):
    v = jnp.where(is_sample, sample_ref[...], prompt_ref[...])
    return v.reshape(TOKEN_TILE, v.shape[-1])


def _load_rows(ref, n_rows):
    return jnp.concatenate([ref[pl.ds(j, n_rows, stride=ROW_TILES), :] for j in range(ROW_TILES)], axis=1)


def _store_rows(ref, x):
    n_rows = x.shape[0]
    for j in range(ROW_TILES):
        ref[pl.ds(j, n_rows, stride=ROW_TILES), :] = x[:, j * LANES:(j + 1) * LANES].astype(ROW_DTYPE)


def _tile_offset(row):
    off = row * ROW_TILES
    return off if isinstance(off, int) else pl.multiple_of(off, ROW_TILES)


def _rows_copy(src_ref, src_row, dst_ref, dst_row, n_rows, sem):
    s = _tile_offset(src_row)
    d = _tile_offset(dst_row)
    return pltpu.make_async_copy(src_ref.at[pl.ds(s, n_rows * ROW_TILES)],
                                 dst_ref.at[pl.ds(d, n_rows * ROW_TILES)], sem)


def _adaln_kernel(c_ref, w_ref, b_ref, o_ref):
    c = c_ref[...]
    s = c * jax.nn.sigmoid(c)
    o_ref[...] = jnp.dot(s, w_ref[...], precision=lax.Precision.HIGHEST,
                         preferred_element_type=F32) + b_ref[...]


def _adaln(c, w_ada, b_ada):
    n, d = c.shape
    cols = w_ada.shape[1]
    tn = 1536
    return pl.pallas_call(
        _adaln_kernel,
        grid=(cols // tn,),
        in_specs=[pl.BlockSpec((n, d), lambda j: (0, 0)),
                  pl.BlockSpec((d, tn), lambda j: (0, j)),
                  pl.BlockSpec((1, tn), lambda j: (0, j))],
        out_specs=pl.BlockSpec((n, tn), lambda j: (0, j)),
        out_shape=jax.ShapeDtypeStruct((n, cols), F32),
        compiler_params=_params(1),
        name="adaln",
    )(c, w_ada, b_ada.reshape(1, cols))


def _stage1_kernel(xp_ref, xs_ref, shp_ref, shs_ref, scp_ref, scs_ref, g1_ref, win_ref, lng_ref, lnb_ref,
                   wm_ref, wms_ref, bs_ref, bss_ref, gsgu_ref,
                   q_ref, k_ref, v_ref, sgu_ref, vn_ref, *, n_prompt_tiles):
    is_sample = pl.program_id(0) >= n_prompt_tiles
    x = _pick_tile(is_sample, xp_ref, xs_ref)
    h = (_rms(x) * g1_ref[...] * (1.0 + _cond_rows(is_sample, scp_ref, scs_ref))
         + _cond_rows(is_sample, shp_ref, shs_ref))
    proj = jnp.dot(h.astype(BF16), win_ref[...], preferred_element_type=F32)
    q_ref[...] = (proj[:, :ATTN_WIDTH] * (HEAD_DIM ** -0.5)).astype(BF16)
    k_ref[...] = proj[:, ATTN_WIDTH:ATTN_WIDTH + KV_WIDTH]
    v_ref[...] = proj[:, ATTN_WIDTH + KV_WIDTH:ATTN_WIDTH + 2 * KV_WIDTH]
    c0 = ATTN_WIDTH + 2 * KV_WIDTH
    u = jax.nn.gelu(proj[:, c0:c0 + SGU_WIDTH])
    vg = jax.nn.gelu(proj[:, c0 + SGU_WIDTH:])
    mu = jnp.mean(vg, axis=-1, keepdims=True)
    xc = vg - mu
    var = jnp.mean(xc * xc, axis=-1, keepdims=True)
    vn = xc * lax.rsqrt(var + EPS) * lng_ref[...] + lnb_ref[...]

    @pl.when(is_sample)
    def _():
        vn_ref[...] = vn

    vnb = vn.astype(BF16)
    wm = jnp.where(is_sample, wms_ref[...], wm_ref[...])
    bs = jnp.where(is_sample, bss_ref[...], bs_ref[...])
    rows = []
    for c in range(TOKEN_TILE // SGU_CHUNK):
        r0 = c * SGU_CHUNK
        cols = []
        for g in range(SGU_GROUPS):
            l0 = g * SGU_CHUNK
            sv = jnp.dot(wm[g], vnb[r0:r0 + SGU_CHUNK, l0:l0 + SGU_CHUNK],
                         preferred_element_type=F32) + bs[g]
            cols.append(sv)
        rows.append(jnp.concatenate(cols, axis=1))
    sgu = u * jnp.concatenate(rows, axis=0)
    sgu_ref[...] = (_rms(sgu) * gsgu_ref[...]).astype(BF16)


def _stage1(xp3, xs3, mod_p, mod_s, g1, win_b, lng, lnb, wm, wms, bs, bss, gsgu, batch, n_prompt_tiles):
    t = (xp3.shape[0] + xs3.shape[0]) * ROW_GROUP
    n_tiles = t // TOKEN_TILE
    last_p = n_prompt_tiles - 1
    x_blk = (TOKEN_TILE // ROW_GROUP, ROW_GROUP, D_MODEL)
    const2 = lambda i: (0, 0)
    const3 = lambda i: (0, 0, 0)
    tile2 = lambda i: (i, 0)
    cond = functools.partial(_cond_specs, tile_rows=TOKEN_TILE, tiles_per_batch=n_prompt_tiles // batch,
                             batch=batch, n_prompt_tiles=n_prompt_tiles, n_sample_tiles=n_tiles - n_prompt_tiles)
    sgu_w_spec = pl.BlockSpec((SGU_GROUPS, SGU_CHUNK, SGU_CHUNK), const3)
    return pl.pallas_call(
        functools.partial(_stage1_kernel, n_prompt_tiles=n_prompt_tiles),
        grid=(n_tiles,),
        in_specs=[pl.BlockSpec(x_blk, lambda i: (jnp.minimum(i, last_p), 0, 0)),
                  pl.BlockSpec(x_blk, const3)]
                 + cond(0) + cond(1)
                 + [pl.BlockSpec((1, D_MODEL), const2),
                    pl.BlockSpec((D_MODEL, IN_COLS), const2),
                    pl.BlockSpec((1, SGU_WIDTH), const2),
                    pl.BlockSpec((1, SGU_WIDTH), const2),
                    sgu_w_spec, sgu_w_spec, sgu_w_spec, sgu_w_spec,
                    pl.BlockSpec((1, SGU_WIDTH), const2)],
        out_specs=[pl.BlockSpec((TOKEN_TILE, ATTN_WIDTH), tile2),
                   pl.BlockSpec((TOKEN_TILE, KV_WIDTH), tile2),
                   pl.BlockSpec((TOKEN_TILE, KV_WIDTH), tile2),
                   pl.BlockSpec((TOKEN_TILE, SGU_WIDTH), tile2),
                   pl.BlockSpec((TOKEN_TILE, SGU_WIDTH), const2)],
        out_shape=[jax.ShapeDtypeStruct((t, ATTN_WIDTH), BF16),
                   jax.ShapeDtypeStruct((t, KV_WIDTH), F32),
                   jax.ShapeDtypeStruct((t, KV_WIDTH), F32),
                   jax.ShapeDtypeStruct((t, SGU_WIDTH), BF16),
                   jax.ShapeDtypeStruct((TOKEN_TILE, SGU_WIDTH), F32)],
        compiler_params=_params(1),
        name="stage1",
    )(xp3, xs3, mod_p, mod_s, mod_p, mod_s, g1, win_b, lng, lnb, wm, wms, bs, bss, gsgu)


def _alibi_slope(h):
    return float(np.float32(2.0) ** np.float32(-8.0 * (h + 1.0) / N_HEADS))


def _sink_attention(q, segments, sinks_ref, gout):
    outs = []
    for h in range(N_HEADS):
        kv0 = (h // GQA_GROUP) * HEAD_DIM
        qh = q[:, h * HEAD_DIM:(h + 1) * HEAD_DIM]
        sink = sinks_ref[h]
        slope = _alibi_slope(h)
        logits = []
        m = jnp.full((q.shape[0], 1), sink, F32)
        for (k, _, dist, valid) in segments:
            l = lax.dot_general(qh, k[:, kv0:kv0 + HEAD_DIM], (((1,), (1,)), ((), ())),
                                preferred_element_type=F32)
            l = l - slope * dist
            if valid is not None:
                l = jnp.where(valid, l, -jnp.inf)
            logits.append(l)
            m = jnp.maximum(m, jnp.max(l, axis=-1, keepdims=True))
        denom = jnp.exp(sink - m)
        acc = None
        for l, (_, v, _, _) in zip(logits, segments):
            e = jnp.exp(l - m)
            denom = denom + jnp.sum(e, axis=-1, keepdims=True)
            pv = jnp.dot(e.astype(BF16), v[:, kv0:kv0 + HEAD_DIM], preferred_element_type=F32)
            acc = pv if acc is None else acc + pv
        outs.append(acc / denom)
    o = jnp.concatenate(outs, axis=1)
    return (_rms(o) * gout).astype(BF16)


def _attn_prompt_kernel(sinks_ref, q_ref, kp_ref, kc_ref, vp_ref, vc_ref, g_ref, o_ref):
    i = pl.program_id(1)
    nq = q_ref.shape[0]
    back = kp_ref.shape[0]
    nk = back + nq
    q_t = q_ref[...].astype(F32).T.astype(BF16)
    k = jnp.concatenate([kp_ref[...], kc_ref[...]], axis=0).astype(BF16)
    v_t = jnp.concatenate([vp_ref[...], vc_ref[...]], axis=0).T.astype(BF16)
    c = lax.broadcasted_iota(I32, (nk, nq), 0)
    r = lax.broadcasted_iota(I32, (nk, nq), 1)
    dist = jnp.abs(r - c + back).astype(F32)
    kc = c // CHUNK
    qc = r // CHUNK + (back // CHUNK)
    valid = jnp.logical_and(jnp.logical_and(kc <= qc, kc >= qc - WINDOW // CHUNK),
                            jnp.logical_or(i > 0, kc >= back // CHUNK))
    outs = []
    for kvh in range(N_KV_HEADS):
        kv0 = kvh * HEAD_DIM
        heads = range(kvh * GQA_GROUP, (kvh + 1) * GQA_GROUP)
        qs = jnp.concatenate([q_t[h * HEAD_DIM:(h + 1) * HEAD_DIM, :] for h in heads], axis=1)
        l = jnp.dot(k[:, kv0:kv0 + HEAD_DIM], qs, preferred_element_type=F32)
        l = jnp.concatenate(
            [jnp.where(valid, l[:, g * nq:(g + 1) * nq] - _alibi_slope(h) * dist, -jnp.inf)
             for g, h in enumerate(heads)], axis=1)
        sink = jnp.concatenate([jnp.full((1, nq), sinks_ref[h], F32) for h in heads], axis=1)
        m = jnp.maximum(jnp.max(l, axis=0, keepdims=True), sink)
        e = jnp.exp(l - m)
        denom = jnp.sum(e, axis=0, keepdims=True) + jnp.exp(sink - m)
        pv = jnp.dot(v_t[kv0:kv0 + HEAD_DIM, :], e.astype(BF16), preferred_element_type=F32)
        o_t = pv / denom
        outs.extend(o_t[:, g * nq:(g + 1) * nq] for g in range(GQA_GROUP))
    o = jnp.concatenate(outs, axis=0).T
    o_ref[...] = (_rms(o) * g_ref[...]).astype(BF16)


def _attn_prompt(q, k, v, sinks, gattn, batch, seq):
    nt = seq // ATTN_TILE
    per_tile = ATTN_TILE // WINDOW
    cur = lambda b, i, s: (b * nt + i, 0)
    prev = lambda b, i, s: (jnp.maximum((b * nt + i) * per_tile - 1, 0), 0)
    kv_blk = (ATTN_TILE, KV_WIDTH)
    back_blk = (WINDOW, KV_WIDTH)
    return pl.pallas_call(
        _attn_prompt_kernel,
        grid_spec=pltpu.PrefetchScalarGridSpec(
            num_scalar_prefetch=1,
            grid=(batch, nt),
            in_specs=[pl.BlockSpec((ATTN_TILE, ATTN_WIDTH), cur),
                      pl.BlockSpec(back_blk, prev), pl.BlockSpec(kv_blk, cur),
                      pl.BlockSpec(back_blk, prev), pl.BlockSpec(kv_blk, cur),
                      pl.BlockSpec((1, ATTN_WIDTH), lambda b, i, s: (0, 0))],
            out_specs=pl.BlockSpec((ATTN_TILE, ATTN_WIDTH), cur)),
        out_shape=jax.ShapeDtypeStruct((batch * seq, ATTN_WIDTH), BF16),
        compiler_params=_params(2),
        name="attn_prompt",
    )(sinks, q, k, k, v, v, gattn)


def _attn_sample_kernel(sinks_ref, q_ref, ck_ref, cv_ref, k_ref, v_ref, g_ref, o_ref):
    nq = q_ref.shape[0]
    w = ck_ref.shape[1]
    r = lax.broadcasted_iota(I32, (nq, w), 0)
    c = lax.broadcasted_iota(I32, (nq, w), 1)
    dist_cache = jnp.abs(r - c + w).astype(F32)
    r2 = lax.broadcasted_iota(I32, (nq, nq), 0)
    c2 = lax.broadcasted_iota(I32, (nq, nq), 1)
    dist_new = jnp.abs(r2 - c2).astype(F32)
    segs = [(ck_ref[0].astype(BF16), cv_ref[0].astype(BF16), dist_cache, None),
            (k_ref[...].astype(BF16), v_ref[...].astype(BF16), dist_new, None)]
    o_ref[...] = _sink_attention(q_ref[...], segs, sinks_ref, g_ref[...])


def _attn_sample(q, k, v, cache_k, cache_v, sinks, gattn, dec_batch, n, row0):
    w = cache_k.shape[1]
    blk0 = row0 // n
    new = lambda b, s: (blk0 + b, 0)
    return pl.pallas_call(
        _attn_sample_kernel,
        grid_spec=pltpu.PrefetchScalarGridSpec(
            num_scalar_prefetch=1,
            grid=(dec_batch,),
            in_specs=[pl.BlockSpec((n, ATTN_WIDTH), new),
                      pl.BlockSpec((1, w, KV_WIDTH), lambda b, s: (b, 0, 0)),
                      pl.BlockSpec((1, w, KV_WIDTH), lambda b, s: (b, 0, 0)),
                      pl.BlockSpec((n, KV_WIDTH), new),
                      pl.BlockSpec((n, KV_WIDTH), new),
                      pl.BlockSpec((1, ATTN_WIDTH), lambda b, s: (0, 0))],
            out_specs=pl.BlockSpec((n, ATTN_WIDTH), lambda b, s: (b, 0))),
        out_shape=jax.ShapeDtypeStruct((dec_batch * n, ATTN_WIDTH), BF16),
        compiler_params=_params(1),
        name="attn_sample",
    )(sinks, q, cache_k, cache_v, k, v, gattn)


def _merge_kernel(ap_ref, as_ref, sgu_ref, xp_ref, xs_ref,
                  g1p_ref, g1s_ref, sh2p_ref, sh2s_ref, sc2p_ref, sc2s_ref, g2p_ref, g2s_ref,
                  woa_ref, wos_ref, g2_ref, wrh_ref, wrl_ref, wsgu_ref, wsd_ref,
                  xb_ref, h2_ref, sc_ref, *, n_prompt_tiles):
    is_sample = pl.program_id(0) >= n_prompt_tiles
    x = _pick_tile(is_sample, xp_ref, xs_ref)
    a = jnp.where(is_sample, as_ref[...], ap_ref[...])
    mix = (jnp.dot(a, woa_ref[...], preferred_element_type=F32)
           + jnp.dot(sgu_ref[...], wos_ref[...], preferred_element_type=F32))
    x1 = x + _cond_rows(is_sample, g1p_ref, g1s_ref) * mix
    h2 = (_rms(x1) * g2_ref[...] * (1.0 + _cond_rows(is_sample, sc2p_ref, sc2s_ref))
          + _cond_rows(is_sample, sh2p_ref, sh2s_ref))
    hh = h2.astype(BF16)
    _store_rows(h2_ref, hh)
    hl = (h2 - hh.astype(F32)).astype(BF16)
    logits = (jnp.dot(hh, wrh_ref[...], preferred_element_type=F32)
              + (jnp.dot(hl, wrh_ref[...], preferred_element_type=F32)
                 + jnp.dot(hh, wrl_ref[...], preferred_element_type=F32)))
    sc_ref[...] = jax.nn.sigmoid(logits)
    gu = jnp.dot(hh, wsgu_ref[...], preferred_element_type=F32)
    g = gu[:, :EXPERT_DIM]
    act = (g * jax.nn.sigmoid(g)) * gu[:, EXPERT_DIM:]
    shared = jnp.dot(act.astype(BF16), wsd_ref[...], preferred_element_type=F32)
    xb_ref[...] = x1 + _cond_rows(is_sample, g2p_ref, g2s_ref) * shared


def _merge(attn_p, attn_s, sgu_n, xp3, xs3, mod_p, mod_s, woa, wos, g2, wrh, wrl, wsgu, wsd, batch, n_prompt_tiles):
    t = sgu_n.shape[0]
    n_tiles = t // TOKEN_TILE
    last_p = n_prompt_tiles - 1
    x_blk = (TOKEN_TILE // ROW_GROUP, ROW_GROUP, D_MODEL)
    const2 = lambda i: (0, 0)
    const3 = lambda i: (0, 0, 0)
    tile2 = lambda i: (i, 0)
    cond = functools.partial(_cond_specs, tile_rows=TOKEN_TILE, tiles_per_batch=n_prompt_tiles // batch,
                             batch=batch, n_prompt_tiles=n_prompt_tiles, n_sample_tiles=n_tiles - n_prompt_tiles)
    return pl.pallas_call(
        functools.partial(_merge_kernel, n_prompt_tiles=n_prompt_tiles),
        grid=(n_tiles,),
        in_specs=[pl.BlockSpec((TOKEN_TILE, ATTN_WIDTH), lambda i: (jnp.minimum(i, last_p), 0)),
                  pl.BlockSpec((TOKEN_TILE, ATTN_WIDTH), const2),
                  pl.BlockSpec((TOKEN_TILE, SGU_WIDTH), tile2),
                  pl.BlockSpec(x_blk, lambda i: (jnp.minimum(i, last_p), 0, 0)),
                  pl.BlockSpec(x_blk, const3)]
                 + cond(2) + cond(3) + cond(4) + cond(5)
                 + [pl.BlockSpec((ATTN_WIDTH, D_MODEL), const2),
                    pl.BlockSpec((SGU_WIDTH, D_MODEL), const2),
                    pl.BlockSpec((1, D_MODEL), const2),
                    pl.BlockSpec((D_MODEL, N_EXPERTS), const2),
                    pl.BlockSpec((D_MODEL, N_EXPERTS), const2),
                    pl.BlockSpec((D_MODEL, 2 * EXPERT_DIM), const2),
                    pl.BlockSpec((EXPERT_DIM, D_MODEL), const2)],
        out_specs=[pl.BlockSpec((TOKEN_TILE, D_MODEL), tile2),
                   pl.BlockSpec((TOKEN_TILE * ROW_TILES, LANES), tile2),
                   pl.BlockSpec((TOKEN_TILE, N_EXPERTS), tile2)],
        out_shape=[jax.ShapeDtypeStruct((t, D_MODEL), F32),
                   jax.ShapeDtypeStruct((t * ROW_TILES, LANES), ROW_DTYPE),
                   jax.ShapeDtypeStruct((t, N_EXPERTS), F32)],
        compiler_params=_params(1),
        name="merge",
    )(attn_p, attn_s, sgu_n, xp3, xs3, *([mod_p, mod_s] * 4), woa, wos, g2, wrh, wrl, wsgu, wsd)


def _first_argmax(x, iota, size, axis):
    m = jnp.max(x, axis=axis, keepdims=True)
    idx = jnp.min(jnp.where(x == m, iota, size), axis=axis, keepdims=True)
    return m, idx


def _route_kernel(sc_ref, bias_ref, tri_ref, eidx_ref, wts_ref, rank_ref, cnt_ref, base_ref):
    tt = sc_ref.shape[0]

    @pl.when(pl.program_id(0) == 0)
    def _():
        base_ref[...] = jnp.zeros_like(base_ref)

    s_t = sc_ref[...].T
    sel = s_t + bias_ref[...]
    sel3 = sel.reshape(N_EXPERT_GROUPS, GROUP_SIZE, tt)
    io3 = lax.broadcasted_iota(I32, sel3.shape, 1)
    m1, i1 = _first_argmax(sel3, io3, GROUP_SIZE, 1)
    m2 = jnp.max(jnp.where(io3 == i1, -jnp.inf, sel3), axis=1, keepdims=True)
    gs = (m1 + m2).reshape(N_EXPERT_GROUPS, tt)
    io8 = lax.broadcasted_iota(I32, gs.shape, 0)
    gmask = jnp.zeros(gs.shape, jnp.bool_)
    for _ in range(TOPK_GROUPS):
        _, gi = _first_argmax(gs, io8, N_EXPERT_GROUPS, 0)
        hit = io8 == gi
        gmask = jnp.logical_or(gmask, hit)
        gs = jnp.where(hit, -jnp.inf, gs)
    emask = jnp.broadcast_to(gmask.reshape(N_EXPERT_GROUPS, 1, tt), sel3.shape).reshape(N_EXPERTS, tt)
    cand = jnp.where(emask, sel, -jnp.inf)
    io = lax.broadcasted_iota(I32, cand.shape, 0)
    chosen = jnp.zeros(cand.shape, jnp.bool_)
    eidx, wts = [], []
    for _ in range(TOP_K):
        _, ei = _first_argmax(cand, io, N_EXPERTS, 0)
        hit = io == ei
        eidx.append(ei)
        wts.append(jnp.sum(jnp.where(hit, s_t, 0.0), axis=0, keepdims=True))
        chosen = jnp.logical_or(chosen, hit)
        cand = jnp.where(hit, -jnp.inf, cand)
    w = jnp.concatenate(wts, axis=0)
    wts_ref[...] = w / jnp.sum(w, axis=0, keepdims=True) * ROUTED_SCALE
    eidx_ref[...] = jnp.concatenate(eidx, axis=0)
    cf = jnp.where(chosen, 1.0, 0.0)
    ahead = jnp.dot(cf.astype(BF16), tri_ref[...], preferred_element_type=F32) + base_ref[...]
    ranks = [jnp.sum(jnp.where(io == ei, ahead, 0.0), axis=0, keepdims=True) for ei in eidx]
    rank_ref[...] = jnp.concatenate(ranks, axis=0).astype(I32)
    total = base_ref[...] + jnp.sum(cf, axis=1, keepdims=True)
    base_ref[...] = total
    cnt_ref[...] = total.astype(I32)


def _route(scores, bias_col, tri):
    t = scores.shape[0]
    n_tiles = t // TOKEN_TILE
    col = lambda i: (0, i)
    return pl.pallas_call(
        _route_kernel,
        grid=(n_tiles,),
        in_specs=[pl.BlockSpec((TOKEN_TILE, N_EXPERTS), lambda i: (i, 0)),
                  pl.BlockSpec((N_EXPERTS, 1), lambda i: (0, 0)),
                  pl.BlockSpec((TOKEN_TILE, TOKEN_TILE), lambda i: (0, 0))],
        out_specs=[pl.BlockSpec((TOP_K, TOKEN_TILE), col),
                   pl.BlockSpec((TOP_K, TOKEN_TILE), col),
                   pl.BlockSpec((TOP_K, TOKEN_TILE), col),
                   pl.BlockSpec((N_EXPERTS, 1), lambda i: (0, 0))],
        out_shape=[jax.ShapeDtypeStruct((TOP_K, t), I32),
                   jax.ShapeDtypeStruct((TOP_K, t), F32),
                   jax.ShapeDtypeStruct((TOP_K, t), I32),
                   jax.ShapeDtypeStruct((N_EXPERTS, 1), I32)],
        scratch_shapes=[pltpu.VMEM((N_EXPERTS, 1), F32)],
        compiler_params=_params(1),
        name="route",
    )(scores, bias_col, tri)


def _pos_kernel(eidx_ref, rank_ref, start_ref, pos_ref):
    e = eidx_ref[...]
    tt = e.shape[1]
    io = lax.broadcasted_iota(I32, (N_EXPERTS, tt), 0)
    st = start_ref[...]
    rows = [jnp.sum(jnp.where(io == e[k:k + 1, :], st, 0), axis=0, keepdims=True) for k in range(TOP_K)]
    pos_ref[...] = jnp.concatenate(rows, axis=0) + rank_ref[...]


def _positions(eidx, rank, start_col):
    t = eidx.shape[1]
    col = lambda i: (0, i)
    blk = pl.BlockSpec((TOP_K, TOKEN_TILE), col)
    return pl.pallas_call(
        _pos_kernel,
        grid=(t // TOKEN_TILE,),
        in_specs=[blk, blk, pl.BlockSpec((N_EXPERTS, 1), lambda i: (0, 0))],
        out_specs=blk,
        out_shape=jax.ShapeDtypeStruct((TOP_K, t), I32),
        compiler_params=_params(1),
        name="positions",
    )(eidx, rank, start_col)


def _dispatch_kernel(pos_ref, h_ref, xs_ref, zero_ref, sem, zsem, *, n_rows):
    tt = h_ref.shape[0] // ROW_TILES
    first = pl.program_id(0) == 0

    @pl.when(first)
    def _():
        zero_ref[...] = jnp.zeros_like(zero_ref)
        _rows_copy(zero_ref, 0, xs_ref, n_rows, EXPERT_BLOCK, zsem).start()

    for t in range(tt):
        for k in range(TOP_K):
            _rows_copy(h_ref, t, xs_ref, pos_ref[k, t], 1, sem).start(priority=k % 2)
    for k in range(TOP_K):
        _rows_copy(h_ref, 0, xs_ref, 0, tt, sem).wait()

    @pl.when(first)
    def _():
        _rows_copy(zero_ref, 0, xs_ref, n_rows, EXPERT_BLOCK, zsem).wait()


def _dispatch(pos, h2_tiles, n_rows):
    t = h2_tiles.shape[0] // ROW_TILES
    return pl.pallas_call(
        functools.partial(_dispatch_kernel, n_rows=n_rows),
        grid=(t // TOKEN_TILE,),
        in_specs=[pl.BlockSpec((TOP_K, TOKEN_TILE), lambda i: (0, i), memory_space=pltpu.SMEM),
                  pl.BlockSpec((TOKEN_TILE * ROW_TILES, LANES), lambda i: (i, 0))],
        out_specs=pl.BlockSpec(memory_space=pl.ANY),
        out_shape=jax.ShapeDtypeStruct(((n_rows + EXPERT_BLOCK) * ROW_TILES, LANES), ROW_DTYPE),
        scratch_shapes=[pltpu.VMEM((EXPERT_BLOCK * ROW_TILES, LANES), ROW_DTYPE),
                        pltpu.SemaphoreType.DMA, pltpu.SemaphoreType.DMA],
        compiler_params=_params(1),
        name="dispatch",
    )(pos, h2_tiles)


def _out_pieces(nvalid, fn):
    s = 1 << (EXPERT_BLOCK.bit_length() - 1)
    while s >= 1:
        @pl.when((nvalid & s) != 0)
        def _(s=s):
            fn(nvalid & ~(2 * s - 1), s)
        s //= 2


def _for_block_size(nvalid, fn):
    lo = 0
    for rows in BLOCK_SIZES:
        @pl.when(jnp.logical_and(nvalid > lo, nvalid <= rows))
        def _(rows=rows):
            fn(rows)
        lo = rows


def _expert_kernel(count_ref, blk0_ref, row_ref, cnt_ref, xs_ref, wg_ref, wu_ref, wd_ref, ys_ref,
                   xbuf, ybuf, wgu_s, wd_s, state, xsem, ysem):
    e = pl.program_id(0)
    cnt = count_ref[e]
    g0 = blk0_ref[e]
    nblk = (cnt + EXPERT_BLOCK - 1) // EXPERT_BLOCK

    def x_copy(row, slot, rows):
        return _rows_copy(xs_ref, row, xbuf.at[slot], 0, rows, xsem.at[slot])

    def x_fetch(g, slot):
        _for_block_size(cnt_ref[g], lambda rows: x_copy(row_ref[g], slot, rows).start())

    def y_wait(nvalid, slot):
        def wait(off, size):
            _rows_copy(ybuf.at[slot], 0, ys_ref, 0, size, ysem.at[slot]).wait()
        _out_pieces(nvalid, wait)

    @pl.when(e == 0)
    def _():
        state[0] = 0
        state[1] = 0
        for g in range(X_SLOTS - 1):
            x_fetch(g, g)

    @pl.when(cnt > 0)
    def _():
        wgu_s[:, :EXPERT_DIM] = wg_ref[0].astype(BF16)
        wgu_s[:, EXPERT_DIM:] = wu_ref[0].astype(BF16)
        wd_s[...] = wd_ref[0].astype(BF16)

    def compute(rows, xslot, yslot, row0, nvalid):
        x_copy(row0, xslot, rows).wait()
        x = _load_rows(xbuf.at[xslot], rows).astype(BF16)
        gu = jnp.dot(x, wgu_s[...], preferred_element_type=F32)
        g = gu[:, :EXPERT_DIM]
        act = (g * jax.nn.sigmoid(g)) * gu[:, EXPERT_DIM:]
        y = jnp.dot(act.astype(BF16), wd_s[...], preferred_element_type=F32)
        y_wait(state[0], 1 - yslot)
        _store_rows(ybuf.at[yslot], y)

        def put(off, size):
            _rows_copy(ybuf.at[yslot], off, ys_ref, row0 + off, size, ysem.at[yslot]).start()
        _out_pieces(nvalid, put)

    def block(j, carry):
        g = g0 + j
        xslot = state[1]
        yslot = g & 1
        row0 = row_ref[g]
        nvalid = cnt_ref[g]
        x_fetch(g + X_SLOTS - 1, jnp.where(xslot == 0, X_SLOTS - 1, xslot - 1))
        _for_block_size(nvalid, lambda rows: compute(rows, xslot, yslot, row0, nvalid))
        state[0] = nvalid
        state[1] = jnp.where(xslot == X_SLOTS - 1, 0, xslot + 1)
        return carry

    lax.fori_loop(0, nblk, block, 0)

    @pl.when(e == pl.num_programs(0) - 1)
    def _():
        y_wait(state[0], (g0 + nblk - 1) & 1)


def _experts(count, blk0, blk_row, blk_cnt, xs, w_gate, w_up, w_down, n_rows):
    w_map = lambda e, c, b, r, n: (e, 0, 0)
    blk_rows = EXPERT_BLOCK * ROW_TILES
    return pl.pallas_call(
        _expert_kernel,
        grid_spec=pltpu.PrefetchScalarGridSpec(
            num_scalar_prefetch=4,
            grid=(N_EXPERTS,),
            in_specs=[pl.BlockSpec(memory_space=pl.ANY),
                      pl.BlockSpec((1, D_MODEL, EXPERT_DIM), w_map),
                      pl.BlockSpec((1, D_MODEL, EXPERT_DIM), w_map),
                      pl.BlockSpec((1, EXPERT_DIM, D_MODEL), w_map)],
            out_specs=pl.BlockSpec(memory_space=pl.ANY),
            scratch_shapes=[pltpu.VMEM((X_SLOTS, blk_rows, LANES), ROW_DTYPE),
                            pltpu.VMEM((2, blk_rows, LANES), ROW_DTYPE),
                            pltpu.VMEM((D_MODEL, 2 * EXPERT_DIM), BF16),
                            pltpu.VMEM((EXPERT_DIM, D_MODEL), BF16),
                            pltpu.SMEM((2,), I32),
                            pltpu.SemaphoreType.DMA((X_SLOTS,)),
                            pltpu.SemaphoreType.DMA((2,))]),
        out_shape=jax.ShapeDtypeStruct((n_rows * ROW_TILES, LANES), ROW_DTYPE),
        compiler_params=_params(1),
        name="experts",
    )(count, blk0, blk_row, blk_cnt, xs, w_gate, w_up, w_down)


def _combine_kernel(pos_ref, posn_ref, ys_ref, xb_ref, wts_ref, g2p_ref, g2s_ref, gf_ref, yp_ref, ysm_ref,
                    buf, ytile, sem, *, n_prompt_tiles, n_tiles):
    i = pl.program_id(0)
    tt = xb_ref.shape[0]
    slot = i & 1
    is_sample = i >= n_prompt_tiles

    def fetch(p_ref, s, t):
        for k in range(TOP_K):
            _rows_copy(ys_ref, p_ref[k, t], buf.at[s, k], t, 1, sem.at[s]).start(priority=k % 2)

    @pl.when(i == 0)
    def _():
        for t in range(tt):
            fetch(pos_ref, 0, t)

    for k in range(TOP_K):
        _rows_copy(ys_ref, 0, buf.at[slot, k], 0, tt, sem.at[slot]).wait()

    def finish(s, g):
        r0 = g * COMBINE_GROUP
        w = wts_ref[r0:r0 + COMBINE_GROUP, :]
        routed = None
        for k in range(TOP_K):
            rows = _load_rows(buf.at[s, k, pl.ds(r0 * ROW_TILES, COMBINE_GROUP * ROW_TILES)], COMBINE_GROUP)
            term = w[:, k:k + 1] * rows
            routed = term if routed is None else routed + term
        gate = jnp.where(is_sample, g2s_ref[g * COMBINE_GROUP // ROW_GROUP], g2p_ref[0])
        x = xb_ref[r0:r0 + COMBINE_GROUP, :] + gate * routed
        return _rms(x) * gf_ref[...]

    def tile_work(s, fetch_next):
        for g in range(tt // COMBINE_GROUP):
            y = finish(s, g)
            if fetch_next:
                for r in range(COMBINE_GROUP):
                    fetch(posn_ref, 1 - s, g * COMBINE_GROUP + r)
            ytile[g * COMBINE_GROUP:(g + 1) * COMBINE_GROUP, :] = y

    for s in range(2):
        @pl.when(jnp.logical_and(slot == s, i + 1 < n_tiles))
        def _(s=s):
            tile_work(s, True)

    @pl.when(i + 1 == n_tiles)
    def _():
        tile_work((n_tiles - 1) % 2, False)

    @pl.when(i < n_prompt_tiles)
    def _():
        yp_ref[...] = ytile[...]

    @pl.when(i >= n_prompt_tiles)
    def _():
        ysm_ref[...] = ytile[...]


def _combine(pos, ys, xb, wts_t, mod_p, mod_s, gf, batch, n_prompt_rows):
    t = xb.shape[0]
    tt = COMBINE_TILE
    n_tiles = t // tt
    n_p = n_prompt_rows // tt
    pos_spec = lambda f: pl.BlockSpec((TOP_K, tt), f, memory_space=pltpu.SMEM)
    return pl.pallas_call(
        functools.partial(_combine_kernel, n_prompt_tiles=n_p, n_tiles=n_tiles),
        grid=(n_tiles,),
        in_specs=[pos_spec(lambda i: (0, i)),
                  pos_spec(lambda i: (0, jnp.minimum(i + 1, n_tiles - 1))),
                  pl.BlockSpec(memory_space=pl.ANY),
                  pl.BlockSpec((tt, D_MODEL), lambda i: (i, 0)),
                  pl.BlockSpec((tt, TOP_K), lambda i: (i, 0))]
                 + _cond_specs(5, tt, n_p // batch, batch, n_p, n_tiles - n_p)
                 + [pl.BlockSpec((1, D_MODEL), lambda i: (0, 0))],
        out_specs=[pl.BlockSpec((tt, D_MODEL), lambda i: (jnp.minimum(i, n_p - 1), 0)),
                   pl.BlockSpec((tt, D_MODEL), lambda i: (jnp.maximum(i - n_p, 0), 0))],
        out_shape=[jax.ShapeDtypeStruct((n_prompt_rows, D_MODEL), F32),
                   jax.ShapeDtypeStruct((t - n_prompt_rows, D_MODEL), F32)],
        scratch_shapes=[pltpu.VMEM((2, TOP_K, tt * ROW_TILES, LANES), ROW_DTYPE),
                        pltpu.VMEM((tt, D_MODEL), F32),
                        pltpu.SemaphoreType.DMA((2,))],
        compiler_params=_params(1),
        name="combine",
    )(pos, pos, ys, xb, wts_t, mod_p, mod_s, gf)


def _sgu_mask():
    pos = np.arange(SGU_CHUNK) // CHUNK
    return jnp.asarray(pos[None, :] <= pos[:, None])


def _layer(xp3, xs3, cache_k, cache_v, mod, final_g, batch, seq, dec_batch, n, p):
    tp = batch * seq
    t = tp + dec_batch * n
    n_prompt_tiles = tp // TOKEN_TILE
    mod_p = mod[:batch].reshape(batch, 1, N_MOD * D_MODEL)
    mod_s = mod[batch:].reshape(dec_batch, 1, N_MOD * D_MODEL)

    wm = jnp.where(_sgu_mask(), p["sgu_w"], 0.0)
    reps = SGU_CHUNK // n
    eye = jnp.eye(reps, dtype=F32)
    wms = jnp.stack([jnp.kron(eye, wm[g, :n, :n]) for g in range(SGU_GROUPS)])
    bs = jnp.broadcast_to(p["sgu_b"][:, :, None], (SGU_GROUPS, SGU_CHUNK, SGU_CHUNK))
    bss = jnp.broadcast_to(jnp.tile(p["sgu_b"][:, :n], (1, reps))[:, :, None], (SGU_GROUPS, SGU_CHUNK, SGU_CHUNK))

    q, k, v, sgu_n, vn_s = _stage1(
        xp3, xs3, mod_p, mod_s, p["norm1_g"].reshape(1, -1), p["w_in"].astype(BF16),
        p["sgu_ln_g"].reshape(1, -1), p["sgu_ln_b"].reshape(1, -1),
        wm.astype(BF16), wms.astype(BF16), bs, bss, p["sgu_out_g"].reshape(1, -1), batch, n_prompt_tiles)

    gattn = p["attn_out_g"].reshape(1, -1)
    attn_p = _attn_prompt(q, k, v, p["attn_sinks"], gattn, batch, seq)
    w = cache_k.shape[1]
    attn_s = _attn_sample(q, k, v, cache_k.reshape(dec_batch, w, KV_WIDTH), cache_v.reshape(dec_batch, w, KV_WIDTH),
                          p["attn_sinks"], gattn, dec_batch, n, tp)

    wo = p["w_out"].astype(BF16)
    wsgu = jnp.concatenate([p["ws_gate"], p["ws_up"]], axis=1).astype(BF16)
    wrh = p["w_router"].astype(BF16)
    wrl = (p["w_router"] - wrh.astype(F32)).astype(BF16)
    xb, h2_tiles, scores = _merge(attn_p, attn_s, sgu_n, xp3, xs3, mod_p, mod_s,
                                  wo[:ATTN_WIDTH], wo[ATTN_WIDTH:], p["norm2_g"].reshape(1, -1), wrh, wrl,
                                  wsgu, p["ws_down"].astype(BF16), batch, n_prompt_tiles)

    ti = np.arange(TOKEN_TILE)
    tri = jnp.asarray(ti[:, None] < ti[None, :], dtype=BF16)
    eidx, wts, rank, counts = _route(scores, p["router_bias"].reshape(N_EXPERTS, 1), tri)

    counts = counts[:, 0]
    ends = jnp.cumsum(counts)
    start = (ends - counts).astype(I32)
    nblk = (counts + EXPERT_BLOCK - 1) // EXPERT_BLOCK
    blk_end = jnp.cumsum(nblk)
    blk0 = (blk_end - nblk).astype(I32)
    pos = _positions(eidx, rank, start.reshape(N_EXPERTS, 1))

    n_rows = t * TOP_K
    g = jnp.arange(n_rows // EXPERT_BLOCK + N_EXPERTS + X_SLOTS, dtype=I32)[:, None]
    mine = jnp.logical_and(g >= blk0[None, :], g < blk_end[None, :])
    off = (g - blk0[None, :]) * EXPERT_BLOCK
    blk_row = jnp.sum(jnp.where(mine, start[None, :] + off, 0), axis=1).astype(I32)
    blk_cnt = jnp.sum(jnp.where(mine, jnp.minimum(counts[None, :] - off, EXPERT_BLOCK), 0), axis=1).astype(I32)

    xs_sorted = _dispatch(pos, h2_tiles, n_rows)
    ys_sorted = _experts(counts.astype(I32), blk0, blk_row, blk_cnt, xs_sorted,
                         p["w_gate"], p["w_up"], p["w_down"], n_rows)
    y_p, y_s = _combine(pos, ys_sorted, xb, wts.T, mod_p, mod_s, final_g.reshape(1, -1), batch, tp)
    return y_p, y_s, k, v, vn_s


def kernel(x_prompt, x_sample, cache_k, cache_v, c_prompt, c_sample, norm1_g, w_ada, b_ada, w_in, sgu_ln_g, sgu_ln_b, sgu_w, sgu_b, attn_sinks, attn_out_g, sgu_out_g, w_out, norm2_g, w_router, router_bias, w_gate, w_up, w_down, ws_gate, ws_up, ws_down, final_g):
    batch, seq, d = x_prompt.shape
    dec_batch, n, _ = x_sample.shape
    depth = norm1_g.shape[0]
    assert depth == 1 and d == D_MODEL
    assert seq % TOKEN_TILE == 0 and dec_batch * n == TOKEN_TILE and n == ROW_GROUP
    assert seq % ATTN_TILE == 0 and seq >= WINDOW and cache_k.shape[2] == WINDOW
    tp = batch * seq

    xp3 = x_prompt.reshape(tp // ROW_GROUP, ROW_GROUP, d)
    xs3 = x_sample.reshape(dec_batch * n // ROW_GROUP, ROW_GROUP, d)
    c_all = jnp.concatenate([c_prompt, c_sample], axis=0)
    l = 0
    mod = _adaln(c_all, w_ada[l], b_ada[l])
    p = dict(norm1_g=norm1_g[l], w_in=w_in[l], sgu_ln_g=sgu_ln_g[l], sgu_ln_b=sgu_ln_b[l], sgu_w=sgu_w[l],
             sgu_b=sgu_b[l], attn_sinks=attn_sinks[l], attn_out_g=attn_out_g[l], sgu_out_g=sgu_out_g[l],
             w_out=w_out[l], norm2_g=norm2_g[l], w_router=w_router[l], router_bias=router_bias[l],
             w_gate=w_gate[l], w_up=w_up[l], w_down=w_down[l], ws_gate=ws_gate[l], ws_up=ws_up[l],
             ws_down=ws_down[l])
    y_p, y_s, k, v, vn_s = _layer(xp3, xs3, cache_k[l], cache_v[l], mod, final_g, batch, seq, dec_batch, n, p)

    keep = min(WINDOW, seq)
    kv_shape = (batch, keep, N_KV_HEADS, HEAD_DIM)
    k_p = jnp.stack([k[(b + 1) * seq - keep:(b + 1) * seq] for b in range(batch)]).reshape(kv_shape)
    v_p = jnp.stack([v[(b + 1) * seq - keep:(b + 1) * seq] for b in range(batch)]).reshape(kv_shape)
    k_s = k[tp:].reshape(dec_batch, n, N_KV_HEADS, HEAD_DIM)
    v_s = v[tp:].reshape(dec_batch, n, N_KV_HEADS, HEAD_DIM)
    return (y_p.reshape(batch, seq, d), y_s.reshape(dec_batch, n, d),
            k_p[None], v_p[None], k_s[None], v_s[None], vn_s.reshape(dec_batch, n, SGU_WIDTH)[None])
```

```python
import functools

import numpy as np
import jax
import jax.numpy as jnp
from jax import lax
from jax.experimental import pallas as pl
from jax.experimental.pallas import tpu as pltpu

F32 = jnp.float32
BF16 = jnp.bfloat16
I32 = jnp.int32

D_MODEL = 1024
CHUNK = 64
HEAD_DIM = 64
ATTN_WIDTH = 512
N_HEADS = 8
N_KV_HEADS = 2
GQA_GROUP = 4
KV_WIDTH = 128
WINDOW = 128
SGU_CHUNK = 128
SGU_WIDTH = 512
SGU_GROUPS = 4
IN_COLS = ATTN_WIDTH + 2 * KV_WIDTH + 2 * SGU_WIDTH
N_MOD = 6
N_EXPERTS = 256
TOP_K = 8
N_EXPERT_GROUPS = 8
GROUP_SIZE = N_EXPERTS // N_EXPERT_GROUPS
TOPK_GROUPS = 4
EXPERT_DIM = 256
ROUTED_SCALE = 2.5
EPS = 1e-6

LANES = 128
ROW_TILES = D_MODEL // LANES
ROW_DTYPE = F32
ROW_GROUP = 16
TOKEN_TILE = 256
ATTN_TILE = 256
EXPERT_BLOCK = 640
BLOCK_SIZES = (128, 256, 384, 512, 576, EXPERT_BLOCK)
X_SLOTS = 3
COMBINE_TILE = 128
COMBINE_GROUP = 8
VMEM_LIMIT_BYTES = 56 * 1024 * 1024


def _params(n_axes=1):
    return pltpu.CompilerParams(dimension_semantics=("arbitrary",) * n_axes,
                                vmem_limit_bytes=VMEM_LIMIT_BYTES)


def _rms(x):
    return x * lax.rsqrt(jnp.mean(x * x, axis=-1, keepdims=True) + EPS)


def _cond_rows(is_sample, mp_ref, ms_ref):
    ms = ms_ref[...]
    g, _, d = ms.shape
    s = jnp.broadcast_to(ms, (g, ROW_GROUP, d)).reshape(g * ROW_GROUP, d)
    return jnp.where(is_sample, s, mp_ref[0])


def _cond_specs(piece, tile_rows, tiles_per_batch, batch, n_prompt_tiles, n_sample_tiles):
    g = tile_rows // ROW_GROUP
    mp = pl.BlockSpec((1, 1, D_MODEL),
                      lambda i: (jnp.minimum(i // tiles_per_batch, batch - 1), 0, piece))
    ms = pl.BlockSpec((g, 1, D_MODEL),
                      lambda i: (jnp.clip(i - n_prompt_tiles, 0, n_sample_tiles - 1), 0, piece))
    return [mp, ms]


def _pick_tile(is_sample, prompt_ref, sample_ref):
    v = jnp.where(is_sample, sample_ref[...], prompt_ref[...])
    return v.reshape(TOKEN_TILE, v.shape[-1])


def _load_rows(ref, n_rows):
    return jnp.concatenate([ref[pl.ds(j, n_rows, stride=ROW_TILES), :] for j in range(ROW_TILES)], axis=1)


def _store_rows(ref, x):
    n_rows = x.shape[0]
    for j in range(ROW_TILES):
        ref[pl.ds(j, n_rows, stride=ROW_TILES), :] = x[:, j * LANES:(j + 1) * LANES].astype(ROW_DTYPE)


def _tile_offset(row):
    off = row * ROW_TILES
    return off if isinstance(off, int) else pl.multiple_of(off, ROW_TILES)


def _rows_copy(src_ref, src_row, dst_ref, dst_row, n_rows, sem):
    s = _tile_offset(src_row)
    d = _tile_offset(dst_row)
    return pltpu.make_async_copy(src_ref.at[pl.ds(s, n_rows * ROW_TILES)],
                                 dst_ref.at[pl.ds(d, n_rows * ROW_TILES)], sem)


def _adaln_kernel(c_ref, w_ref, b_ref, o_ref):
    c = c_ref[...]
    s = c * jax.nn.sigmoid(c)
    o_ref[...] = jnp.dot(s, w_ref[...], precision=lax.Precision.HIGHEST,
                         preferred_element_type=F32) + b_ref[...]


def _adaln(c, w_ada, b_ada):
    n, d = c.shape
    cols = w_ada.shape[1]
    tn = 1536
    return pl.pallas_call(
        _adaln_kernel,
        grid=(cols // tn,),
        in_specs=[pl.BlockSpec((n, d), lambda j: (0, 0)),
                  pl.BlockSpec((d, tn), lambda j: (0, j)),
                  pl.BlockSpec((1, tn), lambda j: (0, j))],
        out_specs=pl.BlockSpec((n, tn), lambda j: (0, j)),
        out_shape=jax.ShapeDtypeStruct((n, cols), F32),
        compiler_params=_params(1),
        name="adaln",
    )(c, w_ada, b_ada.reshape(1, cols))


def _stage1_kernel(xp_ref, xs_ref, shp_ref, shs_ref, scp_ref, scs_ref, g1_ref, win_ref, lng_ref, lnb_ref,
                   wm_ref, wms_ref, bs_ref, bss_ref, gsgu_ref,
                   q_ref, k_ref, v_ref, sgu_ref, vn_ref, *, n_prompt_tiles):
    is_sample = pl.program_id(0) >= n_prompt_tiles
    x = _pick_tile(is_sample, xp_ref, xs_ref)
    h = (_rms(x) * g1_ref[...] * (1.0 + _cond_rows(is_sample, scp_ref, scs_ref))
         + _cond_rows(is_sample, shp_ref, shs_ref))
    proj = jnp.dot(h.astype(BF16), win_ref[...], preferred_element_type=F32)
    q_ref[...] = (proj[:, :ATTN_WIDTH] * (HEAD_DIM ** -0.5)).astype(BF16)
    k_ref[...] = proj[:, ATTN_WIDTH:ATTN_WIDTH + KV_WIDTH]
    v_ref[...] = proj[:, ATTN_WIDTH + KV_WIDTH:ATTN_WIDTH + 2 * KV_WIDTH]
    c0 = ATTN_WIDTH + 2 * KV_WIDTH
    u = jax.nn.gelu(proj[:, c0:c0 + SGU_WIDTH])
    vg = jax.nn.gelu(proj[:, c0 + SGU_WIDTH:])
    mu = jnp.mean(vg, axis=-1, keepdims=True)
    xc = vg - mu
    var = jnp.mean(xc * xc, axis=-1, keepdims=True)
    vn = xc * lax.rsqrt(var + EPS) * lng_ref[...] + lnb_ref[...]

    @pl.when(is_sample)
    def _():
        vn_ref[...] = vn

    vnb = vn.astype(BF16)
    wm = jnp.where(is_sample, wms_ref[...], wm_ref[...])
    bs = jnp.where(is_sample, bss_ref[...], bs_ref[...])
    rows = []
    for c in range(TOKEN_TILE // SGU_CHUNK):
        r0 = c * SGU_CHUNK
        cols = []
        for g in range(SGU_GROUPS):
            l0 = g * SGU_CHUNK
            sv = jnp.dot(wm[g], vnb[r0:r0 + SGU_CHUNK, l0:l0 + SGU_CHUNK],
                         preferred_element_type=F32) + bs[g]
            cols.append(sv)
        rows.append(jnp.concatenate(cols, axis=1))
    sgu = u * jnp.concatenate(rows, axis=0)
    sgu_ref[...] = (_rms(sgu) * gsgu_ref[...]).astype(BF16)


def _stage1(xp3, xs3, mod_p, mod_s, g1, win_b, lng, lnb, wm, wms, bs, bss, gsgu, batch, n_prompt_tiles):
    t = (xp3.shape[0] + xs3.shape[0]) * ROW_GROUP
    n_tiles = t // TOKEN_TILE
    last_p = n_prompt_tiles - 1
    x_blk = (TOKEN_TILE // ROW_GROUP, ROW_GROUP, D_MODEL)
    const2 = lambda i: (0, 0)
    const3 = lambda i: (0, 0, 0)
    tile2 = lambda i: (i, 0)
    cond = functools.partial(_cond_specs, tile_rows=TOKEN_TILE, tiles_per_batch=n_prompt_tiles // batch,
                             batch=batch, n_prompt_tiles=n_prompt_tiles, n_sample_tiles=n_tiles - n_prompt_tiles)
    sgu_w_spec = pl.BlockSpec((SGU_GROUPS, SGU_CHUNK, SGU_CHUNK), const3)
    return pl.pallas_call(
        functools.partial(_stage1_kernel, n_prompt_tiles=n_prompt_tiles),
        grid=(n_tiles,),
        in_specs=[pl.BlockSpec(x_blk, lambda i: (jnp.minimum(i, last_p), 0, 0)),
                  pl.BlockSpec(x_blk, const3)]
                 + cond(0) + cond(1)
                 + [pl.BlockSpec((1, D_MODEL), const2),
                    pl.BlockSpec((D_MODEL, IN_COLS), const2),
                    pl.BlockSpec((1, SGU_WIDTH), const2),
                    pl.BlockSpec((1, SGU_WIDTH), const2),
                    sgu_w_spec, sgu_w_spec, sgu_w_spec, sgu_w_spec,
                    pl.BlockSpec((1, SGU_WIDTH), const2)],
        out_specs=[pl.BlockSpec((TOKEN_TILE, ATTN_WIDTH), tile2),
                   pl.BlockSpec((TOKEN_TILE, KV_WIDTH), tile2),
                   pl.BlockSpec((TOKEN_TILE, KV_WIDTH), tile2),
                   pl.BlockSpec((TOKEN_TILE, SGU_WIDTH), tile2),
                   pl.BlockSpec((TOKEN_TILE, SGU_WIDTH), const2)],
        out_shape=[jax.ShapeDtypeStruct((t, ATTN_WIDTH), BF16),
                   jax.ShapeDtypeStruct((t, KV_WIDTH), F32),
                   jax.ShapeDtypeStruct((t, KV_WIDTH), F32),
                   jax.ShapeDtypeStruct((t, SGU_WIDTH), BF16),
                   jax.ShapeDtypeStruct((TOKEN_TILE, SGU_WIDTH), F32)],
        compiler_params=_params(1),
        name="stage1",
    )(xp3, xs3, mod_p, mod_s, mod_p, mod_s, g1, win_b, lng, lnb, wm, wms, bs, bss, gsgu)


def _alibi_slope(h):
    return float(np.float32(2.0) ** np.float32(-8.0 * (h + 1.0) / N_HEADS))


def _sink_attention(q, segments, sinks_ref, gout):
    outs = []
    for h in range(N_HEADS):
        kv0 = (h // GQA_GROUP) * HEAD_DIM
        qh = q[:, h * HEAD_DIM:(h + 1) * HEAD_DIM]
        sink = sinks_ref[h]
        slope = _alibi_slope(h)
        logits = []
        m = jnp.full((q.shape[0], 1), sink, F32)
        for (k, _, dist, valid) in segments:
            l = lax.dot_general(qh, k[:, kv0:kv0 + HEAD_DIM], (((1,), (1,)), ((), ())),
                                preferred_element_type=F32)
            l = l - slope * dist
            if valid is not None:
                l = jnp.where(valid, l, -jnp.inf)
            logits.append(l)
            m = jnp.maximum(m, jnp.max(l, axis=-1, keepdims=True))
        denom = jnp.exp(sink - m)
        acc = None
        for l, (_, v, _, _) in zip(logits, segments):
            e = jnp.exp(l - m)
            denom = denom + jnp.sum(e, axis=-1, keepdims=True)
            pv = jnp.dot(e.astype(BF16), v[:, kv0:kv0 + HEAD_DIM], preferred_element_type=F32)
            acc = pv if acc is None else acc + pv
        outs.append(acc / denom)
    o = jnp.concatenate(outs, axis=1)
    return (_rms(o) * gout).astype(BF16)


def _attn_prompt_kernel(sinks_ref, q_ref, kp_ref, kc_ref, vp_ref, vc_ref, g_ref, o_ref):
    i = pl.program_id(1)
    nq = q_ref.shape[0]
    back = kp_ref.shape[0]
    nk = back + nq
    q_t = q_ref[...].astype(F32).T.astype(BF16)
    k = jnp.concatenate([kp_ref[...], kc_ref[...]], axis=0).astype(BF16)
    v_t = jnp.concatenate([vp_ref[...], vc_ref[...]], axis=0).T.astype(BF16)
    c = lax.broadcasted_iota(I32, (nk, nq), 0)
    r = lax.broadcasted_iota(I32, (nk, nq), 1)
    dist = jnp.abs(r - c + back).astype(F32)
    kc = c // CHUNK
    qc = r // CHUNK + (back // CHUNK)
    valid = jnp.logical_and(jnp.logical_and(kc <= qc, kc >= qc - WINDOW // CHUNK),
                            jnp.logical_or(i > 0, kc >= back // CHUNK))
    outs = []
    for kvh in range(N_KV_HEADS):
        kv0 = kvh * HEAD_DIM
        heads = range(kvh * GQA_GROUP, (kvh + 1) * GQA_GROUP)
        qs = jnp.concatenate([q_t[h * HEAD_DIM:(h + 1) * HEAD_DIM, :] for h in heads], axis=1)
        l = jnp.dot(k[:, kv0:kv0 + HEAD_DIM], qs, preferred_element_type=F32)
        l = jnp.concatenate(
            [jnp.where(valid, l[:, g * nq:(g + 1) * nq] - _alibi_slope(h) * dist, -jnp.inf)
             for g, h in enumerate(heads)], axis=1)
        sink = jnp.concatenate([jnp.full((1, nq), sinks_ref[h], F32) for h in heads], axis=1)
        m = jnp.maximum(jnp.max(l, axis=0, keepdims=True), sink)
        e = jnp.exp(l - m)
        denom = jnp.sum(e, axis=0, keepdims=True) + jnp.exp(sink - m)
        pv = jnp.dot(v_t[kv0:kv0 + HEAD_DIM, :], e.astype(BF16), preferred_element_type=F32)
        o_t = pv / denom
        outs.extend(o_t[:, g * nq:(g + 1) * nq] for g in range(GQA_GROUP))
    o = jnp.concatenate(outs, axis=0).T
    o_ref[...] = (_rms(o) * g_ref[...]).astype(BF16)


def _attn_prompt(q, k, v, sinks, gattn, batch, seq):
    nt = seq // ATTN_TILE
    per_tile = ATTN_TILE // WINDOW
    cur = lambda b, i, s: (b * nt + i, 0)
    prev = lambda b, i, s: (jnp.maximum((b * nt + i) * per_tile - 1, 0), 0)
    kv_blk = (ATTN_TILE, KV_WIDTH)
    back_blk = (WINDOW, KV_WIDTH)
    return pl.pallas_call(
        _attn_prompt_kernel,
        grid_spec=pltpu.PrefetchScalarGridSpec(
            num_scalar_prefetch=1,
            grid=(batch, nt),
            in_specs=[pl.BlockSpec((ATTN_TILE, ATTN_WIDTH), cur),
                      pl.BlockSpec(back_blk, prev), pl.BlockSpec(kv_blk, cur),
                      pl.BlockSpec(back_blk, prev), pl.BlockSpec(kv_blk, cur),
                      pl.BlockSpec((1, ATTN_WIDTH), lambda b, i, s: (0, 0))],
            out_specs=pl.BlockSpec((ATTN_TILE, ATTN_WIDTH), cur)),
        out_shape=jax.ShapeDtypeStruct((batch * seq, ATTN_WIDTH), BF16),
        compiler_params=_params(2),
        name="attn_prompt",
    )(sinks, q, k, k, v, v, gattn)


def _attn_sample_kernel(sinks_ref, q_ref, ck_ref, cv_ref, k_ref, v_ref, g_ref, o_ref):
    nq = q_ref.shape[0]
    w = ck_ref.shape[1]
    r = lax.broadcasted_iota(I32, (nq, w), 0)
    c = lax.broadcasted_iota(I32, (nq, w), 1)
    dist_cache = jnp.abs(r - c + w).astype(F32)
    r2 = lax.broadcasted_iota(I32, (nq, nq), 0)
    c2 = lax.broadcasted_iota(I32, (nq, nq), 1)
    dist_new = jnp.abs(r2 - c2).astype(F32)
    segs = [(ck_ref[0].astype(BF16), cv_ref[0].astype(BF16), dist_cache, None),
            (k_ref[...].astype(BF16), v_ref[...].astype(BF16), dist_new, None)]
    o_ref[...] = _sink_attention(q_ref[...], segs, sinks_ref, g_ref[...])


def _attn_sample(q, k, v, cache_k, cache_v, sinks, gattn, dec_batch, n, row0):
    w = cache_k.shape[1]
    blk0 = row0 // n
    new = lambda b, s: (blk0 + b, 0)
    return pl.pallas_call(
        _attn_sample_kernel,
        grid_spec=pltpu.PrefetchScalarGridSpec(
            num_scalar_prefetch=1,
            grid=(dec_batch,),
            in_specs=[pl.BlockSpec((n, ATTN_WIDTH), new),
                      pl.BlockSpec((1, w, KV_WIDTH), lambda b, s: (b, 0, 0)),
                      pl.BlockSpec((1, w, KV_WIDTH), lambda b, s: (b, 0, 0)),
                      pl.BlockSpec((n, KV_WIDTH), new),
                      pl.BlockSpec((n, KV_WIDTH), new),
                      pl.BlockSpec((1, ATTN_WIDTH), lambda b, s: (0, 0))],
            out_specs=pl.BlockSpec((n, ATTN_WIDTH), lambda b, s: (b, 0))),
        out_shape=jax.ShapeDtypeStruct((dec_batch * n, ATTN_WIDTH), BF16),
        compiler_params=_params(1),
        name="attn_sample",
    )(sinks, q, cache_k, cache_v, k, v, gattn)


def _merge_kernel(ap_ref, as_ref, sgu_ref, xp_ref, xs_ref,
                  g1p_ref, g1s_ref, sh2p_ref, sh2s_ref, sc2p_ref, sc2s_ref, g2p_ref, g2s_ref,
                  woa_ref, wos_ref, g2_ref, wrh_ref, wrl_ref, wsgu_ref, wsd_ref,
                  xb_ref, h2_ref, sc_ref, *, n_prompt_tiles):
    is_sample = pl.program_id(0) >= n_prompt_tiles
    x = _pick_tile(is_sample, xp_ref, xs_ref)
    a = jnp.where(is_sample, as_ref[...], ap_ref[...])
    mix = (jnp.dot(a, woa_ref[...], preferred_element_type=F32)
           + jnp.dot(sgu_ref[...], wos_ref[...], preferred_element_type=F32))
    x1 = x + _cond_rows(is_sample, g1p_ref, g1s_ref) * mix
    h2 = (_rms(x1) * g2_ref[...] * (1.0 + _cond_rows(is_sample, sc2p_ref, sc2s_ref))
          + _cond_rows(is_sample, sh2p_ref, sh2s_ref))
    hh = h2.astype(BF16)
    _store_rows(h2_ref, hh)
    hl = (h2 - hh.astype(F32)).astype(BF16)
    logits = (jnp.dot(hh, wrh_ref[...], preferred_element_type=F32)
              + (jnp.dot(hl, wrh_ref[...], preferred_element_type=F32)
                 + jnp.dot(hh, wrl_ref[...], preferred_element_type=F32)))
    sc_ref[...] = jax.nn.sigmoid(logits)
    gu = jnp.dot(hh, wsgu_ref[...], preferred_element_type=F32)
    g = gu[:, :EXPERT_DIM]
    act = (g * jax.nn.sigmoid(g)) * gu[:, EXPERT_DIM:]
    shared = jnp.dot(act.astype(BF16), wsd_ref[...], preferred_element_type=F32)
    xb_ref[...] = x1 + _cond_rows(is_sample, g2p_ref, g2s_ref) * shared


def _merge(attn_p, attn_s, sgu_n, xp3, xs3, mod_p, mod_s, woa, wos, g2, wrh, wrl, wsgu, wsd, batch, n_prompt_tiles):
    t = sgu_n.shape[0]
    n_tiles = t // TOKEN_TILE
    last_p = n_prompt_tiles - 1
    x_blk = (TOKEN_TILE // ROW_GROUP, ROW_GROUP, D_MODEL)
    const2 = lambda i: (0, 0)
    const3 = lambda i: (0, 0, 0)
    tile2 = lambda i: (i, 0)
    cond = functools.partial(_cond_specs, tile_rows=TOKEN_TILE, tiles_per_batch=n_prompt_tiles // batch,
                             batch=batch, n_prompt_tiles=n_prompt_tiles, n_sample_tiles=n_tiles - n_prompt_tiles)
    return pl.pallas_call(
        functools.partial(_merge_kernel, n_prompt_tiles=n_prompt_tiles),
        grid=(n_tiles,),
        in_specs=[pl.BlockSpec((TOKEN_TILE, ATTN_WIDTH), lambda i: (jnp.minimum(i, last_p), 0)),
                  pl.BlockSpec((TOKEN_TILE, ATTN_WIDTH), const2),
                  pl.BlockSpec((TOKEN_TILE, SGU_WIDTH), tile2),
                  pl.BlockSpec(x_blk, lambda i: (jnp.minimum(i, last_p), 0, 0)),
                  pl.BlockSpec(x_blk, const3)]
                 + cond(2) + cond(3) + cond(4) + cond(5)
                 + [pl.BlockSpec((ATTN_WIDTH, D_MODEL), const2),
                    pl.BlockSpec((SGU_WIDTH, D_MODEL), const2),
                    pl.BlockSpec((1, D_MODEL), const2),
                    pl.BlockSpec((D_MODEL, N_EXPERTS), const2),
                    pl.BlockSpec((D_MODEL, N_EXPERTS), const2),
                    pl.BlockSpec((D_MODEL, 2 * EXPERT_DIM), const2),
                    pl.BlockSpec((EXPERT_DIM, D_MODEL), const2)],
        out_specs=[pl.BlockSpec((TOKEN_TILE, D_MODEL), tile2),
                   pl.BlockSpec((TOKEN_TILE * ROW_TILES, LANES), tile2),
                   pl.BlockSpec((TOKEN_TILE, N_EXPERTS), tile2)],
        out_shape=[jax.ShapeDtypeStruct((t, D_MODEL), F32),
                   jax.ShapeDtypeStruct((t * ROW_TILES, LANES), ROW_DTYPE),
                   jax.ShapeDtypeStruct((t, N_EXPERTS), F32)],
        compiler_params=_params(1),
        name="merge",
    )(attn_p, attn_s, sgu_n, xp3, xs3, *([mod_p, mod_s] * 4), woa, wos, g2, wrh, wrl, wsgu, wsd)


def _first_argmax(x, iota, size, axis):
    m = jnp.max(x, axis=axis, keepdims=True)
    idx = jnp.min(jnp.where(x == m, iota, size), axis=axis, keepdims=True)
    return m, idx


def _route_kernel(sc_ref, bias_ref, tri_ref, eidx_ref, wts_ref, rank_ref, cnt_ref, base_ref):
    tt = sc_ref.shape[0]

    @pl.when(pl.program_id(0) == 0)
    def _():
        base_ref[...] = jnp.zeros_like(base_ref)

    s_t = sc_ref[...].T
    sel = s_t + bias_ref[...]
    sel3 = sel.reshape(N_EXPERT_GROUPS, GROUP_SIZE, tt)
    io3 = lax.broadcasted_iota(I32, sel3.shape, 1)
    m1, i1 = _first_argmax(sel3, io3, GROUP_SIZE, 1)
    m2 = jnp.max(jnp.where(io3 == i1, -jnp.inf, sel3), axis=1, keepdims=True)
    gs = (m1 + m2).reshape(N_EXPERT_GROUPS, tt)
    io8 = lax.broadcasted_iota(I32, gs.shape, 0)
    gmask = jnp.zeros(gs.shape, jnp.bool_)
    for _ in range(TOPK_GROUPS):
        _, gi = _first_argmax(gs, io8, N_EXPERT_GROUPS, 0)
        hit = io8 == gi
        gmask = jnp.logical_or(gmask, hit)
        gs = jnp.where(hit, -jnp.inf, gs)
    emask = jnp.broadcast_to(gmask.reshape(N_EXPERT_GROUPS, 1, tt), sel3.shape).reshape(N_EXPERTS, tt)
    cand = jnp.where(emask, sel, -jnp.inf)
    io = lax.broadcasted_iota(I32, cand.shape, 0)
    chosen = jnp.zeros(cand.shape, jnp.bool_)
    eidx, wts = [], []
    for _ in range(TOP_K):
        _, ei = _first_argmax(cand, io, N_EXPERTS, 0)
        hit = io == ei
        eidx.append(ei)
        wts.append(jnp.sum(jnp.where(hit, s_t, 0.0), axis=0, keepdims=True))
        chosen = jnp.logical_or(chosen, hit)
        cand = jnp.where(hit, -jnp.inf, cand)
    w = jnp.concatenate(wts, axis=0)
    wts_ref[...] = w / jnp.sum(w, axis=0, keepdims=True) * ROUTED_SCALE
    eidx_ref[...] = jnp.concatenate(eidx, axis=0)
    cf = jnp.where(chosen, 1.0, 0.0)
    ahead = jnp.dot(cf.astype(BF16), tri_ref[...], preferred_element_type=F32) + base_ref[...]
    ranks = [jnp.sum(jnp.where(io == ei, ahead, 0.0), axis=0, keepdims=True) for ei in eidx]
    rank_ref[...] = jnp.concatenate(ranks, axis=0).astype(I32)
    total = base_ref[...] + jnp.sum(cf, axis=1, keepdims=True)
    base_ref[...] = total
    cnt_ref[...] = total.astype(I32)


def _route(scores, bias_col, tri):
    t = scores.shape[0]
    n_tiles = t // TOKEN_TILE
    col = lambda i: (0, i)
    return pl.pallas_call(
        _route_kernel,
        grid=(n_tiles,),
        in_specs=[pl.BlockSpec((TOKEN_TILE, N_EXPERTS), lambda i: (i, 0)),
                  pl.BlockSpec((N_EXPERTS, 1), lambda i: (0, 0)),
                  pl.BlockSpec((TOKEN_TILE, TOKEN_TILE), lambda i: (0, 0))],
        out_specs=[pl.BlockSpec((TOP_K, TOKEN_TILE), col),
                   pl.BlockSpec((TOP_K, TOKEN_TILE), col),
                   pl.BlockSpec((TOP_K, TOKEN_TILE), col),
                   pl.BlockSpec((N_EXPERTS, 1), lambda i: (0, 0))],
        out_shape=[jax.ShapeDtypeStruct((TOP_K, t), I32),
                   jax.ShapeDtypeStruct((TOP_K, t), F32),
                   jax.ShapeDtypeStruct((TOP_K, t), I32),
                   jax.ShapeDtypeStruct((N_EXPERTS, 1), I32)],
        scratch_shapes=[pltpu.VMEM((N_EXPERTS, 1), F32)],
        compiler_params=_params(1),
        name="route",
    )(scores, bias_col, tri)


def _pos_kernel(eidx_ref, rank_ref, start_ref, pos_ref):
    e = eidx_ref[...]
    tt = e.shape[1]
    io = lax.broadcasted_iota(I32, (N_EXPERTS, tt), 0)
    st = start_ref[...]
    rows = [jnp.sum(jnp.where(io == e[k:k + 1, :], st, 0), axis=0, keepdims=True) for k in range(TOP_K)]
    pos_ref[...] = jnp.concatenate(rows, axis=0) + rank_ref[...]


def _positions(eidx, rank, start_col):
    t = eidx.shape[1]
    col = lambda i: (0, i)
    blk = pl.BlockSpec((TOP_K, TOKEN_TILE), col)
    return pl.pallas_call(
        _pos_kernel,
        grid=(t // TOKEN_TILE,),
        in_specs=[blk, blk, pl.BlockSpec((N_EXPERTS, 1), lambda i: (0, 0))],
        out_specs=blk,
        out_shape=jax.ShapeDtypeStruct((TOP_K, t), I32),
        compiler_params=_params(1),
        name="positions",
    )(eidx, rank, start_col)


def _dispatch_kernel(pos_ref, h_ref, hall_ref, xs_ref, zero_ref, sem, zsem, *, n_rows):
    tt = h_ref.shape[0] // ROW_TILES
    i = pl.program_id(0)
    first = i == 0

    @pl.when(first)
    def _():
        zero_ref[...] = jnp.zeros_like(zero_ref)
        _rows_copy(zero_ref, 0, xs_ref, n_rows, EXPERT_BLOCK, zsem).start()

    base = i * tt
    for t in range(tt):
        for k in range(TOP_K):
            if k < TOP_K // 2:
                _rows_copy(h_ref, t, xs_ref, pos_ref[k, t], 1, sem).start(priority=k % 2)
            else:
                _rows_copy(hall_ref, base + t, xs_ref, pos_ref[k, t], 1, sem).start(priority=k % 2)
    for k in range(TOP_K):
        _rows_copy(h_ref, 0, xs_ref, 0, tt, sem).wait()

    @pl.when(first)
    def _():
        _rows_copy(zero_ref, 0, xs_ref, n_rows, EXPERT_BLOCK, zsem).wait()


def _dispatch(pos, h2_tiles, n_rows):
    t = h2_tiles.shape[0] // ROW_TILES
    return pl.pallas_call(
        functools.partial(_dispatch_kernel, n_rows=n_rows),
        grid=(t // TOKEN_TILE,),
        in_specs=[pl.BlockSpec((TOP_K, TOKEN_TILE), lambda i: (0, i), memory_space=pltpu.SMEM),
                  pl.BlockSpec((TOKEN_TILE * ROW_TILES, LANES), lambda i: (i, 0)),
                  pl.BlockSpec(memory_space=pl.ANY)],
        out_specs=pl.BlockSpec(memory_space=pl.ANY),
        out_shape=jax.ShapeDtypeStruct(((n_rows + EXPERT_BLOCK) * ROW_TILES, LANES), ROW_DTYPE),
        scratch_shapes=[pltpu.VMEM((EXPERT_BLOCK * ROW_TILES, LANES), ROW_DTYPE),
                        pltpu.SemaphoreType.DMA, pltpu.SemaphoreType.DMA],
        compiler_params=_params(1),
        name="dispatch",
    )(pos, h2_tiles, h2_tiles)


def _out_pieces(nvalid, fn):
    s = 1 << (EXPERT_BLOCK.bit_length() - 1)
    while s >= 1:
        @pl.when((nvalid & s) != 0)
        def _(s=s):
            fn(nvalid & ~(2 * s - 1), s)
        s //= 2


def _for_block_size(nvalid, fn):
    lo = 0
    for rows in BLOCK_SIZES:
        @pl.when(jnp.logical_and(nvalid > lo, nvalid <= rows))
        def _(rows=rows):
            fn(rows)
        lo = rows


def _expert_kernel(count_ref, blk0_ref, row_ref, cnt_ref, xs_ref, wg_ref, wu_ref, wd_ref, ys_ref,
                   xbuf, ybuf, wgu_s, wd_s, state, xsem, ysem):
    e = pl.program_id(0)
    cnt = count_ref[e]
    g0 = blk0_ref[e]
    nblk = (cnt + EXPERT_BLOCK - 1) // EXPERT_BLOCK

    def x_copy(row, slot, rows):
        return _rows_copy(xs_ref, row, xbuf.at[slot], 0, rows, xsem.at[slot])

    def x_fetch(g, slot):
        _for_block_size(cnt_ref[g], lambda rows: x_copy(row_ref[g], slot, rows).start())

    def y_wait(nvalid, slot):
        def wait(off, size):
            _rows_copy(ybuf.at[slot], 0, ys_ref, 0, size, ysem.at[slot]).wait()
        _out_pieces(nvalid, wait)

    @pl.when(e == 0)
    def _():
        state[0] = 0
        state[1] = 0
        for g in range(X_SLOTS - 1):
            x_fetch(g, g)

    @pl.when(cnt > 0)
    def _():
        wgu_s[:, :EXPERT_DIM] = wg_ref[0].astype(BF16)
        wgu_s[:, EXPERT_DIM:] = wu_ref[0].astype(BF16)
        wd_s[...] = wd_ref[0].astype(BF16)

    def compute(rows, xslot, yslot, row0, nvalid):
        x_copy(row0, xslot, rows).wait()
        x = _load_rows(xbuf.at[xslot], rows).astype(BF16)
        gu = jnp.dot(x, wgu_s[...], preferred_element_type=F32)
        g = gu[:, :EXPERT_DIM]
        act = (g * jax.nn.sigmoid(g)) * gu[:, EXPERT_DIM:]
        y = jnp.dot(act.astype(BF16), wd_s[...], preferred_element_type=F32)
        y_wait(state[0], 1 - yslot)
        _store_rows(ybuf.at[yslot], y)

        def put(off, size):
            _rows_copy(ybuf.at[yslot], off, ys_ref, row0 + off, size, ysem.at[yslot]).start()
        _out_pieces(nvalid, put)

    def block(j, carry):
        g = g0 + j
        xslot = state[1]
        yslot = g & 1
        row0 = row_ref[g]
        nvalid = cnt_ref[g]
        x_fetch(g + X_SLOTS - 1, jnp.where(xslot == 0, X_SLOTS - 1, xslot - 1))
        _for_block_size(nvalid, lambda rows: compute(rows, xslot, yslot, row0, nvalid))
        state[0] = nvalid
        state[1] = jnp.where(xslot == X_SLOTS - 1, 0, xslot + 1)
        return carry

    lax.fori_loop(0, nblk, block, 0)

    @pl.when(e == pl.num_programs(0) - 1)
    def _():
        y_wait(state[0], (g0 + nblk - 1) & 1)


def _experts(count, blk0, blk_row, blk_cnt, xs, w_gate, w_up, w_down, n_rows):
    w_map = lambda e, c, b, r, n: (e, 0, 0)
    blk_rows = EXPERT_BLOCK * ROW_TILES
    return pl.pallas_call(
        _expert_kernel,
        grid_spec=pltpu.PrefetchScalarGridSpec(
            num_scalar_prefetch=4,
            grid=(N_EXPERTS,),
            in_specs=[pl.BlockSpec(memory_space=pl.ANY),
                      pl.BlockSpec((1, D_MODEL, EXPERT_DIM), w_map),
                      pl.BlockSpec((1, D_MODEL, EXPERT_DIM), w_map),
                      pl.BlockSpec((1, EXPERT_DIM, D_MODEL), w_map)],
            out_specs=pl.BlockSpec(memory_space=pl.ANY),
            scratch_shapes=[pltpu.VMEM((X_SLOTS, blk_rows, LANES), ROW_DTYPE),
                            pltpu.VMEM((2, blk_rows, LANES), ROW_DTYPE),
                            pltpu.VMEM((D_MODEL, 2 * EXPERT_DIM), BF16),
                            pltpu.VMEM((EXPERT_DIM, D_MODEL), BF16),
                            pltpu.SMEM((2,), I32),
                            pltpu.SemaphoreType.DMA((X_SLOTS,)),
                            pltpu.SemaphoreType.DMA((2,))]),
        out_shape=jax.ShapeDtypeStruct((n_rows * ROW_TILES, LANES), ROW_DTYPE),
        compiler_params=_params(1),
        name="experts",
    )(count, blk0, blk_row, blk_cnt, xs, w_gate, w_up, w_down)


def _combine_kernel(pos_ref, posn_ref, ys_ref, xb_ref, wts_ref, g2p_ref, g2s_ref, gf_ref, yp_ref, ysm_ref,
                    buf, ytile, sem, *, n_prompt_tiles, n_tiles):
    i = pl.program_id(0)
    tt = xb_ref.shape[0]
    slot = i & 1
    is_sample = i >= n_prompt_tiles

    def fetch(p_ref, s, t):
        for k in range(TOP_K):
            _rows_copy(ys_ref, p_ref[k, t], buf.at[s, k], t, 1, sem.at[s]).start(priority=k % 2)

    @pl.when(i == 0)
    def _():
        for t in range(tt):
            fetch(pos_ref, 0, t)

    for k in range(TOP_K):
        _rows_copy(ys_ref, 0, buf.at[slot, k], 0, tt, sem.at[slot]).wait()

    def finish(s, g):
        r0 = g * COMBINE_GROUP
        w = wts_ref[r0:r0 + COMBINE_GROUP, :]
        routed = None
        for k in range(TOP_K):
            rows = _load_rows(buf.at[s, k, pl.ds(r0 * ROW_TILES, COMBINE_GROUP * ROW_TILES)], COMBINE_GROUP)
            term = w[:, k:k + 1] * rows
            routed = term if routed is None else routed + term
        gate = jnp.where(is_sample, g2s_ref[g * COMBINE_GROUP // ROW_GROUP], g2p_ref[0])
        x = xb_ref[r0:r0 + COMBINE_GROUP, :] + gate * routed
        return _rms(x) * gf_ref[...]

    def tile_work(s, fetch_next):
        for g in range(tt // COMBINE_GROUP):
            y = finish(s, g)
            if fetch_next:
                for r in range(COMBINE_GROUP):
                    fetch(posn_ref, 1 - s, g * COMBINE_GROUP + r)
            ytile[g * COMBINE_GROUP:(g + 1) * COMBINE_GROUP, :] = y

    for s in range(2):
        @pl.when(jnp.logical_and(slot == s, i + 1 < n_tiles))
        def _(s=s):
            tile_work(s, True)

    @pl.when(i + 1 == n_tiles)
    def _():
        tile_work((n_tiles - 1) % 2, False)

    @pl.when(i < n_prompt_tiles)
    def _():
        yp_ref[...] = ytile[...]

    @pl.when(i >= n_prompt_tiles)
    def _():
        ysm_ref[...] = ytile[...]


def _combine(pos, ys, xb, wts_t, mod_p, mod_s, gf, batch, n_prompt_rows):
    t = xb.shape[0]
    tt = COMBINE_TILE
    n_tiles = t // tt
    n_p = n_prompt_rows // tt
    pos_spec = lambda f: pl.BlockSpec((TOP_K, tt), f, memory_space=pltpu.SMEM)
    return pl.pallas_call(
        functools.partial(_combine_kernel, n_prompt_tiles=n_p, n_tiles=n_tiles),
        grid=(n_tiles,),
        in_specs=[pos_spec(lambda i: (0, i)),
                  pos_spec(lambda i: (0, jnp.minimum(i + 1, n_tiles - 1))),
                  pl.BlockSpec(memory_space=pl.ANY),
                  pl.BlockSpec((tt, D_MODEL), lambda i: (i, 0)),
                  pl.BlockSpec((tt, TOP_K), lambda i: (i, 0))]
                 + _cond_specs(5, tt, n_p // batch, batch, n_p, n_tiles - n_p)
                 + [pl.BlockSpec((1, D_MODEL), lambda i: (0, 0))],
        out_specs=[pl.BlockSpec((tt, D_MODEL), lambda i: (jnp.minimum(i, n_p - 1), 0)),
                   pl.BlockSpec((tt, D_MODEL), lambda i: (jnp.maximum(i - n_p, 0), 0))],
        out_shape=[jax.ShapeDtypeStruct((n_prompt_rows, D_MODEL), F32),
                   jax.ShapeDtypeStruct((t - n_prompt_rows, D_MODEL), F32)],
        scratch_shapes=[pltpu.VMEM((2, TOP_K, tt * ROW_TILES, LANES), ROW_DTYPE),
                        pltpu.VMEM((tt, D_MODEL), F32),
                        pltpu.SemaphoreType.DMA((2,))],
        compiler_params=_params(1),
        name="combine",
    )(pos, pos, ys, xb, wts_t, mod_p, mod_s, gf)


def _sgu_mask():
    pos = np.arange(SGU_CHUNK) // CHUNK
    return jnp.asarray(pos[None, :] <= pos[:, None])


def _layer(xp3, xs3, cache_k, cache_v, mod, final_g, batch, seq, dec_batch, n, p):
    tp = batch * seq
    t = tp + dec_batch * n
    n_prompt_tiles = tp // TOKEN_TILE
    mod_p = mod[:batch].reshape(batch, 1, N_MOD * D_MODEL)
    mod_s = mod[batch:].reshape(dec_batch, 1, N_MOD * D_MODEL)

    wm = jnp.where(_sgu_mask(), p["sgu_w"], 0.0)
    reps = SGU_CHUNK // n
    eye = jnp.eye(reps, dtype=F32)
    wms = jnp.stack([jnp.kron(eye, wm[g, :n, :n]) for g in range(SGU_GROUPS)])
    bs = jnp.broadcast_to(p["sgu_b"][:, :, None], (SGU_GROUPS, SGU_CHUNK, SGU_CHUNK))
    bss = jnp.broadcast_to(jnp.tile(p["sgu_b"][:, :n], (1, reps))[:, :, None], (SGU_GROUPS, SGU_CHUNK, SGU_CHUNK))

    q, k, v, sgu_n, vn_s = _stage1(
        xp3, xs3, mod_p, mod_s, p["norm1_g"].reshape(1, -1), p["w_in"].astype(BF16),
        p["sgu_ln_g"].reshape(1, -1), p["sgu_ln_b"].reshape(1, -1),
        wm.astype(BF16), wms.astype(BF16), bs, bss, p["sgu_out_g"].reshape(1, -1), batch, n_prompt_tiles)

    gattn = p["attn_out_g"].reshape(1, -1)
    attn_p = _attn_prompt(q, k, v, p["attn_sinks"], gattn, batch, seq)
    w = cache_k.shape[1]
    attn_s = _attn_sample(q, k, v, cache_k.reshape(dec_batch, w, KV_WIDTH), cache_v.reshape(dec_batch, w, KV_WIDTH),
                          p["attn_sinks"], gattn, dec_batch, n, tp)

    wo = p["w_out"].astype(BF16)
    wsgu = jnp.concatenate([p["ws_gate"], p["ws_up"]], axis=1).astype(BF16)
    wrh = p["w_router"].astype(BF16)
    wrl = (p["w_router"] - wrh.astype(F32)).astype(BF16)
    xb, h2_tiles, scores = _merge(attn_p, attn_s, sgu_n, xp3, xs3, mod_p, mod_s,
                                  wo[:ATTN_WIDTH], wo[ATTN_WIDTH:], p["norm2_g"].reshape(1, -1), wrh, wrl,
                                  wsgu, p["ws_down"].astype(BF16), batch, n_prompt_tiles)

    ti = np.arange(TOKEN_TILE)
    tri = jnp.asarray(ti[:, None] < ti[None, :], dtype=BF16)
    eidx, wts, rank, counts = _route(scores, p["router_bias"].reshape(N_EXPERTS, 1), tri)

    counts = counts[:, 0]
    ends = jnp.cumsum(counts)
    start = (ends - counts).astype(I32)
    nblk = (counts + EXPERT_BLOCK - 1) // EXPERT_BLOCK
    blk_end = jnp.cumsum(nblk)
    blk0 = (blk_end - nblk).astype(I32)
    pos = _positions(eidx, rank, start.reshape(N_EXPERTS, 1))

    n_rows = t * TOP_K
    g = jnp.arange(n_rows // EXPERT_BLOCK + N_EXPERTS + X_SLOTS, dtype=I32)[:, None]
    mine = jnp.logical_and(g >= blk0[None, :], g < blk_end[None, :])
    off = (g - blk0[None, :]) * EXPERT_BLOCK
    blk_row = jnp.sum(jnp.where(mine, start[None, :] + off, 0), axis=1).astype(I32)
    blk_cnt = jnp.sum(jnp.where(mine, jnp.minimum(counts[None, :] - off, EXPERT_BLOCK), 0), axis=1).astype(I32)

    xs_sorted = _dispatch(pos, h2_tiles, n_rows)
    ys_sorted = _experts(counts.astype(I32), blk0, blk_row, blk_cnt, xs_sorted,
                         p["w_gate"], p["w_up"], p["w_down"], n_rows)
    y_p, y_s = _combine(pos, ys_sorted, xb, wts.T, mod_p, mod_s, final_g.reshape(1, -1), batch, tp)
    return y_p, y_s, k, v, vn_s


def kernel(x_prompt, x_sample, cache_k, cache_v, c_prompt, c_sample, norm1_g, w_ada, b_ada, w_in, sgu_ln_g, sgu_ln_b, sgu_w, sgu_b, attn_sinks, attn_out_g, sgu_out_g, w_out, norm2_g, w_router, router_bias, w_gate, w_up, w_down, ws_gate, ws_up, ws_down, final_g):
    batch, seq, d = x_prompt.shape
    dec_batch, n, _ = x_sample.shape
    depth = norm1_g.shape[0]
    assert depth == 1 and d == D_MODEL
    assert seq % TOKEN_TILE == 0 and dec_batch * n == TOKEN_TILE and n == ROW_GROUP
    assert seq % ATTN_TILE == 0 and seq >= WINDOW and cache_k.shape[2] == WINDOW
    tp = batch * seq

    xp3 = x_prompt.reshape(tp // ROW_GROUP, ROW_GROUP, d)
    xs3 = x_sample.reshape(dec_batch * n // ROW_GROUP, ROW_GROUP, d)
    c_all = jnp.concatenate([c_prompt, c_sample], axis=0)
    l = 0
    mod = _adaln(c_all, w_ada[l], b_ada[l])
    p = dict(norm1_g=norm1_g[l], w_in=w_in[l], sgu_ln_g=sgu_ln_g[l], sgu_ln_b=sgu_ln_b[l], sgu_w=sgu_w[l],
             sgu_b=sgu_b[l], attn_sinks=attn_sinks[l], attn_out_g=attn_out_g[l], sgu_out_g=sgu_out_g[l],
             w_out=w_out[l], norm2_g=norm2_g[l], w_router=w_router[l], router_bias=router_bias[l],
             w_gate=w_gate[l], w_up=w_up[l], w_down=w_down[l], ws_gate=ws_gate[l], ws_up=ws_up[l],
             ws_down=ws_down[l])
    y_p, y_s, k, v, vn_s = _layer(xp3, xs3, cache_k[l], cache_v[l], mod, final_g, batch, seq, dec_batch, n, p)

    keep = min(WINDOW, seq)
    kv_shape = (batch, keep, N_KV_HEADS, HEAD_DIM)
    k_p = jnp.stack([k[(b + 1) * seq - keep:(b + 1) * seq] for b in range(batch)]).reshape(kv_shape)
    v_p = jnp.stack([v[(b + 1) * seq - keep:(b + 1) * seq] for b in range(batch)]).reshape(kv_shape)
    k_s = k[tp:].reshape(dec_batch, n, N_KV_HEADS, HEAD_DIM)
    v_s = v[tp:].reshape(dec_batch, n, N_KV_HEADS, HEAD_DIM)
    return (y_p.reshape(batch, seq, d), y_s.reshape(dec_batch, n, d),
            k_p[None], v_p[None], k_s[None], v_s[None], vn_s.reshape(dec_batch, n, SGU_WIDTH)[None])
```

```python
import functools

import numpy as np
import jax
import jax.numpy as jnp
from jax import lax
from jax.experimental import pallas as pl
from jax.experimental.pallas import tpu as pltpu

F32 = jnp.float32
BF16 = jnp.bfloat16
I32 = jnp.int32

D_MODEL = 1024
CHUNK = 64
HEAD_DIM = 64
ATTN_WIDTH = 512
N_HEADS = 8
N_KV_HEADS = 2
GQA_GROUP = 4
KV_WIDTH = 128
WINDOW = 128
SGU_CHUNK = 128
SGU_WIDTH = 512
SGU_GROUPS = 4
IN_COLS = ATTN_WIDTH + 2 * KV_WIDTH + 2 * SGU_WIDTH
N_MOD = 6
N_EXPERTS = 256
TOP_K = 8
N_EXPERT_GROUPS = 8
GROUP_SIZE = N_EXPERTS // N_EXPERT_GROUPS
TOPK_GROUPS = 4
EXPERT_DIM = 256
ROUTED_SCALE = 2.5
EPS = 1e-6

LANES = 128
ROW_WORDS = D_MODEL // 2
ROW_TILES = ROW_WORDS // LANES
ROW_DTYPE = jnp.uint32
ROW_GROUP = 16
TOKEN_TILE = 256
ATTN_TILE = 256
EXPERT_BLOCK = 640
BLOCK_SIZES = (128, 256, 384, 512, 576, EXPERT_BLOCK)
X_SLOTS = 3
COMBINE_TILE = 128
COMBINE_GROUP = 8
VMEM_LIMIT_BYTES = 56 * 1024 * 1024


def _params(n_axes=1):
    return pltpu.CompilerParams(dimension_semantics=("arbitrary",) * n_axes,
                                vmem_limit_bytes=VMEM_LIMIT_BYTES)


def _rms(x):
    return x * lax.rsqrt(jnp.mean(x * x, axis=-1, keepdims=True) + EPS)


def _cond_rows(is_sample, mp_ref, ms_ref):
    ms = ms_ref[...]
    g, _, d = ms.shape
    s = jnp.broadcast_to(ms, (g, ROW_GROUP, d)).reshape(g * ROW_GROUP, d)
    return jnp.where(is_sample, s, mp_ref[0])


def _cond_specs(piece, tile_rows, tiles_per_batch, batch, n_prompt_tiles, n_sample_tiles):
    g = tile_rows // ROW_GROUP
    mp = pl.BlockSpec((1, 1, D_MODEL),
                      lambda i: (jnp.minimum(i // tiles_per_batch, batch - 1), 0, piece))
    ms = pl.BlockSpec((g, 1, D_MODEL),
                      lambda i: (jnp.clip(i - n_prompt_tiles, 0, n_sample_tiles - 1), 0, piece))
    return [mp, ms]


def _pick_tile(is_sample, prompt_ref, sample_ref):
    v = jnp.where(is_sample, sample_ref[...], prompt_ref[...])
    return v.reshape(TOKEN_TILE, v.shape[-1])


def _pack_rows(x):
    x = x.astype(F32)
    return pltpu.pack_elementwise([x[:, :ROW_WORDS], x[:, ROW_WORDS:]], packed_dtype=BF16)


def _unpack_rows(w):
    return jnp.concatenate([pltpu.unpack_elementwise(w, index=i, packed_dtype=BF16, unpacked_dtype=F32)
                            for i in range(2)], axis=1)


def _load_rows(ref, n_rows):
    w = jnp.concatenate([ref[pl.ds(j, n_rows, stride=ROW_TILES), :] for j in range(ROW_TILES)], axis=1)
    return _unpack_rows(w)


def _store_rows(ref, x):
    n_rows = x.shape[0]
    w = _pack_rows(x)
    for j in range(ROW_TILES):
        ref[pl.ds(j, n_rows, stride=ROW_TILES), :] = w[:, j * LANES:(j + 1) * LANES]


def _tile_offset(row):
    off = row * ROW_TILES
    return off if isinstance(off, int) else pl.multiple_of(off, ROW_TILES)


def _rows_copy(src_ref, src_row, dst_ref, dst_row, n_rows, sem):
    s = _tile_offset(src_row)
    d = _tile_offset(dst_row)
    return pltpu.make_async_copy(src_ref.at[pl.ds(s, n_rows * ROW_TILES)],
                                 dst_ref.at[pl.ds(d, n_rows * ROW_TILES)], sem)


def _adaln_kernel(c_ref, w_ref, b_ref, o_ref):
    c = c_ref[...]
    s = c * jax.nn.sigmoid(c)
    o_ref[...] = jnp.dot(s, w_ref[...], precision=lax.Precision.HIGHEST,
                         preferred_element_type=F32) + b_ref[...]


def _adaln(c, w_ada, b_ada):
    n, d = c.shape
    cols = w_ada.shape[1]
    tn = 1536
    return pl.pallas_call(
        _adaln_kernel,
        grid=(cols // tn,),
        in_specs=[pl.BlockSpec((n, d), lambda j: (0, 0)),
                  pl.BlockSpec((d, tn), lambda j: (0, j)),
                  pl.BlockSpec((1, tn), lambda j: (0, j))],
        out_specs=pl.BlockSpec((n, tn), lambda j: (0, j)),
        out_shape=jax.ShapeDtypeStruct((n, cols), F32),
        compiler_params=_params(1),
        name="adaln",
    )(c, w_ada, b_ada.reshape(1, cols))


def _stage1_kernel(xp_ref, xs_ref, shp_ref, shs_ref, scp_ref, scs_ref, g1_ref, win_ref, lng_ref, lnb_ref,
                   wm_ref, wms_ref, bs_ref, bss_ref, gsgu_ref,
                   q_ref, k_ref, v_ref, sgu_ref, vn_ref, *, n_prompt_tiles):
    is_sample = pl.program_id(0) >= n_prompt_tiles
    x = _pick_tile(is_sample, xp_ref, xs_ref)
    h = (_rms(x) * g1_ref[...] * (1.0 + _cond_rows(is_sample, scp_ref, scs_ref))
         + _cond_rows(is_sample, shp_ref, shs_ref))
    proj = jnp.dot(h.astype(BF16), win_ref[...], preferred_element_type=F32)
    q_ref[...] = (proj[:, :ATTN_WIDTH] * (HEAD_DIM ** -0.5)).astype(BF16)
    k_ref[...] = proj[:, ATTN_WIDTH:ATTN_WIDTH + KV_WIDTH]
    v_ref[...] = proj[:, ATTN_WIDTH + KV_WIDTH:ATTN_WIDTH + 2 * KV_WIDTH]
    c0 = ATTN_WIDTH + 2 * KV_WIDTH
    u = jax.nn.gelu(proj[:, c0:c0 + SGU_WIDTH])
    vg = jax.nn.gelu(proj[:, c0 + SGU_WIDTH:])
    mu = jnp.mean(vg, axis=-1, keepdims=True)
    xc = vg - mu
    var = jnp.mean(xc * xc, axis=-1, keepdims=True)
    vn = xc * lax.rsqrt(var + EPS) * lng_ref[...] + lnb_ref[...]

    @pl.when(is_sample)
    def _():
        vn_ref[...] = vn

    vnb = vn.astype(BF16)
    wm = jnp.where(is_sample, wms_ref[...], wm_ref[...])
    bs = jnp.where(is_sample, bss_ref[...], bs_ref[...])
    rows = []
    for c in range(TOKEN_TILE // SGU_CHUNK):
        r0 = c * SGU_CHUNK
        cols = []
        for g in range(SGU_GROUPS):
            l0 = g * SGU_CHUNK
            sv = jnp.dot(wm[g], vnb[r0:r0 + SGU_CHUNK, l0:l0 + SGU_CHUNK],
                         preferred_element_type=F32) + bs[g]
            cols.append(sv)
        rows.append(jnp.concatenate(cols, axis=1))
    sgu = u * jnp.concatenate(rows, axis=0)
    sgu_ref[...] = (_rms(sgu) * gsgu_ref[...]).astype(BF16)


def _stage1(xp3, xs3, mod_p, mod_s, g1, win_b, lng, lnb, wm, wms, bs, bss, gsgu, batch, n_prompt_tiles):
    t = (xp3.shape[0] + xs3.shape[0]) * ROW_GROUP
    n_tiles = t // TOKEN_TILE
    last_p = n_prompt_tiles - 1
    x_blk = (TOKEN_TILE // ROW_GROUP, ROW_GROUP, D_MODEL)
    const2 = lambda i: (0, 0)
    const3 = lambda i: (0, 0, 0)
    tile2 = lambda i: (i, 0)
    cond = functools.partial(_cond_specs, tile_rows=TOKEN_TILE, tiles_per_batch=n_prompt_tiles // batch,
                             batch=batch, n_prompt_tiles=n_prompt_tiles, n_sample_tiles=n_tiles - n_prompt_tiles)
    sgu_w_spec = pl.BlockSpec((SGU_GROUPS, SGU_CHUNK, SGU_CHUNK), const3)
    return pl.pallas_call(
        functools.partial(_stage1_kernel, n_prompt_tiles=n_prompt_tiles),
        grid=(n_tiles,),
        in_specs=[pl.BlockSpec(x_blk, lambda i: (jnp.minimum(i, last_p), 0, 0)),
                  pl.BlockSpec(x_blk, const3)]
                 + cond(0) + cond(1)
                 + [pl.BlockSpec((1, D_MODEL), const2),
                    pl.BlockSpec((D_MODEL, IN_COLS), const2),
                    pl.BlockSpec((1, SGU_WIDTH), const2),
                    pl.BlockSpec((1, SGU_WIDTH), const2),
                    sgu_w_spec, sgu_w_spec, sgu_w_spec, sgu_w_spec,
                    pl.BlockSpec((1, SGU_WIDTH), const2)],
        out_specs=[pl.BlockSpec((TOKEN_TILE, ATTN_WIDTH), tile2),
                   pl.BlockSpec((TOKEN_TILE, KV_WIDTH), tile2),
                   pl.BlockSpec((TOKEN_TILE, KV_WIDTH), tile2),
                   pl.BlockSpec((TOKEN_TILE, SGU_WIDTH), tile2),
                   pl.BlockSpec((TOKEN_TILE, SGU_WIDTH), const2)],
        out_shape=[jax.ShapeDtypeStruct((t, ATTN_WIDTH), BF16),
                   jax.ShapeDtypeStruct((t, KV_WIDTH), F32),
                   jax.ShapeDtypeStruct((t, KV_WIDTH), F32),
                   jax.ShapeDtypeStruct((t, SGU_WIDTH), BF16),
                   jax.ShapeDtypeStruct((TOKEN_TILE, SGU_WIDTH), F32)],
        compiler_params=_params(1),
        name="stage1",
    )(xp3, xs3, mod_p, mod_s, mod_p, mod_s, g1, win_b, lng, lnb, wm, wms, bs, bss, gsgu)


def _alibi_slope(h):
    return float(np.float32(2.0) ** np.float32(-8.0 * (h + 1.0) / N_HEADS))


def _sink_attention(q, segments, sinks_ref, gout):
    outs = []
    for h in range(N_HEADS):
        kv0 = (h // GQA_GROUP) * HEAD_DIM
        qh = q[:, h * HEAD_DIM:(h + 1) * HEAD_DIM]
        sink = sinks_ref[h]
        slope = _alibi_slope(h)
        logits = []
        m = jnp.full((q.shape[0], 1), sink, F32)
        for (k, _, dist, valid) in segments:
            l = lax.dot_general(qh, k[:, kv0:kv0 + HEAD_DIM], (((1,), (1,)), ((), ())),
                                preferred_element_type=F32)
            l = l - slope * dist
            if valid is not None:
                l = jnp.where(valid, l, -jnp.inf)
            logits.append(l)
            m = jnp.maximum(m, jnp.max(l, axis=-1, keepdims=True))
        denom = jnp.exp(sink - m)
        acc = None
        for l, (_, v, _, _) in zip(logits, segments):
            e = jnp.exp(l - m)
            denom = denom + jnp.sum(e, axis=-1, keepdims=True)
            pv = jnp.dot(e.astype(BF16), v[:, kv0:kv0 + HEAD_DIM], preferred_element_type=F32)
            acc = pv if acc is None else acc + pv
        outs.append(acc / denom)
    o = jnp.concatenate(outs, axis=1)
    return (_rms(o) * gout).astype(BF16)


def _attn_prompt_kernel(sinks_ref, q_ref, kp_ref, kc_ref, vp_ref, vc_ref, g_ref, o_ref):
    i = pl.program_id(1)
    nq = q_ref.shape[0]
    back = kp_ref.shape[0]
    nk = back + nq
    q_t = q_ref[...].astype(F32).T.astype(BF16)
    k = jnp.concatenate([kp_ref[...], kc_ref[...]], axis=0).astype(BF16)
    v_t = jnp.concatenate([vp_ref[...], vc_ref[...]], axis=0).T.astype(BF16)
    c = lax.broadcasted_iota(I32, (nk, nq), 0)
    r = lax.broadcasted_iota(I32, (nk, nq), 1)
    dist = jnp.abs(r - c + back).astype(F32)
    kc = c // CHUNK
    qc = r // CHUNK + (back // CHUNK)
    valid = jnp.logical_and(jnp.logical_and(kc <= qc, kc >= qc - WINDOW // CHUNK),
                            jnp.logical_or(i > 0, kc >= back // CHUNK))
    outs = []
    for kvh in range(N_KV_HEADS):
        kv0 = kvh * HEAD_DIM
        heads = range(kvh * GQA_GROUP, (kvh + 1) * GQA_GROUP)
        qs = jnp.concatenate([q_t[h * HEAD_DIM:(h + 1) * HEAD_DIM, :] for h in heads], axis=1)
        l = jnp.dot(k[:, kv0:kv0 + HEAD_DIM], qs, preferred_element_type=F32)
        l = jnp.concatenate(
            [jnp.where(valid, l[:, g * nq:(g + 1) * nq] - _alibi_slope(h) * dist, -jnp.inf)
             for g, h in enumerate(heads)], axis=1)
        sink = jnp.concatenate([jnp.full((1, nq), sinks_ref[h], F32) for h in heads], axis=1)
        m = jnp.maximum(jnp.max(l, axis=0, keepdims=True), sink)
        e = jnp.exp(l - m)
        denom = jnp.sum(e, axis=0, keepdims=True) + jnp.exp(sink - m)
        pv = jnp.dot(v_t[kv0:kv0 + HEAD_DIM, :], e.astype(BF16), preferred_element_type=F32)
        o_t = pv / denom
        outs.extend(o_t[:, g * nq:(g + 1) * nq] for g in range(GQA_GROUP))
    o = jnp.concatenate(outs, axis=0).T
    o_ref[...] = (_rms(o) * g_ref[...]).astype(BF16)


def _attn_prompt(q, k, v, sinks, gattn, batch, seq):
    nt = seq // ATTN_TILE
    per_tile = ATTN_TILE // WINDOW
    cur = lambda b, i, s: (b * nt + i, 0)
    prev = lambda b, i, s: (jnp.maximum((b * nt + i) * per_tile - 1, 0), 0)
    kv_blk = (ATTN_TILE, KV_WIDTH)
    back_blk = (WINDOW, KV_WIDTH)
    return pl.pallas_call(
        _attn_prompt_kernel,
        grid_spec=pltpu.PrefetchScalarGridSpec(
            num_scalar_prefetch=1,
            grid=(batch, nt),
            in_specs=[pl.BlockSpec((ATTN_TILE, ATTN_WIDTH), cur),
                      pl.BlockSpec(back_blk, prev), pl.BlockSpec(kv_blk, cur),
                      pl.BlockSpec(back_blk, prev), pl.BlockSpec(kv_blk, cur),
                      pl.BlockSpec((1, ATTN_WIDTH), lambda b, i, s: (0, 0))],
            out_specs=pl.BlockSpec((ATTN_TILE, ATTN_WIDTH), cur)),
        out_shape=jax.ShapeDtypeStruct((batch * seq, ATTN_WIDTH), BF16),
        compiler_params=_params(2),
        name="attn_prompt",
    )(sinks, q, k, k, v, v, gattn)


def _attn_sample_kernel(sinks_ref, q_ref, ck_ref, cv_ref, k_ref, v_ref, g_ref, o_ref):
    nq = q_ref.shape[0]
    w = ck_ref.shape[1]
    r = lax.broadcasted_iota(I32, (nq, w), 0)
    c = lax.broadcasted_iota(I32, (nq, w), 1)
    dist_cache = jnp.abs(r - c + w).astype(F32)
    r2 = lax.broadcasted_iota(I32, (nq, nq), 0)
    c2 = lax.broadcasted_iota(I32, (nq, nq), 1)
    dist_new = jnp.abs(r2 - c2).astype(F32)
    segs = [(ck_ref[0].astype(BF16), cv_ref[0].astype(BF16), dist_cache, None),
            (k_ref[...].astype(BF16), v_ref[...].astype(BF16), dist_new, None)]
    o_ref[...] = _sink_attention(q_ref[...], segs, sinks_ref, g_ref[...])


def _attn_sample(q, k, v, cache_k, cache_v, sinks, gattn, dec_batch, n, row0):
    w = cache_k.shape[1]
    blk0 = row0 // n
    new = lambda b, s: (blk0 + b, 0)
    return pl.pallas_call(
        _attn_sample_kernel,
        grid_spec=pltpu.PrefetchScalarGridSpec(
            num_scalar_prefetch=1,
            grid=(dec_batch,),
            in_specs=[pl.BlockSpec((n, ATTN_WIDTH), new),
                      pl.BlockSpec((1, w, KV_WIDTH), lambda b, s: (b, 0, 0)),
                      pl.BlockSpec((1, w, KV_WIDTH), lambda b, s: (b, 0, 0)),
                      pl.BlockSpec((n, KV_WIDTH), new),
                      pl.BlockSpec((n, KV_WIDTH), new),
                      pl.BlockSpec((1, ATTN_WIDTH), lambda b, s: (0, 0))],
            out_specs=pl.BlockSpec((n, ATTN_WIDTH), lambda b, s: (b, 0))),
        out_shape=jax.ShapeDtypeStruct((dec_batch * n, ATTN_WIDTH), BF16),
        compiler_params=_params(1),
        name="attn_sample",
    )(sinks, q, cache_k, cache_v, k, v, gattn)


def _merge_kernel(ap_ref, as_ref, sgu_ref, xp_ref, xs_ref,
                  g1p_ref, g1s_ref, sh2p_ref, sh2s_ref, sc2p_ref, sc2s_ref, g2p_ref, g2s_ref,
                  woa_ref, wos_ref, g2_ref, wrh_ref, wrl_ref, wsgu_ref, wsd_ref,
                  xb_ref, h2_ref, sc_ref, *, n_prompt_tiles):
    is_sample = pl.program_id(0) >= n_prompt_tiles
    x = _pick_tile(is_sample, xp_ref, xs_ref)
    a = jnp.where(is_sample, as_ref[...], ap_ref[...])
    mix = (jnp.dot(a, woa_ref[...], preferred_element_type=F32)
           + jnp.dot(sgu_ref[...], wos_ref[...], preferred_element_type=F32))
    x1 = x + _cond_rows(is_sample, g1p_ref, g1s_ref) * mix
    h2 = (_rms(x1) * g2_ref[...] * (1.0 + _cond_rows(is_sample, sc2p_ref, sc2s_ref))
          + _cond_rows(is_sample, sh2p_ref, sh2s_ref))
    hh = h2.astype(BF16)
    _store_rows(h2_ref, hh)
    hl = (h2 - hh.astype(F32)).astype(BF16)
    logits = (jnp.dot(hh, wrh_ref[...], preferred_element_type=F32)
              + (jnp.dot(hl, wrh_ref[...], preferred_element_type=F32)
                 + jnp.dot(hh, wrl_ref[...], preferred_element_type=F32)))
    sc_ref[...] = jax.nn.sigmoid(logits)
    gu = jnp.dot(hh, wsgu_ref[...], preferred_element_type=F32)
    g = gu[:, :EXPERT_DIM]
    act = (g * jax.nn.sigmoid(g)) * gu[:, EXPERT_DIM:]
    shared = jnp.dot(act.astype(BF16), wsd_ref[...], preferred_element_type=F32)
    xb_ref[...] = x1 + _cond_rows(is_sample, g2p_ref, g2s_ref) * shared


def _merge(attn_p, attn_s, sgu_n, xp3, xs3, mod_p, mod_s, woa, wos, g2, wrh, wrl, wsgu, wsd, batch, n_prompt_tiles):
    t = sgu_n.shape[0]
    n_tiles = t // TOKEN_TILE
    last_p = n_prompt_tiles - 1
    x_blk = (TOKEN_TILE // ROW_GROUP, ROW_GROUP, D_MODEL)
    const2 = lambda i: (0, 0)
    const3 = lambda i: (0, 0, 0)
    tile2 = lambda i: (i, 0)
    cond = functools.partial(_cond_specs, tile_rows=TOKEN_TILE, tiles_per_batch=n_prompt_tiles // batch,
                             batch=batch, n_prompt_tiles=n_prompt_tiles, n_sample_tiles=n_tiles - n_prompt_tiles)
    return pl.pallas_call(
        functools.partial(_merge_kernel, n_prompt_tiles=n_prompt_tiles),
        grid=(n_tiles,),
        in_specs=[pl.BlockSpec((TOKEN_TILE, ATTN_WIDTH), lambda i: (jnp.minimum(i, last_p), 0)),
                  pl.BlockSpec((TOKEN_TILE, ATTN_WIDTH), const2),
                  pl.BlockSpec((TOKEN_TILE, SGU_WIDTH), tile2),
                  pl.BlockSpec(x_blk, lambda i: (jnp.minimum(i, last_p), 0, 0)),
                  pl.BlockSpec(x_blk, const3)]
                 + cond(2) + cond(3) + cond(4) + cond(5)
                 + [pl.BlockSpec((ATTN_WIDTH, D_MODEL), const2),
                    pl.BlockSpec((SGU_WIDTH, D_MODEL), const2),
                    pl.BlockSpec((1, D_MODEL), const2),
                    pl.BlockSpec((D_MODEL, N_EXPERTS), const2),
                    pl.BlockSpec((D_MODEL, N_EXPERTS), const2),
                    pl.BlockSpec((D_MODEL, 2 * EXPERT_DIM), const2),
                    pl.BlockSpec((EXPERT_DIM, D_MODEL), const2)],
        out_specs=[pl.BlockSpec((TOKEN_TILE, D_MODEL), tile2),
                   pl.BlockSpec((TOKEN_TILE * ROW_TILES, LANES), tile2),
                   pl.BlockSpec((TOKEN_TILE, N_EXPERTS), tile2)],
        out_shape=[jax.ShapeDtypeStruct((t, D_MODEL), F32),
                   jax.ShapeDtypeStruct((t * ROW_TILES, LANES), ROW_DTYPE),
                   jax.ShapeDtypeStruct((t, N_EXPERTS), F32)],
        compiler_params=_params(1),
        name="merge",
    )(attn_p, attn_s, sgu_n, xp3, xs3, *([mod_p, mod_s] * 4), woa, wos, g2, wrh, wrl, wsgu, wsd)


def _first_argmax(x, iota, size, axis):
    m = jnp.max(x, axis=axis, keepdims=True)
    idx = jnp.min(jnp.where(x == m, iota, size), axis=axis, keepdims=True)
    return m, idx


def _route_kernel(sc_ref, bias_ref, tri_ref, eidx_ref, wts_ref, rank_ref, cnt_ref, base_ref):
    tt = sc_ref.shape[0]

    @pl.when(pl.program_id(0) == 0)
    def _():
        base_ref[...] = jnp.zeros_like(base_ref)

    s_t = sc_ref[...].T
    sel = s_t + bias_ref[...]
    sel3 = sel.reshape(N_EXPERT_GROUPS, GROUP_SIZE, tt)
    io3 = lax.broadcasted_iota(I32, sel3.shape, 1)
    m1, i1 = _first_argmax(sel3, io3, GROUP_SIZE, 1)
    m2 = jnp.max(jnp.where(io3 == i1, -jnp.inf, sel3), axis=1, keepdims=True)
    gs = (m1 + m2).reshape(N_EXPERT_GROUPS, tt)
    io8 = lax.broadcasted_iota(I32, gs.shape, 0)
    gmask = jnp.zeros(gs.shape, jnp.bool_)
    for _ in range(TOPK_GROUPS):
        _, gi = _first_argmax(gs, io8, N_EXPERT_GROUPS, 0)
        hit = io8 == gi
        gmask = jnp.logical_or(gmask, hit)
        gs = jnp.where(hit, -jnp.inf, gs)
    emask = jnp.broadcast_to(gmask.reshape(N_EXPERT_GROUPS, 1, tt), sel3.shape).reshape(N_EXPERTS, tt)
    cand = jnp.where(emask, sel, -jnp.inf)
    io = lax.broadcasted_iota(I32, cand.shape, 0)
    chosen = jnp.zeros(cand.shape, jnp.bool_)
    eidx, wts = [], []
    for _ in range(TOP_K):
        _, ei = _first_argmax(cand, io, N_EXPERTS, 0)
        hit = io == ei
        eidx.append(ei)
        wts.append(jnp.sum(jnp.where(hit, s_t, 0.0), axis=0, keepdims=True))
        chosen = jnp.logical_or(chosen, hit)
        cand = jnp.where(hit, -jnp.inf, cand)
    w = jnp.concatenate(wts, axis=0)
    wts_ref[...] = w / jnp.sum(w, axis=0, keepdims=True) * ROUTED_SCALE
    eidx_ref[...] = jnp.concatenate(eidx, axis=0)
    cf = jnp.where(chosen, 1.0, 0.0)
    ahead = jnp.dot(cf.astype(BF16), tri_ref[...], preferred_element_type=F32) + base_ref[...]
    ranks = [jnp.sum(jnp.where(io == ei, ahead, 0.0), axis=0, keepdims=True) for ei in eidx]
    rank_ref[...] = jnp.concatenate(ranks, axis=0).astype(I32)
    total = base_ref[...] + jnp.sum(cf, axis=1, keepdims=True)
    base_ref[...] = total
    cnt_ref[...] = total.astype(I32)


def _route(scores, bias_col, tri):
    t = scores.shape[0]
    n_tiles = t // TOKEN_TILE
    col = lambda i: (0, i)
    return pl.pallas_call(
        _route_kernel,
        grid=(n_tiles,),
        in_specs=[pl.BlockSpec((TOKEN_TILE, N_EXPERTS), lambda i: (i, 0)),
                  pl.BlockSpec((N_EXPERTS, 1), lambda i: (0, 0)),
                  pl.BlockSpec((TOKEN_TILE, TOKEN_TILE), lambda i: (0, 0))],
        out_specs=[pl.BlockSpec((TOP_K, TOKEN_TILE), col),
                   pl.BlockSpec((TOP_K, TOKEN_TILE), col),
                   pl.BlockSpec((TOP_K, TOKEN_TILE), col),
                   pl.BlockSpec((N_EXPERTS, 1), lambda i: (0, 0))],
        out_shape=[jax.ShapeDtypeStruct((TOP_K, t), I32),
                   jax.ShapeDtypeStruct((TOP_K, t), F32),
                   jax.ShapeDtypeStruct((TOP_K, t), I32),
                   jax.ShapeDtypeStruct((N_EXPERTS, 1), I32)],
        scratch_shapes=[pltpu.VMEM((N_EXPERTS, 1), F32)],
        compiler_params=_params(1),
        name="route",
    )(scores, bias_col, tri)


def _pos_kernel(eidx_ref, rank_ref, start_ref, pos_ref):
    e = eidx_ref[...]
    tt = e.shape[1]
    io = lax.broadcasted_iota(I32, (N_EXPERTS, tt), 0)
    st = start_ref[...]
    rows = [jnp.sum(jnp.where(io == e[k:k + 1, :], st, 0), axis=0, keepdims=True) for k in range(TOP_K)]
    pos_ref[...] = jnp.concatenate(rows, axis=0) + rank_ref[...]


def _positions(eidx, rank, start_col):
    t = eidx.shape[1]
    col = lambda i: (0, i)
    blk = pl.BlockSpec((TOP_K, TOKEN_TILE), col)
    return pl.pallas_call(
        _pos_kernel,
        grid=(t // TOKEN_TILE,),
        in_specs=[blk, blk, pl.BlockSpec((N_EXPERTS, 1), lambda i: (0, 0))],
        out_specs=blk,
        out_shape=jax.ShapeDtypeStruct((TOP_K, t), I32),
        compiler_params=_params(1),
        name="positions",
    )(eidx, rank, start_col)


def _dispatch_kernel(pos_ref, h_ref, xs_ref, zero_ref, sem, zsem, *, n_rows):
    tt = h_ref.shape[0] // ROW_TILES
    first = pl.program_id(0) == 0

    @pl.when(first)
    def _():
        zero_ref[...] = jnp.zeros_like(zero_ref)
        _rows_copy(zero_ref, 0, xs_ref, n_rows, EXPERT_BLOCK, zsem).start()

    for t in range(tt):
        for k in range(TOP_K):
            _rows_copy(h_ref, t, xs_ref, pos_ref[k, t], 1, sem).start(priority=k % 2)
    for k in range(TOP_K):
        _rows_copy(h_ref, 0, xs_ref, 0, tt, sem).wait()

    @pl.when(first)
    def _():
        _rows_copy(zero_ref, 0, xs_ref, n_rows, EXPERT_BLOCK, zsem).wait()


def _dispatch(pos, h2_tiles, n_rows):
    t = h2_tiles.shape[0] // ROW_TILES
    return pl.pallas_call(
        functools.partial(_dispatch_kernel, n_rows=n_rows),
        grid=(t // TOKEN_TILE,),
        in_specs=[pl.BlockSpec((TOP_K, TOKEN_TILE), lambda i: (0, i), memory_space=pltpu.SMEM),
                  pl.BlockSpec((TOKEN_TILE * ROW_TILES, LANES), lambda i: (i, 0))],
        out_specs=pl.BlockSpec(memory_space=pl.ANY),
        out_shape=jax.ShapeDtypeStruct(((n_rows + EXPERT_BLOCK) * ROW_TILES, LANES), ROW_DTYPE),
        scratch_shapes=[pltpu.VMEM((EXPERT_BLOCK * ROW_TILES, LANES), ROW_DTYPE),
                        pltpu.SemaphoreType.DMA, pltpu.SemaphoreType.DMA],
        compiler_params=_params(1),
        name="dispatch",
    )(pos, h2_tiles)


def _out_pieces(nvalid, fn):
    s = 1 << (EXPERT_BLOCK.bit_length() - 1)
    while s >= 1:
        @pl.when((nvalid & s) != 0)
        def _(s=s):
            fn(nvalid & ~(2 * s - 1), s)
        s //= 2


def _for_block_size(nvalid, fn):
    lo = 0
    for rows in BLOCK_SIZES:
        @pl.when(jnp.logical_and(nvalid > lo, nvalid <= rows))
        def _(rows=rows):
            fn(rows)
        lo = rows


def _expert_kernel(count_ref, blk0_ref, row_ref, cnt_ref, xs_ref, wg_ref, wu_ref, wd_ref, ys_ref,
                   xbuf, ybuf, wgu_s, wd_s, state, xsem, ysem):
    e = pl.program_id(0)
    cnt = count_ref[e]
    g0 = blk0_ref[e]
    nblk = (cnt + EXPERT_BLOCK - 1) // EXPERT_BLOCK

    def x_copy(row, slot, rows):
        return _rows_copy(xs_ref, row, xbuf.at[slot], 0, rows, xsem.at[slot])

    def x_fetch(g, slot):
        _for_block_size(cnt_ref[g], lambda rows: x_copy(row_ref[g], slot, rows).start())

    def y_wait(nvalid, slot):
        def wait(off, size):
            _rows_copy(ybuf.at[slot], 0, ys_ref, 0, size, ysem.at[slot]).wait()
        _out_pieces(nvalid, wait)

    @pl.when(e == 0)
    def _():
        state[0] = 0
        state[1] = 0
        for g in range(X_SLOTS - 1):
            x_fetch(g, g)

    @pl.when(cnt > 0)
    def _():
        wgu_s[:, :EXPERT_DIM] = wg_ref[0].astype(BF16)
        wgu_s[:, EXPERT_DIM:] = wu_ref[0].astype(BF16)
        wd_s[...] = wd_ref[0].astype(BF16)

    def compute(rows, xslot, yslot, row0, nvalid):
        x_copy(row0, xslot, rows).wait()
        x = _load_rows(xbuf.at[xslot], rows).astype(BF16)
        gu = jnp.dot(x, wgu_s[...], preferred_element_type=F32)
        g = gu[:, :EXPERT_DIM]
        act = (g * jax.nn.sigmoid(g)) * gu[:, EXPERT_DIM:]
        y = jnp.dot(act.astype(BF16), wd_s[...], preferred_element_type=F32)
        y_wait(state[0], 1 - yslot)
        _store_rows(ybuf.at[yslot], y)

        def put(off, size):
            _rows_copy(ybuf.at[yslot], off, ys_ref, row0 + off, size, ysem.at[yslot]).start()
        _out_pieces(nvalid, put)

    def block(j, carry):
        g = g0 + j
        xslot = state[1]
        yslot = g & 1
        row0 = row_ref[g]
        nvalid = cnt_ref[g]
        x_fetch(g + X_SLOTS - 1, jnp.where(xslot == 0, X_SLOTS - 1, xslot - 1))
        _for_block_size(nvalid, lambda rows: compute(rows, xslot, yslot, row0, nvalid))
        state[0] = nvalid
        state[1] = jnp.where(xslot == X_SLOTS - 1, 0, xslot + 1)
        return carry

    lax.fori_loop(0, nblk, block, 0)

    @pl.when(e == pl.num_programs(0) - 1)
    def _():
        y_wait(state[0], (g0 + nblk - 1) & 1)


def _experts(count, blk0, blk_row, blk_cnt, xs, w_gate, w_up, w_down, n_rows):
    w_map = lambda e, c, b, r, n: (e, 0, 0)
    blk_rows = EXPERT_BLOCK * ROW_TILES
    return pl.pallas_call(
        _expert_kernel,
        grid_spec=pltpu.PrefetchScalarGridSpec(
            num_scalar_prefetch=4,
            grid=(N_EXPERTS,),
            in_specs=[pl.BlockSpec(memory_space=pl.ANY),
                      pl.BlockSpec((1, D_MODEL, EXPERT_DIM), w_map),
                      pl.BlockSpec((1, D_MODEL, EXPERT_DIM), w_map),
                      pl.BlockSpec((1, EXPERT_DIM, D_MODEL), w_map)],
            out_specs=pl.BlockSpec(memory_space=pl.ANY),
            scratch_shapes=[pltpu.VMEM((X_SLOTS, blk_rows, LANES), ROW_DTYPE),
                            pltpu.VMEM((2, blk_rows, LANES), ROW_DTYPE),
                            pltpu.VMEM((D_MODEL, 2 * EXPERT_DIM), BF16),
                            pltpu.VMEM((EXPERT_DIM, D_MODEL), BF16),
                            pltpu.SMEM((2,), I32),
                            pltpu.SemaphoreType.DMA((X_SLOTS,)),
                            pltpu.SemaphoreType.DMA((2,))]),
        out_shape=jax.ShapeDtypeStruct((n_rows * ROW_TILES, LANES), ROW_DTYPE),
        compiler_params=_params(1),
        name="experts",
    )(count, blk0, blk_row, blk_cnt, xs, w_gate, w_up, w_down)


def _combine_kernel(pos_ref, posn_ref, ys_ref, xb_ref, wts_ref, g2p_ref, g2s_ref, gf_ref, yp_ref, ysm_ref,
                    buf, ytile, sem, *, n_prompt_tiles, n_tiles):
    i = pl.program_id(0)
    tt = xb_ref.shape[0]
    slot = i & 1
    is_sample = i >= n_prompt_tiles

    def fetch(p_ref, s, t):
        for k in range(TOP_K):
            _rows_copy(ys_ref, p_ref[k, t], buf.at[s, k], t, 1, sem.at[s]).start(priority=k % 2)

    @pl.when(i == 0)
    def _():
        for t in range(tt):
            fetch(pos_ref, 0, t)

    for k in range(TOP_K):
        _rows_copy(ys_ref, 0, buf.at[slot, k], 0, tt, sem.at[slot]).wait()

    def finish(s, g):
        r0 = g * COMBINE_GROUP
        w = wts_ref[r0:r0 + COMBINE_GROUP, :]
        routed = None
        for k in range(TOP_K):
            rows = _load_rows(buf.at[s, k, pl.ds(r0 * ROW_TILES, COMBINE_GROUP * ROW_TILES)], COMBINE_GROUP)
            term = w[:, k:k + 1] * rows
            routed = term if routed is None else routed + term
        gate = jnp.where(is_sample, g2s_ref[g * COMBINE_GROUP // ROW_GROUP], g2p_ref[0])
        x = xb_ref[r0:r0 + COMBINE_GROUP, :] + gate * routed
        return _rms(x) * gf_ref[...]

    def tile_work(s, fetch_next):
        for g in range(tt // COMBINE_GROUP):
            y = finish(s, g)
            if fetch_next:
                for r in range(COMBINE_GROUP):
                    fetch(posn_ref, 1 - s, g * COMBINE_GROUP + r)
            ytile[g * COMBINE_GROUP:(g + 1) * COMBINE_GROUP, :] = y

    for s in range(2):
        @pl.when(jnp.logical_and(slot == s, i + 1 < n_tiles))
        def _(s=s):
            tile_work(s, True)

    @pl.when(i + 1 == n_tiles)
    def _():
        tile_work((n_tiles - 1) % 2, False)

    @pl.when(i < n_prompt_tiles)
    def _():
        yp_ref[...] = ytile[...]

    @pl.when(i >= n_prompt_tiles)
    def _():
        ysm_ref[...] = ytile[...]


def _combine(pos, ys, xb, wts_t, mod_p, mod_s, gf, batch, n_prompt_rows):
    t = xb.shape[0]
    tt = COMBINE_TILE
    n_tiles = t // tt
    n_p = n_prompt_rows // tt
    pos_spec = lambda f: pl.BlockSpec((TOP_K, tt), f, memory_space=pltpu.SMEM)
    return pl.pallas_call(
        functools.partial(_combine_kernel, n_prompt_tiles=n_p, n_tiles=n_tiles),
        grid=(n_tiles,),
        in_specs=[pos_spec(lambda i: (0, i)),
                  pos_spec(lambda i: (0, jnp.minimum(i + 1, n_tiles - 1))),
                  pl.BlockSpec(memory_space=pl.ANY),
                  pl.BlockSpec((tt, D_MODEL), lambda i: (i, 0)),
                  pl.BlockSpec((tt, TOP_K), lambda i: (i, 0))]
                 + _cond_specs(5, tt, n_p // batch, batch, n_p, n_tiles - n_p)
                 + [pl.BlockSpec((1, D_MODEL), lambda i: (0, 0))],
        out_specs=[pl.BlockSpec((tt, D_MODEL), lambda i: (jnp.minimum(i, n_p - 1), 0)),
                   pl.BlockSpec((tt, D_MODEL), lambda i: (jnp.maximum(i - n_p, 0), 0))],
        out_shape=[jax.ShapeDtypeStruct((n_prompt_rows, D_MODEL), F32),
                   jax.ShapeDtypeStruct((t - n_prompt_rows, D_MODEL), F32)],
        scratch_shapes=[pltpu.VMEM((2, TOP_K, tt * ROW_TILES, LANES), ROW_DTYPE),
                        pltpu.VMEM((tt, D_MODEL), F32),
                        pltpu.SemaphoreType.DMA((2,))],
        compiler_params=_params(1),
        name="combine",
    )(pos, pos, ys, xb, wts_t, mod_p, mod_s, gf)


def _sgu_mask():
    pos = np.arange(SGU_CHUNK) // CHUNK
    return jnp.asarray(pos[None, :] <= pos[:, None])


def _layer(xp3, xs3, cache_k, cache_v, mod, final_g, batch, seq, dec_batch, n, p):
    tp = batch * seq
    t = tp + dec_batch * n
    n_prompt_tiles = tp // TOKEN_TILE
    mod_p = mod[:batch].reshape(batch, 1, N_MOD * D_MODEL)
    mod_s = mod[batch:].reshape(dec_batch, 1, N_MOD * D_MODEL)

    wm = jnp.where(_sgu_mask(), p["sgu_w"], 0.0)
    reps = SGU_CHUNK // n
    eye = jnp.eye(reps, dtype=F32)
    wms = jnp.stack([jnp.kron(eye, wm[g, :n, :n]) for g in range(SGU_GROUPS)])
    bs = jnp.broadcast_to(p["sgu_b"][:, :, None], (SGU_GROUPS, SGU_CHUNK, SGU_CHUNK))
    bss = jnp.broadcast_to(jnp.tile(p["sgu_b"][:, :n], (1, reps))[:, :, None], (SGU_GROUPS, SGU_CHUNK, SGU_CHUNK))

    q, k, v, sgu_n, vn_s = _stage1(
        xp3, xs3, mod_p, mod_s, p["norm1_g"].reshape(1, -1), p["w_in"].astype(BF16),
        p["sgu_ln_g"].reshape(1, -1), p["sgu_ln_b"].reshape(1, -1),
        wm.astype(BF16), wms.astype(BF16), bs, bss, p["sgu_out_g"].reshape(1, -1), batch, n_prompt_tiles)

    gattn = p["attn_out_g"].reshape(1, -1)
    attn_p = _attn_prompt(q, k, v, p["attn_sinks"], gattn, batch, seq)
    w = cache_k.shape[1]
    attn_s = _attn_sample(q, k, v, cache_k.reshape(dec_batch, w, KV_WIDTH), cache_v.reshape(dec_batch, w, KV_WIDTH),
                          p["attn_sinks"], gattn, dec_batch, n, tp)

    wo = p["w_out"].astype(BF16)
    wsgu = jnp.concatenate([p["ws_gate"], p["ws_up"]], axis=1).astype(BF16)
    wrh = p["w_router"].astype(BF16)
    wrl = (p["w_router"] - wrh.astype(F32)).astype(BF16)
    xb, h2_tiles, scores = _merge(attn_p, attn_s, sgu_n, xp3, xs3, mod_p, mod_s,
                                  wo[:ATTN_WIDTH], wo[ATTN_WIDTH:], p["norm2_g"].reshape(1, -1), wrh, wrl,
                                  wsgu, p["ws_down"].astype(BF16), batch, n_prompt_tiles)

    ti = np.arange(TOKEN_TILE)
    tri = jnp.asarray(ti[:, None] < ti[None, :], dtype=BF16)
    eidx, wts, rank, counts = _route(scores, p["router_bias"].reshape(N_EXPERTS, 1), tri)

    counts = counts[:, 0]
    ends = jnp.cumsum(counts)
    start = (ends - counts).astype(I32)
    nblk = (counts + EXPERT_BLOCK - 1) // EXPERT_BLOCK
    blk_end = jnp.cumsum(nblk)
    blk0 = (blk_end - nblk).astype(I32)
    pos = _positions(eidx, rank, start.reshape(N_EXPERTS, 1))

    n_rows = t * TOP_K
    g = jnp.arange(n_rows // EXPERT_BLOCK + N_EXPERTS + X_SLOTS, dtype=I32)[:, None]
    mine = jnp.logical_and(g >= blk0[None, :], g < blk_end[None, :])
    off = (g - blk0[None, :]) * EXPERT_BLOCK
    blk_row = jnp.sum(jnp.where(mine, start[None, :] + off, 0), axis=1).astype(I32)
    blk_cnt = jnp.sum(jnp.where(mine, jnp.minimum(counts[None, :] - off, EXPERT_BLOCK), 0), axis=1).astype(I32)

    xs_sorted = _dispatch(pos, h2_tiles, n_rows)
    ys_sorted = _experts(counts.astype(I32), blk0, blk_row, blk_cnt, xs_sorted,
                         p["w_gate"], p["w_up"], p["w_down"], n_rows)
    y_p, y_s = _combine(pos, ys_sorted, xb, wts.T, mod_p, mod_s, final_g.reshape(1, -1), batch, tp)
    return y_p, y_s, k, v, vn_s


def kernel(x_prompt, x_sample, cache_k, cache_v, c_prompt, c_sample, norm1_g, w_ada, b_ada, w_in, sgu_ln_g, sgu_ln_b, sgu_w, sgu_b, attn_sinks, attn_out_g, sgu_out_g, w_out, norm2_g, w_router, router_bias, w_gate, w_up, w_down, ws_gate, ws_up, ws_down, final_g):
    batch, seq, d = x_prompt.shape
    dec_batch, n, _ = x_sample.shape
    depth = norm1_g.shape[0]
    assert depth == 1 and d == D_MODEL
    assert seq % TOKEN_TILE == 0 and dec_batch * n == TOKEN_TILE and n == ROW_GROUP
    assert seq % ATTN_TILE == 0 and seq >= WINDOW and cache_k.shape[2] == WINDOW
    tp = batch * seq

    xp3 = x_prompt.reshape(tp // ROW_GROUP, ROW_GROUP, d)
    xs3 = x_sample.reshape(dec_batch * n // ROW_GROUP, ROW_GROUP, d)
    c_all = jnp.concatenate([c_prompt, c_sample], axis=0)
    l = 0
    mod = _adaln(c_all, w_ada[l], b_ada[l])
    p = dict(norm1_g=norm1_g[l], w_in=w_in[l], sgu_ln_g=sgu_ln_g[l], sgu_ln_b=sgu_ln_b[l], sgu_w=sgu_w[l],
             sgu_b=sgu_b[l], attn_sinks=attn_sinks[l], attn_out_g=attn_out_g[l], sgu_out_g=sgu_out_g[l],
             w_out=w_out[l], norm2_g=norm2_g[l], w_router=w_router[l], router_bias=router_bias[l],
             w_gate=w_gate[l], w_up=w_up[l], w_down=w_down[l], ws_gate=ws_gate[l], ws_up=ws_up[l],
             ws_down=ws_down[l])
    y_p, y_s, k, v, vn_s = _layer(xp3, xs3, cache_k[l], cache_v[l], mod, final_g, batch, seq, dec_batch, n, p)

    keep = min(WINDOW, seq)
    kv_shape = (batch, keep, N_KV_HEADS, HEAD_DIM)
    k_p = jnp.stack([k[(b + 1) * seq - keep:(b + 1) * seq] for b in range(batch)]).reshape(kv_shape)
    v_p = jnp.stack([v[(b + 1) * seq - keep:(b + 1) * seq] for b in range(batch)]).reshape(kv_shape)
    k_s = k[tp:].reshape(dec_batch, n, N_KV_HEADS, HEAD_DIM)
    v_s = v[tp:].reshape(dec_batch, n, N_KV_HEADS, HEAD_DIM)
    return (y_p.reshape(batch, seq, d), y_s.reshape(dec_batch, n, d),
            k_p[None], v_p[None], k_s[None], v_s[None], vn_s.reshape(dec_batch, n, SGU_WIDTH)[None])
```

```python
import functools

import numpy as np
import jax
import jax.numpy as jnp
from jax import lax
from jax.experimental import pallas as pl
from jax.experimental.pallas import tpu as pltpu

F32 = jnp.float32
BF16 = jnp.bfloat16
I32 = jnp.int32

D_MODEL = 1024
CHUNK = 64
HEAD_DIM = 64
ATTN_WIDTH = 512
N_HEADS = 8
N_KV_HEADS = 2
GQA_GROUP = 4
KV_WIDTH = 128
WINDOW = 128
SGU_CHUNK = 128
SGU_WIDTH = 512
SGU_GROUPS = 4
IN_COLS = ATTN_WIDTH + 2 * KV_WIDTH + 2 * SGU_WIDTH
N_MOD = 6
N_EXPERTS = 256
TOP_K = 8
N_EXPERT_GROUPS = 8
GROUP_SIZE = N_EXPERTS // N_EXPERT_GROUPS
TOPK_GROUPS = 4
EXPERT_DIM = 256
ROUTED_SCALE = 2.5
EPS = 1e-6

LANES = 128
ROW_WORDS = D_MODEL // 2
ROW_TILES = ROW_WORDS // LANES
ROW_DTYPE = jnp.uint32
ROW_GROUP = 16
TOKEN_TILE = 256
ATTN_TILE = 256
EXPERT_BLOCK = 640
BLOCK_SIZES = (128, 256, 384, 512, 576, EXPERT_BLOCK)
X_SLOTS = 3
W_SLOTS = 3
COMBINE_TILE = 128
COMBINE_GROUP = 8
VMEM_LIMIT_BYTES = 56 * 1024 * 1024


def _params(n_axes=1):
    return pltpu.CompilerParams(dimension_semantics=("arbitrary",) * n_axes,
                                vmem_limit_bytes=VMEM_LIMIT_BYTES)


def _rms(x):
    return x * lax.rsqrt(jnp.mean(x * x, axis=-1, keepdims=True) + EPS)


def _cond_rows(is_sample, mp_ref, ms_ref):
    ms = ms_ref[...]
    g, _, d = ms.shape
    s = jnp.broadcast_to(ms, (g, ROW_GROUP, d)).reshape(g * ROW_GROUP, d)
    return jnp.where(is_sample, s, mp_ref[0])


def _cond_specs(piece, tile_rows, tiles_per_batch, batch, n_prompt_tiles, n_sample_tiles):
    g = tile_rows // ROW_GROUP
    mp = pl.BlockSpec((1, 1, D_MODEL),
                      lambda i: (jnp.minimum(i // tiles_per_batch, batch - 1), 0, piece))
    ms = pl.BlockSpec((g, 1, D_MODEL),
                      lambda i: (jnp.clip(i - n_prompt_tiles, 0, n_sample_tiles - 1), 0, piece))
    return [mp, ms]


def _pick_tile(is_sample, prompt_ref, sample_ref):
    v = jnp.where(is_sample, sample_ref[...], prompt_ref[...])
    return v.reshape(TOKEN_TILE, v.shape[-1])


def _pack_rows(x):
    x = x.astype(F32)
    return pltpu.pack_elementwise([x[:, :ROW_WORDS], x[:, ROW_WORDS:]], packed_dtype=BF16)


def _unpack_rows(w):
    return jnp.concatenate([pltpu.unpack_elementwise(w, index=i, packed_dtype=BF16, unpacked_dtype=F32)
                            for i in range(2)], axis=1)


def _load_rows(ref, n_rows):
    w = jnp.concatenate([ref[pl.ds(j, n_rows, stride=ROW_TILES), :] for j in range(ROW_TILES)], axis=1)
    return _unpack_rows(w)


def _store_rows(ref, x):
    n_rows = x.shape[0]
    w = _pack_rows(x)
    for j in range(ROW_TILES):
        ref[pl.ds(j, n_rows, stride=ROW_TILES), :] = w[:, j * LANES:(j + 1) * LANES]


def _tile_offset(row):
    off = row * ROW_TILES
    return off if isinstance(off, int) else pl.multiple_of(off, ROW_TILES)


def _rows_copy(src_ref, src_row, dst_ref, dst_row, n_rows, sem):
    s = _tile_offset(src_row)
    d = _tile_offset(dst_row)
    return pltpu.make_async_copy(src_ref.at[pl.ds(s, n_rows * ROW_TILES)],
                                 dst_ref.at[pl.ds(d, n_rows * ROW_TILES)], sem)


def _adaln_kernel(c_ref, w_ref, b_ref, o_ref):
    c = c_ref[...]
    s = c * jax.nn.sigmoid(c)
    o_ref[...] = jnp.dot(s, w_ref[...], precision=lax.Precision.HIGHEST,
                         preferred_element_type=F32) + b_ref[...]


def _adaln(c, w_ada, b_ada):
    n, d = c.shape
    cols = w_ada.shape[1]
    tn = 1536
    return pl.pallas_call(
        _adaln_kernel,
        grid=(cols // tn,),
        in_specs=[pl.BlockSpec((n, d), lambda j: (0, 0)),
                  pl.BlockSpec((d, tn), lambda j: (0, j)),
                  pl.BlockSpec((1, tn), lambda j: (0, j))],
        out_specs=pl.BlockSpec((n, tn), lambda j: (0, j)),
        out_shape=jax.ShapeDtypeStruct((n, cols), F32),
        compiler_params=_params(1),
        name="adaln",
    )(c, w_ada, b_ada.reshape(1, cols))


def _stage1_kernel(xp_ref, xs_ref, shp_ref, shs_ref, scp_ref, scs_ref, g1_ref, win_ref, lng_ref, lnb_ref,
                   wm_ref, wms_ref, bs_ref, bss_ref, gsgu_ref,
                   q_ref, k_ref, v_ref, sgu_ref, vn_ref, *, n_prompt_tiles):
    is_sample = pl.program_id(0) >= n_prompt_tiles
    x = _pick_tile(is_sample, xp_ref, xs_ref)
    h = (_rms(x) * g1_ref[...] * (1.0 + _cond_rows(is_sample, scp_ref, scs_ref))
         + _cond_rows(is_sample, shp_ref, shs_ref))
    proj = jnp.dot(h.astype(BF16), win_ref[...], preferred_element_type=F32)
    q_ref[...] = (proj[:, :ATTN_WIDTH] * (HEAD_DIM ** -0.5)).astype(BF16)
    k_ref[...] = proj[:, ATTN_WIDTH:ATTN_WIDTH + KV_WIDTH]
    v_ref[...] = proj[:, ATTN_WIDTH + KV_WIDTH:ATTN_WIDTH + 2 * KV_WIDTH]
    c0 = ATTN_WIDTH + 2 * KV_WIDTH
    u = jax.nn.gelu(proj[:, c0:c0 + SGU_WIDTH])
    vg = jax.nn.gelu(proj[:, c0 + SGU_WIDTH:])
    mu = jnp.mean(vg, axis=-1, keepdims=True)
    xc = vg - mu
    var = jnp.mean(xc * xc, axis=-1, keepdims=True)
    vn = xc * lax.rsqrt(var + EPS) * lng_ref[...] + lnb_ref[...]

    @pl.when(is_sample)
    def _():
        vn_ref[...] = vn

    vnb = vn.astype(BF16)
    wm = jnp.where(is_sample, wms_ref[...], wm_ref[...])
    bs = jnp.where(is_sample, bss_ref[...], bs_ref[...])
    rows = []
    for c in range(TOKEN_TILE // SGU_CHUNK):
        r0 = c * SGU_CHUNK
        cols = []
        for g in range(SGU_GROUPS):
            l0 = g * SGU_CHUNK
            sv = jnp.dot(wm[g], vnb[r0:r0 + SGU_CHUNK, l0:l0 + SGU_CHUNK],
                         preferred_element_type=F32) + bs[g]
            cols.append(sv)
        rows.append(jnp.concatenate(cols, axis=1))
    sgu = u * jnp.concatenate(rows, axis=0)
    sgu_ref[...] = (_rms(sgu) * gsgu_ref[...]).astype(BF16)


def _stage1(xp3, xs3, mod_p, mod_s, g1, win_b, lng, lnb, wm, wms, bs, bss, gsgu, batch, n_prompt_tiles):
    t = (xp3.shape[0] + xs3.shape[0]) * ROW_GROUP
    n_tiles = t // TOKEN_TILE
    last_p = n_prompt_tiles - 1
    x_blk = (TOKEN_TILE // ROW_GROUP, ROW_GROUP, D_MODEL)
    const2 = lambda i: (0, 0)
    const3 = lambda i: (0, 0, 0)
    tile2 = lambda i: (i, 0)
    cond = functools.partial(_cond_specs, tile_rows=TOKEN_TILE, tiles_per_batch=n_prompt_tiles // batch,
                             batch=batch, n_prompt_tiles=n_prompt_tiles, n_sample_tiles=n_tiles - n_prompt_tiles)
    sgu_w_spec = pl.BlockSpec((SGU_GROUPS, SGU_CHUNK, SGU_CHUNK), const3)
    return pl.pallas_call(
        functools.partial(_stage1_kernel, n_prompt_tiles=n_prompt_tiles),
        grid=(n_tiles,),
        in_specs=[pl.BlockSpec(x_blk, lambda i: (jnp.minimum(i, last_p), 0, 0)),
                  pl.BlockSpec(x_blk, const3)]
                 + cond(0) + cond(1)
                 + [pl.BlockSpec((1, D_MODEL), const2),
                    pl.BlockSpec((D_MODEL, IN_COLS), const2),
                    pl.BlockSpec((1, SGU_WIDTH), const2),
                    pl.BlockSpec((1, SGU_WIDTH), const2),
                    sgu_w_spec, sgu_w_spec, sgu_w_spec, sgu_w_spec,
                    pl.BlockSpec((1, SGU_WIDTH), const2)],
        out_specs=[pl.BlockSpec((TOKEN_TILE, ATTN_WIDTH), tile2),
                   pl.BlockSpec((TOKEN_TILE, KV_WIDTH), tile2),
                   pl.BlockSpec((TOKEN_TILE, KV_WIDTH), tile2),
                   pl.BlockSpec((TOKEN_TILE, SGU_WIDTH), tile2),
                   pl.BlockSpec((TOKEN_TILE, SGU_WIDTH), const2)],
        out_shape=[jax.ShapeDtypeStruct((t, ATTN_WIDTH), BF16),
                   jax.ShapeDtypeStruct((t, KV_WIDTH), F32),
                   jax.ShapeDtypeStruct((t, KV_WIDTH), F32),
                   jax.ShapeDtypeStruct((t, SGU_WIDTH), BF16),
                   jax.ShapeDtypeStruct((TOKEN_TILE, SGU_WIDTH), F32)],
        compiler_params=_params(1),
        name="stage1",
    )(xp3, xs3, mod_p, mod_s, mod_p, mod_s, g1, win_b, lng, lnb, wm, wms, bs, bss, gsgu)


def _alibi_slope(h):
    return float(np.float32(2.0) ** np.float32(-8.0 * (h + 1.0) / N_HEADS))


def _sink_attention(q, segments, sinks_ref, gout):
    outs = []
    for h in range(N_HEADS):
        kv0 = (h // GQA_GROUP) * HEAD_DIM
        qh = q[:, h * HEAD_DIM:(h + 1) * HEAD_DIM]
        sink = sinks_ref[h]
        slope = _alibi_slope(h)
        logits = []
        m = jnp.full((q.shape[0], 1), sink, F32)
        for (k, _, dist, valid) in segments:
            l = lax.dot_general(qh, k[:, kv0:kv0 + HEAD_DIM], (((1,), (1,)), ((), ())),
                                preferred_element_type=F32)
            l = l - slope * dist
            if valid is not None:
                l = jnp.where(valid, l, -jnp.inf)
            logits.append(l)
            m = jnp.maximum(m, jnp.max(l, axis=-1, keepdims=True))
        denom = jnp.exp(sink - m)
        acc = None
        for l, (_, v, _, _) in zip(logits, segments):
            e = jnp.exp(l - m)
            denom = denom + jnp.sum(e, axis=-1, keepdims=True)
            pv = jnp.dot(e.astype(BF16), v[:, kv0:kv0 + HEAD_DIM], preferred_element_type=F32)
            acc = pv if acc is None else acc + pv
        outs.append(acc / denom)
    o = jnp.concatenate(outs, axis=1)
    return (_rms(o) * gout).astype(BF16)


def _attn_prompt_kernel(sinks_ref, q_ref, kp_ref, kc_ref, vp_ref, vc_ref, g_ref, o_ref):
    i = pl.program_id(1)
    nq = q_ref.shape[0]
    back = kp_ref.shape[0]
    nk = back + nq
    q_t = q_ref[...].astype(F32).T.astype(BF16)
    k = jnp.concatenate([kp_ref[...], kc_ref[...]], axis=0).astype(BF16)
    v_t = jnp.concatenate([vp_ref[...], vc_ref[...]], axis=0).T.astype(BF16)
    c = lax.broadcasted_iota(I32, (nk, nq), 0)
    r = lax.broadcasted_iota(I32, (nk, nq), 1)
    dist = jnp.abs(r - c + back).astype(F32)
    kc = c // CHUNK
    qc = r // CHUNK + (back // CHUNK)
    valid = jnp.logical_and(jnp.logical_and(kc <= qc, kc >= qc - WINDOW // CHUNK),
                            jnp.logical_or(i > 0, kc >= back // CHUNK))
    outs = []
    for kvh in range(N_KV_HEADS):
        kv0 = kvh * HEAD_DIM
        heads = range(kvh * GQA_GROUP, (kvh + 1) * GQA_GROUP)
        qs = jnp.concatenate([q_t[h * HEAD_DIM:(h + 1) * HEAD_DIM, :] for h in heads], axis=1)
        l = jnp.dot(k[:, kv0:kv0 + HEAD_DIM], qs, preferred_element_type=F32)
        l = jnp.concatenate(
            [jnp.where(valid, l[:, g * nq:(g + 1) * nq] - _alibi_slope(h) * dist, -jnp.inf)
             for g, h in enumerate(heads)], axis=1)
        sink = jnp.concatenate([jnp.full((1, nq), sinks_ref[h], F32) for h in heads], axis=1)
        m = jnp.maximum(jnp.max(l, axis=0, keepdims=True), sink)
        e = jnp.exp(l - m)
        denom = jnp.sum(e, axis=0, keepdims=True) + jnp.exp(sink - m)
        pv = jnp.dot(v_t[kv0:kv0 + HEAD_DIM, :], e.astype(BF16), preferred_element_type=F32)
        o_t = pv / denom
        outs.extend(o_t[:, g * nq:(g + 1) * nq] for g in range(GQA_GROUP))
    o = jnp.concatenate(outs, axis=0).T
    o_ref[...] = (_rms(o) * g_ref[...]).astype(BF16)


def _attn_prompt(q, k, v, sinks, gattn, batch, seq):
    nt = seq // ATTN_TILE
    per_tile = ATTN_TILE // WINDOW
    cur = lambda b, i, s: (b * nt + i, 0)
    prev = lambda b, i, s: (jnp.maximum((b * nt + i) * per_tile - 1, 0), 0)
    kv_blk = (ATTN_TILE, KV_WIDTH)
    back_blk = (WINDOW, KV_WIDTH)
    return pl.pallas_call(
        _attn_prompt_kernel,
        grid_spec=pltpu.PrefetchScalarGridSpec(
            num_scalar_prefetch=1,
            grid=(batch, nt),
            in_specs=[pl.BlockSpec((ATTN_TILE, ATTN_WIDTH), cur),
                      pl.BlockSpec(back_blk, prev), pl.BlockSpec(kv_blk, cur),
                      pl.BlockSpec(back_blk, prev), pl.BlockSpec(kv_blk, cur),
                      pl.BlockSpec((1, ATTN_WIDTH), lambda b, i, s: (0, 0))],
            out_specs=pl.BlockSpec((ATTN_TILE, ATTN_WIDTH), cur)),
        out_shape=jax.ShapeDtypeStruct((batch * seq, ATTN_WIDTH), BF16),
        compiler_params=_params(2),
        name="attn_prompt",
    )(sinks, q, k, k, v, v, gattn)


def _attn_sample_kernel(sinks_ref, q_ref, ck_ref, cv_ref, k_ref, v_ref, g_ref, o_ref):
    nq = q_ref.shape[0]
    w = ck_ref.shape[1]
    r = lax.broadcasted_iota(I32, (nq, w), 0)
    c = lax.broadcasted_iota(I32, (nq, w), 1)
    dist_cache = jnp.abs(r - c + w).astype(F32)
    r2 = lax.broadcasted_iota(I32, (nq, nq), 0)
    c2 = lax.broadcasted_iota(I32, (nq, nq), 1)
    dist_new = jnp.abs(r2 - c2).astype(F32)
    segs = [(ck_ref[0].astype(BF16), cv_ref[0].astype(BF16), dist_cache, None),
            (k_ref[...].astype(BF16), v_ref[...].astype(BF16), dist_new, None)]
    o_ref[...] = _sink_attention(q_ref[...], segs, sinks_ref, g_ref[...])


def _attn_sample(q, k, v, cache_k, cache_v, sinks, gattn, dec_batch, n, row0):
    w = cache_k.shape[1]
    blk0 = row0 // n
    new = lambda b, s: (blk0 + b, 0)
    return pl.pallas_call(
        _attn_sample_kernel,
        grid_spec=pltpu.PrefetchScalarGridSpec(
            num_scalar_prefetch=1,
            grid=(dec_batch,),
            in_specs=[pl.BlockSpec((n, ATTN_WIDTH), new),
                      pl.BlockSpec((1, w, KV_WIDTH), lambda b, s: (b, 0, 0)),
                      pl.BlockSpec((1, w, KV_WIDTH), lambda b, s: (b, 0, 0)),
                      pl.BlockSpec((n, KV_WIDTH), new),
                      pl.BlockSpec((n, KV_WIDTH), new),
                      pl.BlockSpec((1, ATTN_WIDTH), lambda b, s: (0, 0))],
            out_specs=pl.BlockSpec((n, ATTN_WIDTH), lambda b, s: (b, 0))),
        out_shape=jax.ShapeDtypeStruct((dec_batch * n, ATTN_WIDTH), BF16),
        compiler_params=_params(1),
        name="attn_sample",
    )(sinks, q, cache_k, cache_v, k, v, gattn)


def _merge_kernel(ap_ref, as_ref, sgu_ref, xp_ref, xs_ref,
                  g1p_ref, g1s_ref, sh2p_ref, sh2s_ref, sc2p_ref, sc2s_ref, g2p_ref, g2s_ref,
                  woa_ref, wos_ref, g2_ref, wrh_ref, wrl_ref, wsgu_ref, wsd_ref,
                  xb_ref, h2_ref, sc_ref, *, n_prompt_tiles):
    is_sample = pl.program_id(0) >= n_prompt_tiles
    x = _pick_tile(is_sample, xp_ref, xs_ref)
    a = jnp.where(is_sample, as_ref[...], ap_ref[...])
    mix = (jnp.dot(a, woa_ref[...], preferred_element_type=F32)
           + jnp.dot(sgu_ref[...], wos_ref[...], preferred_element_type=F32))
    x1 = x + _cond_rows(is_sample, g1p_ref, g1s_ref) * mix
    h2 = (_rms(x1) * g2_ref[...] * (1.0 + _cond_rows(is_sample, sc2p_ref, sc2s_ref))
          + _cond_rows(is_sample, sh2p_ref, sh2s_ref))
    hh = h2.astype(BF16)
    _store_rows(h2_ref, hh)
    hl = (h2 - hh.astype(F32)).astype(BF16)
    logits = (jnp.dot(hh, wrh_ref[...], preferred_element_type=F32)
              + (jnp.dot(hl, wrh_ref[...], preferred_element_type=F32)
                 + jnp.dot(hh, wrl_ref[...], preferred_element_type=F32)))
    sc_ref[...] = jax.nn.sigmoid(logits)
    gu = jnp.dot(hh, wsgu_ref[...], preferred_element_type=F32)
    g = gu[:, :EXPERT_DIM]
    act = (g * jax.nn.sigmoid(g)) * gu[:, EXPERT_DIM:]
    shared = jnp.dot(act.astype(BF16), wsd_ref[...], preferred_element_type=F32)
    xb_ref[...] = x1 + _cond_rows(is_sample, g2p_ref, g2s_ref) * shared


def _merge(attn_p, attn_s, sgu_n, xp3, xs3, mod_p, mod_s, woa, wos, g2, wrh, wrl, wsgu, wsd, batch, n_prompt_tiles):
    t = sgu_n.shape[0]
    n_tiles = t // TOKEN_TILE
    last_p = n_prompt_tiles - 1
    x_blk = (TOKEN_TILE // ROW_GROUP, ROW_GROUP, D_MODEL)
    const2 = lambda i: (0, 0)
    const3 = lambda i: (0, 0, 0)
    tile2 = lambda i: (i, 0)
    cond = functools.partial(_cond_specs, tile_rows=TOKEN_TILE, tiles_per_batch=n_prompt_tiles // batch,
                             batch=batch, n_prompt_tiles=n_prompt_tiles, n_sample_tiles=n_tiles - n_prompt_tiles)
    return pl.pallas_call(
        functools.partial(_merge_kernel, n_prompt_tiles=n_prompt_tiles),
        grid=(n_tiles,),
        in_specs=[pl.BlockSpec((TOKEN_TILE, ATTN_WIDTH), lambda i: (jnp.minimum(i, last_p), 0)),
                  pl.BlockSpec((TOKEN_TILE, ATTN_WIDTH), const2),
                  pl.BlockSpec((TOKEN_TILE, SGU_WIDTH), tile2),
                  pl.BlockSpec(x_blk, lambda i: (jnp.minimum(i, last_p), 0, 0)),
                  pl.BlockSpec(x_blk, const3)]
                 + cond(2) + cond(3) + cond(4) + cond(5)
                 + [pl.BlockSpec((ATTN_WIDTH, D_MODEL), const2),
                    pl.BlockSpec((SGU_WIDTH, D_MODEL), const2),
                    pl.BlockSpec((1, D_MODEL), const2),
                    pl.BlockSpec((D_MODEL, N_EXPERTS), const2),
                    pl.BlockSpec((D_MODEL, N_EXPERTS), const2),
                    pl.BlockSpec((D_MODEL, 2 * EXPERT_DIM), const2),
                    pl.BlockSpec((EXPERT_DIM, D_MODEL), const2)],
        out_specs=[pl.BlockSpec((TOKEN_TILE, D_MODEL), tile2),
                   pl.BlockSpec((TOKEN_TILE * ROW_TILES, LANES), tile2),
                   pl.BlockSpec((TOKEN_TILE, N_EXPERTS), tile2)],
        out_shape=[jax.ShapeDtypeStruct((t, D_MODEL), F32),
                   jax.ShapeDtypeStruct((t * ROW_TILES, LANES), ROW_DTYPE),
                   jax.ShapeDtypeStruct((t, N_EXPERTS), F32)],
        compiler_params=_params(1),
        name="merge",
    )(attn_p, attn_s, sgu_n, xp3, xs3, *([mod_p, mod_s] * 4), woa, wos, g2, wrh, wrl, wsgu, wsd)


def _first_argmax(x, iota, size, axis):
    m = jnp.max(x, axis=axis, keepdims=True)
    idx = jnp.min(jnp.where(x == m, iota, size), axis=axis, keepdims=True)
    return m, idx


def _route_kernel(sc_ref, bias_ref, tri_ref, eidx_ref, wts_ref, rank_ref, cnt_ref, base_ref):
    tt = sc_ref.shape[0]

    @pl.when(pl.program_id(0) == 0)
    def _():
        base_ref[...] = jnp.zeros_like(base_ref)

    s_t = sc_ref[...].T
    sel = s_t + bias_ref[...]
    sel3 = sel.reshape(N_EXPERT_GROUPS, GROUP_SIZE, tt)
    io3 = lax.broadcasted_iota(I32, sel3.shape, 1)
    m1, i1 = _first_argmax(sel3, io3, GROUP_SIZE, 1)
    m2 = jnp.max(jnp.where(io3 == i1, -jnp.inf, sel3), axis=1, keepdims=True)
    gs = (m1 + m2).reshape(N_EXPERT_GROUPS, tt)
    io8 = lax.broadcasted_iota(I32, gs.shape, 0)
    gmask = jnp.zeros(gs.shape, jnp.bool_)
    for _ in range(TOPK_GROUPS):
        _, gi = _first_argmax(gs, io8, N_EXPERT_GROUPS, 0)
        hit = io8 == gi
        gmask = jnp.logical_or(gmask, hit)
        gs = jnp.where(hit, -jnp.inf, gs)
    emask = jnp.broadcast_to(gmask.reshape(N_EXPERT_GROUPS, 1, tt), sel3.shape).reshape(N_EXPERTS, tt)
    cand = jnp.where(emask, sel, -jnp.inf)
    io = lax.broadcasted_iota(I32, cand.shape, 0)
    chosen = jnp.zeros(cand.shape, jnp.bool_)
    eidx, wts = [], []
    for _ in range(TOP_K):
        _, ei = _first_argmax(cand, io, N_EXPERTS, 0)
        hit = io == ei
        eidx.append(ei)
        wts.append(jnp.sum(jnp.where(hit, s_t, 0.0), axis=0, keepdims=True))
        chosen = jnp.logical_or(chosen, hit)
        cand = jnp.where(hit, -jnp.inf, cand)
    w = jnp.concatenate(wts, axis=0)
    wts_ref[...] = w / jnp.sum(w, axis=0, keepdims=True) * ROUTED_SCALE
    eidx_ref[...] = jnp.concatenate(eidx, axis=0)
    cf = jnp.where(chosen, 1.0, 0.0)
    ahead = jnp.dot(cf.astype(BF16), tri_ref[...], preferred_element_type=F32) + base_ref[...]
    ranks = [jnp.sum(jnp.where(io == ei, ahead, 0.0), axis=0, keepdims=True) for ei in eidx]
    rank_ref[...] = jnp.concatenate(ranks, axis=0).astype(I32)
    total = base_ref[...] + jnp.sum(cf, axis=1, keepdims=True)
    base_ref[...] = total
    cnt_ref[...] = total.astype(I32)


def _route(scores, bias_col, tri):
    t = scores.shape[0]
    n_tiles = t // TOKEN_TILE
    col = lambda i: (0, i)
    return pl.pallas_call(
        _route_kernel,
        grid=(n_tiles,),
        in_specs=[pl.BlockSpec((TOKEN_TILE, N_EXPERTS), lambda i: (i, 0)),
                  pl.BlockSpec((N_EXPERTS, 1), lambda i: (0, 0)),
                  pl.BlockSpec((TOKEN_TILE, TOKEN_TILE), lambda i: (0, 0))],
        out_specs=[pl.BlockSpec((TOP_K, TOKEN_TILE), col),
                   pl.BlockSpec((TOP_K, TOKEN_TILE), col),
                   pl.BlockSpec((TOP_K, TOKEN_TILE), col),
                   pl.BlockSpec((N_EXPERTS, 1), lambda i: (0, 0))],
        out_shape=[jax.ShapeDtypeStruct((TOP_K, t), I32),
                   jax.ShapeDtypeStruct((TOP_K, t), F32),
                   jax.ShapeDtypeStruct((TOP_K, t), I32),
                   jax.ShapeDtypeStruct((N_EXPERTS, 1), I32)],
        scratch_shapes=[pltpu.VMEM((N_EXPERTS, 1), F32)],
        compiler_params=_params(1),
        name="route",
    )(scores, bias_col, tri)


def _pos_kernel(eidx_ref, rank_ref, start_ref, pos_ref):
    e = eidx_ref[...]
    tt = e.shape[1]
    io = lax.broadcasted_iota(I32, (N_EXPERTS, tt), 0)
    st = start_ref[...]
    rows = [jnp.sum(jnp.where(io == e[k:k + 1, :], st, 0), axis=0, keepdims=True) for k in range(TOP_K)]
    pos_ref[...] = jnp.concatenate(rows, axis=0) + rank_ref[...]


def _positions(eidx, rank, start_col):
    t = eidx.shape[1]
    col = lambda i: (0, i)
    blk = pl.BlockSpec((TOP_K, TOKEN_TILE), col)
    return pl.pallas_call(
        _pos_kernel,
        grid=(t // TOKEN_TILE,),
        in_specs=[blk, blk, pl.BlockSpec((N_EXPERTS, 1), lambda i: (0, 0))],
        out_specs=blk,
        out_shape=jax.ShapeDtypeStruct((TOP_K, t), I32),
        compiler_params=_params(1),
        name="positions",
    )(eidx, rank, start_col)


def _dispatch_kernel(pos_ref, h_ref, xs_ref, zero_ref, sem, zsem, *, n_rows):
    tt = h_ref.shape[0] // ROW_TILES
    first = pl.program_id(0) == 0

    @pl.when(first)
    def _():
        zero_ref[...] = jnp.zeros_like(zero_ref)
        _rows_copy(zero_ref, 0, xs_ref, n_rows, EXPERT_BLOCK, zsem).start()

    for t in range(tt):
        for k in range(TOP_K):
            _rows_copy(h_ref, t, xs_ref, pos_ref[k, t], 1, sem).start(priority=k % 2)
    for k in range(TOP_K):
        _rows_copy(h_ref, 0, xs_ref, 0, tt, sem).wait()

    @pl.when(first)
    def _():
        _rows_copy(zero_ref, 0, xs_ref, n_rows, EXPERT_BLOCK, zsem).wait()


def _dispatch(pos, h2_tiles, n_rows):
    t = h2_tiles.shape[0] // ROW_TILES
    return pl.pallas_call(
        functools.partial(_dispatch_kernel, n_rows=n_rows),
        grid=(t // TOKEN_TILE,),
        in_specs=[pl.BlockSpec((TOP_K, TOKEN_TILE), lambda i: (0, i), memory_space=pltpu.SMEM),
                  pl.BlockSpec((TOKEN_TILE * ROW_TILES, LANES), lambda i: (i, 0))],
        out_specs=pl.BlockSpec(memory_space=pl.ANY),
        out_shape=jax.ShapeDtypeStruct(((n_rows + EXPERT_BLOCK) * ROW_TILES, LANES), ROW_DTYPE),
        scratch_shapes=[pltpu.VMEM((EXPERT_BLOCK * ROW_TILES, LANES), ROW_DTYPE),
                        pltpu.SemaphoreType.DMA, pltpu.SemaphoreType.DMA],
        compiler_params=_params(1),
        name="dispatch",
    )(pos, h2_tiles)


def _out_pieces(nvalid, fn):
    s = 1 << (EXPERT_BLOCK.bit_length() - 1)
    while s >= 1:
        @pl.when((nvalid & s) != 0)
        def _(s=s):
            fn(nvalid & ~(2 * s - 1), s)
        s //= 2


def _for_block_size(nvalid, fn):
    lo = 0
    for rows in BLOCK_SIZES:
        @pl.when(jnp.logical_and(nvalid > lo, nvalid <= rows))
        def _(rows=rows):
            fn(rows)
        lo = rows


def _expert_kernel(count_ref, blk0_ref, row_ref, cnt_ref, xs_ref, wg_ref, wu_ref, wd_ref, ys_ref,
                   xbuf, ybuf, wg_buf, wu_buf, wd_buf, wgu_s, wd_s, state, xsem, ysem, wsem):
    e = pl.program_id(0)
    n_experts = pl.num_programs(0)
    cnt = count_ref[e]
    g0 = blk0_ref[e]
    nblk = (cnt + EXPERT_BLOCK - 1) // EXPERT_BLOCK
    wslot = lax.rem(e, W_SLOTS)

    def w_copies(expert, slot):
        return [pltpu.make_async_copy(src.at[expert], dst.at[slot], wsem.at[slot])
                for src, dst in ((wg_ref, wg_buf), (wu_ref, wu_buf), (wd_ref, wd_buf))]

    def w_fetch(expert):
        @pl.when(expert < n_experts)
        def _():
            for c in w_copies(expert, lax.rem(expert, W_SLOTS)):
                c.start()

    def x_copy(row, slot, rows):
        return _rows_copy(xs_ref, row, xbuf.at[slot], 0, rows, xsem.at[slot])

    def x_fetch(g, slot):
        _for_block_size(cnt_ref[g], lambda rows: x_copy(row_ref[g], slot, rows).start())

    def y_wait(nvalid, slot):
        def wait(off, size):
            _rows_copy(ybuf.at[slot], 0, ys_ref, 0, size, ysem.at[slot]).wait()
        _out_pieces(nvalid, wait)

    @pl.when(e == 0)
    def _():
        state[0] = 0
        state[1] = 0
        for g in range(X_SLOTS - 1):
            x_fetch(g, g)
        for ahead in range(W_SLOTS - 1):
            w_fetch(ahead)

    w_fetch(e + W_SLOTS - 1)
    for c in w_copies(e, wslot):
        c.wait()
    wgu_s[:, :EXPERT_DIM] = wg_buf[wslot].astype(BF16)
    wgu_s[:, EXPERT_DIM:] = wu_buf[wslot].astype(BF16)
    wd_s[...] = wd_buf[wslot].astype(BF16)

    def compute(rows, xslot, yslot, row0, nvalid):
        x_copy(row0, xslot, rows).wait()
        x = _load_rows(xbuf.at[xslot], rows).astype(BF16)
        gu = jnp.dot(x, wgu_s[...], preferred_element_type=F32)
        g = gu[:, :EXPERT_DIM]
        act = (g * jax.nn.sigmoid(g)) * gu[:, EXPERT_DIM:]
        y = jnp.dot(act.astype(BF16), wd_s[...], preferred_element_type=F32)
        y_wait(state[0], 1 - yslot)
        _store_rows(ybuf.at[yslot], y)

        def put(off, size):
            _rows_copy(ybuf.at[yslot], off, ys_ref, row0 + off, size, ysem.at[yslot]).start()
        _out_pieces(nvalid, put)

    def block(j, carry):
        g = g0 + j
        xslot = state[1]
        yslot = g & 1
        row0 = row_ref[g]
        nvalid = cnt_ref[g]
        x_fetch(g + X_SLOTS - 1, jnp.where(xslot == 0, X_SLOTS - 1, xslot - 1))
        _for_block_size(nvalid, lambda rows: compute(rows, xslot, yslot, row0, nvalid))
        state[0] = nvalid
        state[1] = jnp.where(xslot == X_SLOTS - 1, 0, xslot + 1)
        return carry

    lax.fori_loop(0, nblk, block, 0)

    @pl.when(e == pl.num_programs(0) - 1)
    def _():
        y_wait(state[0], (g0 + nblk - 1) & 1)


def _experts(count, blk0, blk_row, blk_cnt, xs, w_gate, w_up, w_down, n_rows):
    blk_rows = EXPERT_BLOCK * ROW_TILES
    any_spec = pl.BlockSpec(memory_space=pl.ANY)
    return pl.pallas_call(
        _expert_kernel,
        grid_spec=pltpu.PrefetchScalarGridSpec(
            num_scalar_prefetch=4,
            grid=(N_EXPERTS,),
            in_specs=[any_spec, any_spec, any_spec, any_spec],
            out_specs=any_spec,
            scratch_shapes=[pltpu.VMEM((X_SLOTS, blk_rows, LANES), ROW_DTYPE),
                            pltpu.VMEM((2, blk_rows, LANES), ROW_DTYPE),
                            pltpu.VMEM((W_SLOTS, D_MODEL, EXPERT_DIM), F32),
                            pltpu.VMEM((W_SLOTS, D_MODEL, EXPERT_DIM), F32),
                            pltpu.VMEM((W_SLOTS, EXPERT_DIM, D_MODEL), F32),
                            pltpu.VMEM((D_MODEL, 2 * EXPERT_DIM), BF16),
                            pltpu.VMEM((EXPERT_DIM, D_MODEL), BF16),
                            pltpu.SMEM((2,), I32),
                            pltpu.SemaphoreType.DMA((X_SLOTS,)),
                            pltpu.SemaphoreType.DMA((2,)),
                            pltpu.SemaphoreType.DMA((W_SLOTS,))]),
        out_shape=jax.ShapeDtypeStruct((n_rows * ROW_TILES, LANES), ROW_DTYPE),
        compiler_params=_params(1),
        name="experts",
    )(count, blk0, blk_row, blk_cnt, xs, w_gate, w_up, w_down)


def _combine_kernel(pos_ref, posn_ref, ys_ref, xb_ref, wts_ref, g2p_ref, g2s_ref, gf_ref, yp_ref, ysm_ref,
                    buf, ytile, sem, *, n_prompt_tiles, n_tiles):
    i = pl.program_id(0)
    tt = xb_ref.shape[0]
    slot = i & 1
    is_sample = i >= n_prompt_tiles

    def fetch(p_ref, s, t):
        for k in range(TOP_K):
            _rows_copy(ys_ref, p_ref[k, t], buf.at[s, k], t, 1, sem.at[s]).start(priority=k % 2)

    @pl.when(i == 0)
    def _():
        for t in range(tt):
            fetch(pos_ref, 0, t)

    for k in range(TOP_K):
        _rows_copy(ys_ref, 0, buf.at[slot, k], 0, tt, sem.at[slot]).wait()

    def finish(s, g):
        r0 = g * COMBINE_GROUP
        w = wts_ref[r0:r0 + COMBINE_GROUP, :]
        routed = None
        for k in range(TOP_K):
            rows = _load_rows(buf.at[s, k, pl.ds(r0 * ROW_TILES, COMBINE_GROUP * ROW_TILES)], COMBINE_GROUP)
            term = w[:, k:k + 1] * rows
            routed = term if routed is None else routed + term
        gate = jnp.where(is_sample, g2s_ref[g * COMBINE_GROUP // ROW_GROUP], g2p_ref[0])
        x = xb_ref[r0:r0 + COMBINE_GROUP, :] + gate * routed
        return _rms(x) * gf_ref[...]

    def tile_work(s, fetch_next):
        for g in range(tt // COMBINE_GROUP):
            y = finish(s, g)
            if fetch_next:
                for r in range(COMBINE_GROUP):
                    fetch(posn_ref, 1 - s, g * COMBINE_GROUP + r)
            ytile[g * COMBINE_GROUP:(g + 1) * COMBINE_GROUP, :] = y

    for s in range(2):
        @pl.when(jnp.logical_and(slot == s, i + 1 < n_tiles))
        def _(s=s):
            tile_work(s, True)

    @pl.when(i + 1 == n_tiles)
    def _():
        tile_work((n_tiles - 1) % 2, False)

    @pl.when(i < n_prompt_tiles)
    def _():
        yp_ref[...] = ytile[...]

    @pl.when(i >= n_prompt_tiles)
    def _():
        ysm_ref[...] = ytile[...]


def _combine(pos, ys, xb, wts_t, mod_p, mod_s, gf, batch, n_prompt_rows):
    t = xb.shape[0]
    tt = COMBINE_TILE
    n_tiles = t // tt
    n_p = n_prompt_rows // tt
    pos_spec = lambda f: pl.BlockSpec((TOP_K, tt), f, memory_space=pltpu.SMEM)
    return pl.pallas_call(
        functools.partial(_combine_kernel, n_prompt_tiles=n_p, n_tiles=n_tiles),
        grid=(n_tiles,),
        in_specs=[pos_spec(lambda i: (0, i)),
                  pos_spec(lambda i: (0, jnp.minimum(i + 1, n_tiles - 1))),
                  pl.BlockSpec(memory_space=pl.ANY),
                  pl.BlockSpec((tt, D_MODEL), lambda i: (i, 0)),
                  pl.BlockSpec((tt, TOP_K), lambda i: (i, 0))]
                 + _cond_specs(5, tt, n_p // batch, batch, n_p, n_tiles - n_p)
                 + [pl.BlockSpec((1, D_MODEL), lambda i: (0, 0))],
        out_specs=[pl.BlockSpec((tt, D_MODEL), lambda i: (jnp.minimum(i, n_p - 1), 0)),
                   pl.BlockSpec((tt, D_MODEL), lambda i: (jnp.maximum(i - n_p, 0), 0))],
        out_shape=[jax.ShapeDtypeStruct((n_prompt_rows, D_MODEL), F32),
                   jax.ShapeDtypeStruct((t - n_prompt_rows, D_MODEL), F32)],
        scratch_shapes=[pltpu.VMEM((2, TOP_K, tt * ROW_TILES, LANES), ROW_DTYPE),
                        pltpu.VMEM((tt, D_MODEL), F32),
                        pltpu.SemaphoreType.DMA((2,))],
        compiler_params=_params(1),
        name="combine",
    )(pos, pos, ys, xb, wts_t, mod_p, mod_s, gf)


def _sgu_mask():
    pos = np.arange(SGU_CHUNK) // CHUNK
    return jnp.asarray(pos[None, :] <= pos[:, None])


def _layer(xp3, xs3, cache_k, cache_v, mod, final_g, batch, seq, dec_batch, n, p):
    tp = batch * seq
    t = tp + dec_batch * n
    n_prompt_tiles = tp // TOKEN_TILE
    mod_p = mod[:batch].reshape(batch, 1, N_MOD * D_MODEL)
    mod_s = mod[batch:].reshape(dec_batch, 1, N_MOD * D_MODEL)

    wm = jnp.where(_sgu_mask(), p["sgu_w"], 0.0)
    reps = SGU_CHUNK // n
    eye = jnp.eye(reps, dtype=F32)
    wms = jnp.stack([jnp.kron(eye, wm[g, :n, :n]) for g in range(SGU_GROUPS)])
    bs = jnp.broadcast_to(p["sgu_b"][:, :, None], (SGU_GROUPS, SGU_CHUNK, SGU_CHUNK))
    bss = jnp.broadcast_to(jnp.tile(p["sgu_b"][:, :n], (1, reps))[:, :, None], (SGU_GROUPS, SGU_CHUNK, SGU_CHUNK))

    q, k, v, sgu_n, vn_s = _stage1(
        xp3, xs3, mod_p, mod_s, p["norm1_g"].reshape(1, -1), p["w_in"].astype(BF16),
        p["sgu_ln_g"].reshape(1, -1), p["sgu_ln_b"].reshape(1, -1),
        wm.astype(BF16), wms.astype(BF16), bs, bss, p["sgu_out_g"].reshape(1, -1), batch, n_prompt_tiles)

    gattn = p["attn_out_g"].reshape(1, -1)
    attn_p = _attn_prompt(q, k, v, p["attn_sinks"], gattn, batch, seq)
    w = cache_k.shape[1]
    attn_s = _attn_sample(q, k, v, cache_k.reshape(dec_batch, w, KV_WIDTH), cache_v.reshape(dec_batch, w, KV_WIDTH),
                          p["attn_sinks"], gattn, dec_batch, n, tp)

    wo = p["w_out"].astype(BF16)
    wsgu = jnp.concatenate([p["ws_gate"], p["ws_up"]], axis=1).astype(BF16)
    wrh = p["w_router"].astype(BF16)
    wrl = (p["w_router"] - wrh.astype(F32)).astype(BF16)
    xb, h2_tiles, scores = _merge(attn_p, attn_s, sgu_n, xp3, xs3, mod_p, mod_s,
                                  wo[:ATTN_WIDTH], wo[ATTN_WIDTH:], p["norm2_g"].reshape(1, -1), wrh, wrl,
                                  wsgu, p["ws_down"].astype(BF16), batch, n_prompt_tiles)

    ti = np.arange(TOKEN_TILE)
    tri = jnp.asarray(ti[:, None] < ti[None, :], dtype=BF16)
    eidx, wts, rank, counts = _route(scores, p["router_bias"].reshape(N_EXPERTS, 1), tri)

    counts = counts[:, 0]
    ends = jnp.cumsum(counts)
    start = (ends - counts).astype(I32)
    nblk = (counts + EXPERT_BLOCK - 1) // EXPERT_BLOCK
    blk_end = jnp.cumsum(nblk)
    blk0 = (blk_end - nblk).astype(I32)
    pos = _positions(eidx, rank, start.reshape(N_EXPERTS, 1))

    n_rows = t * TOP_K
    g = jnp.arange(n_rows // EXPERT_BLOCK + N_EXPERTS + X_SLOTS, dtype=I32)[:, None]
    mine = jnp.logical_and(g >= blk0[None, :], g < blk_end[None, :])
    off = (g - blk0[None, :]) * EXPERT_BLOCK
    blk_row = jnp.sum(jnp.where(mine, start[None, :] + off, 0), axis=1).astype(I32)
    blk_cnt = jnp.sum(jnp.where(mine, jnp.minimum(counts[None, :] - off, EXPERT_BLOCK), 0), axis=1).astype(I32)

    xs_sorted = _dispatch(pos, h2_tiles, n_rows)
    ys_sorted = _experts(counts.astype(I32), blk0, blk_row, blk_cnt, xs_sorted,
                         p["w_gate"], p["w_up"], p["w_down"], n_rows)
    y_p, y_s = _combine(pos, ys_sorted, xb, wts.T, mod_p, mod_s, final_g.reshape(1, -1), batch, tp)
    return y_p, y_s, k, v, vn_s


def kernel(x_prompt, x_sample, cache_k, cache_v, c_prompt, c_sample, norm1_g, w_ada, b_ada, w_in, sgu_ln_g, sgu_ln_b, sgu_w, sgu_b, attn_sinks, attn_out_g, sgu_out_g, w_out, norm2_g, w_router, router_bias, w_gate, w_up, w_down, ws_gate, ws_up, ws_down, final_g):
    batch, seq, d = x_prompt.shape
    dec_batch, n, _ = x_sample.shape
    depth = norm1_g.shape[0]
    assert depth == 1 and d == D_MODEL
    assert seq % TOKEN_TILE == 0 and dec_batch * n == TOKEN_TILE and n == ROW_GROUP
    assert seq % ATTN_TILE == 0 and seq >= WINDOW and cache_k.shape[2] == WINDOW
    tp = batch * seq

    xp3 = x_prompt.reshape(tp // ROW_GROUP, ROW_GROUP, d)
    xs3 = x_sample.reshape(dec_batch * n // ROW_GROUP, ROW_GROUP, d)
    c_all = jnp.concatenate([c_prompt, c_sample], axis=0)
    l = 0
    mod = _adaln(c_all, w_ada[l], b_ada[l])
    p = dict(norm1_g=norm1_g[l], w_in=w_in[l], sgu_ln_g=sgu_ln_g[l], sgu_ln_b=sgu_ln_b[l], sgu_w=sgu_w[l],
             sgu_b=sgu_b[l], attn_sinks=attn_sinks[l], attn_out_g=attn_out_g[l], sgu_out_g=sgu_out_g[l],
             w_out=w_out[l], norm2_g=norm2_g[l], w_router=w_router[l], router_bias=router_bias[l],
             w_gate=w_gate[l], w_up=w_up[l], w_down=w_down[l], ws_gate=ws_gate[l], ws_up=ws_up[l],
             ws_down=ws_down[l])
    y_p, y_s, k, v, vn_s = _layer(xp3, xs3, cache_k[l], cache_v[l], mod, final_g, batch, seq, dec_batch, n, p)

    keep = min(WINDOW, seq)
    kv_shape = (batch, keep, N_KV_HEADS, HEAD_DIM)
    k_p = jnp.stack([k[(b + 1) * seq - keep:(b + 1) * seq] for b in range(batch)]).reshape(kv_shape)
    v_p = jnp.stack([v[(b + 1) * seq - keep:(b + 1) * seq] for b in range(batch)]).reshape(kv_shape)
    k_s = k[tp:].reshape(dec_batch, n, N_KV_HEADS, HEAD_DIM)
    v_s = v[tp:].reshape(dec_batch, n, N_KV_HEADS, HEAD_DIM)
    return (y_p.reshape(batch, seq, d), y_s.reshape(dec_batch, n, d),
            k_p[None], v_p[None], k_s[None], v_s[None], vn_s.reshape(dec_batch, n, SGU_WIDTH)[None])
```

```python
import functools

import numpy as np
import jax
import jax.numpy as jnp
from jax import lax
from jax.experimental import pallas as pl
from jax.experimental.pallas import tpu as pltpu

F32 = jnp.float32
BF16 = jnp.bfloat16
I32 = jnp.int32

D_MODEL = 1024
CHUNK = 64
HEAD_DIM = 64
ATTN_WIDTH = 512
N_HEADS = 8
N_KV_HEADS = 2
GQA_GROUP = 4
KV_WIDTH = 128
WINDOW = 128
SGU_CHUNK = 128
SGU_WIDTH = 512
SGU_GROUPS = 4
IN_COLS = ATTN_WIDTH + 2 * KV_WIDTH + 2 * SGU_WIDTH
N_MOD = 6
N_EXPERTS = 256
TOP_K = 8
N_EXPERT_GROUPS = 8
GROUP_SIZE = N_EXPERTS // N_EXPERT_GROUPS
TOPK_GROUPS = 4
EXPERT_DIM = 256
ROUTED_SCALE = 2.5
EPS = 1e-6

LANES = 128
ROW_WORDS = D_MODEL // 2
ROW_TILES = ROW_WORDS // LANES
ROW_DTYPE = jnp.uint32
ROW_GROUP = 16
TOKEN_TILE = 256
ATTN_TILE = 256
EXPERT_BLOCK = 640
BLOCK_SIZES = (128, 256, 384, 512, 576, EXPERT_BLOCK)
X_SLOTS = 3
W_SLOTS = 3
POS_TILE_MAX = 4096
COMBINE_TILE = 128
COMBINE_GROUP = 8
VMEM_LIMIT_BYTES = 56 * 1024 * 1024


def _params(n_axes=1):
    return pltpu.CompilerParams(dimension_semantics=("arbitrary",) * n_axes,
                                vmem_limit_bytes=VMEM_LIMIT_BYTES)


def _rms(x):
    return x * lax.rsqrt(jnp.mean(x * x, axis=-1, keepdims=True) + EPS)


def _cond_rows(is_sample, mp_ref, ms_ref):
    ms = ms_ref[...]
    g, _, d = ms.shape
    s = jnp.broadcast_to(ms, (g, ROW_GROUP, d)).reshape(g * ROW_GROUP, d)
    return jnp.where(is_sample, s, mp_ref[0])


def _cond_specs(piece, tile_rows, tiles_per_batch, batch, n_prompt_tiles, n_sample_tiles):
    g = tile_rows // ROW_GROUP
    mp = pl.BlockSpec((1, 1, D_MODEL),
                      lambda i: (jnp.minimum(i // tiles_per_batch, batch - 1), 0, piece))
    ms = pl.BlockSpec((g, 1, D_MODEL),
                      lambda i: (jnp.clip(i - n_prompt_tiles, 0, n_sample_tiles - 1), 0, piece))
    return [mp, ms]


def _pick_tile(is_sample, prompt_ref, sample_ref):
    v = jnp.where(is_sample, sample_ref[...], prompt_ref[...])
    return v.reshape(TOKEN_TILE, v.shape[-1])


def _pack_rows(x):
    x = x.astype(F32)
    return pltpu.pack_elementwise([x[:, :ROW_WORDS], x[:, ROW_WORDS:]], packed_dtype=BF16)


def _unpack_rows(w):
    return jnp.concatenate([pltpu.unpack_elementwise(w, index=i, packed_dtype=BF16, unpacked_dtype=F32)
                            for i in range(2)], axis=1)


def _load_rows(ref, n_rows):
    w = jnp.concatenate([ref[pl.ds(j, n_rows, stride=ROW_TILES), :] for j in range(ROW_TILES)], axis=1)
    return _unpack_rows(w)


def _store_rows(ref, x):
    n_rows = x.shape[0]
    w = _pack_rows(x)
    for j in range(ROW_TILES):
        ref[pl.ds(j, n_rows, stride=ROW_TILES), :] = w[:, j * LANES:(j + 1) * LANES]


def _tile_offset(row):
    off = row * ROW_TILES
    return off if isinstance(off, int) else pl.multiple_of(off, ROW_TILES)


def _rows_copy(src_ref, src_row, dst_ref, dst_row, n_rows, sem):
    s = _tile_offset(src_row)
    d = _tile_offset(dst_row)
    return pltpu.make_async_copy(src_ref.at[pl.ds(s, n_rows * ROW_TILES)],
                                 dst_ref.at[pl.ds(d, n_rows * ROW_TILES)], sem)


def _adaln_kernel(c_ref, w_ref, b_ref, o_ref):
    c = c_ref[...]
    s = c * jax.nn.sigmoid(c)
    o_ref[...] = jnp.dot(s, w_ref[...], precision=lax.Precision.HIGHEST,
                         preferred_element_type=F32) + b_ref[...]


def _adaln(c, w_ada, b_ada):
    n, d = c.shape
    cols = w_ada.shape[1]
    tn = 1536
    return pl.pallas_call(
        _adaln_kernel,
        grid=(cols // tn,),
        in_specs=[pl.BlockSpec((n, d), lambda j: (0, 0)),
                  pl.BlockSpec((d, tn), lambda j: (0, j)),
                  pl.BlockSpec((1, tn), lambda j: (0, j))],
        out_specs=pl.BlockSpec((n, tn), lambda j: (0, j)),
        out_shape=jax.ShapeDtypeStruct((n, cols), F32),
        compiler_params=_params(1),
        name="adaln",
    )(c, w_ada, b_ada.reshape(1, cols))


def _stage1_kernel(xp_ref, xs_ref, shp_ref, shs_ref, scp_ref, scs_ref, g1_ref, win_ref, lng_ref, lnb_ref,
                   wm_ref, wms_ref, bs_ref, bss_ref, gsgu_ref,
                   q_ref, k_ref, v_ref, sgu_ref, vn_ref, *, n_prompt_tiles):
    is_sample = pl.program_id(0) >= n_prompt_tiles
    x = _pick_tile(is_sample, xp_ref, xs_ref)
    h = (_rms(x) * g1_ref[...] * (1.0 + _cond_rows(is_sample, scp_ref, scs_ref))
         + _cond_rows(is_sample, shp_ref, shs_ref))
    proj = jnp.dot(h.astype(BF16), win_ref[...], preferred_element_type=F32)
    q_ref[...] = (proj[:, :ATTN_WIDTH] * (HEAD_DIM ** -0.5)).astype(BF16)
    k_ref[...] = proj[:, ATTN_WIDTH:ATTN_WIDTH + KV_WIDTH]
    v_ref[...] = proj[:, ATTN_WIDTH + KV_WIDTH:ATTN_WIDTH + 2 * KV_WIDTH]
    c0 = ATTN_WIDTH + 2 * KV_WIDTH
    u = jax.nn.gelu(proj[:, c0:c0 + SGU_WIDTH])
    vg = jax.nn.gelu(proj[:, c0 + SGU_WIDTH:])
    mu = jnp.mean(vg, axis=-1, keepdims=True)
    xc = vg - mu
    var = jnp.mean(xc * xc, axis=-1, keepdims=True)
    vn = xc * lax.rsqrt(var + EPS) * lng_ref[...] + lnb_ref[...]

    @pl.when(is_sample)
    def _():
        vn_ref[...] = vn

    vnb = vn.astype(BF16)
    wm = jnp.where(is_sample, wms_ref[...], wm_ref[...])
    bs = jnp.where(is_sample, bss_ref[...], bs_ref[...])
    rows = []
    for c in range(TOKEN_TILE // SGU_CHUNK):
        r0 = c * SGU_CHUNK
        cols = []
        for g in range(SGU_GROUPS):
            l0 = g * SGU_CHUNK
            sv = jnp.dot(wm[g], vnb[r0:r0 + SGU_CHUNK, l0:l0 + SGU_CHUNK],
                         preferred_element_type=F32) + bs[g]
            cols.append(sv)
        rows.append(jnp.concatenate(cols, axis=1))
    sgu = u * jnp.concatenate(rows, axis=0)
    sgu_ref[...] = (_rms(sgu) * gsgu_ref[...]).astype(BF16)


def _stage1(xp3, xs3, mod_p, mod_s, g1, win_b, lng, lnb, wm, wms, bs, bss, gsgu, batch, n_prompt_tiles):
    t = (xp3.shape[0] + xs3.shape[0]) * ROW_GROUP
    n_tiles = t // TOKEN_TILE
    last_p = n_prompt_tiles - 1
    x_blk = (TOKEN_TILE // ROW_GROUP, ROW_GROUP, D_MODEL)
    const2 = lambda i: (0, 0)
    const3 = lambda i: (0, 0, 0)
    tile2 = lambda i: (i, 0)
    cond = functools.partial(_cond_specs, tile_rows=TOKEN_TILE, tiles_per_batch=n_prompt_tiles // batch,
                             batch=batch, n_prompt_tiles=n_prompt_tiles, n_sample_tiles=n_tiles - n_prompt_tiles)
    sgu_w_spec = pl.BlockSpec((SGU_GROUPS, SGU_CHUNK, SGU_CHUNK), const3)
    return pl.pallas_call(
        functools.partial(_stage1_kernel, n_prompt_tiles=n_prompt_tiles),
        grid=(n_tiles,),
        in_specs=[pl.BlockSpec(x_blk, lambda i: (jnp.minimum(i, last_p), 0, 0)),
                  pl.BlockSpec(x_blk, const3)]
                 + cond(0) + cond(1)
                 + [pl.BlockSpec((1, D_MODEL), const2),
                    pl.BlockSpec((D_MODEL, IN_COLS), const2),
                    pl.BlockSpec((1, SGU_WIDTH), const2),
                    pl.BlockSpec((1, SGU_WIDTH), const2),
                    sgu_w_spec, sgu_w_spec, sgu_w_spec, sgu_w_spec,
                    pl.BlockSpec((1, SGU_WIDTH), const2)],
        out_specs=[pl.BlockSpec((TOKEN_TILE, ATTN_WIDTH), tile2),
                   pl.BlockSpec((TOKEN_TILE, KV_WIDTH), tile2),
                   pl.BlockSpec((TOKEN_TILE, KV_WIDTH), tile2),
                   pl.BlockSpec((TOKEN_TILE, SGU_WIDTH), tile2),
                   pl.BlockSpec((TOKEN_TILE, SGU_WIDTH), const2)],
        out_shape=[jax.ShapeDtypeStruct((t, ATTN_WIDTH), BF16),
                   jax.ShapeDtypeStruct((t, KV_WIDTH), F32),
                   jax.ShapeDtypeStruct((t, KV_WIDTH), F32),
                   jax.ShapeDtypeStruct((t, SGU_WIDTH), BF16),
                   jax.ShapeDtypeStruct((TOKEN_TILE, SGU_WIDTH), F32)],
        compiler_params=_params(1),
        name="stage1",
    )(xp3, xs3, mod_p, mod_s, mod_p, mod_s, g1, win_b, lng, lnb, wm, wms, bs, bss, gsgu)


def _alibi_slope(h):
    return float(np.float32(2.0) ** np.float32(-8.0 * (h + 1.0) / N_HEADS))


def _sink_attention(q, k, v, back, valid_fn, sinks_ref, gout):
    nq = q.shape[0]
    nk = k.shape[0]
    q_t = q.astype(F32).T.astype(BF16)
    k = k.astype(BF16)
    v_t = v.T.astype(BF16)
    c = lax.broadcasted_iota(I32, (nk, nq), 0)
    r = lax.broadcasted_iota(I32, (nk, nq), 1)
    dist = jnp.abs(r - c + back).astype(F32)
    valid = None if valid_fn is None else valid_fn(c, r)
    outs = []
    for kvh in range(N_KV_HEADS):
        kv0 = kvh * HEAD_DIM
        heads = range(kvh * GQA_GROUP, (kvh + 1) * GQA_GROUP)
        qs = jnp.concatenate([q_t[h * HEAD_DIM:(h + 1) * HEAD_DIM, :] for h in heads], axis=1)
        l = jnp.dot(k[:, kv0:kv0 + HEAD_DIM], qs, preferred_element_type=F32)
        biased = [l[:, g * nq:(g + 1) * nq] - _alibi_slope(h) * dist for g, h in enumerate(heads)]
        if valid is not None:
            biased = [jnp.where(valid, b, -jnp.inf) for b in biased]
        l = jnp.concatenate(biased, axis=1)
        sink = jnp.concatenate([jnp.full((1, nq), sinks_ref[h], F32) for h in heads], axis=1)
        m = jnp.maximum(jnp.max(l, axis=0, keepdims=True), sink)
        e = jnp.exp(l - m)
        denom = jnp.sum(e, axis=0, keepdims=True) + jnp.exp(sink - m)
        pv = jnp.dot(v_t[kv0:kv0 + HEAD_DIM, :], e.astype(BF16), preferred_element_type=F32)
        o_t = pv / denom
        outs.extend(o_t[:, g * nq:(g + 1) * nq] for g in range(GQA_GROUP))
    o = jnp.concatenate(outs, axis=0).T
    return (_rms(o) * gout).astype(BF16)


def _attn_prompt_kernel(sinks_ref, q_ref, kp_ref, kc_ref, vp_ref, vc_ref, g_ref, o_ref):
    i = pl.program_id(1)
    back = kp_ref.shape[0]

    def valid_fn(c, r):
        kc = c // CHUNK
        qc = r // CHUNK + (back // CHUNK)
        return jnp.logical_and(jnp.logical_and(kc <= qc, kc >= qc - WINDOW // CHUNK),
                               jnp.logical_or(i > 0, kc >= back // CHUNK))

    k = jnp.concatenate([kp_ref[...], kc_ref[...]], axis=0)
    v = jnp.concatenate([vp_ref[...], vc_ref[...]], axis=0)
    o_ref[...] = _sink_attention(q_ref[...], k, v, back, valid_fn, sinks_ref, g_ref[...])


def _attn_prompt(q, k, v, sinks, gattn, batch, seq):
    nt = seq // ATTN_TILE
    per_tile = ATTN_TILE // WINDOW
    cur = lambda b, i, s: (b * nt + i, 0)
    prev = lambda b, i, s: (jnp.maximum((b * nt + i) * per_tile - 1, 0), 0)
    kv_blk = (ATTN_TILE, KV_WIDTH)
    back_blk = (WINDOW, KV_WIDTH)
    return pl.pallas_call(
        _attn_prompt_kernel,
        grid_spec=pltpu.PrefetchScalarGridSpec(
            num_scalar_prefetch=1,
            grid=(batch, nt),
            in_specs=[pl.BlockSpec((ATTN_TILE, ATTN_WIDTH), cur),
                      pl.BlockSpec(back_blk, prev), pl.BlockSpec(kv_blk, cur),
                      pl.BlockSpec(back_blk, prev), pl.BlockSpec(kv_blk, cur),
                      pl.BlockSpec((1, ATTN_WIDTH), lambda b, i, s: (0, 0))],
            out_specs=pl.BlockSpec((ATTN_TILE, ATTN_WIDTH), cur)),
        out_shape=jax.ShapeDtypeStruct((batch * seq, ATTN_WIDTH), BF16),
        compiler_params=_params(2),
        name="attn_prompt",
    )(sinks, q, k, k, v, v, gattn)


def _attn_sample_kernel(sinks_ref, q_ref, ck_ref, cv_ref, k_ref, v_ref, g_ref, o_ref):
    k = jnp.concatenate([ck_ref[0], k_ref[...]], axis=0)
    v = jnp.concatenate([cv_ref[0], v_ref[...]], axis=0)
    o_ref[...] = _sink_attention(q_ref[...], k, v, ck_ref.shape[1], None, sinks_ref, g_ref[...])


def _attn_sample(q, k, v, cache_k, cache_v, sinks, gattn, dec_batch, n, row0):
    w = cache_k.shape[1]
    blk0 = row0 // n
    new = lambda b, s: (blk0 + b, 0)
    return pl.pallas_call(
        _attn_sample_kernel,
        grid_spec=pltpu.PrefetchScalarGridSpec(
            num_scalar_prefetch=1,
            grid=(dec_batch,),
            in_specs=[pl.BlockSpec((n, ATTN_WIDTH), new),
                      pl.BlockSpec((1, w, KV_WIDTH), lambda b, s: (b, 0, 0)),
                      pl.BlockSpec((1, w, KV_WIDTH), lambda b, s: (b, 0, 0)),
                      pl.BlockSpec((n, KV_WIDTH), new),
                      pl.BlockSpec((n, KV_WIDTH), new),
                      pl.BlockSpec((1, ATTN_WIDTH), lambda b, s: (0, 0))],
            out_specs=pl.BlockSpec((n, ATTN_WIDTH), lambda b, s: (b, 0))),
        out_shape=jax.ShapeDtypeStruct((dec_batch * n, ATTN_WIDTH), BF16),
        compiler_params=_params(1),
        name="attn_sample",
    )(sinks, q, cache_k, cache_v, k, v, gattn)


def _merge_kernel(ap_ref, as_ref, sgu_ref, xp_ref, xs_ref,
                  g1p_ref, g1s_ref, sh2p_ref, sh2s_ref, sc2p_ref, sc2s_ref, g2p_ref, g2s_ref,
                  woa_ref, wos_ref, g2_ref, wrh_ref, wrl_ref, wsgu_ref, wsd_ref,
                  xb_ref, h2_ref, sc_ref, *, n_prompt_tiles):
    is_sample = pl.program_id(0) >= n_prompt_tiles
    x = _pick_tile(is_sample, xp_ref, xs_ref)
    a = jnp.where(is_sample, as_ref[...], ap_ref[...])
    mix = (jnp.dot(a, woa_ref[...], preferred_element_type=F32)
           + jnp.dot(sgu_ref[...], wos_ref[...], preferred_element_type=F32))
    x1 = x + _cond_rows(is_sample, g1p_ref, g1s_ref) * mix
    h2 = (_rms(x1) * g2_ref[...] * (1.0 + _cond_rows(is_sample, sc2p_ref, sc2s_ref))
          + _cond_rows(is_sample, sh2p_ref, sh2s_ref))
    hh = h2.astype(BF16)
    _store_rows(h2_ref, hh)
    hl = (h2 - hh.astype(F32)).astype(BF16)
    logits = (jnp.dot(hh, wrh_ref[...], preferred_element_type=F32)
              + (jnp.dot(hl, wrh_ref[...], preferred_element_type=F32)
                 + jnp.dot(hh, wrl_ref[...], preferred_element_type=F32)))
    sc_ref[...] = jax.nn.sigmoid(logits)
    gu = jnp.dot(hh, wsgu_ref[...], preferred_element_type=F32)
    g = gu[:, :EXPERT_DIM]
    act = (g * jax.nn.sigmoid(g)) * gu[:, EXPERT_DIM:]
    shared = jnp.dot(act.astype(BF16), wsd_ref[...], preferred_element_type=F32)
    xb_ref[...] = x1 + _cond_rows(is_sample, g2p_ref, g2s_ref) * shared


def _merge(attn_p, attn_s, sgu_n, xp3, xs3, mod_p, mod_s, woa, wos, g2, wrh, wrl, wsgu, wsd, batch, n_prompt_tiles):
    t = sgu_n.shape[0]
    n_tiles = t // TOKEN_TILE
    last_p = n_prompt_tiles - 1
    x_blk = (TOKEN_TILE // ROW_GROUP, ROW_GROUP, D_MODEL)
    const2 = lambda i: (0, 0)
    const3 = lambda i: (0, 0, 0)
    tile2 = lambda i: (i, 0)
    cond = functools.partial(_cond_specs, tile_rows=TOKEN_TILE, tiles_per_batch=n_prompt_tiles // batch,
                             batch=batch, n_prompt_tiles=n_prompt_tiles, n_sample_tiles=n_tiles - n_prompt_tiles)
    return pl.pallas_call(
        functools.partial(_merge_kernel, n_prompt_tiles=n_prompt_tiles),
        grid=(n_tiles,),
        in_specs=[pl.BlockSpec((TOKEN_TILE, ATTN_WIDTH), lambda i: (jnp.minimum(i, last_p), 0)),
                  pl.BlockSpec((TOKEN_TILE, ATTN_WIDTH), const2),
                  pl.BlockSpec((TOKEN_TILE, SGU_WIDTH), tile2),
                  pl.BlockSpec(x_blk, lambda i: (jnp.minimum(i, last_p), 0, 0)),
                  pl.BlockSpec(x_blk, const3)]
                 + cond(2) + cond(3) + cond(4) + cond(5)
                 + [pl.BlockSpec((ATTN_WIDTH, D_MODEL), const2),
                    pl.BlockSpec((SGU_WIDTH, D_MODEL), const2),
                    pl.BlockSpec((1, D_MODEL), const2),
                    pl.BlockSpec((D_MODEL, N_EXPERTS), const2),
                    pl.BlockSpec((D_MODEL, N_EXPERTS), const2),
                    pl.BlockSpec((D_MODEL, 2 * EXPERT_DIM), const2),
                    pl.BlockSpec((EXPERT_DIM, D_MODEL), const2)],
        out_specs=[pl.BlockSpec((TOKEN_TILE, D_MODEL), tile2),
                   pl.BlockSpec((TOKEN_TILE * ROW_TILES, LANES), tile2),
                   pl.BlockSpec((TOKEN_TILE, N_EXPERTS), tile2)],
        out_shape=[jax.ShapeDtypeStruct((t, D_MODEL), F32),
                   jax.ShapeDtypeStruct((t * ROW_TILES, LANES), ROW_DTYPE),
                   jax.ShapeDtypeStruct((t, N_EXPERTS), F32)],
        compiler_params=_params(1),
        name="merge",
    )(attn_p, attn_s, sgu_n, xp3, xs3, *([mod_p, mod_s] * 4), woa, wos, g2, wrh, wrl, wsgu, wsd)


def _first_argmax(x, iota, size, axis):
    m = jnp.max(x, axis=axis, keepdims=True)
    idx = jnp.min(jnp.where(x == m, iota, size), axis=axis, keepdims=True)
    return m, idx


def _route_kernel(sc_ref, bias_ref, tri_ref, eidx_ref, wts_ref, rank_ref, cnt_ref, base_ref):
    tt = sc_ref.shape[0]

    @pl.when(pl.program_id(0) == 0)
    def _():
        base_ref[...] = jnp.zeros_like(base_ref)

    s_t = sc_ref[...].T
    sel = s_t + bias_ref[...]
    sel3 = sel.reshape(N_EXPERT_GROUPS, GROUP_SIZE, tt)
    io3 = lax.broadcasted_iota(I32, sel3.shape, 1)
    m1, i1 = _first_argmax(sel3, io3, GROUP_SIZE, 1)
    m2 = jnp.max(jnp.where(io3 == i1, -jnp.inf, sel3), axis=1, keepdims=True)
    gs = (m1 + m2).reshape(N_EXPERT_GROUPS, tt)
    io8 = lax.broadcasted_iota(I32, gs.shape, 0)
    gmask = jnp.zeros(gs.shape, jnp.bool_)
    for _ in range(TOPK_GROUPS):
        _, gi = _first_argmax(gs, io8, N_EXPERT_GROUPS, 0)
        hit = io8 == gi
        gmask = jnp.logical_or(gmask, hit)
        gs = jnp.where(hit, -jnp.inf, gs)
    emask = jnp.broadcast_to(gmask.reshape(N_EXPERT_GROUPS, 1, tt), sel3.shape).reshape(N_EXPERTS, tt)
    cand = jnp.where(emask, sel, -jnp.inf)
    io = lax.broadcasted_iota(I32, cand.shape, 0)
    chosen = jnp.zeros(cand.shape, jnp.bool_)
    eidx, wts = [], []
    for _ in range(TOP_K):
        _, ei = _first_argmax(cand, io, N_EXPERTS, 0)
        hit = io == ei
        eidx.append(ei)
        wts.append(jnp.sum(jnp.where(hit, s_t, 0.0), axis=0, keepdims=True))
        chosen = jnp.logical_or(chosen, hit)
        cand = jnp.where(hit, -jnp.inf, cand)
    w = jnp.concatenate(wts, axis=0)
    wts_ref[...] = w / jnp.sum(w, axis=0, keepdims=True) * ROUTED_SCALE
    eidx_ref[...] = jnp.concatenate(eidx, axis=0)
    cf = jnp.where(chosen, 1.0, 0.0)
    ahead = jnp.dot(cf.astype(BF16), tri_ref[...], preferred_element_type=F32) + base_ref[...]
    ranks = [jnp.sum(jnp.where(io == ei, ahead, 0.0), axis=0, keepdims=True) for ei in eidx]
    rank_ref[...] = jnp.concatenate(ranks, axis=0).astype(I32)
    total = base_ref[...] + jnp.sum(cf, axis=1, keepdims=True)
    base_ref[...] = total
    cnt_ref[...] = total.astype(I32)


def _route(scores, bias_col, tri):
    t = scores.shape[0]
    n_tiles = t // TOKEN_TILE
    col = lambda i: (0, i)
    return pl.pallas_call(
        _route_kernel,
        grid=(n_tiles,),
        in_specs=[pl.BlockSpec((TOKEN_TILE, N_EXPERTS), lambda i: (i, 0)),
                  pl.BlockSpec((N_EXPERTS, 1), lambda i: (0, 0)),
                  pl.BlockSpec((TOKEN_TILE, TOKEN_TILE), lambda i: (0, 0))],
        out_specs=[pl.BlockSpec((TOP_K, TOKEN_TILE), col),
                   pl.BlockSpec((TOP_K, TOKEN_TILE), col),
                   pl.BlockSpec((TOP_K, TOKEN_TILE), col),
                   pl.BlockSpec((N_EXPERTS, 1), lambda i: (0, 0))],
        out_shape=[jax.ShapeDtypeStruct((TOP_K, t), I32),
                   jax.ShapeDtypeStruct((TOP_K, t), F32),
                   jax.ShapeDtypeStruct((TOP_K, t), I32),
                   jax.ShapeDtypeStruct((N_EXPERTS, 1), I32)],
        scratch_shapes=[pltpu.VMEM((N_EXPERTS, 1), F32)],
        compiler_params=_params(1),
        name="route",
    )(scores, bias_col, tri)


def _pos_kernel(eidx_ref, rank_ref, start_ref, pos_ref):
    e = eidx_ref[...]
    tt = e.shape[1]
    io = lax.broadcasted_iota(I32, (N_EXPERTS, tt), 0)
    st = start_ref[...]
    rows = [jnp.sum(jnp.where(io == e[k:k + 1, :], st, 0), axis=0, keepdims=True) for k in range(TOP_K)]
    pos_ref[...] = jnp.concatenate(rows, axis=0) + rank_ref[...]


def _positions(eidx, rank, start_col):
    t = eidx.shape[1]
    lane_tiles = t // LANES
    width = max(d for d in range(1, POS_TILE_MAX // LANES + 1) if lane_tiles % d == 0) * LANES
    col = lambda i: (0, i)
    blk = pl.BlockSpec((TOP_K, width), col)
    return pl.pallas_call(
        _pos_kernel,
        grid=(t // width,),
        in_specs=[blk, blk, pl.BlockSpec((N_EXPERTS, 1), lambda i: (0, 0))],
        out_specs=blk,
        out_shape=jax.ShapeDtypeStruct((TOP_K, t), I32),
        compiler_params=_params(1),
        name="positions",
    )(eidx, rank, start_col)


def _dispatch_kernel(pos_ref, h_ref, xs_ref, zero_ref, sem, zsem, *, n_rows):
    tt = h_ref.shape[0] // ROW_TILES
    first = pl.program_id(0) == 0

    @pl.when(first)
    def _():
        zero_ref[...] = jnp.zeros_like(zero_ref)
        _rows_copy(zero_ref, 0, xs_ref, n_rows, EXPERT_BLOCK, zsem).start()

    for t in range(tt):
        for k in range(TOP_K):
            _rows_copy(h_ref, t, xs_ref, pos_ref[k, t], 1, sem).start(priority=k % 2)
    for k in range(TOP_K):
        _rows_copy(h_ref, 0, xs_ref, 0, tt, sem).wait()

    @pl.when(first)
    def _():
        _rows_copy(zero_ref, 0, xs_ref, n_rows, EXPERT_BLOCK, zsem).wait()


def _dispatch(pos, h2_tiles, n_rows):
    t = h2_tiles.shape[0] // ROW_TILES
    return pl.pallas_call(
        functools.partial(_dispatch_kernel, n_rows=n_rows),
        grid=(t // TOKEN_TILE,),
        in_specs=[pl.BlockSpec((TOP_K, TOKEN_TILE), lambda i: (0, i), memory_space=pltpu.SMEM),
                  pl.BlockSpec((TOKEN_TILE * ROW_TILES, LANES), lambda i: (i, 0))],
        out_specs=pl.BlockSpec(memory_space=pl.ANY),
        out_shape=jax.ShapeDtypeStruct(((n_rows + EXPERT_BLOCK) * ROW_TILES, LANES), ROW_DTYPE),
        scratch_shapes=[pltpu.VMEM((EXPERT_BLOCK * ROW_TILES, LANES), ROW_DTYPE),
                        pltpu.SemaphoreType.DMA, pltpu.SemaphoreType.DMA],
        compiler_params=_params(1),
        name="dispatch",
    )(pos, h2_tiles)


def _out_pieces(nvalid, fn):
    s = 1 << (EXPERT_BLOCK.bit_length() - 1)
    while s >= 1:
        @pl.when((nvalid & s) != 0)
        def _(s=s):
            fn(nvalid & ~(2 * s - 1), s)
        s //= 2


def _for_block_size(nvalid, fn):
    lo = 0
    for rows in BLOCK_SIZES:
        @pl.when(jnp.logical_and(nvalid > lo, nvalid <= rows))
        def _(rows=rows):
            fn(rows)
        lo = rows


def _expert_kernel(count_ref, blk0_ref, row_ref, cnt_ref, xs_ref, wg_ref, wu_ref, wd_ref, ys_ref,
                   xbuf, ybuf, wg_buf, wu_buf, wd_buf, wgu_s, wd_s, state, xsem, ysem, wsem):
    e = pl.program_id(0)
    n_experts = pl.num_programs(0)
    cnt = count_ref[e]
    g0 = blk0_ref[e]
    nblk = (cnt + EXPERT_BLOCK - 1) // EXPERT_BLOCK
    wslot = lax.rem(e, W_SLOTS)

    def w_copies(expert, slot):
        return [pltpu.make_async_copy(src.at[expert], dst.at[slot], wsem.at[slot])
                for src, dst in ((wg_ref, wg_buf), (wu_ref, wu_buf), (wd_ref, wd_buf))]

    def w_fetch(expert):
        @pl.when(expert < n_experts)
        def _():
            for c in w_copies(expert, lax.rem(expert, W_SLOTS)):
                c.start()

    def x_copy(row, slot, rows):
        return _rows_copy(xs_ref, row, xbuf.at[slot], 0, rows, xsem.at[slot])

    def x_fetch(g, slot):
        _for_block_size(cnt_ref[g], lambda rows: x_copy(row_ref[g], slot, rows).start())

    def y_wait(nvalid, slot):
        def wait(off, size):
            _rows_copy(ybuf.at[slot], 0, ys_ref, 0, size, ysem.at[slot]).wait()
        _out_pieces(nvalid, wait)

    @pl.when(e == 0)
    def _():
        state[0] = 0
        state[1] = 0
        for g in range(X_SLOTS - 1):
            x_fetch(g, g)
        for ahead in range(W_SLOTS - 1):
            w_fetch(ahead)

    w_fetch(e + W_SLOTS - 1)
    for c in w_copies(e, wslot):
        c.wait()
    wgu_s[:, :EXPERT_DIM] = wg_buf[wslot].astype(BF16)
    wgu_s[:, EXPERT_DIM:] = wu_buf[wslot].astype(BF16)
    wd_s[...] = wd_buf[wslot].astype(BF16)

    def compute(rows, xslot, yslot, row0, nvalid):
        x_copy(row0, xslot, rows).wait()
        x = _load_rows(xbuf.at[xslot], rows).astype(BF16)
        gu = jnp.dot(x, wgu_s[...], preferred_element_type=F32)
        g = gu[:, :EXPERT_DIM]
        act = (g * jax.nn.sigmoid(g)) * gu[:, EXPERT_DIM:]
        y = jnp.dot(act.astype(BF16), wd_s[...], preferred_element_type=F32)
        y_wait(state[0], 1 - yslot)
        _store_rows(ybuf.at[yslot], y)

        def put(off, size):
            _rows_copy(ybuf.at[yslot], off, ys_ref, row0 + off, size, ysem.at[yslot]).start()
        _out_pieces(nvalid, put)

    def block(j, carry):
        g = g0 + j
        xslot = state[1]
        yslot = g & 1
        row0 = row_ref[g]
        nvalid = cnt_ref[g]
        x_fetch(g + X_SLOTS - 1, jnp.where(xslot == 0, X_SLOTS - 1, xslot - 1))
        _for_block_size(nvalid, lambda rows: compute(rows, xslot, yslot, row0, nvalid))
        state[0] = nvalid
        state[1] = jnp.where(xslot == X_SLOTS - 1, 0, xslot + 1)
        return carry

    lax.fori_loop(0, nblk, block, 0)

    @pl.when(e == pl.num_programs(0) - 1)
    def _():
        y_wait(state[0], (g0 + nblk - 1) & 1)


def _experts(count, blk0, blk_row, blk_cnt, xs, w_gate, w_up, w_down, n_rows):
    blk_rows = EXPERT_BLOCK * ROW_TILES
    any_spec = pl.BlockSpec(memory_space=pl.ANY)
    return pl.pallas_call(
        _expert_kernel,
        grid_spec=pltpu.PrefetchScalarGridSpec(
            num_scalar_prefetch=4,
            grid=(N_EXPERTS,),
            in_specs=[any_spec, any_spec, any_spec, any_spec],
            out_specs=any_spec,
            scratch_shapes=[pltpu.VMEM((X_SLOTS, blk_rows, LANES), ROW_DTYPE),
                            pltpu.VMEM((2, blk_rows, LANES), ROW_DTYPE),
                            pltpu.VMEM((W_SLOTS, D_MODEL, EXPERT_DIM), F32),
                            pltpu.VMEM((W_SLOTS, D_MODEL, EXPERT_DIM), F32),
                            pltpu.VMEM((W_SLOTS, EXPERT_DIM, D_MODEL), F32),
                            pltpu.VMEM((D_MODEL, 2 * EXPERT_DIM), BF16),
                            pltpu.VMEM((EXPERT_DIM, D_MODEL), BF16),
                            pltpu.SMEM((2,), I32),
                            pltpu.SemaphoreType.DMA((X_SLOTS,)),
                            pltpu.SemaphoreType.DMA((2,)),
                            pltpu.SemaphoreType.DMA((W_SLOTS,))]),
        out_shape=jax.ShapeDtypeStruct((n_rows * ROW_TILES, LANES), ROW_DTYPE),
        compiler_params=_params(1),
        name="experts",
    )(count, blk0, blk_row, blk_cnt, xs, w_gate, w_up, w_down)


def _combine_kernel(pos_ref, posn_ref, ys_ref, xb_ref, wts_ref, g2p_ref, g2s_ref, gf_ref, yp_ref, ysm_ref,
                    buf, ytile, sem, *, n_prompt_tiles, n_tiles):
    i = pl.program_id(0)
    tt = xb_ref.shape[0]
    slot = i & 1
    is_sample = i >= n_prompt_tiles

    def fetch(p_ref, s, t):
        for k in range(TOP_K):
            _rows_copy(ys_ref, p_ref[k, t], buf.at[s, k], t, 1, sem.at[s]).start(priority=k % 2)

    @pl.when(i == 0)
    def _():
        for t in range(tt):
            fetch(pos_ref, 0, t)

    for k in range(TOP_K):
        _rows_copy(ys_ref, 0, buf.at[slot, k], 0, tt, sem.at[slot]).wait()

    def finish(s, g):
        r0 = g * COMBINE_GROUP
        w = wts_ref[r0:r0 + COMBINE_GROUP, :]
        routed = None
        for k in range(TOP_K):
            rows = _load_rows(buf.at[s, k, pl.ds(r0 * ROW_TILES, COMBINE_GROUP * ROW_TILES)], COMBINE_GROUP)
            term = w[:, k:k + 1] * rows
            routed = term if routed is None else routed + term
        gate = jnp.where(is_sample, g2s_ref[g * COMBINE_GROUP // ROW_GROUP], g2p_ref[0])
        x = xb_ref[r0:r0 + COMBINE_GROUP, :] + gate * routed
        return _rms(x) * gf_ref[...]

    def tile_work(s, fetch_next):
        for g in range(tt // COMBINE_GROUP):
            y = finish(s, g)
            if fetch_next:
                for r in range(COMBINE_GROUP):
                    fetch(posn_ref, 1 - s, g * COMBINE_GROUP + r)
            ytile[g * COMBINE_GROUP:(g + 1) * COMBINE_GROUP, :] = y

    for s in range(2):
        @pl.when(jnp.logical_and(slot == s, i + 1 < n_tiles))
        def _(s=s):
            tile_work(s, True)

    @pl.when(i + 1 == n_tiles)
    def _():
        tile_work((n_tiles - 1) % 2, False)

    @pl.when(i < n_prompt_tiles)
    def _():
        yp_ref[...] = ytile[...]

    @pl.when(i >= n_prompt_tiles)
    def _():
        ysm_ref[...] = ytile[...]


def _combine(pos, ys, xb, wts_t, mod_p, mod_s, gf, batch, n_prompt_rows):
    t = xb.shape[0]
    tt = COMBINE_TILE
    n_tiles = t // tt
    n_p = n_prompt_rows // tt
    pos_spec = lambda f: pl.BlockSpec((TOP_K, tt), f, memory_space=pltpu.SMEM)
    return pl.pallas_call(
        functools.partial(_combine_kernel, n_prompt_tiles=n_p, n_tiles=n_tiles),
        grid=(n_tiles,),
        in_specs=[pos_spec(lambda i: (0, i)),
                  pos_spec(lambda i: (0, jnp.minimum(i + 1, n_tiles - 1))),
                  pl.BlockSpec(memory_space=pl.ANY),
                  pl.BlockSpec((tt, D_MODEL), lambda i: (i, 0)),
                  pl.BlockSpec((tt, TOP_K), lambda i: (i, 0))]
                 + _cond_specs(5, tt, n_p // batch, batch, n_p, n_tiles - n_p)
                 + [pl.BlockSpec((1, D_MODEL), lambda i: (0, 0))],
        out_specs=[pl.BlockSpec((tt, D_MODEL), lambda i: (jnp.minimum(i, n_p - 1), 0)),
                   pl.BlockSpec((tt, D_MODEL), lambda i: (jnp.maximum(i - n_p, 0), 0))],
        out_shape=[jax.ShapeDtypeStruct((n_prompt_rows, D_MODEL), F32),
                   jax.ShapeDtypeStruct((t - n_prompt_rows, D_MODEL), F32)],
        scratch_shapes=[pltpu.VMEM((2, TOP_K, tt * ROW_TILES, LANES), ROW_DTYPE),
                        pltpu.VMEM((tt, D_MODEL), F32),
                        pltpu.SemaphoreType.DMA((2,))],
        compiler_params=_params(1),
        name="combine",
    )(pos, pos, ys, xb, wts_t, mod_p, mod_s, gf)


def _sgu_mask():
    pos = np.arange(SGU_CHUNK) // CHUNK
    return jnp.asarray(pos[None, :] <= pos[:, None])


def _layer(xp3, xs3, cache_k, cache_v, mod, final_g, batch, seq, dec_batch, n, p):
    tp = batch * seq
    t = tp + dec_batch * n
    n_prompt_tiles = tp // TOKEN_TILE
    mod_p = mod[:batch].reshape(batch, 1, N_MOD * D_MODEL)
    mod_s = mod[batch:].reshape(dec_batch, 1, N_MOD * D_MODEL)

    wm = jnp.where(_sgu_mask(), p["sgu_w"], 0.0)
    reps = SGU_CHUNK // n
    eye = jnp.eye(reps, dtype=F32)
    wms = jnp.stack([jnp.kron(eye, wm[g, :n, :n]) for g in range(SGU_GROUPS)])
    bs = jnp.broadcast_to(p["sgu_b"][:, :, None], (SGU_GROUPS, SGU_CHUNK, SGU_CHUNK))
    bss = jnp.broadcast_to(jnp.tile(p["sgu_b"][:, :n], (1, reps))[:, :, None], (SGU_GROUPS, SGU_CHUNK, SGU_CHUNK))

    q, k, v, sgu_n, vn_s = _stage1(
        xp3, xs3, mod_p, mod_s, p["norm1_g"].reshape(1, -1), p["w_in"].astype(BF16),
        p["sgu_ln_g"].reshape(1, -1), p["sgu_ln_b"].reshape(1, -1),
        wm.astype(BF16), wms.astype(BF16), bs, bss, p["sgu_out_g"].reshape(1, -1), batch, n_prompt_tiles)

    gattn = p["attn_out_g"].reshape(1, -1)
    attn_p = _attn_prompt(q, k, v, p["attn_sinks"], gattn, batch, seq)
    w = cache_k.shape[1]
    attn_s = _attn_sample(q, k, v, cache_k.reshape(dec_batch, w, KV_WIDTH), cache_v.reshape(dec_batch, w, KV_WIDTH),
                          p["attn_sinks"], gattn, dec_batch, n, tp)

    wo = p["w_out"].astype(BF16)
    wsgu = jnp.concatenate([p["ws_gate"], p["ws_up"]], axis=1).astype(BF16)
    wrh = p["w_router"].astype(BF16)
    wrl = (p["w_router"] - wrh.astype(F32)).astype(BF16)
    xb, h2_tiles, scores = _merge(attn_p, attn_s, sgu_n, xp3, xs3, mod_p, mod_s,
                                  wo[:ATTN_WIDTH], wo[ATTN_WIDTH:], p["norm2_g"].reshape(1, -1), wrh, wrl,
                                  wsgu, p["ws_down"].astype(BF16), batch, n_prompt_tiles)

    ti = np.arange(TOKEN_TILE)
    tri = jnp.asarray(ti[:, None] < ti[None, :], dtype=BF16)
    eidx, wts, rank, counts = _route(scores, p["router_bias"].reshape(N_EXPERTS, 1), tri)

    counts = counts[:, 0]
    ends = jnp.cumsum(counts)
    start = (ends - counts).astype(I32)
    nblk = (counts + EXPERT_BLOCK - 1) // EXPERT_BLOCK
    blk_end = jnp.cumsum(nblk)
    blk0 = (blk_end - nblk).astype(I32)
    pos = _positions(eidx, rank, start.reshape(N_EXPERTS, 1))

    n_rows = t * TOP_K
    g = jnp.arange(n_rows // EXPERT_BLOCK + N_EXPERTS + X_SLOTS, dtype=I32)[:, None]
    mine = jnp.logical_and(g >= blk0[None, :], g < blk_end[None, :])
    off = (g - blk0[None, :]) * EXPERT_BLOCK
    blk_row = jnp.sum(jnp.where(mine, start[None, :] + off, 0), axis=1).astype(I32)
    blk_cnt = jnp.sum(jnp.where(mine, jnp.minimum(counts[None, :] - off, EXPERT_BLOCK), 0), axis=1).astype(I32)

    xs_sorted = _dispatch(pos, h2_tiles, n_rows)
    ys_sorted = _experts(counts.astype(I32), blk0, blk_row, blk_cnt, xs_sorted,
                         p["w_gate"], p["w_up"], p["w_down"], n_rows)
    y_p, y_s = _combine(pos, ys_sorted, xb, wts.T, mod_p, mod_s, final_g.reshape(1, -1), batch, tp)
    return y_p, y_s, k, v, vn_s


def kernel(x_prompt, x_sample, cache_k, cache_v, c_prompt, c_sample, norm1_g, w_ada, b_ada, w_in, sgu_ln_g, sgu_ln_b, sgu_w, sgu_b, attn_sinks, attn_out_g, sgu_out_g, w_out, norm2_g, w_router, router_bias, w_gate, w_up, w_down, ws_gate, ws_up, ws_down, final_g):
    batch, seq, d = x_prompt.shape
    dec_batch, n, _ = x_sample.shape
    depth = norm1_g.shape[0]
    assert depth == 1 and d == D_MODEL
    assert seq % TOKEN_TILE == 0 and dec_batch * n == TOKEN_TILE and n == ROW_GROUP
    assert seq % ATTN_TILE == 0 and seq >= WINDOW and cache_k.shape[2] == WINDOW
    tp = batch * seq

    xp3 = x_prompt.reshape(tp // ROW_GROUP, ROW_GROUP, d)
    xs3 = x_sample.reshape(dec_batch * n // ROW_GROUP, ROW_GROUP, d)
    c_all = jnp.concatenate([c_prompt, c_sample], axis=0)
    l = 0
    mod = _adaln(c_all, w_ada[l], b_ada[l])
    p = dict(norm1_g=norm1_g[l], w_in=w_in[l], sgu_ln_g=sgu_ln_g[l], sgu_ln_b=sgu_ln_b[l], sgu_w=sgu_w[l],
             sgu_b=sgu_b[l], attn_sinks=attn_sinks[l], attn_out_g=attn_out_g[l], sgu_out_g=sgu_out_g[l],
             w_out=w_out[l], norm2_g=norm2_g[l], w_router=w_router[l], router_bias=router_bias[l],
             w_gate=w_gate[l], w_up=w_up[l], w_down=w_down[l], ws_gate=ws_gate[l], ws_up=ws_up[l],
             ws_down=ws_down[l])
    y_p, y_s, k, v, vn_s = _layer(xp3, xs3, cache_k[l], cache_v[l], mod, final_g, batch, seq, dec_batch, n, p)

    keep = min(WINDOW, seq)
    kv_shape = (batch, keep, N_KV_HEADS, HEAD_DIM)
    k_p = jnp.stack([k[(b + 1) * seq - keep:(b + 1) * seq] for b in range(batch)]).reshape(kv_shape)
    v_p = jnp.stack([v[(b + 1) * seq - keep:(b + 1) * seq] for b in range(batch)]).reshape(kv_shape)
    k_s = k[tp:].reshape(dec_batch, n, N_KV_HEADS, HEAD_DIM)
    v_s = v[tp:].reshape(dec_batch, n, N_KV_HEADS, HEAD_DIM)
    return (y_p.reshape(batch, seq, d), y_s.reshape(dec_batch, n, d),
            k_p[None], v_p[None], k_s[None], v_s[None], vn_s.reshape(dec_batch, n, SGU_WIDTH)[None])
```

```python
import functools

import numpy as np
import jax
import jax.numpy as jnp
from jax import lax
from jax.experimental import pallas as pl
from jax.experimental.pallas import tpu as pltpu

F32 = jnp.float32
BF16 = jnp.bfloat16
I32 = jnp.int32

D_MODEL = 1024
CHUNK = 64
HEAD_DIM = 64
ATTN_WIDTH = 512
N_HEADS = 8
N_KV_HEADS = 2
GQA_GROUP = 4
KV_WIDTH = 128
WINDOW = 128
SGU_CHUNK = 128
SGU_WIDTH = 512
SGU_GROUPS = 4
IN_COLS = ATTN_WIDTH + 2 * KV_WIDTH + 2 * SGU_WIDTH
N_MOD = 6
N_EXPERTS = 256
TOP_K = 8
N_EXPERT_GROUPS = 8
GROUP_SIZE = N_EXPERTS // N_EXPERT_GROUPS
TOPK_GROUPS = 4
EXPERT_DIM = 256
ROUTED_SCALE = 2.5
EPS = 1e-6

LANES = 128
ROW_WORDS = D_MODEL // 2
ROW_TILES = ROW_WORDS // LANES
ROW_DTYPE = jnp.uint32
ROW_GROUP = 16
TOKEN_TILE = 256
COND_REPEAT = TOKEN_TILE // ROW_GROUP
ATTN_TILE = 256
EXPERT_BLOCK = 640
BLOCK_SIZES = (128, 256, 384, 512, 576, EXPERT_BLOCK)
X_SLOTS = 4
W_SLOTS = 4
POS_TILE_MAX = 4096
COMBINE_TILE = 128
COMBINE_GROUP = 8
VMEM_LIMIT_BYTES = 56 * 1024 * 1024


def _params(n_axes=1):
    return pltpu.CompilerParams(dimension_semantics=("arbitrary",) * n_axes,
                                vmem_limit_bytes=VMEM_LIMIT_BYTES)


def _rms(x):
    return x * lax.rsqrt(jnp.mean(x * x, axis=-1, keepdims=True) + EPS)


def _cond_rows(m_ref):
    m = m_ref[...]
    g, _, d = m.shape
    return jnp.broadcast_to(m, (g, ROW_GROUP, d)).reshape(g * ROW_GROUP, d)


def _cond_spec(piece, tile_rows, tiles_per_batch, batch, n_prompt_tiles):
    g = tile_rows // ROW_GROUP
    per = COND_REPEAT // g
    return pl.BlockSpec(
        (g, 1, D_MODEL),
        lambda i: (jnp.where(i < n_prompt_tiles, (i // tiles_per_batch) * per, batch * per + i - n_prompt_tiles),
                   0, piece))


def _pick_tile(is_sample, prompt_ref, sample_ref):
    v = jnp.where(is_sample, sample_ref[...], prompt_ref[...])
    return v.reshape(TOKEN_TILE, v.shape[-1])


def _pack_rows(x):
    x = x.astype(F32)
    return pltpu.pack_elementwise([x[:, :ROW_WORDS], x[:, ROW_WORDS:]], packed_dtype=BF16)


def _unpack_rows(w):
    return jnp.concatenate([pltpu.unpack_elementwise(w, index=i, packed_dtype=BF16, unpacked_dtype=F32)
                            for i in range(2)], axis=1)


def _load_rows(ref, n_rows):
    w = jnp.concatenate([ref[pl.ds(j, n_rows, stride=ROW_TILES), :] for j in range(ROW_TILES)], axis=1)
    return _unpack_rows(w)


def _store_rows(ref, x):
    n_rows = x.shape[0]
    w = _pack_rows(x)
    for j in range(ROW_TILES):
        ref[pl.ds(j, n_rows, stride=ROW_TILES), :] = w[:, j * LANES:(j + 1) * LANES]


def _tile_offset(row):
    off = row * ROW_TILES
    return off if isinstance(off, int) else pl.multiple_of(off, ROW_TILES)


def _rows_copy(src_ref, src_row, dst_ref, dst_row, n_rows, sem):
    s = _tile_offset(src_row)
    d = _tile_offset(dst_row)
    return pltpu.make_async_copy(src_ref.at[pl.ds(s, n_rows * ROW_TILES)],
                                 dst_ref.at[pl.ds(d, n_rows * ROW_TILES)], sem)


def _adaln_kernel(c_ref, w_ref, b_ref, o_ref):
    c = c_ref[...]
    s = c * jax.nn.sigmoid(c)
    o_ref[...] = jnp.dot(s, w_ref[...], precision=lax.Precision.HIGHEST,
                         preferred_element_type=F32) + b_ref[...]


def _adaln(c, w_ada, b_ada):
    n, d = c.shape
    cols = w_ada.shape[1]
    tn = 1536
    return pl.pallas_call(
        _adaln_kernel,
        grid=(cols // tn,),
        in_specs=[pl.BlockSpec((n, d), lambda j: (0, 0)),
                  pl.BlockSpec((d, tn), lambda j: (0, j)),
                  pl.BlockSpec((1, tn), lambda j: (0, j))],
        out_specs=pl.BlockSpec((n, tn), lambda j: (0, j)),
        out_shape=jax.ShapeDtypeStruct((n, cols), F32),
        compiler_params=_params(1),
        name="adaln",
    )(c, w_ada, b_ada.reshape(1, cols))


def _stage1_kernel(xp_ref, xs_ref, sh_ref, sc_ref, g1_ref, win_ref, lng_ref, lnb_ref,
                   wm_ref, wms_ref, bs_ref, bss_ref, gsgu_ref,
                   q_ref, k_ref, v_ref, sgu_ref, vn_ref, *, n_prompt_tiles):
    is_sample = pl.program_id(0) >= n_prompt_tiles
    x = _pick_tile(is_sample, xp_ref, xs_ref)
    h = _rms(x) * g1_ref[...] * (1.0 + _cond_rows(sc_ref)) + _cond_rows(sh_ref)
    proj = jnp.dot(h.astype(BF16), win_ref[...], preferred_element_type=F32)
    q_ref[...] = (proj[:, :ATTN_WIDTH] * (HEAD_DIM ** -0.5)).astype(BF16)
    k_ref[...] = proj[:, ATTN_WIDTH:ATTN_WIDTH + KV_WIDTH]
    v_ref[...] = proj[:, ATTN_WIDTH + KV_WIDTH:ATTN_WIDTH + 2 * KV_WIDTH]
    c0 = ATTN_WIDTH + 2 * KV_WIDTH
    u = jax.nn.gelu(proj[:, c0:c0 + SGU_WIDTH])
    vg = jax.nn.gelu(proj[:, c0 + SGU_WIDTH:])
    mu = jnp.mean(vg, axis=-1, keepdims=True)
    xc = vg - mu
    var = jnp.mean(xc * xc, axis=-1, keepdims=True)
    vn = xc * lax.rsqrt(var + EPS) * lng_ref[...] + lnb_ref[...]

    @pl.when(is_sample)
    def _():
        vn_ref[...] = vn

    vnb = vn.astype(BF16)
    wm = jnp.where(is_sample, wms_ref[...], wm_ref[...])
    bs = jnp.where(is_sample, bss_ref[...], bs_ref[...])
    rows = []
    for c in range(TOKEN_TILE // SGU_CHUNK):
        r0 = c * SGU_CHUNK
        cols = []
        for g in range(SGU_GROUPS):
            l0 = g * SGU_CHUNK
            sv = jnp.dot(wm[g], vnb[r0:r0 + SGU_CHUNK, l0:l0 + SGU_CHUNK],
                         preferred_element_type=F32) + bs[g]
            cols.append(sv)
        rows.append(jnp.concatenate(cols, axis=1))
    sgu = u * jnp.concatenate(rows, axis=0)
    sgu_ref[...] = (_rms(sgu) * gsgu_ref[...]).astype(BF16)


def _stage1(xp3, xs3, cond_tbl, g1, win_b, lng, lnb, wm, wms, bs, bss, gsgu, batch, n_prompt_tiles):
    t = (xp3.shape[0] + xs3.shape[0]) * ROW_GROUP
    n_tiles = t // TOKEN_TILE
    last_p = n_prompt_tiles - 1
    x_blk = (TOKEN_TILE // ROW_GROUP, ROW_GROUP, D_MODEL)
    const2 = lambda i: (0, 0)
    const3 = lambda i: (0, 0, 0)
    tile2 = lambda i: (i, 0)
    cond = functools.partial(_cond_spec, tile_rows=TOKEN_TILE, tiles_per_batch=n_prompt_tiles // batch,
                             batch=batch, n_prompt_tiles=n_prompt_tiles)
    sgu_w_spec = pl.BlockSpec((SGU_GROUPS, SGU_CHUNK, SGU_CHUNK), const3)
    return pl.pallas_call(
        functools.partial(_stage1_kernel, n_prompt_tiles=n_prompt_tiles),
        grid=(n_tiles,),
        in_specs=[pl.BlockSpec(x_blk, lambda i: (jnp.minimum(i, last_p), 0, 0)),
                  pl.BlockSpec(x_blk, const3)]
                 + [cond(0), cond(1)]
                 + [pl.BlockSpec((1, D_MODEL), const2),
                    pl.BlockSpec((D_MODEL, IN_COLS), const2),
                    pl.BlockSpec((1, SGU_WIDTH), const2),
                    pl.BlockSpec((1, SGU_WIDTH), const2),
                    sgu_w_spec, sgu_w_spec, sgu_w_spec, sgu_w_spec,
                    pl.BlockSpec((1, SGU_WIDTH), const2)],
        out_specs=[pl.BlockSpec((TOKEN_TILE, ATTN_WIDTH), tile2),
                   pl.BlockSpec((TOKEN_TILE, KV_WIDTH), tile2),
                   pl.BlockSpec((TOKEN_TILE, KV_WIDTH), tile2),
                   pl.BlockSpec((TOKEN_TILE, SGU_WIDTH), tile2),
                   pl.BlockSpec((TOKEN_TILE, SGU_WIDTH), const2)],
        out_shape=[jax.ShapeDtypeStruct((t, ATTN_WIDTH), BF16),
                   jax.ShapeDtypeStruct((t, KV_WIDTH), F32),
                   jax.ShapeDtypeStruct((t, KV_WIDTH), F32),
                   jax.ShapeDtypeStruct((t, SGU_WIDTH), BF16),
                   jax.ShapeDtypeStruct((TOKEN_TILE, SGU_WIDTH), F32)],
        compiler_params=_params(1),
        name="stage1",
    )(xp3, xs3, cond_tbl, cond_tbl, g1, win_b, lng, lnb, wm, wms, bs, bss, gsgu)


def _alibi_slope(h):
    return float(np.float32(2.0) ** np.float32(-8.0 * (h + 1.0) / N_HEADS))


def _sink_attention(q, k, v, back, valid_fn, sinks_ref, gout):
    nq = q.shape[0]
    nk = k.shape[0]
    q_t = q.astype(F32).T.astype(BF16)
    k = k.astype(BF16)
    v_t = v.T.astype(BF16)
    c = lax.broadcasted_iota(I32, (nk, nq), 0)
    r = lax.broadcasted_iota(I32, (nk, nq), 1)
    dist = jnp.abs(r - c + back).astype(F32)
    valid = None if valid_fn is None else valid_fn(c, r)
    outs = []
    for kvh in range(N_KV_HEADS):
        kv0 = kvh * HEAD_DIM
        heads = range(kvh * GQA_GROUP, (kvh + 1) * GQA_GROUP)
        qs = jnp.concatenate([q_t[h * HEAD_DIM:(h + 1) * HEAD_DIM, :] for h in heads], axis=1)
        l = jnp.dot(k[:, kv0:kv0 + HEAD_DIM], qs, preferred_element_type=F32)
        biased = [l[:, g * nq:(g + 1) * nq] - _alibi_slope(h) * dist for g, h in enumerate(heads)]
        if valid is not None:
            biased = [jnp.where(valid, b, -jnp.inf) for b in biased]
        l = jnp.concatenate(biased, axis=1)
        sink = jnp.concatenate([jnp.full((1, nq), sinks_ref[h], F32) for h in heads], axis=1)
        m = jnp.maximum(jnp.max(l, axis=0, keepdims=True), sink)
        e = jnp.exp(l - m)
        denom = jnp.sum(e, axis=0, keepdims=True) + jnp.exp(sink - m)
        pv = jnp.dot(v_t[kv0:kv0 + HEAD_DIM, :], e.astype(BF16), preferred_element_type=F32)
        o_t = pv / denom
        outs.extend(o_t[:, g * nq:(g + 1) * nq] for g in range(GQA_GROUP))
    o = jnp.concatenate(outs, axis=0).T
    return (_rms(o) * gout).astype(BF16)


def _attn_prompt_kernel(sinks_ref, q_ref, kp_ref, kc_ref, vp_ref, vc_ref, g_ref, o_ref):
    i = pl.program_id(1)
    back = kp_ref.shape[0]

    def valid_fn(c, r):
        kc = c // CHUNK
        qc = r // CHUNK + (back // CHUNK)
        return jnp.logical_and(jnp.logical_and(kc <= qc, kc >= qc - WINDOW // CHUNK),
                               jnp.logical_or(i > 0, kc >= back // CHUNK))

    k = jnp.concatenate([kp_ref[...], kc_ref[...]], axis=0)
    v = jnp.concatenate([vp_ref[...], vc_ref[...]], axis=0)
    o_ref[...] = _sink_attention(q_ref[...], k, v, back, valid_fn, sinks_ref, g_ref[...])


def _attn_prompt(q, k, v, sinks, gattn, batch, seq):
    nt = seq // ATTN_TILE
    per_tile = ATTN_TILE // WINDOW
    cur = lambda b, i, s: (b * nt + i, 0)
    prev = lambda b, i, s: (jnp.maximum((b * nt + i) * per_tile - 1, 0), 0)
    kv_blk = (ATTN_TILE, KV_WIDTH)
    back_blk = (WINDOW, KV_WIDTH)
    return pl.pallas_call(
        _attn_prompt_kernel,
        grid_spec=pltpu.PrefetchScalarGridSpec(
            num_scalar_prefetch=1,
            grid=(batch, nt),
            in_specs=[pl.BlockSpec((ATTN_TILE, ATTN_WIDTH), cur),
                      pl.BlockSpec(back_blk, prev), pl.BlockSpec(kv_blk, cur),
                      pl.BlockSpec(back_blk, prev), pl.BlockSpec(kv_blk, cur),
                      pl.BlockSpec((1, ATTN_WIDTH), lambda b, i, s: (0, 0))],
            out_specs=pl.BlockSpec((ATTN_TILE, ATTN_WIDTH), cur)),
        out_shape=jax.ShapeDtypeStruct((batch * seq, ATTN_WIDTH), BF16),
        compiler_params=_params(2),
        name="attn_prompt",
    )(sinks, q, k, k, v, v, gattn)


def _attn_sample_kernel(sinks_ref, q_ref, ck_ref, cv_ref, k_ref, v_ref, g_ref, o_ref):
    k = jnp.concatenate([ck_ref[0], k_ref[...]], axis=0)
    v = jnp.concatenate([cv_ref[0], v_ref[...]], axis=0)
    o_ref[...] = _sink_attention(q_ref[...], k, v, ck_ref.shape[1], None, sinks_ref, g_ref[...])


def _attn_sample(q, k, v, cache_k, cache_v, sinks, gattn, dec_batch, n, row0):
    w = cache_k.shape[1]
    blk0 = row0 // n
    new = lambda b, s: (blk0 + b, 0)
    return pl.pallas_call(
        _attn_sample_kernel,
        grid_spec=pltpu.PrefetchScalarGridSpec(
            num_scalar_prefetch=1,
            grid=(dec_batch,),
            in_specs=[pl.BlockSpec((n, ATTN_WIDTH), new),
                      pl.BlockSpec((1, w, KV_WIDTH), lambda b, s: (b, 0, 0)),
                      pl.BlockSpec((1, w, KV_WIDTH), lambda b, s: (b, 0, 0)),
                      pl.BlockSpec((n, KV_WIDTH), new),
                      pl.BlockSpec((n, KV_WIDTH), new),
                      pl.BlockSpec((1, ATTN_WIDTH), lambda b, s: (0, 0))],
            out_specs=pl.BlockSpec((n, ATTN_WIDTH), lambda b, s: (b, 0))),
        out_shape=jax.ShapeDtypeStruct((dec_batch * n, ATTN_WIDTH), BF16),
        compiler_params=_params(1),
        name="attn_sample",
    )(sinks, q, cache_k, cache_v, k, v, gattn)


def _merge_kernel(ap_ref, as_ref, sgu_ref, xp_ref, xs_ref,
                  gate1_ref, sh2_ref, sc2_ref, gate2_ref,
                  woa_ref, wos_ref, g2_ref, wrh_ref, wrl_ref, wsgu_ref, wsd_ref,
                  xb_ref, h2_ref, sc_ref, *, n_prompt_tiles):
    is_sample = pl.program_id(0) >= n_prompt_tiles
    x = _pick_tile(is_sample, xp_ref, xs_ref)
    a = jnp.where(is_sample, as_ref[...], ap_ref[...])
    mix = (jnp.dot(a, woa_ref[...], preferred_element_type=F32)
           + jnp.dot(sgu_ref[...], wos_ref[...], preferred_element_type=F32))
    x1 = x + _cond_rows(gate1_ref) * mix
    h2 = _rms(x1) * g2_ref[...] * (1.0 + _cond_rows(sc2_ref)) + _cond_rows(sh2_ref)
    hh = h2.astype(BF16)
    _store_rows(h2_ref, hh)
    hl = (h2 - hh.astype(F32)).astype(BF16)
    logits = (jnp.dot(hh, wrh_ref[...], preferred_element_type=F32)
              + (jnp.dot(hl, wrh_ref[...], preferred_element_type=F32)
                 + jnp.dot(hh, wrl_ref[...], preferred_element_type=F32)))
    sc_ref[...] = jax.nn.sigmoid(logits)
    gu = jnp.dot(hh, wsgu_ref[...], preferred_element_type=F32)
    g = gu[:, :EXPERT_DIM]
    act = (g * jax.nn.sigmoid(g)) * gu[:, EXPERT_DIM:]
    shared = jnp.dot(act.astype(BF16), wsd_ref[...], preferred_element_type=F32)
    xb_ref[...] = x1 + _cond_rows(gate2_ref) * shared


def _merge(attn_p, attn_s, sgu_n, xp3, xs3, cond_tbl, woa, wos, g2, wrh, wrl, wsgu, wsd, batch, n_prompt_tiles):
    t = sgu_n.shape[0]
    n_tiles = t // TOKEN_TILE
    last_p = n_prompt_tiles - 1
    x_blk = (TOKEN_TILE // ROW_GROUP, ROW_GROUP, D_MODEL)
    const2 = lambda i: (0, 0)
    const3 = lambda i: (0, 0, 0)
    tile2 = lambda i: (i, 0)
    cond = functools.partial(_cond_spec, tile_rows=TOKEN_TILE, tiles_per_batch=n_prompt_tiles // batch,
                             batch=batch, n_prompt_tiles=n_prompt_tiles)
    return pl.pallas_call(
        functools.partial(_merge_kernel, n_prompt_tiles=n_prompt_tiles),
        grid=(n_tiles,),
        in_specs=[pl.BlockSpec((TOKEN_TILE, ATTN_WIDTH), lambda i: (jnp.minimum(i, last_p), 0)),
                  pl.BlockSpec((TOKEN_TILE, ATTN_WIDTH), const2),
                  pl.BlockSpec((TOKEN_TILE, SGU_WIDTH), tile2),
                  pl.BlockSpec(x_blk, lambda i: (jnp.minimum(i, last_p), 0, 0)),
                  pl.BlockSpec(x_blk, const3)]
                 + [cond(2), cond(3), cond(4), cond(5)]
                 + [pl.BlockSpec((ATTN_WIDTH, D_MODEL), const2),
                    pl.BlockSpec((SGU_WIDTH, D_MODEL), const2),
                    pl.BlockSpec((1, D_MODEL), const2),
                    pl.BlockSpec((D_MODEL, N_EXPERTS), const2),
                    pl.BlockSpec((D_MODEL, N_EXPERTS), const2),
                    pl.BlockSpec((D_MODEL, 2 * EXPERT_DIM), const2),
                    pl.BlockSpec((EXPERT_DIM, D_MODEL), const2)],
        out_specs=[pl.BlockSpec((TOKEN_TILE, D_MODEL), tile2),
                   pl.BlockSpec((TOKEN_TILE * ROW_TILES, LANES), tile2),
                   pl.BlockSpec((TOKEN_TILE, N_EXPERTS), tile2)],
        out_shape=[jax.ShapeDtypeStruct((t, D_MODEL), F32),
                   jax.ShapeDtypeStruct((t * ROW_TILES, LANES), ROW_DTYPE),
                   jax.ShapeDtypeStruct((t, N_EXPERTS), F32)],
        compiler_params=_params(1),
        name="merge",
    )(attn_p, attn_s, sgu_n, xp3, xs3, *([cond_tbl] * 4), woa, wos, g2, wrh, wrl, wsgu, wsd)


def _first_argmax(x, iota, size, axis):
    m = jnp.max(x, axis=axis, keepdims=True)
    idx = jnp.min(jnp.where(x == m, iota, size), axis=axis, keepdims=True)
    return m, idx


def _route_kernel(sc_ref, bias_ref, tri_ref, eidx_ref, wts_ref, rank_ref, cnt_ref, base_ref):
    tt = sc_ref.shape[0]

    @pl.when(pl.program_id(0) == 0)
    def _():
        base_ref[...] = jnp.zeros_like(base_ref)

    s_t = sc_ref[...].T
    sel = s_t + bias_ref[...]
    sel3 = sel.reshape(N_EXPERT_GROUPS, GROUP_SIZE, tt)
    io3 = lax.broadcasted_iota(I32, sel3.shape, 1)
    m1, i1 = _first_argmax(sel3, io3, GROUP_SIZE, 1)
    m2 = jnp.max(jnp.where(io3 == i1, -jnp.inf, sel3), axis=1, keepdims=True)
    gs = (m1 + m2).reshape(N_EXPERT_GROUPS, tt)
    io8 = lax.broadcasted_iota(I32, gs.shape, 0)
    gmask = jnp.zeros(gs.shape, jnp.bool_)
    for _ in range(TOPK_GROUPS):
        _, gi = _first_argmax(gs, io8, N_EXPERT_GROUPS, 0)
        hit = io8 == gi
        gmask = jnp.logical_or(gmask, hit)
        gs = jnp.where(hit, -jnp.inf, gs)
    emask = jnp.broadcast_to(gmask.reshape(N_EXPERT_GROUPS, 1, tt), sel3.shape).reshape(N_EXPERTS, tt)
    cand = jnp.where(emask, sel, -jnp.inf)
    io = lax.broadcasted_iota(I32, cand.shape, 0)
    chosen = jnp.zeros(cand.shape, jnp.bool_)
    eidx, wts = [], []
    for _ in range(TOP_K):
        _, ei = _first_argmax(cand, io, N_EXPERTS, 0)
        hit = io == ei
        eidx.append(ei)
        wts.append(jnp.sum(jnp.where(hit, s_t, 0.0), axis=0, keepdims=True))
        chosen = jnp.logical_or(chosen, hit)
        cand = jnp.where(hit, -jnp.inf, cand)
    w = jnp.concatenate(wts, axis=0)
    wts_ref[...] = w / jnp.sum(w, axis=0, keepdims=True) * ROUTED_SCALE
    eidx_ref[...] = jnp.concatenate(eidx, axis=0)
    cf = jnp.where(chosen, 1.0, 0.0)
    ahead = jnp.dot(cf.astype(BF16), tri_ref[...], preferred_element_type=F32) + base_ref[...]
    ranks = [jnp.sum(jnp.where(io == ei, ahead, 0.0), axis=0, keepdims=True) for ei in eidx]
    rank_ref[...] = jnp.concatenate(ranks, axis=0).astype(I32)
    total = base_ref[...] + jnp.sum(cf, axis=1, keepdims=True)
    base_ref[...] = total
    cnt_ref[...] = total.astype(I32)


def _route(scores, bias_col, tri):
    t = scores.shape[0]
    n_tiles = t // TOKEN_TILE
    col = lambda i: (0, i)
    return pl.pallas_call(
        _route_kernel,
        grid=(n_tiles,),
        in_specs=[pl.BlockSpec((TOKEN_TILE, N_EXPERTS), lambda i: (i, 0)),
                  pl.BlockSpec((N_EXPERTS, 1), lambda i: (0, 0)),
                  pl.BlockSpec((TOKEN_TILE, TOKEN_TILE), lambda i: (0, 0))],
        out_specs=[pl.BlockSpec((TOP_K, TOKEN_TILE), col),
                   pl.BlockSpec((TOP_K, TOKEN_TILE), col),
                   pl.BlockSpec((TOP_K, TOKEN_TILE), col),
                   pl.BlockSpec((N_EXPERTS, 1), lambda i: (0, 0))],
        out_shape=[jax.ShapeDtypeStruct((TOP_K, t), I32),
                   jax.ShapeDtypeStruct((TOP_K, t), F32),
                   jax.ShapeDtypeStruct((TOP_K, t), I32),
                   jax.ShapeDtypeStruct((N_EXPERTS, 1), I32)],
        scratch_shapes=[pltpu.VMEM((N_EXPERTS, 1), F32)],
        compiler_params=_params(1),
        name="route",
    )(scores, bias_col, tri)


def _pos_kernel(eidx_ref, rank_ref, start_ref, pos_ref):
    e = eidx_ref[...]
    tt = e.shape[1]
    io = lax.broadcasted_iota(I32, (N_EXPERTS, tt), 0)
    st = start_ref[...]
    rows = [jnp.sum(jnp.where(io == e[k:k + 1, :], st, 0), axis=0, keepdims=True) for k in range(TOP_K)]
    pos_ref[...] = jnp.concatenate(rows, axis=0) + rank_ref[...]


def _positions(eidx, rank, start_col):
    t = eidx.shape[1]
    lane_tiles = t // LANES
    width = max(d for d in range(1, POS_TILE_MAX // LANES + 1) if lane_tiles % d == 0) * LANES
    col = lambda i: (0, i)
    blk = pl.BlockSpec((TOP_K, width), col)
    return pl.pallas_call(
        _pos_kernel,
        grid=(t // width,),
        in_specs=[blk, blk, pl.BlockSpec((N_EXPERTS, 1), lambda i: (0, 0))],
        out_specs=blk,
        out_shape=jax.ShapeDtypeStruct((TOP_K, t), I32),
        compiler_params=_params(1),
        name="positions",
    )(eidx, rank, start_col)


def _dispatch_kernel(pos_ref, h_ref, xs_ref, zero_ref, sem, zsem, *, n_rows):
    tt = h_ref.shape[0] // ROW_TILES
    first = pl.program_id(0) == 0

    @pl.when(first)
    def _():
        zero_ref[...] = jnp.zeros_like(zero_ref)
        _rows_copy(zero_ref, 0, xs_ref, n_rows, EXPERT_BLOCK, zsem).start()

    for t in range(tt):
        for k in range(TOP_K):
            _rows_copy(h_ref, t, xs_ref, pos_ref[k, t], 1, sem).start(priority=k % 2)
    for k in range(TOP_K):
        _rows_copy(h_ref, 0, xs_ref, 0, tt, sem).wait()

    @pl.when(first)
    def _():
        _rows_copy(zero_ref, 0, xs_ref, n_rows, EXPERT_BLOCK, zsem).wait()


def _dispatch(pos, h2_tiles, n_rows):
    t = h2_tiles.shape[0] // ROW_TILES
    return pl.pallas_call(
        functools.partial(_dispatch_kernel, n_rows=n_rows),
        grid=(t // TOKEN_TILE,),
        in_specs=[pl.BlockSpec((TOP_K, TOKEN_TILE), lambda i: (0, i), memory_space=pltpu.SMEM),
                  pl.BlockSpec((TOKEN_TILE * ROW_TILES, LANES), lambda i: (i, 0))],
        out_specs=pl.BlockSpec(memory_space=pl.ANY),
        out_shape=jax.ShapeDtypeStruct(((n_rows + EXPERT_BLOCK) * ROW_TILES, LANES), ROW_DTYPE),
        scratch_shapes=[pltpu.VMEM((EXPERT_BLOCK * ROW_TILES, LANES), ROW_DTYPE),
                        pltpu.SemaphoreType.DMA, pltpu.SemaphoreType.DMA],
        compiler_params=_params(1),
        name="dispatch",
    )(pos, h2_tiles)


def _out_pieces(nvalid, fn):
    s = 1 << (EXPERT_BLOCK.bit_length() - 1)
    while s >= 1:
        @pl.when((nvalid & s) != 0)
        def _(s=s):
            fn(nvalid & ~(2 * s - 1), s)
        s //= 2


def _for_block_size(nvalid, fn):
    lo = 0
    for rows in BLOCK_SIZES:
        @pl.when(jnp.logical_and(nvalid > lo, nvalid <= rows))
        def _(rows=rows):
            fn(rows)
        lo = rows


def _expert_kernel(count_ref, blk0_ref, row_ref, cnt_ref, xs_ref, wg_ref, wu_ref, wd_ref, ys_ref,
                   xbuf, ybuf, wg_buf, wu_buf, wd_buf, wgu_s, wd_s, state, xsem, ysem, wsem):
    e = pl.program_id(0)
    n_experts = pl.num_programs(0)
    cnt = count_ref[e]
    g0 = blk0_ref[e]
    nblk = (cnt + EXPERT_BLOCK - 1) // EXPERT_BLOCK
    wslot = lax.rem(e, W_SLOTS)

    def w_copies(expert, slot):
        return [pltpu.make_async_copy(src.at[expert], dst.at[slot], wsem.at[slot])
                for src, dst in ((wg_ref, wg_buf), (wu_ref, wu_buf), (wd_ref, wd_buf))]

    def w_fetch(expert):
        @pl.when(expert < n_experts)
        def _():
            for c in w_copies(expert, lax.rem(expert, W_SLOTS)):
                c.start()

    def x_copy(row, slot, rows):
        return _rows_copy(xs_ref, row, xbuf.at[slot], 0, rows, xsem.at[slot])

    def x_fetch(g, slot):
        _for_block_size(cnt_ref[g], lambda rows: x_copy(row_ref[g], slot, rows).start())

    def y_wait(nvalid, slot):
        def wait(off, size):
            _rows_copy(ybuf.at[slot], 0, ys_ref, 0, size, ysem.at[slot]).wait()
        _out_pieces(nvalid, wait)

    @pl.when(e == 0)
    def _():
        state[0] = 0
        state[1] = 0
        for g in range(X_SLOTS - 1):
            x_fetch(g, g)
        for ahead in range(W_SLOTS - 1):
            w_fetch(ahead)

    w_fetch(e + W_SLOTS - 1)
    for c in w_copies(e, wslot):
        c.wait()
    wgu_s[:, :EXPERT_DIM] = wg_buf[wslot].astype(BF16)
    wgu_s[:, EXPERT_DIM:] = wu_buf[wslot].astype(BF16)
    wd_s[...] = wd_buf[wslot].astype(BF16)

    def compute(rows, xslot, yslot, row0, nvalid):
        x_copy(row0, xslot, rows).wait()
        x = _load_rows(xbuf.at[xslot], rows).astype(BF16)
        gu = jnp.dot(x, wgu_s[...], preferred_element_type=F32)
        g = gu[:, :EXPERT_DIM]
        act = (g * jax.nn.sigmoid(g)) * gu[:, EXPERT_DIM:]
        y = jnp.dot(act.astype(BF16), wd_s[...], preferred_element_type=F32)
        y_wait(state[0], 1 - yslot)
        _store_rows(ybuf.at[yslot], y)

        def put(off, size):
            _rows_copy(ybuf.at[yslot], off, ys_ref, row0 + off, size, ysem.at[yslot]).start()
        _out_pieces(nvalid, put)

    def block(j, carry):
        g = g0 + j
        xslot = state[1]
        yslot = g & 1
        row0 = row_ref[g]
        nvalid = cnt_ref[g]
        x_fetch(g + X_SLOTS - 1, jnp.where(xslot == 0, X_SLOTS - 1, xslot - 1))
        _for_block_size(nvalid, lambda rows: compute(rows, xslot, yslot, row0, nvalid))
        state[0] = nvalid
        state[1] = jnp.where(xslot == X_SLOTS - 1, 0, xslot + 1)
        return carry

    lax.fori_loop(0, nblk, block, 0)

    @pl.when(e == pl.num_programs(0) - 1)
    def _():
        y_wait(state[0], (g0 + nblk - 1) & 1)


def _experts(count, blk0, blk_row, blk_cnt, xs, w_gate, w_up, w_down, n_rows):
    blk_rows = EXPERT_BLOCK * ROW_TILES
    any_spec = pl.BlockSpec(memory_space=pl.ANY)
    return pl.pallas_call(
        _expert_kernel,
        grid_spec=pltpu.PrefetchScalarGridSpec(
            num_scalar_prefetch=4,
            grid=(N_EXPERTS,),
            in_specs=[any_spec, any_spec, any_spec, any_spec],
            out_specs=any_spec,
            scratch_shapes=[pltpu.VMEM((X_SLOTS, blk_rows, LANES), ROW_DTYPE),
                            pltpu.VMEM((2, blk_rows, LANES), ROW_DTYPE),
                            pltpu.VMEM((W_SLOTS, D_MODEL, EXPERT_DIM), F32),
                            pltpu.VMEM((W_SLOTS, D_MODEL, EXPERT_DIM), F32),
                            pltpu.VMEM((W_SLOTS, EXPERT_DIM, D_MODEL), F32),
                            pltpu.VMEM((D_MODEL, 2 * EXPERT_DIM), BF16),
                            pltpu.VMEM((EXPERT_DIM, D_MODEL), BF16),
                            pltpu.SMEM((2,), I32),
                            pltpu.SemaphoreType.DMA((X_SLOTS,)),
                            pltpu.SemaphoreType.DMA((2,)),
                            pltpu.SemaphoreType.DMA((W_SLOTS,))]),
        out_shape=jax.ShapeDtypeStruct((n_rows * ROW_TILES, LANES), ROW_DTYPE),
        compiler_params=_params(1),
        name="experts",
    )(count, blk0, blk_row, blk_cnt, xs, w_gate, w_up, w_down)


def _combine_kernel(pos_ref, posn_ref, ys_ref, xb_ref, wts_ref, gate2_ref, gf_ref, yp_ref, ysm_ref,
                    buf, ytile, sem, *, n_prompt_tiles, n_tiles):
    i = pl.program_id(0)
    tt = xb_ref.shape[0]
    slot = i & 1

    def fetch(p_ref, s, t):
        for k in range(TOP_K):
            _rows_copy(ys_ref, p_ref[k, t], buf.at[s, k], t, 1, sem.at[s]).start(priority=k % 2)

    @pl.when(i == 0)
    def _():
        for t in range(tt):
            fetch(pos_ref, 0, t)

    for k in range(TOP_K):
        _rows_copy(ys_ref, 0, buf.at[slot, k], 0, tt, sem.at[slot]).wait()

    def finish(s, g):
        r0 = g * COMBINE_GROUP
        w = wts_ref[r0:r0 + COMBINE_GROUP, :]
        routed = None
        for k in range(TOP_K):
            rows = _load_rows(buf.at[s, k, pl.ds(r0 * ROW_TILES, COMBINE_GROUP * ROW_TILES)], COMBINE_GROUP)
            term = w[:, k:k + 1] * rows
            routed = term if routed is None else routed + term
        gate = gate2_ref[g * COMBINE_GROUP // ROW_GROUP]
        x = xb_ref[r0:r0 + COMBINE_GROUP, :] + gate * routed
        return _rms(x) * gf_ref[...]

    def tile_work(s, fetch_next):
        for g in range(tt // COMBINE_GROUP):
            y = finish(s, g)
            if fetch_next:
                for r in range(COMBINE_GROUP):
                    fetch(posn_ref, 1 - s, g * COMBINE_GROUP + r)
            ytile[g * COMBINE_GROUP:(g + 1) * COMBINE_GROUP, :] = y

    for s in range(2):
        @pl.when(jnp.logical_and(slot == s, i + 1 < n_tiles))
        def _(s=s):
            tile_work(s, True)

    @pl.when(i + 1 == n_tiles)
    def _():
        tile_work((n_tiles - 1) % 2, False)

    @pl.when(i < n_prompt_tiles)
    def _():
        yp_ref[...] = ytile[...]

    @pl.when(i >= n_prompt_tiles)
    def _():
        ysm_ref[...] = ytile[...]


def _combine(pos, ys, xb, wts_t, cond_tbl, gf, batch, n_prompt_rows):
    t = xb.shape[0]
    tt = COMBINE_TILE
    n_tiles = t // tt
    n_p = n_prompt_rows // tt
    pos_spec = lambda f: pl.BlockSpec((TOP_K, tt), f, memory_space=pltpu.SMEM)
    return pl.pallas_call(
        functools.partial(_combine_kernel, n_prompt_tiles=n_p, n_tiles=n_tiles),
        grid=(n_tiles,),
        in_specs=[pos_spec(lambda i: (0, i)),
                  pos_spec(lambda i: (0, jnp.minimum(i + 1, n_tiles - 1))),
                  pl.BlockSpec(memory_space=pl.ANY),
                  pl.BlockSpec((tt, D_MODEL), lambda i: (i, 0)),
                  pl.BlockSpec((tt, TOP_K), lambda i: (i, 0))]
                 + [_cond_spec(5, tt, n_p // batch, batch, n_p)]
                 + [pl.BlockSpec((1, D_MODEL), lambda i: (0, 0))],
        out_specs=[pl.BlockSpec((tt, D_MODEL), lambda i: (jnp.minimum(i, n_p - 1), 0)),
                   pl.BlockSpec((tt, D_MODEL), lambda i: (jnp.maximum(i - n_p, 0), 0))],
        out_shape=[jax.ShapeDtypeStruct((n_prompt_rows, D_MODEL), F32),
                   jax.ShapeDtypeStruct((t - n_prompt_rows, D_MODEL), F32)],
        scratch_shapes=[pltpu.VMEM((2, TOP_K, tt * ROW_TILES, LANES), ROW_DTYPE),
                        pltpu.VMEM((tt, D_MODEL), F32),
                        pltpu.SemaphoreType.DMA((2,))],
        compiler_params=_params(1),
        name="combine",
    )(pos, pos, ys, xb, wts_t, cond_tbl, gf)


def _sgu_mask():
    pos = np.arange(SGU_CHUNK) // CHUNK
    return jnp.asarray(pos[None, :] <= pos[:, None])


def _layer(xp3, xs3, cache_k, cache_v, mod, final_g, batch, seq, dec_batch, n, p):
    tp = batch * seq
    t = tp + dec_batch * n
    n_prompt_tiles = tp // TOKEN_TILE
    cond_tbl = jnp.concatenate([jnp.repeat(mod[:batch], COND_REPEAT, axis=0), mod[batch:]], axis=0)
    cond_tbl = cond_tbl.reshape(batch * COND_REPEAT + dec_batch, 1, N_MOD * D_MODEL)

    wm = jnp.where(_sgu_mask(), p["sgu_w"], 0.0)
    reps = SGU_CHUNK // n
    eye = jnp.eye(reps, dtype=F32)
    wms = jnp.stack([jnp.kron(eye, wm[g, :n, :n]) for g in range(SGU_GROUPS)])
    bs = jnp.broadcast_to(p["sgu_b"][:, :, None], (SGU_GROUPS, SGU_CHUNK, SGU_CHUNK))
    bss = jnp.broadcast_to(jnp.tile(p["sgu_b"][:, :n], (1, reps))[:, :, None], (SGU_GROUPS, SGU_CHUNK, SGU_CHUNK))

    q, k, v, sgu_n, vn_s = _stage1(
        xp3, xs3, cond_tbl, p["norm1_g"].reshape(1, -1), p["w_in"].astype(BF16),
        p["sgu_ln_g"].reshape(1, -1), p["sgu_ln_b"].reshape(1, -1),
        wm.astype(BF16), wms.astype(BF16), bs, bss, p["sgu_out_g"].reshape(1, -1), batch, n_prompt_tiles)

    gattn = p["attn_out_g"].reshape(1, -1)
    attn_p = _attn_prompt(q, k, v, p["attn_sinks"], gattn, batch, seq)
    w = cache_k.shape[1]
    attn_s = _attn_sample(q, k, v, cache_k.reshape(dec_batch, w, KV_WIDTH), cache_v.reshape(dec_batch, w, KV_WIDTH),
                          p["attn_sinks"], gattn, dec_batch, n, tp)

    wo = p["w_out"].astype(BF16)
    wsgu = jnp.concatenate([p["ws_gate"], p["ws_up"]], axis=1).astype(BF16)
    wrh = p["w_router"].astype(BF16)
    wrl = (p["w_router"] - wrh.astype(F32)).astype(BF16)
    xb, h2_tiles, scores = _merge(attn_p, attn_s, sgu_n, xp3, xs3, cond_tbl,
                                  wo[:ATTN_WIDTH], wo[ATTN_WIDTH:], p["norm2_g"].reshape(1, -1), wrh, wrl,
                                  wsgu, p["ws_down"].astype(BF16), batch, n_prompt_tiles)

    ti = np.arange(TOKEN_TILE)
    tri = jnp.asarray(ti[:, None] < ti[None, :], dtype=BF16)
    eidx, wts, rank, counts = _route(scores, p["router_bias"].reshape(N_EXPERTS, 1), tri)

    counts = counts[:, 0]
    ends = jnp.cumsum(counts)
    start = (ends - counts).astype(I32)
    nblk = (counts + EXPERT_BLOCK - 1) // EXPERT_BLOCK
    blk_end = jnp.cumsum(nblk)
    blk0 = (blk_end - nblk).astype(I32)
    pos = _positions(eidx, rank, start.reshape(N_EXPERTS, 1))

    n_rows = t * TOP_K
    g = jnp.arange(n_rows // EXPERT_BLOCK + N_EXPERTS + X_SLOTS, dtype=I32)[:, None]
    mine = jnp.logical_and(g >= blk0[None, :], g < blk_end[None, :])
    off = (g - blk0[None, :]) * EXPERT_BLOCK
    blk_row = jnp.sum(jnp.where(mine, start[None, :] + off, 0), axis=1).astype(I32)
    blk_cnt = jnp.sum(jnp.where(mine, jnp.minimum(counts[None, :] - off, EXPERT_BLOCK), 0), axis=1).astype(I32)

    xs_sorted = _dispatch(pos, h2_tiles, n_rows)
    ys_sorted = _experts(counts.astype(I32), blk0, blk_row, blk_cnt, xs_sorted,
                         p["w_gate"], p["w_up"], p["w_down"], n_rows)
    y_p, y_s = _combine(pos, ys_sorted, xb, wts.T, cond_tbl, final_g.reshape(1, -1), batch, tp)
    return y_p, y_s, k, v, vn_s


def kernel(x_prompt, x_sample, cache_k, cache_v, c_prompt, c_sample, norm1_g, w_ada, b_ada, w_in, sgu_ln_g, sgu_ln_b, sgu_w, sgu_b, attn_sinks, attn_out_g, sgu_out_g, w_out, norm2_g, w_router, router_bias, w_gate, w_up, w_down, ws_gate, ws_up, ws_down, final_g):
    batch, seq, d = x_prompt.shape
    dec_batch, n, _ = x_sample.shape
    depth = norm1_g.shape[0]
    assert depth == 1 and d == D_MODEL
    assert seq % TOKEN_TILE == 0 and dec_batch * n == TOKEN_TILE and n == ROW_GROUP
    assert seq % ATTN_TILE == 0 and seq >= WINDOW and cache_k.shape[2] == WINDOW
    tp = batch * seq

    xp3 = x_prompt.reshape(tp // ROW_GROUP, ROW_GROUP, d)
    xs3 = x_sample.reshape(dec_batch * n // ROW_GROUP, ROW_GROUP, d)
    c_all = jnp.concatenate([c_prompt, c_sample], axis=0)
    l = 0
    mod = _adaln(c_all, w_ada[l], b_ada[l])
    p = dict(norm1_g=norm1_g[l], w_in=w_in[l], sgu_ln_g=sgu_ln_g[l], sgu_ln_b=sgu_ln_b[l], sgu_w=sgu_w[l],
             sgu_b=sgu_b[l], attn_sinks=attn_sinks[l], attn_out_g=attn_out_g[l], sgu_out_g=sgu_out_g[l],
             w_out=w_out[l], norm2_g=norm2_g[l], w_router=w_router[l], router_bias=router_bias[l],
             w_gate=w_gate[l], w_up=w_up[l], w_down=w_down[l], ws_gate=ws_gate[l], ws_up=ws_up[l],
             ws_down=ws_down[l])
    y_p, y_s, k, v, vn_s = _layer(xp3, xs3, cache_k[l], cache_v[l], mod, final_g, batch, seq, dec_batch, n, p)

    keep = min(WINDOW, seq)
    kv_shape = (batch, keep, N_KV_HEADS, HEAD_DIM)
    k_p = jnp.stack([k[(b + 1) * seq - keep:(b + 1) * seq] for b in range(batch)]).reshape(kv_shape)
    v_p = jnp.stack([v[(b + 1) * seq - keep:(b + 1) * seq] for b in range(batch)]).reshape(kv_shape)
    k_s = k[tp:].reshape(dec_batch, n, N_KV_HEADS, HEAD_DIM)
    v_s = v[tp:].reshape(dec_batch, n, N_KV_HEADS, HEAD_DIM)
    return (y_p.reshape(batch, seq, d), y_s.reshape(dec_batch, n, d),
            k_p[None], v_p[None], k_s[None], v_s[None], vn_s.reshape(dec_batch, n, SGU_WIDTH)[None])
```

```python
import functools

import numpy as np
import jax
import jax.numpy as jnp
from jax import lax
from jax.experimental import pallas as pl
from jax.experimental.pallas import tpu as pltpu

F32 = jnp.float32
BF16 = jnp.bfloat16
I32 = jnp.int32

D_MODEL = 1024
CHUNK = 64
HEAD_DIM = 64
ATTN_WIDTH = 512
N_HEADS = 8
N_KV_HEADS = 2
GQA_GROUP = 4
KV_WIDTH = 128
WINDOW = 128
SGU_CHUNK = 128
SGU_WIDTH = 512
SGU_GROUPS = 4
IN_COLS = ATTN_WIDTH + 2 * KV_WIDTH + 2 * SGU_WIDTH
N_MOD = 6
N_EXPERTS = 256
TOP_K = 8
N_EXPERT_GROUPS = 8
GROUP_SIZE = N_EXPERTS // N_EXPERT_GROUPS
TOPK_GROUPS = 4
EXPERT_DIM = 256
ROUTED_SCALE = 2.5
EPS = 1e-6

LANES = 128
ROW_WORDS = D_MODEL // 2
ROW_TILES = ROW_WORDS // LANES
ROW_DTYPE = jnp.uint32
ROW_GROUP = 16
TOKEN_TILE = 256
COND_REPEAT = TOKEN_TILE // ROW_GROUP
ATTN_TILE = 512
EXPERT_BLOCK = 640
BLOCK_SIZES = (128, 256, 384, 512, 576, EXPERT_BLOCK)
X_SLOTS = 4
W_SLOTS = 4
POS_TILE_MAX = 4096
COMBINE_TILE = 256
COMBINE_GROUP = 8
VMEM_LIMIT_BYTES = 56 * 1024 * 1024


def _params(n_axes=1):
    return pltpu.CompilerParams(dimension_semantics=("arbitrary",) * n_axes,
                                vmem_limit_bytes=VMEM_LIMIT_BYTES)


def _rms(x):
    return x * lax.rsqrt(jnp.mean(x * x, axis=-1, keepdims=True) + EPS)


def _cond_rows(m_ref):
    m = m_ref[...]
    g, _, d = m.shape
    return jnp.broadcast_to(m, (g, ROW_GROUP, d)).reshape(g * ROW_GROUP, d)


def _cond_spec(piece, tile_rows, tiles_per_batch, batch, n_prompt_tiles):
    g = tile_rows // ROW_GROUP
    per = COND_REPEAT // g
    return pl.BlockSpec(
        (g, 1, D_MODEL),
        lambda i: (jnp.where(i < n_prompt_tiles, (i // tiles_per_batch) * per, batch * per + i - n_prompt_tiles),
                   0, piece))


def _pick_tile(is_sample, prompt_ref, sample_ref):
    v = jnp.where(is_sample, sample_ref[...], prompt_ref[...])
    return v.reshape(TOKEN_TILE, v.shape[-1])


def _pack_rows(x):
    x = x.astype(F32)
    return pltpu.pack_elementwise([x[:, :ROW_WORDS], x[:, ROW_WORDS:]], packed_dtype=BF16)


def _unpack_rows(w):
    return jnp.concatenate([pltpu.unpack_elementwise(w, index=i, packed_dtype=BF16, unpacked_dtype=F32)
                            for i in range(2)], axis=1)


def _load_rows(ref, n_rows):
    w = jnp.concatenate([ref[pl.ds(j, n_rows, stride=ROW_TILES), :] for j in range(ROW_TILES)], axis=1)
    return _unpack_rows(w)


def _store_rows(ref, x):
    n_rows = x.shape[0]
    w = _pack_rows(x)
    for j in range(ROW_TILES):
        ref[pl.ds(j, n_rows, stride=ROW_TILES), :] = w[:, j * LANES:(j + 1) * LANES]


def _tile_offset(row):
    off = row * ROW_TILES
    return off if isinstance(off, int) else pl.multiple_of(off, ROW_TILES)


def _rows_copy(src_ref, src_row, dst_ref, dst_row, n_rows, sem):
    s = _tile_offset(src_row)
    d = _tile_offset(dst_row)
    return pltpu.make_async_copy(src_ref.at[pl.ds(s, n_rows * ROW_TILES)],
                                 dst_ref.at[pl.ds(d, n_rows * ROW_TILES)], sem)


def _adaln_kernel(c_ref, w_ref, b_ref, o_ref):
    c = c_ref[...]
    s = c * jax.nn.sigmoid(c)
    o_ref[...] = jnp.dot(s, w_ref[...], precision=lax.Precision.HIGHEST,
                         preferred_element_type=F32) + b_ref[...]


def _adaln(c, w_ada, b_ada):
    n, d = c.shape
    cols = w_ada.shape[1]
    tn = 1536
    return pl.pallas_call(
        _adaln_kernel,
        grid=(cols // tn,),
        in_specs=[pl.BlockSpec((n, d), lambda j: (0, 0)),
                  pl.BlockSpec((d, tn), lambda j: (0, j)),
                  pl.BlockSpec((1, tn), lambda j: (0, j))],
        out_specs=pl.BlockSpec((n, tn), lambda j: (0, j)),
        out_shape=jax.ShapeDtypeStruct((n, cols), F32),
        compiler_params=_params(1),
        name="adaln",
    )(c, w_ada, b_ada.reshape(1, cols))


def _stage1_kernel(xp_ref, xs_ref, sh_ref, sc_ref, g1_ref, win_ref, lng_ref, lnb_ref,
                   wm_ref, wms_ref, bs_ref, bss_ref, gsgu_ref,
                   q_ref, k_ref, v_ref, sgu_ref, vn_ref, *, n_prompt_tiles):
    is_sample = pl.program_id(0) >= n_prompt_tiles
    x = _pick_tile(is_sample, xp_ref, xs_ref)
    h = _rms(x) * g1_ref[...] * (1.0 + _cond_rows(sc_ref)) + _cond_rows(sh_ref)
    proj = jnp.dot(h.astype(BF16), win_ref[...], preferred_element_type=F32)
    q_ref[...] = (proj[:, :ATTN_WIDTH] * (HEAD_DIM ** -0.5)).astype(BF16)
    k_ref[...] = proj[:, ATTN_WIDTH:ATTN_WIDTH + KV_WIDTH]
    v_ref[...] = proj[:, ATTN_WIDTH + KV_WIDTH:ATTN_WIDTH + 2 * KV_WIDTH]
    c0 = ATTN_WIDTH + 2 * KV_WIDTH
    u = jax.nn.gelu(proj[:, c0:c0 + SGU_WIDTH])
    vg = jax.nn.gelu(proj[:, c0 + SGU_WIDTH:])
    mu = jnp.mean(vg, axis=-1, keepdims=True)
    xc = vg - mu
    var = jnp.mean(xc * xc, axis=-1, keepdims=True)
    vn = xc * lax.rsqrt(var + EPS) * lng_ref[...] + lnb_ref[...]

    @pl.when(is_sample)
    def _():
        vn_ref[...] = vn

    vnb = vn.astype(BF16)
    wm = jnp.where(is_sample, wms_ref[...], wm_ref[...])
    bs = jnp.where(is_sample, bss_ref[...], bs_ref[...])
    rows = []
    for c in range(TOKEN_TILE // SGU_CHUNK):
        r0 = c * SGU_CHUNK
        cols = []
        for g in range(SGU_GROUPS):
            l0 = g * SGU_CHUNK
            sv = jnp.dot(wm[g], vnb[r0:r0 + SGU_CHUNK, l0:l0 + SGU_CHUNK],
                         preferred_element_type=F32) + bs[g]
            cols.append(sv)
        rows.append(jnp.concatenate(cols, axis=1))
    sgu = u * jnp.concatenate(rows, axis=0)
    sgu_ref[...] = (_rms(sgu) * gsgu_ref[...]).astype(BF16)


def _stage1(xp3, xs3, cond_tbl, g1, win_b, lng, lnb, wm, wms, bs, bss, gsgu, batch, n_prompt_tiles):
    t = (xp3.shape[0] + xs3.shape[0]) * ROW_GROUP
    n_tiles = t // TOKEN_TILE
    last_p = n_prompt_tiles - 1
    x_blk = (TOKEN_TILE // ROW_GROUP, ROW_GROUP, D_MODEL)
    const2 = lambda i: (0, 0)
    const3 = lambda i: (0, 0, 0)
    tile2 = lambda i: (i, 0)
    cond = functools.partial(_cond_spec, tile_rows=TOKEN_TILE, tiles_per_batch=n_prompt_tiles // batch,
                             batch=batch, n_prompt_tiles=n_prompt_tiles)
    sgu_w_spec = pl.BlockSpec((SGU_GROUPS, SGU_CHUNK, SGU_CHUNK), const3)
    return pl.pallas_call(
        functools.partial(_stage1_kernel, n_prompt_tiles=n_prompt_tiles),
        grid=(n_tiles,),
        in_specs=[pl.BlockSpec(x_blk, lambda i: (jnp.minimum(i, last_p), 0, 0)),
                  pl.BlockSpec(x_blk, const3)]
                 + [cond(0), cond(1)]
                 + [pl.BlockSpec((1, D_MODEL), const2),
                    pl.BlockSpec((D_MODEL, IN_COLS), const2),
                    pl.BlockSpec((1, SGU_WIDTH), const2),
                    pl.BlockSpec((1, SGU_WIDTH), const2),
                    sgu_w_spec, sgu_w_spec, sgu_w_spec, sgu_w_spec,
                    pl.BlockSpec((1, SGU_WIDTH), const2)],
        out_specs=[pl.BlockSpec((TOKEN_TILE, ATTN_WIDTH), tile2),
                   pl.BlockSpec((TOKEN_TILE, KV_WIDTH), tile2),
                   pl.BlockSpec((TOKEN_TILE, KV_WIDTH), tile2),
                   pl.BlockSpec((TOKEN_TILE, SGU_WIDTH), tile2),
                   pl.BlockSpec((TOKEN_TILE, SGU_WIDTH), const2)],
        out_shape=[jax.ShapeDtypeStruct((t, ATTN_WIDTH), BF16),
                   jax.ShapeDtypeStruct((t, KV_WIDTH), F32),
                   jax.ShapeDtypeStruct((t, KV_WIDTH), F32),
                   jax.ShapeDtypeStruct((t, SGU_WIDTH), BF16),
                   jax.ShapeDtypeStruct((TOKEN_TILE, SGU_WIDTH), F32)],
        compiler_params=_params(1),
        name="stage1",
    )(xp3, xs3, cond_tbl, cond_tbl, g1, win_b, lng, lnb, wm, wms, bs, bss, gsgu)


def _alibi_slope(h):
    return float(np.float32(2.0) ** np.float32(-8.0 * (h + 1.0) / N_HEADS))


def _sink_attention(q, k, v, back, valid_fn, sinks_ref, gout):
    nq = q.shape[0]
    nk = k.shape[0]
    q_t = q.astype(F32).T.astype(BF16)
    k = k.astype(BF16)
    v_t = v.T.astype(BF16)
    c = lax.broadcasted_iota(I32, (nk, nq), 0)
    r = lax.broadcasted_iota(I32, (nk, nq), 1)
    dist = jnp.abs(r - c + back).astype(F32)
    valid = None if valid_fn is None else valid_fn(c, r)
    outs = []
    for kvh in range(N_KV_HEADS):
        kv0 = kvh * HEAD_DIM
        heads = range(kvh * GQA_GROUP, (kvh + 1) * GQA_GROUP)
        qs = jnp.concatenate([q_t[h * HEAD_DIM:(h + 1) * HEAD_DIM, :] for h in heads], axis=1)
        l = jnp.dot(k[:, kv0:kv0 + HEAD_DIM], qs, preferred_element_type=F32)
        biased = [l[:, g * nq:(g + 1) * nq] - _alibi_slope(h) * dist for g, h in enumerate(heads)]
        if valid is not None:
            biased = [jnp.where(valid, b, -jnp.inf) for b in biased]
        l = jnp.concatenate(biased, axis=1)
        sink = jnp.concatenate([jnp.full((1, nq), sinks_ref[h], F32) for h in heads], axis=1)
        m = jnp.maximum(jnp.max(l, axis=0, keepdims=True), sink)
        e = jnp.exp(l - m)
        denom = jnp.sum(e, axis=0, keepdims=True) + jnp.exp(sink - m)
        pv = jnp.dot(v_t[kv0:kv0 + HEAD_DIM, :], e.astype(BF16), preferred_element_type=F32)
        o_t = pv / denom
        outs.extend(o_t[:, g * nq:(g + 1) * nq] for g in range(GQA_GROUP))
    o = jnp.concatenate(outs, axis=0).T
    return (_rms(o) * gout).astype(BF16)


def _attn_prompt_kernel(sinks_ref, q_ref, kp_ref, kc_ref, vp_ref, vc_ref, g_ref, o_ref):
    i = pl.program_id(1)
    back = kp_ref.shape[0]

    k = jnp.concatenate([kp_ref[...], kc_ref[...]], axis=0)
    v = jnp.concatenate([vp_ref[...], vc_ref[...]], axis=0)
    for s in range(q_ref.shape[0] // WINDOW):
        def valid_fn(c, r, s=s):
            kc = c // CHUNK
            qc = r // CHUNK + (back // CHUNK)
            in_band = jnp.logical_and(kc <= qc, kc >= qc - WINDOW // CHUNK)
            if s > 0:
                return in_band
            return jnp.logical_and(in_band, jnp.logical_or(i > 0, kc >= back // CHUNK))

        rows = slice(s * WINDOW, (s + 1) * WINDOW)
        keys = slice(s * WINDOW, s * WINDOW + back + WINDOW)
        o_ref[rows, :] = _sink_attention(q_ref[rows, :], k[keys], v[keys], back, valid_fn, sinks_ref, g_ref[...])


def _attn_prompt(q, k, v, sinks, gattn, batch, seq):
    nt = seq // ATTN_TILE
    per_tile = ATTN_TILE // WINDOW
    cur = lambda b, i, s: (b * nt + i, 0)
    prev = lambda b, i, s: (jnp.maximum((b * nt + i) * per_tile - 1, 0), 0)
    kv_blk = (ATTN_TILE, KV_WIDTH)
    back_blk = (WINDOW, KV_WIDTH)
    return pl.pallas_call(
        _attn_prompt_kernel,
        grid_spec=pltpu.PrefetchScalarGridSpec(
            num_scalar_prefetch=1,
            grid=(batch, nt),
            in_specs=[pl.BlockSpec((ATTN_TILE, ATTN_WIDTH), cur),
                      pl.BlockSpec(back_blk, prev), pl.BlockSpec(kv_blk, cur),
                      pl.BlockSpec(back_blk, prev), pl.BlockSpec(kv_blk, cur),
                      pl.BlockSpec((1, ATTN_WIDTH), lambda b, i, s: (0, 0))],
            out_specs=pl.BlockSpec((ATTN_TILE, ATTN_WIDTH), cur)),
        out_shape=jax.ShapeDtypeStruct((batch * seq, ATTN_WIDTH), BF16),
        compiler_params=_params(2),
        name="attn_prompt",
    )(sinks, q, k, k, v, v, gattn)


def _attn_sample_kernel(sinks_ref, q_ref, ck_ref, cv_ref, k_ref, v_ref, g_ref, o_ref):
    k = jnp.concatenate([ck_ref[0], k_ref[...]], axis=0)
    v = jnp.concatenate([cv_ref[0], v_ref[...]], axis=0)
    o_ref[...] = _sink_attention(q_ref[...], k, v, ck_ref.shape[1], None, sinks_ref, g_ref[...])


def _attn_sample(q, k, v, cache_k, cache_v, sinks, gattn, dec_batch, n, row0):
    w = cache_k.shape[1]
    blk0 = row0 // n
    new = lambda b, s: (blk0 + b, 0)
    return pl.pallas_call(
        _attn_sample_kernel,
        grid_spec=pltpu.PrefetchScalarGridSpec(
            num_scalar_prefetch=1,
            grid=(dec_batch,),
            in_specs=[pl.BlockSpec((n, ATTN_WIDTH), new),
                      pl.BlockSpec((1, w, KV_WIDTH), lambda b, s: (b, 0, 0)),
                      pl.BlockSpec((1, w, KV_WIDTH), lambda b, s: (b, 0, 0)),
                      pl.BlockSpec((n, KV_WIDTH), new),
                      pl.BlockSpec((n, KV_WIDTH), new),
                      pl.BlockSpec((1, ATTN_WIDTH), lambda b, s: (0, 0))],
            out_specs=pl.BlockSpec((n, ATTN_WIDTH), lambda b, s: (b, 0))),
        out_shape=jax.ShapeDtypeStruct((dec_batch * n, ATTN_WIDTH), BF16),
        compiler_params=_params(1),
        name="attn_sample",
    )(sinks, q, cache_k, cache_v, k, v, gattn)


def _merge_kernel(ap_ref, as_ref, sgu_ref, xp_ref, xs_ref,
                  gate1_ref, sh2_ref, sc2_ref, gate2_ref,
                  woa_ref, wos_ref, g2_ref, wrh_ref, wrl_ref, wsgu_ref, wsd_ref,
                  xb_ref, h2_ref, sc_ref, *, n_prompt_tiles):
    is_sample = pl.program_id(0) >= n_prompt_tiles
    x = _pick_tile(is_sample, xp_ref, xs_ref)
    a = jnp.where(is_sample, as_ref[...], ap_ref[...])
    mix = (jnp.dot(a, woa_ref[...], preferred_element_type=F32)
           + jnp.dot(sgu_ref[...], wos_ref[...], preferred_element_type=F32))
    x1 = x + _cond_rows(gate1_ref) * mix
    h2 = _rms(x1) * g2_ref[...] * (1.0 + _cond_rows(sc2_ref)) + _cond_rows(sh2_ref)
    hh = h2.astype(BF16)
    _store_rows(h2_ref, hh)
    hl = (h2 - hh.astype(F32)).astype(BF16)
    logits = (jnp.dot(hh, wrh_ref[...], preferred_element_type=F32)
              + (jnp.dot(hl, wrh_ref[...], preferred_element_type=F32)
                 + jnp.dot(hh, wrl_ref[...], preferred_element_type=F32)))
    sc_ref[...] = jax.nn.sigmoid(logits)
    gu = jnp.dot(hh, wsgu_ref[...], preferred_element_type=F32)
    g = gu[:, :EXPERT_DIM]
    act = (g * jax.nn.sigmoid(g)) * gu[:, EXPERT_DIM:]
    shared = jnp.dot(act.astype(BF16), wsd_ref[...], preferred_element_type=F32)
    xb_ref[...] = x1 + _cond_rows(gate2_ref) * shared


def _merge(attn_p, attn_s, sgu_n, xp3, xs3, cond_tbl, woa, wos, g2, wrh, wrl, wsgu, wsd, batch, n_prompt_tiles):
    t = sgu_n.shape[0]
    n_tiles = t // TOKEN_TILE
    last_p = n_prompt_tiles - 1
    x_blk = (TOKEN_TILE // ROW_GROUP, ROW_GROUP, D_MODEL)
    const2 = lambda i: (0, 0)
    const3 = lambda i: (0, 0, 0)
    tile2 = lambda i: (i, 0)
    cond = functools.partial(_cond_spec, tile_rows=TOKEN_TILE, tiles_per_batch=n_prompt_tiles // batch,
                             batch=batch, n_prompt_tiles=n_prompt_tiles)
    return pl.pallas_call(
        functools.partial(_merge_kernel, n_prompt_tiles=n_prompt_tiles),
        grid=(n_tiles,),
        in_specs=[pl.BlockSpec((TOKEN_TILE, ATTN_WIDTH), lambda i: (jnp.minimum(i, last_p), 0)),
                  pl.BlockSpec((TOKEN_TILE, ATTN_WIDTH), const2),
                  pl.BlockSpec((TOKEN_TILE, SGU_WIDTH), tile2),
                  pl.BlockSpec(x_blk, lambda i: (jnp.minimum(i, last_p), 0, 0)),
                  pl.BlockSpec(x_blk, const3)]
                 + [cond(2), cond(3), cond(4), cond(5)]
                 + [pl.BlockSpec((ATTN_WIDTH, D_MODEL), const2),
                    pl.BlockSpec((SGU_WIDTH, D_MODEL), const2),
                    pl.BlockSpec((1, D_MODEL), const2),
                    pl.BlockSpec((D_MODEL, N_EXPERTS), const2),
                    pl.BlockSpec((D_MODEL, N_EXPERTS), const2),
                    pl.BlockSpec((D_MODEL, 2 * EXPERT_DIM), const2),
                    pl.BlockSpec((EXPERT_DIM, D_MODEL), const2)],
        out_specs=[pl.BlockSpec((TOKEN_TILE, D_MODEL), tile2),
                   pl.BlockSpec((TOKEN_TILE * ROW_TILES, LANES), tile2),
                   pl.BlockSpec((TOKEN_TILE, N_EXPERTS), tile2)],
        out_shape=[jax.ShapeDtypeStruct((t, D_MODEL), F32),
                   jax.ShapeDtypeStruct((t * ROW_TILES, LANES), ROW_DTYPE),
                   jax.ShapeDtypeStruct((t, N_EXPERTS), F32)],
        compiler_params=_params(1),
        name="merge",
    )(attn_p, attn_s, sgu_n, xp3, xs3, *([cond_tbl] * 4), woa, wos, g2, wrh, wrl, wsgu, wsd)


def _first_argmax(x, iota, size, axis):
    m = jnp.max(x, axis=axis, keepdims=True)
    idx = jnp.min(jnp.where(x == m, iota, size), axis=axis, keepdims=True)
    return m, idx


def _route_kernel(sc_ref, bias_ref, tri_ref, eidx_ref, wts_ref, rank_ref, cnt_ref, base_ref):
    tt = sc_ref.shape[0]

    @pl.when(pl.program_id(0) == 0)
    def _():
        base_ref[...] = jnp.zeros_like(base_ref)

    s_t = sc_ref[...].T
    sel = s_t + bias_ref[...]
    sel3 = sel.reshape(N_EXPERT_GROUPS, GROUP_SIZE, tt)
    io3 = lax.broadcasted_iota(I32, sel3.shape, 1)
    m1, i1 = _first_argmax(sel3, io3, GROUP_SIZE, 1)
    m2 = jnp.max(jnp.where(io3 == i1, -jnp.inf, sel3), axis=1, keepdims=True)
    gs = (m1 + m2).reshape(N_EXPERT_GROUPS, tt)
    io8 = lax.broadcasted_iota(I32, gs.shape, 0)
    gmask = jnp.zeros(gs.shape, jnp.bool_)
    for _ in range(TOPK_GROUPS):
        _, gi = _first_argmax(gs, io8, N_EXPERT_GROUPS, 0)
        hit = io8 == gi
        gmask = jnp.logical_or(gmask, hit)
        gs = jnp.where(hit, -jnp.inf, gs)
    emask = jnp.broadcast_to(gmask.reshape(N_EXPERT_GROUPS, 1, tt), sel3.shape).reshape(N_EXPERTS, tt)
    cand = jnp.where(emask, sel, -jnp.inf)
    io = lax.broadcasted_iota(I32, cand.shape, 0)
    chosen = jnp.zeros(cand.shape, jnp.bool_)
    eidx, wts = [], []
    for _ in range(TOP_K):
        _, ei = _first_argmax(cand, io, N_EXPERTS, 0)
        hit = io == ei
        eidx.append(ei)
        wts.append(jnp.sum(jnp.where(hit, s_t, 0.0), axis=0, keepdims=True))
        chosen = jnp.logical_or(chosen, hit)
        cand = jnp.where(hit, -jnp.inf, cand)
    w = jnp.concatenate(wts, axis=0)
    wts_ref[...] = w / jnp.sum(w, axis=0, keepdims=True) * ROUTED_SCALE
    eidx_ref[...] = jnp.concatenate(eidx, axis=0)
    cf = jnp.where(chosen, 1.0, 0.0)
    ahead = jnp.dot(cf.astype(BF16), tri_ref[...], preferred_element_type=F32) + base_ref[...]
    ranks = [jnp.sum(jnp.where(io == ei, ahead, 0.0), axis=0, keepdims=True) for ei in eidx]
    rank_ref[...] = jnp.concatenate(ranks, axis=0).astype(I32)
    total = base_ref[...] + jnp.sum(cf, axis=1, keepdims=True)
    base_ref[...] = total
    cnt_ref[...] = total.astype(I32)


def _route(scores, bias_col, tri):
    t = scores.shape[0]
    n_tiles = t // TOKEN_TILE
    col = lambda i: (0, i)
    return pl.pallas_call(
        _route_kernel,
        grid=(n_tiles,),
        in_specs=[pl.BlockSpec((TOKEN_TILE, N_EXPERTS), lambda i: (i, 0)),
                  pl.BlockSpec((N_EXPERTS, 1), lambda i: (0, 0)),
                  pl.BlockSpec((TOKEN_TILE, TOKEN_TILE), lambda i: (0, 0))],
        out_specs=[pl.BlockSpec((TOP_K, TOKEN_TILE), col),
                   pl.BlockSpec((TOP_K, TOKEN_TILE), col),
                   pl.BlockSpec((TOP_K, TOKEN_TILE), col),
                   pl.BlockSpec((N_EXPERTS, 1), lambda i: (0, 0))],
        out_shape=[jax.ShapeDtypeStruct((TOP_K, t), I32),
                   jax.ShapeDtypeStruct((TOP_K, t), F32),
                   jax.ShapeDtypeStruct((TOP_K, t), I32),
                   jax.ShapeDtypeStruct((N_EXPERTS, 1), I32)],
        scratch_shapes=[pltpu.VMEM((N_EXPERTS, 1), F32)],
        compiler_params=_params(1),
        name="route",
    )(scores, bias_col, tri)


def _pos_kernel(eidx_ref, rank_ref, start_ref, pos_ref):
    e = eidx_ref[...]
    tt = e.shape[1]
    io = lax.broadcasted_iota(I32, (N_EXPERTS, tt), 0)
    st = start_ref[...]
    rows = [jnp.sum(jnp.where(io == e[k:k + 1, :], st, 0), axis=0, keepdims=True) for k in range(TOP_K)]
    pos_ref[...] = jnp.concatenate(rows, axis=0) + rank_ref[...]


def _positions(eidx, rank, start_col):
    t = eidx.shape[1]
    lane_tiles = t // LANES
    width = max(d for d in range(1, POS_TILE_MAX // LANES + 1) if lane_tiles % d == 0) * LANES
    col = lambda i: (0, i)
    blk = pl.BlockSpec((TOP_K, width), col)
    return pl.pallas_call(
        _pos_kernel,
        grid=(t // width,),
        in_specs=[blk, blk, pl.BlockSpec((N_EXPERTS, 1), lambda i: (0, 0))],
        out_specs=blk,
        out_shape=jax.ShapeDtypeStruct((TOP_K, t), I32),
        compiler_params=_params(1),
        name="positions",
    )(eidx, rank, start_col)


def _dispatch_kernel(pos_ref, h_ref, xs_ref, zero_ref, sem, zsem, *, n_rows):
    tt = h_ref.shape[0] // ROW_TILES
    first = pl.program_id(0) == 0

    @pl.when(first)
    def _():
        zero_ref[...] = jnp.zeros_like(zero_ref)
        _rows_copy(zero_ref, 0, xs_ref, n_rows, EXPERT_BLOCK, zsem).start()

    for t in range(tt):
        for k in range(TOP_K):
            _rows_copy(h_ref, t, xs_ref, pos_ref[k, t], 1, sem).start(priority=k % 2)
    for k in range(TOP_K):
        _rows_copy(h_ref, 0, xs_ref, 0, tt, sem).wait()

    @pl.when(first)
    def _():
        _rows_copy(zero_ref, 0, xs_ref, n_rows, EXPERT_BLOCK, zsem).wait()


def _dispatch(pos, h2_tiles, n_rows):
    t = h2_tiles.shape[0] // ROW_TILES
    return pl.pallas_call(
        functools.partial(_dispatch_kernel, n_rows=n_rows),
        grid=(t // TOKEN_TILE,),
        in_specs=[pl.BlockSpec((TOP_K, TOKEN_TILE), lambda i: (0, i), memory_space=pltpu.SMEM),
                  pl.BlockSpec((TOKEN_TILE * ROW_TILES, LANES), lambda i: (i, 0))],
        out_specs=pl.BlockSpec(memory_space=pl.ANY),
        out_shape=jax.ShapeDtypeStruct(((n_rows + EXPERT_BLOCK) * ROW_TILES, LANES), ROW_DTYPE),
        scratch_shapes=[pltpu.VMEM((EXPERT_BLOCK * ROW_TILES, LANES), ROW_DTYPE),
                        pltpu.SemaphoreType.DMA, pltpu.SemaphoreType.DMA],
        compiler_params=_params(1),
        name="dispatch",
    )(pos, h2_tiles)


def _out_pieces(nvalid, fn):
    s = 1 << (EXPERT_BLOCK.bit_length() - 1)
    while s >= 1:
        @pl.when((nvalid & s) != 0)
        def _(s=s):
            fn(nvalid & ~(2 * s - 1), s)
        s //= 2


def _for_block_size(nvalid, fn):
    lo = 0
    for rows in BLOCK_SIZES:
        @pl.when(jnp.logical_and(nvalid > lo, nvalid <= rows))
        def _(rows=rows):
            fn(rows)
        lo = rows


def _expert_kernel(count_ref, blk0_ref, row_ref, cnt_ref, xs_ref, wg_ref, wu_ref, wd_ref, ys_ref,
                   xbuf, ybuf, wg_buf, wu_buf, wd_buf, wgu_s, wd_s, state, xsem, ysem, wsem):
    e = pl.program_id(0)
    n_experts = pl.num_programs(0)
    cnt = count_ref[e]
    g0 = blk0_ref[e]
    nblk = (cnt + EXPERT_BLOCK - 1) // EXPERT_BLOCK
    wslot = lax.rem(e, W_SLOTS)

    def w_copies(expert, slot):
        return [pltpu.make_async_copy(src.at[expert], dst.at[slot], wsem.at[slot])
                for src, dst in ((wg_ref, wg_buf), (wu_ref, wu_buf), (wd_ref, wd_buf))]

    def w_fetch(expert):
        @pl.when(expert < n_experts)
        def _():
            for c in w_copies(expert, lax.rem(expert, W_SLOTS)):
                c.start()

    def x_copy(row, slot, rows):
        return _rows_copy(xs_ref, row, xbuf.at[slot], 0, rows, xsem.at[slot])

    def x_fetch(g, slot):
        _for_block_size(cnt_ref[g], lambda rows: x_copy(row_ref[g], slot, rows).start())

    def y_wait(nvalid, slot):
        def wait(off, size):
            _rows_copy(ybuf.at[slot], 0, ys_ref, 0, size, ysem.at[slot]).wait()
        _out_pieces(nvalid, wait)

    @pl.when(e == 0)
    def _():
        state[0] = 0
        state[1] = 0
        for g in range(X_SLOTS - 1):
            x_fetch(g, g)
        for ahead in range(W_SLOTS - 1):
            w_fetch(ahead)

    w_fetch(e + W_SLOTS - 1)
    for c in w_copies(e, wslot):
        c.wait()
    wgu_s[:, :EXPERT_DIM] = wg_buf[wslot].astype(BF16)
    wgu_s[:, EXPERT_DIM:] = wu_buf[wslot].astype(BF16)
    wd_s[...] = wd_buf[wslot].astype(BF16)

    def compute(rows, xslot, yslot, row0, nvalid):
        x_copy(row0, xslot, rows).wait()
        x = _load_rows(xbuf.at[xslot], rows).astype(BF16)
        gu = jnp.dot(x, wgu_s[...], preferred_element_type=F32)
        g = gu[:, :EXPERT_DIM]
        act = (g * jax.nn.sigmoid(g)) * gu[:, EXPERT_DIM:]
        y = jnp.dot(act.astype(BF16), wd_s[...], preferred_element_type=F32)
        y_wait(state[0], 1 - yslot)
        _store_rows(ybuf.at[yslot], y)

        def put(off, size):
            _rows_copy(ybuf.at[yslot], off, ys_ref, row0 + off, size, ysem.at[yslot]).start()
        _out_pieces(nvalid, put)

    def block(j, carry):
        g = g0 + j
        xslot = state[1]
        yslot = g & 1
        row0 = row_ref[g]
        nvalid = cnt_ref[g]
        x_fetch(g + X_SLOTS - 1, jnp.where(xslot == 0, X_SLOTS - 1, xslot - 1))
        _for_block_size(nvalid, lambda rows: compute(rows, xslot, yslot, row0, nvalid))
        state[0] = nvalid
        state[1] = jnp.where(xslot == X_SLOTS - 1, 0, xslot + 1)
        return carry

    lax.fori_loop(0, nblk, block, 0)

    @pl.when(e == pl.num_programs(0) - 1)
    def _():
        y_wait(state[0], (g0 + nblk - 1) & 1)


def _experts(count, blk0, blk_row, blk_cnt, xs, w_gate, w_up, w_down, n_rows):
    blk_rows = EXPERT_BLOCK * ROW_TILES
    any_spec = pl.BlockSpec(memory_space=pl.ANY)
    return pl.pallas_call(
        _expert_kernel,
        grid_spec=pltpu.PrefetchScalarGridSpec(
            num_scalar_prefetch=4,
            grid=(N_EXPERTS,),
            in_specs=[any_spec, any_spec, any_spec, any_spec],
            out_specs=any_spec,
            scratch_shapes=[pltpu.VMEM((X_SLOTS, blk_rows, LANES), ROW_DTYPE),
                            pltpu.VMEM((2, blk_rows, LANES), ROW_DTYPE),
                            pltpu.VMEM((W_SLOTS, D_MODEL, EXPERT_DIM), F32),
                            pltpu.VMEM((W_SLOTS, D_MODEL, EXPERT_DIM), F32),
                            pltpu.VMEM((W_SLOTS, EXPERT_DIM, D_MODEL), F32),
                            pltpu.VMEM((D_MODEL, 2 * EXPERT_DIM), BF16),
                            pltpu.VMEM((EXPERT_DIM, D_MODEL), BF16),
                            pltpu.SMEM((2,), I32),
                            pltpu.SemaphoreType.DMA((X_SLOTS,)),
                            pltpu.SemaphoreType.DMA((2,)),
                            pltpu.SemaphoreType.DMA((W_SLOTS,))]),
        out_shape=jax.ShapeDtypeStruct((n_rows * ROW_TILES, LANES), ROW_DTYPE),
        compiler_params=_params(1),
        name="experts",
    )(count, blk0, blk_row, blk_cnt, xs, w_gate, w_up, w_down)


def _combine_kernel(pos_ref, posn_ref, ys_ref, xb_ref, wts_ref, gate2_ref, gf_ref, yp_ref, ysm_ref,
                    buf, ytile, sem, *, n_prompt_tiles, n_tiles):
    i = pl.program_id(0)
    tt = xb_ref.shape[0]
    slot = i & 1

    def fetch(p_ref, s, t):
        for k in range(TOP_K):
            _rows_copy(ys_ref, p_ref[k, t], buf.at[s, k], t, 1, sem.at[s]).start(priority=k % 2)

    @pl.when(i == 0)
    def _():
        for t in range(tt):
            fetch(pos_ref, 0, t)

    for k in range(TOP_K):
        _rows_copy(ys_ref, 0, buf.at[slot, k], 0, tt, sem.at[slot]).wait()

    def finish(s, g):
        r0 = g * COMBINE_GROUP
        w = wts_ref[r0:r0 + COMBINE_GROUP, :]
        routed = None
        for k in range(TOP_K):
            rows = _load_rows(buf.at[s, k, pl.ds(r0 * ROW_TILES, COMBINE_GROUP * ROW_TILES)], COMBINE_GROUP)
            term = w[:, k:k + 1] * rows
            routed = term if routed is None else routed + term
        gate = gate2_ref[g * COMBINE_GROUP // ROW_GROUP]
        x = xb_ref[r0:r0 + COMBINE_GROUP, :] + gate * routed
        return _rms(x) * gf_ref[...]

    def tile_work(s, fetch_next):
        for g in range(tt // COMBINE_GROUP):
            y = finish(s, g)
            if fetch_next:
                for r in range(COMBINE_GROUP):
                    fetch(posn_ref, 1 - s, g * COMBINE_GROUP + r)
            ytile[g * COMBINE_GROUP:(g + 1) * COMBINE_GROUP, :] = y

    for s in range(2):
        @pl.when(jnp.logical_and(slot == s, i + 1 < n_tiles))
        def _(s=s):
            tile_work(s, True)

    @pl.when(i + 1 == n_tiles)
    def _():
        tile_work((n_tiles - 1) % 2, False)

    @pl.when(i < n_prompt_tiles)
    def _():
        yp_ref[...] = ytile[...]

    @pl.when(i >= n_prompt_tiles)
    def _():
        ysm_ref[...] = ytile[...]


def _combine(pos, ys, xb, wts_t, cond_tbl, gf, batch, n_prompt_rows):
    t = xb.shape[0]
    tt = COMBINE_TILE
    n_tiles = t // tt
    n_p = n_prompt_rows // tt
    pos_spec = lambda f: pl.BlockSpec((TOP_K, tt), f, memory_space=pltpu.SMEM)
    return pl.pallas_call(
        functools.partial(_combine_kernel, n_prompt_tiles=n_p, n_tiles=n_tiles),
        grid=(n_tiles,),
        in_specs=[pos_spec(lambda i: (0, i)),
                  pos_spec(lambda i: (0, jnp.minimum(i + 1, n_tiles - 1))),
                  pl.BlockSpec(memory_space=pl.ANY),
                  pl.BlockSpec((tt, D_MODEL), lambda i: (i, 0)),
                  pl.BlockSpec((tt, TOP_K), lambda i: (i, 0))]
                 + [_cond_spec(5, tt, n_p // batch, batch, n_p)]
                 + [pl.BlockSpec((1, D_MODEL), lambda i: (0, 0))],
        out_specs=[pl.BlockSpec((tt, D_MODEL), lambda i: (jnp.minimum(i, n_p - 1), 0)),
                   pl.BlockSpec((tt, D_MODEL), lambda i: (jnp.maximum(i - n_p, 0), 0))],
        out_shape=[jax.ShapeDtypeStruct((n_prompt_rows, D_MODEL), F32),
                   jax.ShapeDtypeStruct((t - n_prompt_rows, D_MODEL), F32)],
        scratch_shapes=[pltpu.VMEM((2, TOP_K, tt * ROW_TILES, LANES), ROW_DTYPE),
                        pltpu.VMEM((tt, D_MODEL), F32),
                        pltpu.SemaphoreType.DMA((2,))],
        compiler_params=_params(1),
        name="combine",
    )(pos, pos, ys, xb, wts_t, cond_tbl, gf)


def _sgu_mask():
    pos = np.arange(SGU_CHUNK) // CHUNK
    return jnp.asarray(pos[None, :] <= pos[:, None])


def _layer(xp3, xs3, cache_k, cache_v, mod, final_g, batch, seq, dec_batch, n, p):
    tp = batch * seq
    t = tp + dec_batch * n
    n_prompt_tiles = tp // TOKEN_TILE
    cond_tbl = jnp.concatenate([jnp.repeat(mod[:batch], COND_REPEAT, axis=0), mod[batch:]], axis=0)
    cond_tbl = cond_tbl.reshape(batch * COND_REPEAT + dec_batch, 1, N_MOD * D_MODEL)

    wm = jnp.where(_sgu_mask(), p["sgu_w"], 0.0)
    reps = SGU_CHUNK // n
    eye = jnp.eye(reps, dtype=F32)
    wms = jnp.stack([jnp.kron(eye, wm[g, :n, :n]) for g in range(SGU_GROUPS)])
    bs = jnp.broadcast_to(p["sgu_b"][:, :, None], (SGU_GROUPS, SGU_CHUNK, SGU_CHUNK))
    bss = jnp.broadcast_to(jnp.tile(p["sgu_b"][:, :n], (1, reps))[:, :, None], (SGU_GROUPS, SGU_CHUNK, SGU_CHUNK))

    q, k, v, sgu_n, vn_s = _stage1(
        xp3, xs3, cond_tbl, p["norm1_g"].reshape(1, -1), p["w_in"].astype(BF16),
        p["sgu_ln_g"].reshape(1, -1), p["sgu_ln_b"].reshape(1, -1),
        wm.astype(BF16), wms.astype(BF16), bs, bss, p["sgu_out_g"].reshape(1, -1), batch, n_prompt_tiles)

    gattn = p["attn_out_g"].reshape(1, -1)
    attn_p = _attn_prompt(q, k, v, p["attn_sinks"], gattn, batch, seq)
    w = cache_k.shape[1]
    attn_s = _attn_sample(q, k, v, cache_k.reshape(dec_batch, w, KV_WIDTH), cache_v.reshape(dec_batch, w, KV_WIDTH),
                          p["attn_sinks"], gattn, dec_batch, n, tp)

    wo = p["w_out"].astype(BF16)
    wsgu = jnp.concatenate([p["ws_gate"], p["ws_up"]], axis=1).astype(BF16)
    wrh = p["w_router"].astype(BF16)
    wrl = (p["w_router"] - wrh.astype(F32)).astype(BF16)
    xb, h2_tiles, scores = _merge(attn_p, attn_s, sgu_n, xp3, xs3, cond_tbl,
                                  wo[:ATTN_WIDTH], wo[ATTN_WIDTH:], p["norm2_g"].reshape(1, -1), wrh, wrl,
                                  wsgu, p["ws_down"].astype(BF16), batch, n_prompt_tiles)

    ti = np.arange(TOKEN_TILE)
    tri = jnp.asarray(ti[:, None] < ti[None, :], dtype=BF16)
    eidx, wts, rank, counts = _route(scores, p["router_bias"].reshape(N_EXPERTS, 1), tri)

    counts = counts[:, 0]
    ends = jnp.cumsum(counts)
    start = (ends - counts).astype(I32)
    nblk = (counts + EXPERT_BLOCK - 1) // EXPERT_BLOCK
    blk_end = jnp.cumsum(nblk)
    blk0 = (blk_end - nblk).astype(I32)
    pos = _positions(eidx, rank, start.reshape(N_EXPERTS, 1))

    n_rows = t * TOP_K
    g = jnp.arange(n_rows // EXPERT_BLOCK + N_EXPERTS + X_SLOTS, dtype=I32)[:, None]
    mine = jnp.logical_and(g >= blk0[None, :], g < blk_end[None, :])
    off = (g - blk0[None, :]) * EXPERT_BLOCK
    blk_row = jnp.sum(jnp.where(mine, start[None, :] + off, 0), axis=1).astype(I32)
    blk_cnt = jnp.sum(jnp.where(mine, jnp.minimum(counts[None, :] - off, EXPERT_BLOCK), 0), axis=1).astype(I32)

    xs_sorted = _dispatch(pos, h2_tiles, n_rows)
    ys_sorted = _experts(counts.astype(I32), blk0, blk_row, blk_cnt, xs_sorted,
                         p["w_gate"], p["w_up"], p["w_down"], n_rows)
    y_p, y_s = _combine(pos, ys_sorted, xb, wts.T, cond_tbl, final_g.reshape(1, -1), batch, tp)
    return y_p, y_s, k, v, vn_s


def kernel(x_prompt, x_sample, cache_k, cache_v, c_prompt, c_sample, norm1_g, w_ada, b_ada, w_in, sgu_ln_g, sgu_ln_b, sgu_w, sgu_b, attn_sinks, attn_out_g, sgu_out_g, w_out, norm2_g, w_router, router_bias, w_gate, w_up, w_down, ws_gate, ws_up, ws_down, final_g):
    batch, seq, d = x_prompt.shape
    dec_batch, n, _ = x_sample.shape
    depth = norm1_g.shape[0]
    assert depth == 1 and d == D_MODEL
    assert seq % TOKEN_TILE == 0 and dec_batch * n == TOKEN_TILE and n == ROW_GROUP
    assert seq % ATTN_TILE == 0 and seq >= WINDOW and cache_k.shape[2] == WINDOW
    tp = batch * seq

    xp3 = x_prompt.reshape(tp // ROW_GROUP, ROW_GROUP, d)
    xs3 = x_sample.reshape(dec_batch * n // ROW_GROUP, ROW_GROUP, d)
    c_all = jnp.concatenate([c_prompt, c_sample], axis=0)
    l = 0
    mod = _adaln(c_all, w_ada[l], b_ada[l])
    p = dict(norm1_g=norm1_g[l], w_in=w_in[l], sgu_ln_g=sgu_ln_g[l], sgu_ln_b=sgu_ln_b[l], sgu_w=sgu_w[l],
             sgu_b=sgu_b[l], attn_sinks=attn_sinks[l], attn_out_g=attn_out_g[l], sgu_out_g=sgu_out_g[l],
             w_out=w_out[l], norm2_g=norm2_g[l], w_router=w_router[l], router_bias=router_bias[l],
             w_gate=w_gate[l], w_up=w_up[l], w_down=w_down[l], ws_gate=ws_gate[l], ws_up=ws_up[l],
             ws_down=ws_down[l])
    y_p, y_s, k, v, vn_s = _layer(xp3, xs3, cache_k[l], cache_v[l], mod, final_g, batch, seq, dec_batch, n, p)

    keep = min(WINDOW, seq)
    kv_shape = (batch, keep, N_KV_HEADS, HEAD_DIM)
    k_p = jnp.stack([k[(b + 1) * seq - keep:(b + 1) * seq] for b in range(batch)]).reshape(kv_shape)
    v_p = jnp.stack([v[(b + 1) * seq - keep:(b + 1) * seq] for b in range(batch)]).reshape(kv_shape)
    k_s = k[tp:].reshape(dec_batch, n, N_KV_HEADS, HEAD_DIM)
    v_s = v[tp:].reshape(dec_batch, n, N_KV_HEADS, HEAD_DIM)
    return (y_p.reshape(batch, seq, d), y_s.reshape(dec_batch, n, d),
            k_p[None], v_p[None], k_s[None], v_s[None], vn_s.reshape(dec_batch, n, SGU_WIDTH)[None])
```

```python
import functools

import numpy as np
import jax
import jax.numpy as jnp
from jax import lax
from jax.experimental import pallas as pl
from jax.experimental.pallas import tpu as pltpu

F32 = jnp.float32
BF16 = jnp.bfloat16
I32 = jnp.int32

D_MODEL = 1024
CHUNK = 64
HEAD_DIM = 64
ATTN_WIDTH = 512
N_HEADS = 8
N_KV_HEADS = 2
GQA_GROUP = 4
KV_WIDTH = 128
WINDOW = 128
SGU_CHUNK = 128
SGU_WIDTH = 512
SGU_GROUPS = 4
IN_COLS = ATTN_WIDTH + 2 * KV_WIDTH + 2 * SGU_WIDTH
N_MOD = 6
N_EXPERTS = 256
TOP_K = 8
N_EXPERT_GROUPS = 8
GROUP_SIZE = N_EXPERTS // N_EXPERT_GROUPS
TOPK_GROUPS = 4
EXPERT_DIM = 256
ROUTED_SCALE = 2.5
EPS = 1e-6

LANES = 128
ROW_WORDS = D_MODEL // 2
ROW_TILES = ROW_WORDS // LANES
ROW_DTYPE = jnp.uint32
ROW_GROUP = 16
TOKEN_TILE = 256
DENSE_TILE = 512
COND_REPEAT = DENSE_TILE // ROW_GROUP
ATTN_TILE = 512
EXPERT_BLOCK = 640
BLOCK_SIZES = (128, 256, 384, 512, 576, EXPERT_BLOCK)
X_SLOTS = 4
W_SLOTS = 4
POS_TILE_MAX = 4096
COMBINE_TILE = 256
COMBINE_GROUP = 8
VMEM_LIMIT_BYTES = 56 * 1024 * 1024


def _params(n_axes=1):
    return pltpu.CompilerParams(dimension_semantics=("arbitrary",) * n_axes,
                                vmem_limit_bytes=VMEM_LIMIT_BYTES)


def _rms(x):
    return x * lax.rsqrt(jnp.mean(x * x, axis=-1, keepdims=True) + EPS)


def _cond_rows(m_ref):
    m = m_ref[...]
    g, _, d = m.shape
    return jnp.broadcast_to(m, (g, ROW_GROUP, d)).reshape(g * ROW_GROUP, d)


def _cond_spec(piece, tile_rows, tiles_per_batch, batch, n_prompt_tiles):
    g = tile_rows // ROW_GROUP
    per = COND_REPEAT // g
    return pl.BlockSpec(
        (g, 1, D_MODEL),
        lambda i: (jnp.where(i < n_prompt_tiles, (i // tiles_per_batch) * per, batch * per + i - n_prompt_tiles),
                   0, piece))


def _pick_tile(is_sample, prompt_ref, sample_ref):
    v = jnp.where(is_sample, sample_ref[...], prompt_ref[...])
    return v.reshape(v.shape[0] * v.shape[1], v.shape[-1])


def _pack_rows(x):
    x = x.astype(F32)
    return pltpu.pack_elementwise([x[:, :ROW_WORDS], x[:, ROW_WORDS:]], packed_dtype=BF16)


def _unpack_rows(w):
    return jnp.concatenate([pltpu.unpack_elementwise(w, index=i, packed_dtype=BF16, unpacked_dtype=F32)
                            for i in range(2)], axis=1)


def _load_rows(ref, n_rows):
    w = jnp.concatenate([ref[pl.ds(j, n_rows, stride=ROW_TILES), :] for j in range(ROW_TILES)], axis=1)
    return _unpack_rows(w)


def _store_rows(ref, x):
    n_rows = x.shape[0]
    w = _pack_rows(x)
    for j in range(ROW_TILES):
        ref[pl.ds(j, n_rows, stride=ROW_TILES), :] = w[:, j * LANES:(j + 1) * LANES]


def _tile_offset(row):
    off = row * ROW_TILES
    return off if isinstance(off, int) else pl.multiple_of(off, ROW_TILES)


def _rows_copy(src_ref, src_row, dst_ref, dst_row, n_rows, sem):
    s = _tile_offset(src_row)
    d = _tile_offset(dst_row)
    return pltpu.make_async_copy(src_ref.at[pl.ds(s, n_rows * ROW_TILES)],
                                 dst_ref.at[pl.ds(d, n_rows * ROW_TILES)], sem)


def _adaln_kernel(c_ref, w_ref, b_ref, o_ref):
    c = c_ref[...]
    s = c * jax.nn.sigmoid(c)
    o_ref[...] = jnp.dot(s, w_ref[...], precision=lax.Precision.HIGHEST,
                         preferred_element_type=F32) + b_ref[...]


def _adaln(c, w_ada, b_ada):
    n, d = c.shape
    cols = w_ada.shape[1]
    tn = 1536
    return pl.pallas_call(
        _adaln_kernel,
        grid=(cols // tn,),
        in_specs=[pl.BlockSpec((n, d), lambda j: (0, 0)),
                  pl.BlockSpec((d, tn), lambda j: (0, j)),
                  pl.BlockSpec((1, tn), lambda j: (0, j))],
        out_specs=pl.BlockSpec((n, tn), lambda j: (0, j)),
        out_shape=jax.ShapeDtypeStruct((n, cols), F32),
        compiler_params=_params(1),
        name="adaln",
    )(c, w_ada, b_ada.reshape(1, cols))


def _stage1_kernel(xp_ref, xs_ref, sh_ref, sc_ref, g1_ref, win_ref, lng_ref, lnb_ref,
                   wm_ref, wms_ref, bs_ref, bss_ref, gsgu_ref,
                   q_ref, k_ref, v_ref, sgu_ref, vn_ref, *, n_prompt_tiles):
    is_sample = pl.program_id(0) >= n_prompt_tiles
    x = _pick_tile(is_sample, xp_ref, xs_ref)
    h = _rms(x) * g1_ref[...] * (1.0 + _cond_rows(sc_ref)) + _cond_rows(sh_ref)
    proj = jnp.dot(h.astype(BF16), win_ref[...], preferred_element_type=F32)
    q_ref[...] = (proj[:, :ATTN_WIDTH] * (HEAD_DIM ** -0.5)).astype(BF16)
    k_ref[...] = proj[:, ATTN_WIDTH:ATTN_WIDTH + KV_WIDTH]
    v_ref[...] = proj[:, ATTN_WIDTH + KV_WIDTH:ATTN_WIDTH + 2 * KV_WIDTH]
    c0 = ATTN_WIDTH + 2 * KV_WIDTH
    u = jax.nn.gelu(proj[:, c0:c0 + SGU_WIDTH])
    vg = jax.nn.gelu(proj[:, c0 + SGU_WIDTH:])
    mu = jnp.mean(vg, axis=-1, keepdims=True)
    xc = vg - mu
    var = jnp.mean(xc * xc, axis=-1, keepdims=True)
    vn = xc * lax.rsqrt(var + EPS) * lng_ref[...] + lnb_ref[...]

    @pl.when(is_sample)
    def _():
        vn_ref[...] = vn

    vnb = vn.astype(BF16)
    wm = jnp.where(is_sample, wms_ref[...], wm_ref[...])
    bs = jnp.where(is_sample, bss_ref[...], bs_ref[...])
    rows = []
    for c in range(DENSE_TILE // SGU_CHUNK):
        r0 = c * SGU_CHUNK
        cols = []
        for g in range(SGU_GROUPS):
            l0 = g * SGU_CHUNK
            sv = jnp.dot(wm[g], vnb[r0:r0 + SGU_CHUNK, l0:l0 + SGU_CHUNK],
                         preferred_element_type=F32) + bs[g]
            cols.append(sv)
        rows.append(jnp.concatenate(cols, axis=1))
    sgu = u * jnp.concatenate(rows, axis=0)
    sgu_ref[...] = (_rms(sgu) * gsgu_ref[...]).astype(BF16)


def _stage1(xp3, xs3, cond_tbl, g1, win_b, lng, lnb, wm, wms, bs, bss, gsgu, batch, n_prompt_tiles):
    t = (xp3.shape[0] + xs3.shape[0]) * ROW_GROUP
    n_tiles = t // DENSE_TILE
    last_p = n_prompt_tiles - 1
    x_blk = (DENSE_TILE // ROW_GROUP, ROW_GROUP, D_MODEL)
    const2 = lambda i: (0, 0)
    const3 = lambda i: (0, 0, 0)
    tile2 = lambda i: (i, 0)
    cond = functools.partial(_cond_spec, tile_rows=DENSE_TILE, tiles_per_batch=n_prompt_tiles // batch,
                             batch=batch, n_prompt_tiles=n_prompt_tiles)
    sgu_w_spec = pl.BlockSpec((SGU_GROUPS, SGU_CHUNK, SGU_CHUNK), const3)
    return pl.pallas_call(
        functools.partial(_stage1_kernel, n_prompt_tiles=n_prompt_tiles),
        grid=(n_tiles,),
        in_specs=[pl.BlockSpec(x_blk, lambda i: (jnp.minimum(i, last_p), 0, 0)),
                  pl.BlockSpec(x_blk, const3)]
                 + [cond(0), cond(1)]
                 + [pl.BlockSpec((1, D_MODEL), const2),
                    pl.BlockSpec((D_MODEL, IN_COLS), const2),
                    pl.BlockSpec((1, SGU_WIDTH), const2),
                    pl.BlockSpec((1, SGU_WIDTH), const2),
                    sgu_w_spec, sgu_w_spec, sgu_w_spec, sgu_w_spec,
                    pl.BlockSpec((1, SGU_WIDTH), const2)],
        out_specs=[pl.BlockSpec((DENSE_TILE, ATTN_WIDTH), tile2),
                   pl.BlockSpec((DENSE_TILE, KV_WIDTH), tile2),
                   pl.BlockSpec((DENSE_TILE, KV_WIDTH), tile2),
                   pl.BlockSpec((DENSE_TILE, SGU_WIDTH), tile2),
                   pl.BlockSpec((DENSE_TILE, SGU_WIDTH), const2)],
        out_shape=[jax.ShapeDtypeStruct((t, ATTN_WIDTH), BF16),
                   jax.ShapeDtypeStruct((t, KV_WIDTH), F32),
                   jax.ShapeDtypeStruct((t, KV_WIDTH), F32),
                   jax.ShapeDtypeStruct((t, SGU_WIDTH), BF16),
                   jax.ShapeDtypeStruct((DENSE_TILE, SGU_WIDTH), F32)],
        compiler_params=_params(1),
        name="stage1",
    )(xp3, xs3, cond_tbl, cond_tbl, g1, win_b, lng, lnb, wm, wms, bs, bss, gsgu)


def _alibi_slope(h):
    return float(np.float32(2.0) ** np.float32(-8.0 * (h + 1.0) / N_HEADS))


def _sink_attention(q, k, v, back, valid_fn, sinks_ref, gout):
    nq = q.shape[0]
    nk = k.shape[0]
    q_t = q.astype(F32).T.astype(BF16)
    k = k.astype(BF16)
    v_t = v.T.astype(BF16)
    c = lax.broadcasted_iota(I32, (nk, nq), 0)
    r = lax.broadcasted_iota(I32, (nk, nq), 1)
    dist = jnp.abs(r - c + back).astype(F32)
    valid = None if valid_fn is None else valid_fn(c, r)
    outs = []
    for kvh in range(N_KV_HEADS):
        kv0 = kvh * HEAD_DIM
        heads = range(kvh * GQA_GROUP, (kvh + 1) * GQA_GROUP)
        qs = jnp.concatenate([q_t[h * HEAD_DIM:(h + 1) * HEAD_DIM, :] for h in heads], axis=1)
        l = jnp.dot(k[:, kv0:kv0 + HEAD_DIM], qs, preferred_element_type=F32)
        biased = [l[:, g * nq:(g + 1) * nq] - _alibi_slope(h) * dist for g, h in enumerate(heads)]
        if valid is not None:
            biased = [jnp.where(valid, b, -jnp.inf) for b in biased]
        l = jnp.concatenate(biased, axis=1)
        sink = jnp.concatenate([jnp.full((1, nq), sinks_ref[h], F32) for h in heads], axis=1)
        m = jnp.maximum(jnp.max(l, axis=0, keepdims=True), sink)
        e = jnp.exp(l - m)
        denom = jnp.sum(e, axis=0, keepdims=True) + jnp.exp(sink - m)
        pv = jnp.dot(v_t[kv0:kv0 + HEAD_DIM, :], e.astype(BF16), preferred_element_type=F32)
        o_t = pv / denom
        outs.extend(o_t[:, g * nq:(g + 1) * nq] for g in range(GQA_GROUP))
    o = jnp.concatenate(outs, axis=0).T
    return (_rms(o) * gout).astype(BF16)


def _attn_prompt_kernel(sinks_ref, q_ref, kp_ref, kc_ref, vp_ref, vc_ref, g_ref, o_ref):
    i = pl.program_id(1)
    back = kp_ref.shape[0]

    k = jnp.concatenate([kp_ref[...], kc_ref[...]], axis=0)
    v = jnp.concatenate([vp_ref[...], vc_ref[...]], axis=0)
    for s in range(q_ref.shape[0] // WINDOW):
        def valid_fn(c, r, s=s):
            kc = c // CHUNK
            qc = r // CHUNK + (back // CHUNK)
            in_band = jnp.logical_and(kc <= qc, kc >= qc - WINDOW // CHUNK)
            if s > 0:
                return in_band
            return jnp.logical_and(in_band, jnp.logical_or(i > 0, kc >= back // CHUNK))

        rows = slice(s * WINDOW, (s + 1) * WINDOW)
        keys = slice(s * WINDOW, s * WINDOW + back + WINDOW)
        o_ref[rows, :] = _sink_attention(q_ref[rows, :], k[keys], v[keys], back, valid_fn, sinks_ref, g_ref[...])


def _attn_prompt(q, k, v, sinks, gattn, batch, seq):
    nt = seq // ATTN_TILE
    per_tile = ATTN_TILE // WINDOW
    cur = lambda b, i, s: (b * nt + i, 0)
    prev = lambda b, i, s: (jnp.maximum((b * nt + i) * per_tile - 1, 0), 0)
    kv_blk = (ATTN_TILE, KV_WIDTH)
    back_blk = (WINDOW, KV_WIDTH)
    return pl.pallas_call(
        _attn_prompt_kernel,
        grid_spec=pltpu.PrefetchScalarGridSpec(
            num_scalar_prefetch=1,
            grid=(batch, nt),
            in_specs=[pl.BlockSpec((ATTN_TILE, ATTN_WIDTH), cur),
                      pl.BlockSpec(back_blk, prev), pl.BlockSpec(kv_blk, cur),
                      pl.BlockSpec(back_blk, prev), pl.BlockSpec(kv_blk, cur),
                      pl.BlockSpec((1, ATTN_WIDTH), lambda b, i, s: (0, 0))],
            out_specs=pl.BlockSpec((ATTN_TILE, ATTN_WIDTH), cur)),
        out_shape=jax.ShapeDtypeStruct((batch * seq, ATTN_WIDTH), BF16),
        compiler_params=_params(2),
        name="attn_prompt",
    )(sinks, q, k, k, v, v, gattn)


def _attn_sample_kernel(sinks_ref, q_ref, ck_ref, cv_ref, k_ref, v_ref, g_ref, o_ref):
    k = jnp.concatenate([ck_ref[0], k_ref[...]], axis=0)
    v = jnp.concatenate([cv_ref[0], v_ref[...]], axis=0)
    o_ref[...] = _sink_attention(q_ref[...], k, v, ck_ref.shape[1], None, sinks_ref, g_ref[...])


def _attn_sample(q, k, v, cache_k, cache_v, sinks, gattn, dec_batch, n, row0):
    w = cache_k.shape[1]
    blk0 = row0 // n
    new = lambda b, s: (blk0 + b, 0)
    return pl.pallas_call(
        _attn_sample_kernel,
        grid_spec=pltpu.PrefetchScalarGridSpec(
            num_scalar_prefetch=1,
            grid=(dec_batch,),
            in_specs=[pl.BlockSpec((n, ATTN_WIDTH), new),
                      pl.BlockSpec((1, w, KV_WIDTH), lambda b, s: (b, 0, 0)),
                      pl.BlockSpec((1, w, KV_WIDTH), lambda b, s: (b, 0, 0)),
                      pl.BlockSpec((n, KV_WIDTH), new),
                      pl.BlockSpec((n, KV_WIDTH), new),
                      pl.BlockSpec((1, ATTN_WIDTH), lambda b, s: (0, 0))],
            out_specs=pl.BlockSpec((n, ATTN_WIDTH), lambda b, s: (b, 0))),
        out_shape=jax.ShapeDtypeStruct((dec_batch * n, ATTN_WIDTH), BF16),
        compiler_params=_params(1),
        name="attn_sample",
    )(sinks, q, cache_k, cache_v, k, v, gattn)


def _merge_kernel(ap_ref, as_ref, sgu_ref, xp_ref, xs_ref,
                  gate1_ref, sh2_ref, sc2_ref, gate2_ref,
                  woa_ref, wos_ref, g2_ref, wrh_ref, wrl_ref, wsgu_ref, wsd_ref,
                  xb_ref, h2_ref, sc_ref, *, n_prompt_tiles):
    is_sample = pl.program_id(0) >= n_prompt_tiles
    x = _pick_tile(is_sample, xp_ref, xs_ref)
    a = jnp.where(is_sample, as_ref[...], ap_ref[...])
    mix = (jnp.dot(a, woa_ref[...], preferred_element_type=F32)
           + jnp.dot(sgu_ref[...], wos_ref[...], preferred_element_type=F32))
    x1 = x + _cond_rows(gate1_ref) * mix
    h2 = _rms(x1) * g2_ref[...] * (1.0 + _cond_rows(sc2_ref)) + _cond_rows(sh2_ref)
    hh = h2.astype(BF16)
    _store_rows(h2_ref, hh)
    hl = (h2 - hh.astype(F32)).astype(BF16)
    logits = (jnp.dot(hh, wrh_ref[...], preferred_element_type=F32)
              + (jnp.dot(hl, wrh_ref[...], preferred_element_type=F32)
                 + jnp.dot(hh, wrl_ref[...], preferred_element_type=F32)))
    sc_ref[...] = jax.nn.sigmoid(logits)
    gu = jnp.dot(hh, wsgu_ref[...], preferred_element_type=F32)
    g = gu[:, :EXPERT_DIM]
    act = (g * jax.nn.sigmoid(g)) * gu[:, EXPERT_DIM:]
    shared = jnp.dot(act.astype(BF16), wsd_ref[...], preferred_element_type=F32)
    xb_ref[...] = x1 + _cond_rows(gate2_ref) * shared


def _merge(attn_p, attn_s, sgu_n, xp3, xs3, cond_tbl, woa, wos, g2, wrh, wrl, wsgu, wsd, batch, n_prompt_tiles):
    t = sgu_n.shape[0]
    n_tiles = t // DENSE_TILE
    last_p = n_prompt_tiles - 1
    x_blk = (DENSE_TILE // ROW_GROUP, ROW_GROUP, D_MODEL)
    const2 = lambda i: (0, 0)
    const3 = lambda i: (0, 0, 0)
    tile2 = lambda i: (i, 0)
    cond = functools.partial(_cond_spec, tile_rows=DENSE_TILE, tiles_per_batch=n_prompt_tiles // batch,
                             batch=batch, n_prompt_tiles=n_prompt_tiles)
    return pl.pallas_call(
        functools.partial(_merge_kernel, n_prompt_tiles=n_prompt_tiles),
        grid=(n_tiles,),
        in_specs=[pl.BlockSpec((DENSE_TILE, ATTN_WIDTH), lambda i: (jnp.minimum(i, last_p), 0)),
                  pl.BlockSpec((DENSE_TILE, ATTN_WIDTH), const2),
                  pl.BlockSpec((DENSE_TILE, SGU_WIDTH), tile2),
                  pl.BlockSpec(x_blk, lambda i: (jnp.minimum(i, last_p), 0, 0)),
                  pl.BlockSpec(x_blk, const3)]
                 + [cond(2), cond(3), cond(4), cond(5)]
                 + [pl.BlockSpec((ATTN_WIDTH, D_MODEL), const2),
                    pl.BlockSpec((SGU_WIDTH, D_MODEL), const2),
                    pl.BlockSpec((1, D_MODEL), const2),
                    pl.BlockSpec((D_MODEL, N_EXPERTS), const2),
                    pl.BlockSpec((D_MODEL, N_EXPERTS), const2),
                    pl.BlockSpec((D_MODEL, 2 * EXPERT_DIM), const2),
                    pl.BlockSpec((EXPERT_DIM, D_MODEL), const2)],
        out_specs=[pl.BlockSpec((DENSE_TILE, D_MODEL), tile2),
                   pl.BlockSpec((DENSE_TILE * ROW_TILES, LANES), tile2),
                   pl.BlockSpec((DENSE_TILE, N_EXPERTS), tile2)],
        out_shape=[jax.ShapeDtypeStruct((t, D_MODEL), F32),
                   jax.ShapeDtypeStruct((t * ROW_TILES, LANES), ROW_DTYPE),
                   jax.ShapeDtypeStruct((t, N_EXPERTS), F32)],
        compiler_params=_params(1),
        name="merge",
    )(attn_p, attn_s, sgu_n, xp3, xs3, *([cond_tbl] * 4), woa, wos, g2, wrh, wrl, wsgu, wsd)


def _first_argmax(x, iota, size, axis):
    m = jnp.max(x, axis=axis, keepdims=True)
    idx = jnp.min(jnp.where(x == m, iota, size), axis=axis, keepdims=True)
    return m, idx


def _route_kernel(sc_ref, bias_ref, tri_ref, eidx_ref, wts_ref, rank_ref, cnt_ref, base_ref):
    tt = sc_ref.shape[0]

    @pl.when(pl.program_id(0) == 0)
    def _():
        base_ref[...] = jnp.zeros_like(base_ref)

    s_t = sc_ref[...].T
    sel = s_t + bias_ref[...]
    sel3 = sel.reshape(N_EXPERT_GROUPS, GROUP_SIZE, tt)
    io3 = lax.broadcasted_iota(I32, sel3.shape, 1)
    m1, i1 = _first_argmax(sel3, io3, GROUP_SIZE, 1)
    m2 = jnp.max(jnp.where(io3 == i1, -jnp.inf, sel3), axis=1, keepdims=True)
    gs = (m1 + m2).reshape(N_EXPERT_GROUPS, tt)
    io8 = lax.broadcasted_iota(I32, gs.shape, 0)
    gmask = jnp.zeros(gs.shape, jnp.bool_)
    for _ in range(TOPK_GROUPS):
        _, gi = _first_argmax(gs, io8, N_EXPERT_GROUPS, 0)
        hit = io8 == gi
        gmask = jnp.logical_or(gmask, hit)
        gs = jnp.where(hit, -jnp.inf, gs)
    emask = jnp.broadcast_to(gmask.reshape(N_EXPERT_GROUPS, 1, tt), sel3.shape).reshape(N_EXPERTS, tt)
    cand = jnp.where(emask, sel, -jnp.inf)
    io = lax.broadcasted_iota(I32, cand.shape, 0)
    chosen = jnp.zeros(cand.shape, jnp.bool_)
    eidx, wts = [], []
    for _ in range(TOP_K):
        _, ei = _first_argmax(cand, io, N_EXPERTS, 0)
        hit = io == ei
        eidx.append(ei)
        wts.append(jnp.sum(jnp.where(hit, s_t, 0.0), axis=0, keepdims=True))
        chosen = jnp.logical_or(chosen, hit)
        cand = jnp.where(hit, -jnp.inf, cand)
    w = jnp.concatenate(wts, axis=0)
    wts_ref[...] = w / jnp.sum(w, axis=0, keepdims=True) * ROUTED_SCALE
    eidx_ref[...] = jnp.concatenate(eidx, axis=0)
    cf = jnp.where(chosen, 1.0, 0.0)
    ahead = jnp.dot(cf.astype(BF16), tri_ref[...], preferred_element_type=F32) + base_ref[...]
    ranks = [jnp.sum(jnp.where(io == ei, ahead, 0.0), axis=0, keepdims=True) for ei in eidx]
    rank_ref[...] = jnp.concatenate(ranks, axis=0).astype(I32)
    total = base_ref[...] + jnp.sum(cf, axis=1, keepdims=True)
    base_ref[...] = total
    cnt_ref[...] = total.astype(I32)


def _route(scores, bias_col, tri, t):
    n_tiles = t // TOKEN_TILE
    col = lambda i: (0, i)
    return pl.pallas_call(
        _route_kernel,
        grid=(n_tiles,),
        in_specs=[pl.BlockSpec((TOKEN_TILE, N_EXPERTS), lambda i: (i, 0)),
                  pl.BlockSpec((N_EXPERTS, 1), lambda i: (0, 0)),
                  pl.BlockSpec((TOKEN_TILE, TOKEN_TILE), lambda i: (0, 0))],
        out_specs=[pl.BlockSpec((TOP_K, TOKEN_TILE), col),
                   pl.BlockSpec((TOP_K, TOKEN_TILE), col),
                   pl.BlockSpec((TOP_K, TOKEN_TILE), col),
                   pl.BlockSpec((N_EXPERTS, 1), lambda i: (0, 0))],
        out_shape=[jax.ShapeDtypeStruct((TOP_K, t), I32),
                   jax.ShapeDtypeStruct((TOP_K, t), F32),
                   jax.ShapeDtypeStruct((TOP_K, t), I32),
                   jax.ShapeDtypeStruct((N_EXPERTS, 1), I32)],
        scratch_shapes=[pltpu.VMEM((N_EXPERTS, 1), F32)],
        compiler_params=_params(1),
        name="route",
    )(scores, bias_col, tri)


def _pos_kernel(eidx_ref, rank_ref, start_ref, pos_ref):
    e = eidx_ref[...]
    tt = e.shape[1]
    io = lax.broadcasted_iota(I32, (N_EXPERTS, tt), 0)
    st = start_ref[...]
    rows = [jnp.sum(jnp.where(io == e[k:k + 1, :], st, 0), axis=0, keepdims=True) for k in range(TOP_K)]
    pos_ref[...] = jnp.concatenate(rows, axis=0) + rank_ref[...]


def _positions(eidx, rank, start_col):
    t = eidx.shape[1]
    lane_tiles = t // LANES
    width = max(d for d in range(1, POS_TILE_MAX // LANES + 1) if lane_tiles % d == 0) * LANES
    col = lambda i: (0, i)
    blk = pl.BlockSpec((TOP_K, width), col)
    return pl.pallas_call(
        _pos_kernel,
        grid=(t // width,),
        in_specs=[blk, blk, pl.BlockSpec((N_EXPERTS, 1), lambda i: (0, 0))],
        out_specs=blk,
        out_shape=jax.ShapeDtypeStruct((TOP_K, t), I32),
        compiler_params=_params(1),
        name="positions",
    )(eidx, rank, start_col)


def _dispatch_kernel(pos_ref, h_ref, xs_ref, zero_ref, sem, zsem, *, n_rows):
    tt = h_ref.shape[0] // ROW_TILES
    first = pl.program_id(0) == 0

    @pl.when(first)
    def _():
        zero_ref[...] = jnp.zeros_like(zero_ref)
        _rows_copy(zero_ref, 0, xs_ref, n_rows, EXPERT_BLOCK, zsem).start()

    for t in range(tt):
        for k in range(TOP_K):
            _rows_copy(h_ref, t, xs_ref, pos_ref[k, t], 1, sem).start(priority=k % 2)
    for k in range(TOP_K):
        _rows_copy(h_ref, 0, xs_ref, 0, tt, sem).wait()

    @pl.when(first)
    def _():
        _rows_copy(zero_ref, 0, xs_ref, n_rows, EXPERT_BLOCK, zsem).wait()


def _dispatch(pos, h2_tiles, t, n_rows):
    return pl.pallas_call(
        functools.partial(_dispatch_kernel, n_rows=n_rows),
        grid=(t // TOKEN_TILE,),
        in_specs=[pl.BlockSpec((TOP_K, TOKEN_TILE), lambda i: (0, i), memory_space=pltpu.SMEM),
                  pl.BlockSpec((TOKEN_TILE * ROW_TILES, LANES), lambda i: (i, 0))],
        out_specs=pl.BlockSpec(memory_space=pl.ANY),
        out_shape=jax.ShapeDtypeStruct(((n_rows + EXPERT_BLOCK) * ROW_TILES, LANES), ROW_DTYPE),
        scratch_shapes=[pltpu.VMEM((EXPERT_BLOCK * ROW_TILES, LANES), ROW_DTYPE),
                        pltpu.SemaphoreType.DMA, pltpu.SemaphoreType.DMA],
        compiler_params=_params(1),
        name="dispatch",
    )(pos, h2_tiles)


def _out_pieces(nvalid, fn):
    s = 1 << (EXPERT_BLOCK.bit_length() - 1)
    while s >= 1:
        @pl.when((nvalid & s) != 0)
        def _(s=s):
            fn(nvalid & ~(2 * s - 1), s)
        s //= 2


def _for_block_size(nvalid, fn):
    lo = 0
    for rows in BLOCK_SIZES:
        @pl.when(jnp.logical_and(nvalid > lo, nvalid <= rows))
        def _(rows=rows):
            fn(rows)
        lo = rows


def _expert_kernel(count_ref, blk0_ref, row_ref, cnt_ref, xs_ref, wg_ref, wu_ref, wd_ref, ys_ref,
                   xbuf, ybuf, wg_buf, wu_buf, wd_buf, wgu_s, wd_s, state, xsem, ysem, wsem):
    e = pl.program_id(0)
    n_experts = pl.num_programs(0)
    cnt = count_ref[e]
    g0 = blk0_ref[e]
    nblk = (cnt + EXPERT_BLOCK - 1) // EXPERT_BLOCK
    wslot = lax.rem(e, W_SLOTS)

    def w_copies(expert, slot):
        return [pltpu.make_async_copy(src.at[expert], dst.at[slot], wsem.at[slot])
                for src, dst in ((wg_ref, wg_buf), (wu_ref, wu_buf), (wd_ref, wd_buf))]

    def w_fetch(expert):
        @pl.when(expert < n_experts)
        def _():
            for c in w_copies(expert, lax.rem(expert, W_SLOTS)):
                c.start()

    def x_copy(row, slot, rows):
        return _rows_copy(xs_ref, row, xbuf.at[slot], 0, rows, xsem.at[slot])

    def x_fetch(g, slot):
        _for_block_size(cnt_ref[g], lambda rows: x_copy(row_ref[g], slot, rows).start())

    def y_wait(nvalid, slot):
        def wait(off, size):
            _rows_copy(ybuf.at[slot], 0, ys_ref, 0, size, ysem.at[slot]).wait()
        _out_pieces(nvalid, wait)

    @pl.when(e == 0)
    def _():
        state[0] = 0
        state[1] = 0
        for g in range(X_SLOTS - 1):
            x_fetch(g, g)
        for ahead in range(W_SLOTS - 1):
            w_fetch(ahead)

    w_fetch(e + W_SLOTS - 1)
    for c in w_copies(e, wslot):
        c.wait()
    wgu_s[:, :EXPERT_DIM] = wg_buf[wslot].astype(BF16)
    wgu_s[:, EXPERT_DIM:] = wu_buf[wslot].astype(BF16)
    wd_s[...] = wd_buf[wslot].astype(BF16)

    def compute(rows, xslot, yslot, row0, nvalid):
        x_copy(row0, xslot, rows).wait()
        x = _load_rows(xbuf.at[xslot], rows).astype(BF16)
        gu = jnp.dot(x, wgu_s[...], preferred_element_type=F32)
        g = gu[:, :EXPERT_DIM]
        act = (g * jax.nn.sigmoid(g)) * gu[:, EXPERT_DIM:]
        y = jnp.dot(act.astype(BF16), wd_s[...], preferred_element_type=F32)
        y_wait(state[0], 1 - yslot)
        _store_rows(ybuf.at[yslot], y)

        def put(off, size):
            _rows_copy(ybuf.at[yslot], off, ys_ref, row0 + off, size, ysem.at[yslot]).start()
        _out_pieces(nvalid, put)

    def block(j, carry):
        g = g0 + j
        xslot = state[1]
        yslot = g & 1
        row0 = row_ref[g]
        nvalid = cnt_ref[g]
        x_fetch(g + X_SLOTS - 1, jnp.where(xslot == 0, X_SLOTS - 1, xslot - 1))
        _for_block_size(nvalid, lambda rows: compute(rows, xslot, yslot, row0, nvalid))
        state[0] = nvalid
        state[1] = jnp.where(xslot == X_SLOTS - 1, 0, xslot + 1)
        return carry

    lax.fori_loop(0, nblk, block, 0)

    @pl.when(e == pl.num_programs(0) - 1)
    def _():
        y_wait(state[0], (g0 + nblk - 1) & 1)


def _experts(count, blk0, blk_row, blk_cnt, xs, w_gate, w_up, w_down, n_rows):
    blk_rows = EXPERT_BLOCK * ROW_TILES
    any_spec = pl.BlockSpec(memory_space=pl.ANY)
    return pl.pallas_call(
        _expert_kernel,
        grid_spec=pltpu.PrefetchScalarGridSpec(
            num_scalar_prefetch=4,
            grid=(N_EXPERTS,),
            in_specs=[any_spec, any_spec, any_spec, any_spec],
            out_specs=any_spec,
            scratch_shapes=[pltpu.VMEM((X_SLOTS, blk_rows, LANES), ROW_DTYPE),
                            pltpu.VMEM((2, blk_rows, LANES), ROW_DTYPE),
                            pltpu.VMEM((W_SLOTS, D_MODEL, EXPERT_DIM), F32),
                            pltpu.VMEM((W_SLOTS, D_MODEL, EXPERT_DIM), F32),
                            pltpu.VMEM((W_SLOTS, EXPERT_DIM, D_MODEL), F32),
                            pltpu.VMEM((D_MODEL, 2 * EXPERT_DIM), BF16),
                            pltpu.VMEM((EXPERT_DIM, D_MODEL), BF16),
                            pltpu.SMEM((2,), I32),
                            pltpu.SemaphoreType.DMA((X_SLOTS,)),
                            pltpu.SemaphoreType.DMA((2,)),
                            pltpu.SemaphoreType.DMA((W_SLOTS,))]),
        out_shape=jax.ShapeDtypeStruct((n_rows * ROW_TILES, LANES), ROW_DTYPE),
        compiler_params=_params(1),
        name="experts",
    )(count, blk0, blk_row, blk_cnt, xs, w_gate, w_up, w_down)


def _combine_kernel(pos_ref, posn_ref, ys_ref, xb_ref, wts_ref, gate2_ref, gf_ref, yp_ref, ysm_ref,
                    buf, ytile, sem, *, n_prompt_tiles, n_tiles):
    i = pl.program_id(0)
    tt = xb_ref.shape[0]
    slot = i & 1

    def fetch(p_ref, s, t):
        for k in range(TOP_K):
            _rows_copy(ys_ref, p_ref[k, t], buf.at[s, k], t, 1, sem.at[s]).start(priority=k % 2)

    @pl.when(i == 0)
    def _():
        for t in range(tt):
            fetch(pos_ref, 0, t)

    for k in range(TOP_K):
        _rows_copy(ys_ref, 0, buf.at[slot, k], 0, tt, sem.at[slot]).wait()

    def finish(s, g):
        r0 = g * COMBINE_GROUP
        w = wts_ref[r0:r0 + COMBINE_GROUP, :]
        routed = None
        for k in range(TOP_K):
            rows = _load_rows(buf.at[s, k, pl.ds(r0 * ROW_TILES, COMBINE_GROUP * ROW_TILES)], COMBINE_GROUP)
            term = w[:, k:k + 1] * rows
            routed = term if routed is None else routed + term
        gate = gate2_ref[g * COMBINE_GROUP // ROW_GROUP]
        x = xb_ref[r0:r0 + COMBINE_GROUP, :] + gate * routed
        return _rms(x) * gf_ref[...]

    def tile_work(s, fetch_next):
        for g in range(tt // COMBINE_GROUP):
            y = finish(s, g)
            if fetch_next:
                for r in range(COMBINE_GROUP):
                    fetch(posn_ref, 1 - s, g * COMBINE_GROUP + r)
            ytile[g * COMBINE_GROUP:(g + 1) * COMBINE_GROUP, :] = y

    for s in range(2):
        @pl.when(jnp.logical_and(slot == s, i + 1 < n_tiles))
        def _(s=s):
            tile_work(s, True)

    @pl.when(i + 1 == n_tiles)
    def _():
        tile_work((n_tiles - 1) % 2, False)

    @pl.when(i < n_prompt_tiles)
    def _():
        yp_ref[...] = ytile[...]

    @pl.when(i >= n_prompt_tiles)
    def _():
        ysm_ref[...] = ytile[...]


def _combine(pos, ys, xb, wts_t, cond_tbl, gf, batch, n_prompt_rows, t):
    tt = COMBINE_TILE
    n_tiles = t // tt
    n_p = n_prompt_rows // tt
    pos_spec = lambda f: pl.BlockSpec((TOP_K, tt), f, memory_space=pltpu.SMEM)
    return pl.pallas_call(
        functools.partial(_combine_kernel, n_prompt_tiles=n_p, n_tiles=n_tiles),
        grid=(n_tiles,),
        in_specs=[pos_spec(lambda i: (0, i)),
                  pos_spec(lambda i: (0, jnp.minimum(i + 1, n_tiles - 1))),
                  pl.BlockSpec(memory_space=pl.ANY),
                  pl.BlockSpec((tt, D_MODEL), lambda i: (i, 0)),
                  pl.BlockSpec((tt, TOP_K), lambda i: (i, 0))]
                 + [_cond_spec(5, tt, n_p // batch, batch, n_p)]
                 + [pl.BlockSpec((1, D_MODEL), lambda i: (0, 0))],
        out_specs=[pl.BlockSpec((tt, D_MODEL), lambda i: (jnp.minimum(i, n_p - 1), 0)),
                   pl.BlockSpec((tt, D_MODEL), lambda i: (jnp.maximum(i - n_p, 0), 0))],
        out_shape=[jax.ShapeDtypeStruct((n_prompt_rows, D_MODEL), F32),
                   jax.ShapeDtypeStruct((t - n_prompt_rows, D_MODEL), F32)],
        scratch_shapes=[pltpu.VMEM((2, TOP_K, tt * ROW_TILES, LANES), ROW_DTYPE),
                        pltpu.VMEM((tt, D_MODEL), F32),
                        pltpu.SemaphoreType.DMA((2,))],
        compiler_params=_params(1),
        name="combine",
    )(pos, pos, ys, xb, wts_t, cond_tbl, gf)


def _sgu_mask():
    pos = np.arange(SGU_CHUNK) // CHUNK
    return jnp.asarray(pos[None, :] <= pos[:, None])


def _layer(xp3, xs3, cache_k, cache_v, mod, final_g, batch, seq, dec_batch, n, p):
    tp = batch * seq
    t = tp + dec_batch * n
    n_prompt_tiles = tp // DENSE_TILE
    pad_groups = (DENSE_TILE - dec_batch * n) // ROW_GROUP
    xs3 = jnp.pad(xs3, ((0, pad_groups), (0, 0), (0, 0)))
    cond_tbl = jnp.concatenate([jnp.repeat(mod[:batch], COND_REPEAT, axis=0), mod[batch:]], axis=0)
    cond_tbl = jnp.pad(cond_tbl, ((0, pad_groups), (0, 0)))
    cond_tbl = cond_tbl.reshape(batch * COND_REPEAT + dec_batch + pad_groups, 1, N_MOD * D_MODEL)

    wm = jnp.where(_sgu_mask(), p["sgu_w"], 0.0)
    reps = SGU_CHUNK // n
    eye = jnp.eye(reps, dtype=F32)
    wms = jnp.stack([jnp.kron(eye, wm[g, :n, :n]) for g in range(SGU_GROUPS)])
    bs = jnp.broadcast_to(p["sgu_b"][:, :, None], (SGU_GROUPS, SGU_CHUNK, SGU_CHUNK))
    bss = jnp.broadcast_to(jnp.tile(p["sgu_b"][:, :n], (1, reps))[:, :, None], (SGU_GROUPS, SGU_CHUNK, SGU_CHUNK))

    q, k, v, sgu_n, vn_s = _stage1(
        xp3, xs3, cond_tbl, p["norm1_g"].reshape(1, -1), p["w_in"].astype(BF16),
        p["sgu_ln_g"].reshape(1, -1), p["sgu_ln_b"].reshape(1, -1),
        wm.astype(BF16), wms.astype(BF16), bs, bss, p["sgu_out_g"].reshape(1, -1), batch, n_prompt_tiles)

    gattn = p["attn_out_g"].reshape(1, -1)
    attn_p = _attn_prompt(q, k, v, p["attn_sinks"], gattn, batch, seq)
    w = cache_k.shape[1]
    attn_s = _attn_sample(q, k, v, cache_k.reshape(dec_batch, w, KV_WIDTH), cache_v.reshape(dec_batch, w, KV_WIDTH),
                          p["attn_sinks"], gattn, dec_batch, n, tp)

    wo = p["w_out"].astype(BF16)
    wsgu = jnp.concatenate([p["ws_gate"], p["ws_up"]], axis=1).astype(BF16)
    wrh = p["w_router"].astype(BF16)
    wrl = (p["w_router"] - wrh.astype(F32)).astype(BF16)
    attn_s = jnp.pad(attn_s, ((0, DENSE_TILE - dec_batch * n), (0, 0)))
    xb, h2_tiles, scores = _merge(attn_p, attn_s, sgu_n, xp3, xs3, cond_tbl,
                                  wo[:ATTN_WIDTH], wo[ATTN_WIDTH:], p["norm2_g"].reshape(1, -1), wrh, wrl,
                                  wsgu, p["ws_down"].astype(BF16), batch, n_prompt_tiles)

    ti = np.arange(TOKEN_TILE)
    tri = jnp.asarray(ti[:, None] < ti[None, :], dtype=BF16)
    eidx, wts, rank, counts = _route(scores, p["router_bias"].reshape(N_EXPERTS, 1), tri, t)

    counts = counts[:, 0]
    ends = jnp.cumsum(counts)
    start = (ends - counts).astype(I32)
    nblk = (counts + EXPERT_BLOCK - 1) // EXPERT_BLOCK
    blk_end = jnp.cumsum(nblk)
    blk0 = (blk_end - nblk).astype(I32)
    pos = _positions(eidx, rank, start.reshape(N_EXPERTS, 1))

    n_rows = t * TOP_K
    g = jnp.arange(n_rows // EXPERT_BLOCK + N_EXPERTS + X_SLOTS, dtype=I32)[:, None]
    mine = jnp.logical_and(g >= blk0[None, :], g < blk_end[None, :])
    off = (g - blk0[None, :]) * EXPERT_BLOCK
    blk_row = jnp.sum(jnp.where(mine, start[None, :] + off, 0), axis=1).astype(I32)
    blk_cnt = jnp.sum(jnp.where(mine, jnp.minimum(counts[None, :] - off, EXPERT_BLOCK), 0), axis=1).astype(I32)

    xs_sorted = _dispatch(pos, h2_tiles, t, n_rows)
    ys_sorted = _experts(counts.astype(I32), blk0, blk_row, blk_cnt, xs_sorted,
                         p["w_gate"], p["w_up"], p["w_down"], n_rows)
    y_p, y_s = _combine(pos, ys_sorted, xb, wts.T, cond_tbl, final_g.reshape(1, -1), batch, tp, t)
    return y_p, y_s, k, v, vn_s[:dec_batch * n]


def kernel(x_prompt, x_sample, cache_k, cache_v, c_prompt, c_sample, norm1_g, w_ada, b_ada, w_in, sgu_ln_g, sgu_ln_b, sgu_w, sgu_b, attn_sinks, attn_out_g, sgu_out_g, w_out, norm2_g, w_router, router_bias, w_gate, w_up, w_down, ws_gate, ws_up, ws_down, final_g):
    batch, seq, d = x_prompt.shape
    dec_batch, n, _ = x_sample.shape
    depth = norm1_g.shape[0]
    assert depth == 1 and d == D_MODEL
    assert seq % DENSE_TILE == 0 and dec_batch * n == TOKEN_TILE and n == ROW_GROUP
    assert seq % ATTN_TILE == 0 and seq >= WINDOW and cache_k.shape[2] == WINDOW
    tp = batch * seq

    xp3 = x_prompt.reshape(tp // ROW_GROUP, ROW_GROUP, d)
    xs3 = x_sample.reshape(dec_batch * n // ROW_GROUP, ROW_GROUP, d)
    c_all = jnp.concatenate([c_prompt, c_sample], axis=0)
    l = 0
    mod = _adaln(c_all, w_ada[l], b_ada[l])
    p = dict(norm1_g=norm1_g[l], w_in=w_in[l], sgu_ln_g=sgu_ln_g[l], sgu_ln_b=sgu_ln_b[l], sgu_w=sgu_w[l],
             sgu_b=sgu_b[l], attn_sinks=attn_sinks[l], attn_out_g=attn_out_g[l], sgu_out_g=sgu_out_g[l],
             w_out=w_out[l], norm2_g=norm2_g[l], w_router=w_router[l], router_bias=router_bias[l],
             w_gate=w_gate[l], w_up=w_up[l], w_down=w_down[l], ws_gate=ws_gate[l], ws_up=ws_up[l],
             ws_down=ws_down[l])
    y_p, y_s, k, v, vn_s = _layer(xp3, xs3, cache_k[l], cache_v[l], mod, final_g, batch, seq, dec_batch, n, p)

    keep = min(WINDOW, seq)
    kv_shape = (batch, keep, N_KV_HEADS, HEAD_DIM)
    k_p = jnp.stack([k[(b + 1) * seq - keep:(b + 1) * seq] for b in range(batch)]).reshape(kv_shape)
    v_p = jnp.stack([v[(b + 1) * seq - keep:(b + 1) * seq] for b in range(batch)]).reshape(kv_shape)
    ts = dec_batch * n
    k_s = k[tp:tp + ts].reshape(dec_batch, n, N_KV_HEADS, HEAD_DIM)
    v_s = v[tp:tp + ts].reshape(dec_batch, n, N_KV_HEADS, HEAD_DIM)
    return (y_p.reshape(batch, seq, d), y_s.reshape(dec_batch, n, d),
            k_p[None], v_p[None], k_s[None], v_s[None], vn_s.reshape(dec_batch, n, SGU_WIDTH)[None])
```

```python
import functools

import numpy as np
import jax
import jax.numpy as jnp
from jax import lax
from jax.experimental import pallas as pl
from jax.experimental.pallas import tpu as pltpu

F32 = jnp.float32
BF16 = jnp.bfloat16
I32 = jnp.int32

D_MODEL = 1024
CHUNK = 64
HEAD_DIM = 64
ATTN_WIDTH = 512
N_HEADS = 8
N_KV_HEADS = 2
GQA_GROUP = 4
KV_WIDTH = 128
WINDOW = 128
SGU_CHUNK = 128
SGU_WIDTH = 512
SGU_GROUPS = 4
IN_COLS = ATTN_WIDTH + 2 * KV_WIDTH + 2 * SGU_WIDTH
N_MOD = 6
N_EXPERTS = 256
TOP_K = 8
N_EXPERT_GROUPS = 8
GROUP_SIZE = N_EXPERTS // N_EXPERT_GROUPS
TOPK_GROUPS = 4
EXPERT_DIM = 256
ROUTED_SCALE = 2.5
EPS = 1e-6

LANES = 128
ROW_WORDS = D_MODEL // 2
ROW_TILES = ROW_WORDS // LANES
ROW_DTYPE = jnp.uint32
ROW_GROUP = 16
TOKEN_TILE = 256
DENSE_TILE = 1024
COND_REPEAT = DENSE_TILE // ROW_GROUP
ATTN_TILE = 512
EXPERT_BLOCK = 640
BLOCK_SIZES = (128, 256, 384, 512, 576, EXPERT_BLOCK)
X_SLOTS = 4
W_SLOTS = 4
POS_TILE_MAX = 4096
COMBINE_TILE = 256
COMBINE_GROUP = 8
VMEM_LIMIT_BYTES = 56 * 1024 * 1024


def _params(n_axes=1):
    return pltpu.CompilerParams(dimension_semantics=("arbitrary",) * n_axes,
                                vmem_limit_bytes=VMEM_LIMIT_BYTES)


def _rms(x):
    return x * lax.rsqrt(jnp.mean(x * x, axis=-1, keepdims=True) + EPS)


def _cond_rows(m_ref):
    m = m_ref[...]
    g, _, d = m.shape
    return jnp.broadcast_to(m, (g, ROW_GROUP, d)).reshape(g * ROW_GROUP, d)


def _cond_spec(piece, tile_rows, tiles_per_batch, batch, n_prompt_tiles):
    g = tile_rows // ROW_GROUP
    per = COND_REPEAT // g
    return pl.BlockSpec(
        (g, 1, D_MODEL),
        lambda i: (jnp.where(i < n_prompt_tiles, (i // tiles_per_batch) * per, batch * per + i - n_prompt_tiles),
                   0, piece))


def _pick_tile(is_sample, prompt_ref, sample_ref):
    v = jnp.where(is_sample, sample_ref[...], prompt_ref[...])
    return v.reshape(v.shape[0] * v.shape[1], v.shape[-1])


def _pack_rows(x):
    x = x.astype(F32)
    return pltpu.pack_elementwise([x[:, :ROW_WORDS], x[:, ROW_WORDS:]], packed_dtype=BF16)


def _unpack_rows(w):
    return jnp.concatenate([pltpu.unpack_elementwise(w, index=i, packed_dtype=BF16, unpacked_dtype=F32)
                            for i in range(2)], axis=1)


def _load_rows(ref, n_rows):
    w = jnp.concatenate([ref[pl.ds(j, n_rows, stride=ROW_TILES), :] for j in range(ROW_TILES)], axis=1)
    return _unpack_rows(w)


def _store_rows(ref, x):
    n_rows = x.shape[0]
    w = _pack_rows(x)
    for j in range(ROW_TILES):
        ref[pl.ds(j, n_rows, stride=ROW_TILES), :] = w[:, j * LANES:(j + 1) * LANES]


def _tile_offset(row):
    off = row * ROW_TILES
    return off if isinstance(off, int) else pl.multiple_of(off, ROW_TILES)


def _rows_copy(src_ref, src_row, dst_ref, dst_row, n_rows, sem):
    s = _tile_offset(src_row)
    d = _tile_offset(dst_row)
    return pltpu.make_async_copy(src_ref.at[pl.ds(s, n_rows * ROW_TILES)],
                                 dst_ref.at[pl.ds(d, n_rows * ROW_TILES)], sem)


def _adaln_kernel(c_ref, w_ref, b_ref, o_ref):
    c = c_ref[...]
    s = c * jax.nn.sigmoid(c)
    w = w_ref[...]
    sh = s.astype(BF16)
    sl = (s - sh.astype(F32)).astype(BF16)
    wh = w.astype(BF16)
    wl = (w - wh.astype(F32)).astype(BF16)
    o_ref[...] = (jnp.dot(sh, wh, preferred_element_type=F32)
                  + (jnp.dot(sl, wh, preferred_element_type=F32)
                     + jnp.dot(sh, wl, preferred_element_type=F32))) + b_ref[...]


def _adaln(c, w_ada, b_ada):
    n, d = c.shape
    cols = w_ada.shape[1]
    tn = 1536
    return pl.pallas_call(
        _adaln_kernel,
        grid=(cols // tn,),
        in_specs=[pl.BlockSpec((n, d), lambda j: (0, 0)),
                  pl.BlockSpec((d, tn), lambda j: (0, j)),
                  pl.BlockSpec((1, tn), lambda j: (0, j))],
        out_specs=pl.BlockSpec((n, tn), lambda j: (0, j)),
        out_shape=jax.ShapeDtypeStruct((n, cols), F32),
        compiler_params=_params(1),
        name="adaln",
    )(c, w_ada, b_ada.reshape(1, cols))


def _stage1_kernel(xp_ref, xs_ref, sh_ref, sc_ref, g1_ref, win_ref, lng_ref, lnb_ref,
                   wm_ref, wms_ref, bs_ref, bss_ref, gsgu_ref,
                   q_ref, k_ref, v_ref, sgu_ref, vn_ref, *, n_prompt_tiles):
    is_sample = pl.program_id(0) >= n_prompt_tiles
    x = _pick_tile(is_sample, xp_ref, xs_ref)
    h = _rms(x) * g1_ref[...] * (1.0 + _cond_rows(sc_ref)) + _cond_rows(sh_ref)
    proj = jnp.dot(h.astype(BF16), win_ref[...], preferred_element_type=F32)
    q_ref[...] = (proj[:, :ATTN_WIDTH] * (HEAD_DIM ** -0.5)).astype(BF16)
    k_ref[...] = proj[:, ATTN_WIDTH:ATTN_WIDTH + KV_WIDTH]
    v_ref[...] = proj[:, ATTN_WIDTH + KV_WIDTH:ATTN_WIDTH + 2 * KV_WIDTH]
    c0 = ATTN_WIDTH + 2 * KV_WIDTH
    u = jax.nn.gelu(proj[:, c0:c0 + SGU_WIDTH])
    vg = jax.nn.gelu(proj[:, c0 + SGU_WIDTH:])
    mu = jnp.mean(vg, axis=-1, keepdims=True)
    xc = vg - mu
    var = jnp.mean(xc * xc, axis=-1, keepdims=True)
    vn = xc * lax.rsqrt(var + EPS) * lng_ref[...] + lnb_ref[...]

    @pl.when(is_sample)
    def _():
        vn_ref[...] = vn

    vnb = vn.astype(BF16)
    wm = jnp.where(is_sample, wms_ref[...], wm_ref[...])
    bs = jnp.where(is_sample, bss_ref[...], bs_ref[...])
    rows = []
    for c in range(DENSE_TILE // SGU_CHUNK):
        r0 = c * SGU_CHUNK
        cols = []
        for g in range(SGU_GROUPS):
            l0 = g * SGU_CHUNK
            sv = jnp.dot(wm[g], vnb[r0:r0 + SGU_CHUNK, l0:l0 + SGU_CHUNK],
                         preferred_element_type=F32) + bs[g]
            cols.append(sv)
        rows.append(jnp.concatenate(cols, axis=1))
    sgu = u * jnp.concatenate(rows, axis=0)
    sgu_ref[...] = (_rms(sgu) * gsgu_ref[...]).astype(BF16)


def _stage1(xp3, xs3, cond_tbl, g1, win_b, lng, lnb, wm, wms, bs, bss, gsgu, batch, n_prompt_tiles):
    t = (xp3.shape[0] + xs3.shape[0]) * ROW_GROUP
    n_tiles = t // DENSE_TILE
    last_p = n_prompt_tiles - 1
    x_blk = (DENSE_TILE // ROW_GROUP, ROW_GROUP, D_MODEL)
    const2 = lambda i: (0, 0)
    const3 = lambda i: (0, 0, 0)
    tile2 = lambda i: (i, 0)
    cond = functools.partial(_cond_spec, tile_rows=DENSE_TILE, tiles_per_batch=n_prompt_tiles // batch,
                             batch=batch, n_prompt_tiles=n_prompt_tiles)
    sgu_w_spec = pl.BlockSpec((SGU_GROUPS, SGU_CHUNK, SGU_CHUNK), const3)
    return pl.pallas_call(
        functools.partial(_stage1_kernel, n_prompt_tiles=n_prompt_tiles),
        grid=(n_tiles,),
        in_specs=[pl.BlockSpec(x_blk, lambda i: (jnp.minimum(i, last_p), 0, 0)),
                  pl.BlockSpec(x_blk, const3)]
                 + [cond(0), cond(1)]
                 + [pl.BlockSpec((1, D_MODEL), const2),
                    pl.BlockSpec((D_MODEL, IN_COLS), const2),
                    pl.BlockSpec((1, SGU_WIDTH), const2),
                    pl.BlockSpec((1, SGU_WIDTH), const2),
                    sgu_w_spec, sgu_w_spec, sgu_w_spec, sgu_w_spec,
                    pl.BlockSpec((1, SGU_WIDTH), const2)],
        out_specs=[pl.BlockSpec((DENSE_TILE, ATTN_WIDTH), tile2),
                   pl.BlockSpec((DENSE_TILE, KV_WIDTH), tile2),
                   pl.BlockSpec((DENSE_TILE, KV_WIDTH), tile2),
                   pl.BlockSpec((DENSE_TILE, SGU_WIDTH), tile2),
                   pl.BlockSpec((DENSE_TILE, SGU_WIDTH), const2)],
        out_shape=[jax.ShapeDtypeStruct((t, ATTN_WIDTH), BF16),
                   jax.ShapeDtypeStruct((t, KV_WIDTH), F32),
                   jax.ShapeDtypeStruct((t, KV_WIDTH), F32),
                   jax.ShapeDtypeStruct((t, SGU_WIDTH), BF16),
                   jax.ShapeDtypeStruct((DENSE_TILE, SGU_WIDTH), F32)],
        compiler_params=_params(1),
        name="stage1",
    )(xp3, xs3, cond_tbl, cond_tbl, g1, win_b, lng, lnb, wm, wms, bs, bss, gsgu)


def _alibi_slope(h):
    return float(np.float32(2.0) ** np.float32(-8.0 * (h + 1.0) / N_HEADS))


def _sink_attention(q, k, v, back, valid_fn, sinks_ref, gout):
    nq = q.shape[0]
    nk = k.shape[0]
    q_t = q.astype(F32).T.astype(BF16)
    k = k.astype(BF16)
    v_t = v.T.astype(BF16)
    c = lax.broadcasted_iota(I32, (nk, nq), 0)
    r = lax.broadcasted_iota(I32, (nk, nq), 1)
    dist = jnp.abs(r - c + back).astype(F32)
    valid = None if valid_fn is None else valid_fn(c, r)
    outs = []
    for kvh in range(N_KV_HEADS):
        kv0 = kvh * HEAD_DIM
        heads = range(kvh * GQA_GROUP, (kvh + 1) * GQA_GROUP)
        qs = jnp.concatenate([q_t[h * HEAD_DIM:(h + 1) * HEAD_DIM, :] for h in heads], axis=1)
        l = jnp.dot(k[:, kv0:kv0 + HEAD_DIM], qs, preferred_element_type=F32)
        biased = [l[:, g * nq:(g + 1) * nq] - _alibi_slope(h) * dist for g, h in enumerate(heads)]
        if valid is not None:
            biased = [jnp.where(valid, b, -jnp.inf) for b in biased]
        l = jnp.concatenate(biased, axis=1)
        sink = jnp.concatenate([jnp.full((1, nq), sinks_ref[h], F32) for h in heads], axis=1)
        m = jnp.maximum(jnp.max(l, axis=0, keepdims=True), sink)
        e = jnp.exp(l - m)
        denom = jnp.sum(e, axis=0, keepdims=True) + jnp.exp(sink - m)
        pv = jnp.dot(v_t[kv0:kv0 + HEAD_DIM, :], e.astype(BF16), preferred_element_type=F32)
        o_t = pv / denom
        outs.extend(o_t[:, g * nq:(g + 1) * nq] for g in range(GQA_GROUP))
    o = jnp.concatenate(outs, axis=0).T
    return (_rms(o) * gout).astype(BF16)


def _attn_prompt_kernel(sinks_ref, q_ref, kp_ref, kc_ref, vp_ref, vc_ref, g_ref, o_ref):
    i = pl.program_id(1)
    back = kp_ref.shape[0]

    k = jnp.concatenate([kp_ref[...], kc_ref[...]], axis=0)
    v = jnp.concatenate([vp_ref[...], vc_ref[...]], axis=0)
    for s in range(q_ref.shape[0] // WINDOW):
        def valid_fn(c, r, s=s):
            kc = c // CHUNK
            qc = r // CHUNK + (back // CHUNK)
            in_band = jnp.logical_and(kc <= qc, kc >= qc - WINDOW // CHUNK)
            if s > 0:
                return in_band
            return jnp.logical_and(in_band, jnp.logical_or(i > 0, kc >= back // CHUNK))

        rows = slice(s * WINDOW, (s + 1) * WINDOW)
        keys = slice(s * WINDOW, s * WINDOW + back + WINDOW)
        o_ref[rows, :] = _sink_attention(q_ref[rows, :], k[keys], v[keys], back, valid_fn, sinks_ref, g_ref[...])


def _attn_prompt(q, k, v, sinks, gattn, batch, seq):
    nt = seq // ATTN_TILE
    per_tile = ATTN_TILE // WINDOW
    cur = lambda b, i, s: (b * nt + i, 0)
    prev = lambda b, i, s: (jnp.maximum((b * nt + i) * per_tile - 1, 0), 0)
    kv_blk = (ATTN_TILE, KV_WIDTH)
    back_blk = (WINDOW, KV_WIDTH)
    return pl.pallas_call(
        _attn_prompt_kernel,
        grid_spec=pltpu.PrefetchScalarGridSpec(
            num_scalar_prefetch=1,
            grid=(batch, nt),
            in_specs=[pl.BlockSpec((ATTN_TILE, ATTN_WIDTH), cur),
                      pl.BlockSpec(back_blk, prev), pl.BlockSpec(kv_blk, cur),
                      pl.BlockSpec(back_blk, prev), pl.BlockSpec(kv_blk, cur),
                      pl.BlockSpec((1, ATTN_WIDTH), lambda b, i, s: (0, 0))],
            out_specs=pl.BlockSpec((ATTN_TILE, ATTN_WIDTH), cur)),
        out_shape=jax.ShapeDtypeStruct((batch * seq, ATTN_WIDTH), BF16),
        compiler_params=_params(2),
        name="attn_prompt",
    )(sinks, q, k, k, v, v, gattn)


def _attn_sample_kernel(sinks_ref, q_ref, ck_ref, cv_ref, k_ref, v_ref, g_ref, o_ref):
    k = jnp.concatenate([ck_ref[0], k_ref[...]], axis=0)
    v = jnp.concatenate([cv_ref[0], v_ref[...]], axis=0)
    o_ref[...] = _sink_attention(q_ref[...], k, v, ck_ref.shape[1], None, sinks_ref, g_ref[...])


def _attn_sample(q, k, v, cache_k, cache_v, sinks, gattn, dec_batch, n, row0):
    w = cache_k.shape[1]
    blk0 = row0 // n
    new = lambda b, s: (blk0 + b, 0)
    return pl.pallas_call(
        _attn_sample_kernel,
        grid_spec=pltpu.PrefetchScalarGridSpec(
            num_scalar_prefetch=1,
            grid=(dec_batch,),
            in_specs=[pl.BlockSpec((n, ATTN_WIDTH), new),
                      pl.BlockSpec((1, w, KV_WIDTH), lambda b, s: (b, 0, 0)),
                      pl.BlockSpec((1, w, KV_WIDTH), lambda b, s: (b, 0, 0)),
                      pl.BlockSpec((n, KV_WIDTH), new),
                      pl.BlockSpec((n, KV_WIDTH), new),
                      pl.BlockSpec((1, ATTN_WIDTH), lambda b, s: (0, 0))],
            out_specs=pl.BlockSpec((n, ATTN_WIDTH), lambda b, s: (b, 0))),
        out_shape=jax.ShapeDtypeStruct((dec_batch * n, ATTN_WIDTH), BF16),
        compiler_params=_params(1),
        name="attn_sample",
    )(sinks, q, cache_k, cache_v, k, v, gattn)


def _merge_kernel(ap_ref, as_ref, sgu_ref, xp_ref, xs_ref,
                  gate1_ref, sh2_ref, sc2_ref, gate2_ref,
                  woa_ref, wos_ref, g2_ref, wrh_ref, wrl_ref, wsgu_ref, wsd_ref,
                  xb_ref, h2_ref, sc_ref, *, n_prompt_tiles):
    is_sample = pl.program_id(0) >= n_prompt_tiles
    x = _pick_tile(is_sample, xp_ref, xs_ref)
    a = jnp.where(is_sample, as_ref[...], ap_ref[...])
    mix = (jnp.dot(a, woa_ref[...], preferred_element_type=F32)
           + jnp.dot(sgu_ref[...], wos_ref[...], preferred_element_type=F32))
    x1 = x + _cond_rows(gate1_ref) * mix
    h2 = _rms(x1) * g2_ref[...] * (1.0 + _cond_rows(sc2_ref)) + _cond_rows(sh2_ref)
    hh = h2.astype(BF16)
    _store_rows(h2_ref, hh)
    hl = (h2 - hh.astype(F32)).astype(BF16)
    logits = (jnp.dot(hh, wrh_ref[...], preferred_element_type=F32)
              + (jnp.dot(hl, wrh_ref[...], preferred_element_type=F32)
                 + jnp.dot(hh, wrl_ref[...], preferred_element_type=F32)))
    sc_ref[...] = jax.nn.sigmoid(logits)
    gu = jnp.dot(hh, wsgu_ref[...], preferred_element_type=F32)
    g = gu[:, :EXPERT_DIM]
    act = (g * jax.nn.sigmoid(g)) * gu[:, EXPERT_DIM:]
    shared = jnp.dot(act.astype(BF16), wsd_ref[...], preferred_element_type=F32)
    xb_ref[...] = x1 + _cond_rows(gate2_ref) * shared


def _merge(attn_p, attn_s, sgu_n, xp3, xs3, cond_tbl, woa, wos, g2, wrh, wrl, wsgu, wsd, batch, n_prompt_tiles):
    t = sgu_n.shape[0]
    n_tiles = t // DENSE_TILE
    last_p = n_prompt_tiles - 1
    x_blk = (DENSE_TILE // ROW_GROUP, ROW_GROUP, D_MODEL)
    const2 = lambda i: (0, 0)
    const3 = lambda i: (0, 0, 0)
    tile2 = lambda i: (i, 0)
    cond = functools.partial(_cond_spec, tile_rows=DENSE_TILE, tiles_per_batch=n_prompt_tiles // batch,
                             batch=batch, n_prompt_tiles=n_prompt_tiles)
    return pl.pallas_call(
        functools.partial(_merge_kernel, n_prompt_tiles=n_prompt_tiles),
        grid=(n_tiles,),
        in_specs=[pl.BlockSpec((DENSE_TILE, ATTN_WIDTH), lambda i: (jnp.minimum(i, last_p), 0)),
                  pl.BlockSpec((DENSE_TILE, ATTN_WIDTH), const2),
                  pl.BlockSpec((DENSE_TILE, SGU_WIDTH), tile2),
                  pl.BlockSpec(x_blk, lambda i: (jnp.minimum(i, last_p), 0, 0)),
                  pl.BlockSpec(x_blk, const3)]
                 + [cond(2), cond(3), cond(4), cond(5)]
                 + [pl.BlockSpec((ATTN_WIDTH, D_MODEL), const2),
                    pl.BlockSpec((SGU_WIDTH, D_MODEL), const2),
                    pl.BlockSpec((1, D_MODEL), const2),
                    pl.BlockSpec((D_MODEL, N_EXPERTS), const2),
                    pl.BlockSpec((D_MODEL, N_EXPERTS), const2),
                    pl.BlockSpec((D_MODEL, 2 * EXPERT_DIM), const2),
                    pl.BlockSpec((EXPERT_DIM, D_MODEL), const2)],
        out_specs=[pl.BlockSpec((DENSE_TILE, D_MODEL), tile2),
                   pl.BlockSpec((DENSE_TILE * ROW_TILES, LANES), tile2),
                   pl.BlockSpec((DENSE_TILE, N_EXPERTS), tile2)],
        out_shape=[jax.ShapeDtypeStruct((t, D_MODEL), F32),
                   jax.ShapeDtypeStruct((t * ROW_TILES, LANES), ROW_DTYPE),
                   jax.ShapeDtypeStruct((t, N_EXPERTS), F32)],
        compiler_params=_params(1),
        name="merge",
    )(attn_p, attn_s, sgu_n, xp3, xs3, *([cond_tbl] * 4), woa, wos, g2, wrh, wrl, wsgu, wsd)


def _first_argmax(x, iota, size, axis):
    m = jnp.max(x, axis=axis, keepdims=True)
    idx = jnp.min(jnp.where(x == m, iota, size), axis=axis, keepdims=True)
    return m, idx


def _route_kernel(sc_ref, bias_ref, tri_ref, eidx_ref, wts_ref, rank_ref, cnt_ref, base_ref):
    tt = sc_ref.shape[0]

    @pl.when(pl.program_id(0) == 0)
    def _():
        base_ref[...] = jnp.zeros_like(base_ref)

    s_t = sc_ref[...].T
    sel = s_t + bias_ref[...]
    sel3 = sel.reshape(N_EXPERT_GROUPS, GROUP_SIZE, tt)
    io3 = lax.broadcasted_iota(I32, sel3.shape, 1)
    m1, i1 = _first_argmax(sel3, io3, GROUP_SIZE, 1)
    m2 = jnp.max(jnp.where(io3 == i1, -jnp.inf, sel3), axis=1, keepdims=True)
    gs = (m1 + m2).reshape(N_EXPERT_GROUPS, tt)
    io8 = lax.broadcasted_iota(I32, gs.shape, 0)
    gmask = jnp.zeros(gs.shape, jnp.bool_)
    for _ in range(TOPK_GROUPS):
        _, gi = _first_argmax(gs, io8, N_EXPERT_GROUPS, 0)
        hit = io8 == gi
        gmask = jnp.logical_or(gmask, hit)
        gs = jnp.where(hit, -jnp.inf, gs)
    emask = jnp.broadcast_to(gmask.reshape(N_EXPERT_GROUPS, 1, tt), sel3.shape).reshape(N_EXPERTS, tt)
    cand = jnp.where(emask, sel, -jnp.inf)
    io = lax.broadcasted_iota(I32, cand.shape, 0)
    chosen = jnp.zeros(cand.shape, jnp.bool_)
    eidx, wts = [], []
    for _ in range(TOP_K):
        _, ei = _first_argmax(cand, io, N_EXPERTS, 0)
        hit = io == ei
        eidx.append(ei)
        wts.append(jnp.sum(jnp.where(hit, s_t, 0.0), axis=0, keepdims=True))
        chosen = jnp.logical_or(chosen, hit)
        cand = jnp.where(hit, -jnp.inf, cand)
    w = jnp.concatenate(wts, axis=0)
    wts_ref[...] = w / jnp.sum(w, axis=0, keepdims=True) * ROUTED_SCALE
    eidx_ref[...] = jnp.concatenate(eidx, axis=0)
    cf = jnp.where(chosen, 1.0, 0.0)
    ahead = jnp.dot(cf.astype(BF16), tri_ref[...], preferred_element_type=F32) + base_ref[...]
    ranks = [jnp.sum(jnp.where(io == ei, ahead, 0.0), axis=0, keepdims=True) for ei in eidx]
    rank_ref[...] = jnp.concatenate(ranks, axis=0).astype(I32)
    total = base_ref[...] + jnp.sum(cf, axis=1, keepdims=True)
    base_ref[...] = total
    cnt_ref[...] = total.astype(I32)


def _route(scores, bias_col, tri, t):
    n_tiles = t // TOKEN_TILE
    col = lambda i: (0, i)
    return pl.pallas_call(
        _route_kernel,
        grid=(n_tiles,),
        in_specs=[pl.BlockSpec((TOKEN_TILE, N_EXPERTS), lambda i: (i, 0)),
                  pl.BlockSpec((N_EXPERTS, 1), lambda i: (0, 0)),
                  pl.BlockSpec((TOKEN_TILE, TOKEN_TILE), lambda i: (0, 0))],
        out_specs=[pl.BlockSpec((TOP_K, TOKEN_TILE), col),
                   pl.BlockSpec((TOP_K, TOKEN_TILE), col),
                   pl.BlockSpec((TOP_K, TOKEN_TILE), col),
                   pl.BlockSpec((N_EXPERTS, 1), lambda i: (0, 0))],
        out_shape=[jax.ShapeDtypeStruct((TOP_K, t), I32),
                   jax.ShapeDtypeStruct((TOP_K, t), F32),
                   jax.ShapeDtypeStruct((TOP_K, t), I32),
                   jax.ShapeDtypeStruct((N_EXPERTS, 1), I32)],
        scratch_shapes=[pltpu.VMEM((N_EXPERTS, 1), F32)],
        compiler_params=_params(1),
        name="route",
    )(scores, bias_col, tri)


def _pos_kernel(eidx_ref, rank_ref, start_ref, pos_ref):
    e = eidx_ref[...]
    tt = e.shape[1]
    io = lax.broadcasted_iota(I32, (N_EXPERTS, tt), 0)
    st = start_ref[...]
    rows = [jnp.sum(jnp.where(io == e[k:k + 1, :], st, 0), axis=0, keepdims=True) for k in range(TOP_K)]
    pos_ref[...] = jnp.concatenate(rows, axis=0) + rank_ref[...]


def _positions(eidx, rank, start_col):
    t = eidx.shape[1]
    lane_tiles = t // LANES
    width = max(d for d in range(1, POS_TILE_MAX // LANES + 1) if lane_tiles % d == 0) * LANES
    col = lambda i: (0, i)
    blk = pl.BlockSpec((TOP_K, width), col)
    return pl.pallas_call(
        _pos_kernel,
        grid=(t // width,),
        in_specs=[blk, blk, pl.BlockSpec((N_EXPERTS, 1), lambda i: (0, 0))],
        out_specs=blk,
        out_shape=jax.ShapeDtypeStruct((TOP_K, t), I32),
        compiler_params=_params(1),
        name="positions",
    )(eidx, rank, start_col)


def _dispatch_kernel(pos_ref, h_ref, xs_ref, zero_ref, sem, zsem, *, n_rows):
    tt = h_ref.shape[0] // ROW_TILES
    first = pl.program_id(0) == 0

    @pl.when(first)
    def _():
        zero_ref[...] = jnp.zeros_like(zero_ref)
        _rows_copy(zero_ref, 0, xs_ref, n_rows, EXPERT_BLOCK, zsem).start()

    for t in range(tt):
        for k in range(TOP_K):
            _rows_copy(h_ref, t, xs_ref, pos_ref[k, t], 1, sem).start(priority=k % 2)
    for k in range(TOP_K):
        _rows_copy(h_ref, 0, xs_ref, 0, tt, sem).wait()

    @pl.when(first)
    def _():
        _rows_copy(zero_ref, 0, xs_ref, n_rows, EXPERT_BLOCK, zsem).wait()


def _dispatch(pos, h2_tiles, t, n_rows):
    return pl.pallas_call(
        functools.partial(_dispatch_kernel, n_rows=n_rows),
        grid=(t // TOKEN_TILE,),
        in_specs=[pl.BlockSpec((TOP_K, TOKEN_TILE), lambda i: (0, i), memory_space=pltpu.SMEM),
                  pl.BlockSpec((TOKEN_TILE * ROW_TILES, LANES), lambda i: (i, 0))],
        out_specs=pl.BlockSpec(memory_space=pl.ANY),
        out_shape=jax.ShapeDtypeStruct(((n_rows + EXPERT_BLOCK) * ROW_TILES, LANES), ROW_DTYPE),
        scratch_shapes=[pltpu.VMEM((EXPERT_BLOCK * ROW_TILES, LANES), ROW_DTYPE),
                        pltpu.SemaphoreType.DMA, pltpu.SemaphoreType.DMA],
        compiler_params=_params(1),
        name="dispatch",
    )(pos, h2_tiles)


def _out_pieces(nvalid, fn):
    s = 1 << (EXPERT_BLOCK.bit_length() - 1)
    while s >= 1:
        @pl.when((nvalid & s) != 0)
        def _(s=s):
            fn(nvalid & ~(2 * s - 1), s)
        s //= 2


def _for_block_size(nvalid, fn):
    lo = 0
    for rows in BLOCK_SIZES:
        @pl.when(jnp.logical_and(nvalid > lo, nvalid <= rows))
        def _(rows=rows):
            fn(rows)
        lo = rows


def _expert_kernel(count_ref, blk0_ref, row_ref, cnt_ref, xs_ref, wg_ref, wu_ref, wd_ref, ys_ref,
                   xbuf, ybuf, wg_buf, wu_buf, wd_buf, wgu_s, wd_s, state, xsem, ysem, wsem):
    e = pl.program_id(0)
    n_experts = pl.num_programs(0)
    cnt = count_ref[e]
    g0 = blk0_ref[e]
    nblk = (cnt + EXPERT_BLOCK - 1) // EXPERT_BLOCK
    wslot = lax.rem(e, W_SLOTS)

    def w_copies(expert, slot):
        return [pltpu.make_async_copy(src.at[expert], dst.at[slot], wsem.at[slot])
                for src, dst in ((wg_ref, wg_buf), (wu_ref, wu_buf), (wd_ref, wd_buf))]

    def w_fetch(expert):
        @pl.when(expert < n_experts)
        def _():
            for c in w_copies(expert, lax.rem(expert, W_SLOTS)):
                c.start()

    def x_copy(row, slot, rows):
        return _rows_copy(xs_ref, row, xbuf.at[slot], 0, rows, xsem.at[slot])

    def x_fetch(g, slot):
        _for_block_size(cnt_ref[g], lambda rows: x_copy(row_ref[g], slot, rows).start())

    def y_wait(nvalid, slot):
        def wait(off, size):
            _rows_copy(ybuf.at[slot], 0, ys_ref, 0, size, ysem.at[slot]).wait()
        _out_pieces(nvalid, wait)

    @pl.when(e == 0)
    def _():
        state[0] = 0
        state[1] = 0
        for g in range(X_SLOTS - 1):
            x_fetch(g, g)
        for ahead in range(W_SLOTS - 1):
            w_fetch(ahead)

    w_fetch(e + W_SLOTS - 1)
    for c in w_copies(e, wslot):
        c.wait()
    wgu_s[:, :EXPERT_DIM] = wg_buf[wslot].astype(BF16)
    wgu_s[:, EXPERT_DIM:] = wu_buf[wslot].astype(BF16)
    wd_s[...] = wd_buf[wslot].astype(BF16)

    def compute(rows, xslot, yslot, row0, nvalid):
        x_copy(row0, xslot, rows).wait()
        x = _load_rows(xbuf.at[xslot], rows).astype(BF16)
        gu = jnp.dot(x, wgu_s[...], preferred_element_type=F32)
        g = gu[:, :EXPERT_DIM]
        act = (g * jax.nn.sigmoid(g)) * gu[:, EXPERT_DIM:]
        y = jnp.dot(act.astype(BF16), wd_s[...], preferred_element_type=F32)
        y_wait(state[0], 1 - yslot)
        _store_rows(ybuf.at[yslot], y)

        def put(off, size):
            _rows_copy(ybuf.at[yslot], off, ys_ref, row0 + off, size, ysem.at[yslot]).start()
        _out_pieces(nvalid, put)

    def block(j, carry):
        g = g0 + j
        xslot = state[1]
        yslot = g & 1
        row0 = row_ref[g]
        nvalid = cnt_ref[g]
        x_fetch(g + X_SLOTS - 1, jnp.where(xslot == 0, X_SLOTS - 1, xslot - 1))
        _for_block_size(nvalid, lambda rows: compute(rows, xslot, yslot, row0, nvalid))
        state[0] = nvalid
        state[1] = jnp.where(xslot == X_SLOTS - 1, 0, xslot + 1)
        return carry

    lax.fori_loop(0, nblk, block, 0)

    @pl.when(e == pl.num_programs(0) - 1)
    def _():
        y_wait(state[0], (g0 + nblk - 1) & 1)


def _experts(count, blk0, blk_row, blk_cnt, xs, w_gate, w_up, w_down, n_rows):
    blk_rows = EXPERT_BLOCK * ROW_TILES
    any_spec = pl.BlockSpec(memory_space=pl.ANY)
    return pl.pallas_call(
        _expert_kernel,
        grid_spec=pltpu.PrefetchScalarGridSpec(
            num_scalar_prefetch=4,
            grid=(N_EXPERTS,),
            in_specs=[any_spec, any_spec, any_spec, any_spec],
            out_specs=any_spec,
            scratch_shapes=[pltpu.VMEM((X_SLOTS, blk_rows, LANES), ROW_DTYPE),
                            pltpu.VMEM((2, blk_rows, LANES), ROW_DTYPE),
                            pltpu.VMEM((W_SLOTS, D_MODEL, EXPERT_DIM), F32),
                            pltpu.VMEM((W_SLOTS, D_MODEL, EXPERT_DIM), F32),
                            pltpu.VMEM((W_SLOTS, EXPERT_DIM, D_MODEL), F32),
                            pltpu.VMEM((D_MODEL, 2 * EXPERT_DIM), BF16),
                            pltpu.VMEM((EXPERT_DIM, D_MODEL), BF16),
                            pltpu.SMEM((2,), I32),
                            pltpu.SemaphoreType.DMA((X_SLOTS,)),
                            pltpu.SemaphoreType.DMA((2,)),
                            pltpu.SemaphoreType.DMA((W_SLOTS,))]),
        out_shape=jax.ShapeDtypeStruct((n_rows * ROW_TILES, LANES), ROW_DTYPE),
        compiler_params=_params(1),
        name="experts",
    )(count, blk0, blk_row, blk_cnt, xs, w_gate, w_up, w_down)


def _combine_kernel(pos_ref, posn_ref, ys_ref, xb_ref, wts_ref, gate2_ref, gf_ref, yp_ref, ysm_ref,
                    buf, ytile, sem, *, n_prompt_tiles, n_tiles):
    i = pl.program_id(0)
    tt = xb_ref.shape[0]
    slot = i & 1

    def fetch(p_ref, s, t):
        for k in range(TOP_K):
            _rows_copy(ys_ref, p_ref[k, t], buf.at[s, k], t, 1, sem.at[s]).start(priority=k % 2)

    @pl.when(i == 0)
    def _():
        for t in range(tt):
            fetch(pos_ref, 0, t)

    for k in range(TOP_K):
        _rows_copy(ys_ref, 0, buf.at[slot, k], 0, tt, sem.at[slot]).wait()

    def finish(s, g):
        r0 = g * COMBINE_GROUP
        w = wts_ref[r0:r0 + COMBINE_GROUP, :]
        routed = None
        for k in range(TOP_K):
            rows = _load_rows(buf.at[s, k, pl.ds(r0 * ROW_TILES, COMBINE_GROUP * ROW_TILES)], COMBINE_GROUP)
            term = w[:, k:k + 1] * rows
            routed = term if routed is None else routed + term
        gate = gate2_ref[g * COMBINE_GROUP // ROW_GROUP]
        x = xb_ref[r0:r0 + COMBINE_GROUP, :] + gate * routed
        return _rms(x) * gf_ref[...]

    def tile_work(s, fetch_next):
        for g in range(tt // COMBINE_GROUP):
            y = finish(s, g)
            if fetch_next:
                for r in range(COMBINE_GROUP):
                    fetch(posn_ref, 1 - s, g * COMBINE_GROUP + r)
            ytile[g * COMBINE_GROUP:(g + 1) * COMBINE_GROUP, :] = y

    for s in range(2):
        @pl.when(jnp.logical_and(slot == s, i + 1 < n_tiles))
        def _(s=s):
            tile_work(s, True)

    @pl.when(i + 1 == n_tiles)
    def _():
        tile_work((n_tiles - 1) % 2, False)

    @pl.when(i < n_prompt_tiles)
    def _():
        yp_ref[...] = ytile[...]

    @pl.when(i >= n_prompt_tiles)
    def _():
        ysm_ref[...] = ytile[...]


def _combine(pos, ys, xb, wts_t, cond_tbl, gf, batch, n_prompt_rows, t):
    tt = COMBINE_TILE
    n_tiles = t // tt
    n_p = n_prompt_rows // tt
    pos_spec = lambda f: pl.BlockSpec((TOP_K, tt), f, memory_space=pltpu.SMEM)
    return pl.pallas_call(
        functools.partial(_combine_kernel, n_prompt_tiles=n_p, n_tiles=n_tiles),
        grid=(n_tiles,),
        in_specs=[pos_spec(lambda i: (0, i)),
                  pos_spec(lambda i: (0, jnp.minimum(i + 1, n_tiles - 1))),
                  pl.BlockSpec(memory_space=pl.ANY),
                  pl.BlockSpec((tt, D_MODEL), lambda i: (i, 0)),
                  pl.BlockSpec((tt, TOP_K), lambda i: (i, 0))]
                 + [_cond_spec(5, tt, n_p // batch, batch, n_p)]
                 + [pl.BlockSpec((1, D_MODEL), lambda i: (0, 0))],
        out_specs=[pl.BlockSpec((tt, D_MODEL), lambda i: (jnp.minimum(i, n_p - 1), 0)),
                   pl.BlockSpec((tt, D_MODEL), lambda i: (jnp.maximum(i - n_p, 0), 0))],
        out_shape=[jax.ShapeDtypeStruct((n_prompt_rows, D_MODEL), F32),
                   jax.ShapeDtypeStruct((t - n_prompt_rows, D_MODEL), F32)],
        scratch_shapes=[pltpu.VMEM((2, TOP_K, tt * ROW_TILES, LANES), ROW_DTYPE),
                        pltpu.VMEM((tt, D_MODEL), F32),
                        pltpu.SemaphoreType.DMA((2,))],
        compiler_params=_params(1),
        name="combine",
    )(pos, pos, ys, xb, wts_t, cond_tbl, gf)


def _sgu_mask():
    pos = np.arange(SGU_CHUNK) // CHUNK
    return jnp.asarray(pos[None, :] <= pos[:, None])


def _layer(xp3, xs3, cache_k, cache_v, mod, final_g, batch, seq, dec_batch, n, p):
    tp = batch * seq
    t = tp + dec_batch * n
    n_prompt_tiles = tp // DENSE_TILE
    pad_groups = (DENSE_TILE - dec_batch * n) // ROW_GROUP
    xs3 = jnp.pad(xs3, ((0, pad_groups), (0, 0), (0, 0)))
    cond_tbl = jnp.concatenate([jnp.repeat(mod[:batch], COND_REPEAT, axis=0), mod[batch:]], axis=0)
    cond_tbl = jnp.pad(cond_tbl, ((0, pad_groups), (0, 0)))
    cond_tbl = cond_tbl.reshape(batch * COND_REPEAT + dec_batch + pad_groups, 1, N_MOD * D_MODEL)

    wm = jnp.where(_sgu_mask(), p["sgu_w"], 0.0)
    reps = SGU_CHUNK // n
    eye = jnp.eye(reps, dtype=F32)
    wms = jnp.stack([jnp.kron(eye, wm[g, :n, :n]) for g in range(SGU_GROUPS)])
    bs = jnp.broadcast_to(p["sgu_b"][:, :, None], (SGU_GROUPS, SGU_CHUNK, SGU_CHUNK))
    bss = jnp.broadcast_to(jnp.tile(p["sgu_b"][:, :n], (1, reps))[:, :, None], (SGU_GROUPS, SGU_CHUNK, SGU_CHUNK))

    q, k, v, sgu_n, vn_s = _stage1(
        xp3, xs3, cond_tbl, p["norm1_g"].reshape(1, -1), p["w_in"].astype(BF16),
        p["sgu_ln_g"].reshape(1, -1), p["sgu_ln_b"].reshape(1, -1),
        wm.astype(BF16), wms.astype(BF16), bs, bss, p["sgu_out_g"].reshape(1, -1), batch, n_prompt_tiles)

    gattn = p["attn_out_g"].reshape(1, -1)
    attn_p = _attn_prompt(q, k, v, p["attn_sinks"], gattn, batch, seq)
    w = cache_k.shape[1]
    attn_s = _attn_sample(q, k, v, cache_k.reshape(dec_batch, w, KV_WIDTH), cache_v.reshape(dec_batch, w, KV_WIDTH),
                          p["attn_sinks"], gattn, dec_batch, n, tp)

    wo = p["w_out"].astype(BF16)
    wsgu = jnp.concatenate([p["ws_gate"], p["ws_up"]], axis=1).astype(BF16)
    wrh = p["w_router"].astype(BF16)
    wrl = (p["w_router"] - wrh.astype(F32)).astype(BF16)
    attn_s = jnp.pad(attn_s, ((0, DENSE_TILE - dec_batch * n), (0, 0)))
    xb, h2_tiles, scores = _merge(attn_p, attn_s, sgu_n, xp3, xs3, cond_tbl,
                                  wo[:ATTN_WIDTH], wo[ATTN_WIDTH:], p["norm2_g"].reshape(1, -1), wrh, wrl,
                                  wsgu, p["ws_down"].astype(BF16), batch, n_prompt_tiles)

    ti = np.arange(TOKEN_TILE)
    tri = jnp.asarray(ti[:, None] < ti[None, :], dtype=BF16)
    eidx, wts, rank, counts = _route(scores, p["router_bias"].reshape(N_EXPERTS, 1), tri, t)

    counts = counts[:, 0]
    ends = jnp.cumsum(counts)
    start = (ends - counts).astype(I32)
    nblk = (counts + EXPERT_BLOCK - 1) // EXPERT_BLOCK
    blk_end = jnp.cumsum(nblk)
    blk0 = (blk_end - nblk).astype(I32)
    pos = _positions(eidx, rank, start.reshape(N_EXPERTS, 1))

    n_rows = t * TOP_K
    g = jnp.arange(n_rows // EXPERT_BLOCK + N_EXPERTS + X_SLOTS, dtype=I32)[:, None]
    mine = jnp.logical_and(g >= blk0[None, :], g < blk_end[None, :])
    off = (g - blk0[None, :]) * EXPERT_BLOCK
    blk_row = jnp.sum(jnp.where(mine, start[None, :] + off, 0), axis=1).astype(I32)
    blk_cnt = jnp.sum(jnp.where(mine, jnp.minimum(counts[None, :] - off, EXPERT_BLOCK), 0), axis=1).astype(I32)

    xs_sorted = _dispatch(pos, h2_tiles, t, n_rows)
    ys_sorted = _experts(counts.astype(I32), blk0, blk_row, blk_cnt, xs_sorted,
                         p["w_gate"], p["w_up"], p["w_down"], n_rows)
    y_p, y_s = _combine(pos, ys_sorted, xb, wts.T, cond_tbl, final_g.reshape(1, -1), batch, tp, t)
    return y_p, y_s, k, v, vn_s[:dec_batch * n]


def kernel(x_prompt, x_sample, cache_k, cache_v, c_prompt, c_sample, norm1_g, w_ada, b_ada, w_in, sgu_ln_g, sgu_ln_b, sgu_w, sgu_b, attn_sinks, attn_out_g, sgu_out_g, w_out, norm2_g, w_router, router_bias, w_gate, w_up, w_down, ws_gate, ws_up, ws_down, final_g):
    batch, seq, d = x_prompt.shape
    dec_batch, n, _ = x_sample.shape
    depth = norm1_g.shape[0]
    assert depth == 1 and d == D_MODEL
    assert seq % DENSE_TILE == 0 and dec_batch * n == TOKEN_TILE and n == ROW_GROUP
    assert seq % ATTN_TILE == 0 and seq >= WINDOW and cache_k.shape[2] == WINDOW
    tp = batch * seq

    xp3 = x_prompt.reshape(tp // ROW_GROUP, ROW_GROUP, d)
    xs3 = x_sample.reshape(dec_batch * n // ROW_GROUP, ROW_GROUP, d)
    c_all = jnp.concatenate([c_prompt, c_sample], axis=0)
    l = 0
    mod = _adaln(c_all, w_ada[l], b_ada[l])
    p = dict(norm1_g=norm1_g[l], w_in=w_in[l], sgu_ln_g=sgu_ln_g[l], sgu_ln_b=sgu_ln_b[l], sgu_w=sgu_w[l],
             sgu_b=sgu_b[l], attn_sinks=attn_sinks[l], attn_out_g=attn_out_g[l], sgu_out_g=sgu_out_g[l],
             w_out=w_out[l], norm2_g=norm2_g[l], w_router=w_router[l], router_bias=router_bias[l],
             w_gate=w_gate[l], w_up=w_up[l], w_down=w_down[l], ws_gate=ws_gate[l], ws_up=ws_up[l],
             ws_down=ws_down[l])
    y_p, y_s, k, v, vn_s = _layer(xp3, xs3, cache_k[l], cache_v[l], mod, final_g, batch, seq, dec_batch, n, p)

    keep = min(WINDOW, seq)
    kv_shape = (batch, keep, N_KV_HEADS, HEAD_DIM)
    k_p = jnp.stack([k[(b + 1) * seq - keep:(b + 1) * seq] for b in range(batch)]).reshape(kv_shape)
    v_p = jnp.stack([v[(b + 1) * seq - keep:(b + 1) * seq] for b in range(batch)]).reshape(kv_shape)
    ts = dec_batch * n
    k_s = k[tp:tp + ts].reshape(dec_batch, n, N_KV_HEADS, HEAD_DIM)
    v_s = v[tp:tp + ts].reshape(dec_batch, n, N_KV_HEADS, HEAD_DIM)
    return (y_p.reshape(batch, seq, d), y_s.reshape(dec_batch, n, d),
            k_p[None], v_p[None], k_s[None], v_s[None], vn_s.reshape(dec_batch, n, SGU_WIDTH)[None])
```

```python
import functools

import numpy as np
import jax
import jax.numpy as jnp
from jax import lax
from jax.experimental import pallas as pl
from jax.experimental.pallas import tpu as pltpu

F32 = jnp.float32
BF16 = jnp.bfloat16
I32 = jnp.int32

D_MODEL = 1024
CHUNK = 64
HEAD_DIM = 64
ATTN_WIDTH = 512
N_HEADS = 8
N_KV_HEADS = 2
GQA_GROUP = 4
KV_WIDTH = 128
WINDOW = 128
SGU_CHUNK = 128
SGU_WIDTH = 512
SGU_GROUPS = 4
IN_COLS = ATTN_WIDTH + 2 * KV_WIDTH + 2 * SGU_WIDTH
N_MOD = 6
N_EXPERTS = 256
TOP_K = 8
N_EXPERT_GROUPS = 8
GROUP_SIZE = N_EXPERTS // N_EXPERT_GROUPS
TOPK_GROUPS = 4
EXPERT_DIM = 256
ROUTED_SCALE = 2.5
EPS = 1e-6

LANES = 128
ROW_WORDS = D_MODEL // 2
ROW_TILES = ROW_WORDS // LANES
ROW_DTYPE = jnp.uint32
ROW_GROUP = 16
TOKEN_TILE = 256
DENSE_TILE = 512
COND_REPEAT = DENSE_TILE // ROW_GROUP
ATTN_TILE = 512
EXPERT_BLOCK = 640
BLOCK_SIZES = (128, 256, 384, 512, 576, EXPERT_BLOCK)
X_SLOTS = 3
W_SLOTS = 3
POS_TILE_MAX = 4096
COMBINE_TILE = 256
COMBINE_GROUP = 8
VMEM_LIMIT_BYTES = 56 * 1024 * 1024


def _params(n_axes=1):
    return pltpu.CompilerParams(dimension_semantics=("arbitrary",) * n_axes,
                                vmem_limit_bytes=VMEM_LIMIT_BYTES)


def _rms(x):
    return x * lax.rsqrt(jnp.mean(x * x, axis=-1, keepdims=True) + EPS)


def _cond_rows(m_ref):
    m = m_ref[...]
    g, _, d = m.shape
    return jnp.broadcast_to(m, (g, ROW_GROUP, d)).reshape(g * ROW_GROUP, d)


def _cond_spec(piece, tile_rows, tiles_per_batch, batch, n_prompt_tiles):
    g = tile_rows // ROW_GROUP
    per = COND_REPEAT // g
    return pl.BlockSpec(
        (g, 1, D_MODEL),
        lambda i: (jnp.where(i < n_prompt_tiles, (i // tiles_per_batch) * per, batch * per + i - n_prompt_tiles),
                   0, piece))


def _pick_tile(is_sample, prompt_ref, sample_ref):
    v = jnp.where(is_sample, sample_ref[...], prompt_ref[...])
    return v.reshape(v.shape[0] * v.shape[1], v.shape[-1])


def _pack_rows(x):
    x = x.astype(F32)
    return pltpu.pack_elementwise([x[:, :ROW_WORDS], x[:, ROW_WORDS:]], packed_dtype=BF16)


def _unpack_rows(w):
    return jnp.concatenate([pltpu.unpack_elementwise(w, index=i, packed_dtype=BF16, unpacked_dtype=F32)
                            for i in range(2)], axis=1)


def _load_rows(ref, n_rows):
    w = jnp.concatenate([ref[pl.ds(j, n_rows, stride=ROW_TILES), :] for j in range(ROW_TILES)], axis=1)
    return _unpack_rows(w)


def _store_rows(ref, x):
    n_rows = x.shape[0]
    w = _pack_rows(x)
    for j in range(ROW_TILES):
        ref[pl.ds(j, n_rows, stride=ROW_TILES), :] = w[:, j * LANES:(j + 1) * LANES]


def _tile_offset(row):
    off = row * ROW_TILES
    return off if isinstance(off, int) else pl.multiple_of(off, ROW_TILES)


def _rows_copy(src_ref, src_row, dst_ref, dst_row, n_rows, sem):
    s = _tile_offset(src_row)
    d = _tile_offset(dst_row)
    return pltpu.make_async_copy(src_ref.at[pl.ds(s, n_rows * ROW_TILES)],
                                 dst_ref.at[pl.ds(d, n_rows * ROW_TILES)], sem)


def _adaln_kernel(c_ref, w_ref, b_ref, o_ref):
    c = c_ref[...]
    s = c * jax.nn.sigmoid(c)
    w = w_ref[...]
    sh = s.astype(BF16)
    sl = (s - sh.astype(F32)).astype(BF16)
    wh = w.astype(BF16)
    wl = (w - wh.astype(F32)).astype(BF16)
    o_ref[...] = (jnp.dot(sh, wh, preferred_element_type=F32)
                  + (jnp.dot(sl, wh, preferred_element_type=F32)
                     + jnp.dot(sh, wl, preferred_element_type=F32))) + b_ref[...]


def _adaln(c, w_ada, b_ada):
    n, d = c.shape
    cols = w_ada.shape[1]
    tn = 1536
    return pl.pallas_call(
        _adaln_kernel,
        grid=(cols // tn,),
        in_specs=[pl.BlockSpec((n, d), lambda j: (0, 0)),
                  pl.BlockSpec((d, tn), lambda j: (0, j)),
                  pl.BlockSpec((1, tn), lambda j: (0, j))],
        out_specs=pl.BlockSpec((n, tn), lambda j: (0, j)),
        out_shape=jax.ShapeDtypeStruct((n, cols), F32),
        compiler_params=_params(1),
        name="adaln",
    )(c, w_ada, b_ada.reshape(1, cols))


def _stage1_kernel(xp_ref, xs_ref, sh_ref, sc_ref, g1_ref, win_ref, lng_ref, lnb_ref,
                   wm_ref, wms_ref, bs_ref, bss_ref, gsgu_ref,
                   q_ref, k_ref, v_ref, sgu_ref, vn_ref, *, n_prompt_tiles):
    is_sample = pl.program_id(0) >= n_prompt_tiles
    x = _pick_tile(is_sample, xp_ref, xs_ref)
    h = _rms(x) * g1_ref[...] * (1.0 + _cond_rows(sc_ref)) + _cond_rows(sh_ref)
    proj = jnp.dot(h.astype(BF16), win_ref[...], preferred_element_type=F32)
    q_ref[...] = (proj[:, :ATTN_WIDTH] * (HEAD_DIM ** -0.5)).astype(BF16)
    k_ref[...] = proj[:, ATTN_WIDTH:ATTN_WIDTH + KV_WIDTH]
    v_ref[...] = proj[:, ATTN_WIDTH + KV_WIDTH:ATTN_WIDTH + 2 * KV_WIDTH]
    c0 = ATTN_WIDTH + 2 * KV_WIDTH
    u = jax.nn.gelu(proj[:, c0:c0 + SGU_WIDTH])
    vg = jax.nn.gelu(proj[:, c0 + SGU_WIDTH:])
    mu = jnp.mean(vg, axis=-1, keepdims=True)
    xc = vg - mu
    var = jnp.mean(xc * xc, axis=-1, keepdims=True)
    vn = xc * lax.rsqrt(var + EPS) * lng_ref[...] + lnb_ref[...]

    @pl.when(is_sample)
    def _():
        vn_ref[...] = vn

    vnb = vn.astype(BF16)
    wm = jnp.where(is_sample, wms_ref[...], wm_ref[...])
    bs = jnp.where(is_sample, bss_ref[...], bs_ref[...])
    rows = []
    for c in range(DENSE_TILE // SGU_CHUNK):
        r0 = c * SGU_CHUNK
        cols = []
        for g in range(SGU_GROUPS):
            l0 = g * SGU_CHUNK
            sv = jnp.dot(wm[g], vnb[r0:r0 + SGU_CHUNK, l0:l0 + SGU_CHUNK],
                         preferred_element_type=F32) + bs[g]
            cols.append(sv)
        rows.append(jnp.concatenate(cols, axis=1))
    sgu = u * jnp.concatenate(rows, axis=0)
    sgu_ref[...] = (_rms(sgu) * gsgu_ref[...]).astype(BF16)


def _stage1(xp3, xs3, cond_tbl, g1, win_b, lng, lnb, wm, wms, bs, bss, gsgu, batch, n_prompt_tiles):
    t = (xp3.shape[0] + xs3.shape[0]) * ROW_GROUP
    n_tiles = t // DENSE_TILE
    last_p = n_prompt_tiles - 1
    x_blk = (DENSE_TILE // ROW_GROUP, ROW_GROUP, D_MODEL)
    const2 = lambda i: (0, 0)
    const3 = lambda i: (0, 0, 0)
    tile2 = lambda i: (i, 0)
    cond = functools.partial(_cond_spec, tile_rows=DENSE_TILE, tiles_per_batch=n_prompt_tiles // batch,
                             batch=batch, n_prompt_tiles=n_prompt_tiles)
    sgu_w_spec = pl.BlockSpec((SGU_GROUPS, SGU_CHUNK, SGU_CHUNK), const3)
    return pl.pallas_call(
        functools.partial(_stage1_kernel, n_prompt_tiles=n_prompt_tiles),
        grid=(n_tiles,),
        in_specs=[pl.BlockSpec(x_blk, lambda i: (jnp.minimum(i, last_p), 0, 0)),
                  pl.BlockSpec(x_blk, const3)]
                 + [cond(0), cond(1)]
                 + [pl.BlockSpec((1, D_MODEL), const2),
                    pl.BlockSpec((D_MODEL, IN_COLS), const2),
                    pl.BlockSpec((1, SGU_WIDTH), const2),
                    pl.BlockSpec((1, SGU_WIDTH), const2),
                    sgu_w_spec, sgu_w_spec, sgu_w_spec, sgu_w_spec,
                    pl.BlockSpec((1, SGU_WIDTH), const2)],
        out_specs=[pl.BlockSpec((DENSE_TILE, ATTN_WIDTH), tile2),
                   pl.BlockSpec((DENSE_TILE, KV_WIDTH), tile2),
                   pl.BlockSpec((DENSE_TILE, KV_WIDTH), tile2),
                   pl.BlockSpec((DENSE_TILE, SGU_WIDTH), tile2),
                   pl.BlockSpec((DENSE_TILE, SGU_WIDTH), const2)],
        out_shape=[jax.ShapeDtypeStruct((t, ATTN_WIDTH), BF16),
                   jax.ShapeDtypeStruct((t, KV_WIDTH), F32),
                   jax.ShapeDtypeStruct((t, KV_WIDTH), F32),
                   jax.ShapeDtypeStruct((t, SGU_WIDTH), BF16),
                   jax.ShapeDtypeStruct((DENSE_TILE, SGU_WIDTH), F32)],
        compiler_params=_params(1),
        name="stage1",
    )(xp3, xs3, cond_tbl, cond_tbl, g1, win_b, lng, lnb, wm, wms, bs, bss, gsgu)


def _alibi_slope(h):
    return float(np.float32(2.0) ** np.float32(-8.0 * (h + 1.0) / N_HEADS))


def _sink_attention(q, k, v, back, valid_fn, sinks_ref, gout):
    nq = q.shape[0]
    nk = k.shape[0]
    q_t = q.astype(F32).T.astype(BF16)
    k = k.astype(BF16)
    v_t = v.T.astype(BF16)
    c = lax.broadcasted_iota(I32, (nk, nq), 0)
    r = lax.broadcasted_iota(I32, (nk, nq), 1)
    dist = jnp.abs(r - c + back).astype(F32)
    valid = None if valid_fn is None else valid_fn(c, r)
    outs = []
    for kvh in range(N_KV_HEADS):
        kv0 = kvh * HEAD_DIM
        heads = range(kvh * GQA_GROUP, (kvh + 1) * GQA_GROUP)
        qs = jnp.concatenate([q_t[h * HEAD_DIM:(h + 1) * HEAD_DIM, :] for h in heads], axis=1)
        l = jnp.dot(k[:, kv0:kv0 + HEAD_DIM], qs, preferred_element_type=F32)
        biased = [l[:, g * nq:(g + 1) * nq] - _alibi_slope(h) * dist for g, h in enumerate(heads)]
        if valid is not None:
            biased = [jnp.where(valid, b, -jnp.inf) for b in biased]
        l = jnp.concatenate(biased, axis=1)
        sink = jnp.concatenate([jnp.full((1, nq), sinks_ref[h], F32) for h in heads], axis=1)
        m = jnp.maximum(jnp.max(l, axis=0, keepdims=True), sink)
        e = jnp.exp(l - m)
        denom = jnp.sum(e, axis=0, keepdims=True) + jnp.exp(sink - m)
        pv = jnp.dot(v_t[kv0:kv0 + HEAD_DIM, :], e.astype(BF16), preferred_element_type=F32)
        o_t = pv / denom
        outs.extend(o_t[:, g * nq:(g + 1) * nq] for g in range(GQA_GROUP))
    o = jnp.concatenate(outs, axis=0).T
    return (_rms(o) * gout).astype(BF16)


def _attn_prompt_kernel(sinks_ref, q_ref, kp_ref, kc_ref, vp_ref, vc_ref, g_ref, o_ref):
    i = pl.program_id(1)
    back = kp_ref.shape[0]

    k = jnp.concatenate([kp_ref[...], kc_ref[...]], axis=0)
    v = jnp.concatenate([vp_ref[...], vc_ref[...]], axis=0)
    for s in range(q_ref.shape[0] // WINDOW):
        def valid_fn(c, r, s=s):
            kc = c // CHUNK
            qc = r // CHUNK + (back // CHUNK)
            in_band = jnp.logical_and(kc <= qc, kc >= qc - WINDOW // CHUNK)
            if s > 0:
                return in_band
            return jnp.logical_and(in_band, jnp.logical_or(i > 0, kc >= back // CHUNK))

        rows = slice(s * WINDOW, (s + 1) * WINDOW)
        keys = slice(s * WINDOW, s * WINDOW + back + WINDOW)
        o_ref[rows, :] = _sink_attention(q_ref[rows, :], k[keys], v[keys], back, valid_fn, sinks_ref, g_ref[...])


def _attn_prompt(q, k, v, sinks, gattn, batch, seq):
    nt = seq // ATTN_TILE
    per_tile = ATTN_TILE // WINDOW
    cur = lambda b, i, s: (b * nt + i, 0)
    prev = lambda b, i, s: (jnp.maximum((b * nt + i) * per_tile - 1, 0), 0)
    kv_blk = (ATTN_TILE, KV_WIDTH)
    back_blk = (WINDOW, KV_WIDTH)
    return pl.pallas_call(
        _attn_prompt_kernel,
        grid_spec=pltpu.PrefetchScalarGridSpec(
            num_scalar_prefetch=1,
            grid=(batch, nt),
            in_specs=[pl.BlockSpec((ATTN_TILE, ATTN_WIDTH), cur),
                      pl.BlockSpec(back_blk, prev), pl.BlockSpec(kv_blk, cur),
                      pl.BlockSpec(back_blk, prev), pl.BlockSpec(kv_blk, cur),
                      pl.BlockSpec((1, ATTN_WIDTH), lambda b, i, s: (0, 0))],
            out_specs=pl.BlockSpec((ATTN_TILE, ATTN_WIDTH), cur)),
        out_shape=jax.ShapeDtypeStruct((batch * seq, ATTN_WIDTH), BF16),
        compiler_params=_params(2),
        name="attn_prompt",
    )(sinks, q, k, k, v, v, gattn)


def _attn_sample_kernel(sinks_ref, q_ref, ck_ref, cv_ref, k_ref, v_ref, g_ref, o_ref):
    k = jnp.concatenate([ck_ref[0], k_ref[...]], axis=0)
    v = jnp.concatenate([cv_ref[0], v_ref[...]], axis=0)
    o_ref[...] = _sink_attention(q_ref[...], k, v, ck_ref.shape[1], None, sinks_ref, g_ref[...])


def _attn_sample(q, k, v, cache_k, cache_v, sinks, gattn, dec_batch, n, row0):
    w = cache_k.shape[1]
    blk0 = row0 // n
    new = lambda b, s: (blk0 + b, 0)
    return pl.pallas_call(
        _attn_sample_kernel,
        grid_spec=pltpu.PrefetchScalarGridSpec(
            num_scalar_prefetch=1,
            grid=(dec_batch,),
            in_specs=[pl.BlockSpec((n, ATTN_WIDTH), new),
                      pl.BlockSpec((1, w, KV_WIDTH), lambda b, s: (b, 0, 0)),
                      pl.BlockSpec((1, w, KV_WIDTH), lambda b, s: (b, 0, 0)),
                      pl.BlockSpec((n, KV_WIDTH), new),
                      pl.BlockSpec((n, KV_WIDTH), new),
                      pl.BlockSpec((1, ATTN_WIDTH), lambda b, s: (0, 0))],
            out_specs=pl.BlockSpec((n, ATTN_WIDTH), lambda b, s: (b, 0))),
        out_shape=jax.ShapeDtypeStruct((dec_batch * n, ATTN_WIDTH), BF16),
        compiler_params=_params(1),
        name="attn_sample",
    )(sinks, q, cache_k, cache_v, k, v, gattn)


def _merge_kernel(ap_ref, as_ref, sgu_ref, xp_ref, xs_ref,
                  gate1_ref, sh2_ref, sc2_ref, gate2_ref,
                  woa_ref, wos_ref, g2_ref, wrh_ref, wrl_ref, wsgu_ref, wsd_ref,
                  xb_ref, h2_ref, sc_ref, *, n_prompt_tiles):
    is_sample = pl.program_id(0) >= n_prompt_tiles
    x = _pick_tile(is_sample, xp_ref, xs_ref)
    a = jnp.where(is_sample, as_ref[...], ap_ref[...])
    mix = (jnp.dot(a, woa_ref[...], preferred_element_type=F32)
           + jnp.dot(sgu_ref[...], wos_ref[...], preferred_element_type=F32))
    x1 = x + _cond_rows(gate1_ref) * mix
    h2 = _rms(x1) * g2_ref[...] * (1.0 + _cond_rows(sc2_ref)) + _cond_rows(sh2_ref)
    hh = h2.astype(BF16)
    _store_rows(h2_ref, hh)
    hl = (h2 - hh.astype(F32)).astype(BF16)
    logits = (jnp.dot(hh, wrh_ref[...], preferred_element_type=F32)
              + (jnp.dot(hl, wrh_ref[...], preferred_element_type=F32)
                 + jnp.dot(hh, wrl_ref[...], preferred_element_type=F32)))
    sc_ref[...] = jax.nn.sigmoid(logits)
    gu = jnp.dot(hh, wsgu_ref[...], preferred_element_type=F32)
    g = gu[:, :EXPERT_DIM]
    act = (g * jax.nn.sigmoid(g)) * gu[:, EXPERT_DIM:]
    shared = jnp.dot(act.astype(BF16), wsd_ref[...], preferred_element_type=F32)
    xb_ref[...] = x1 + _cond_rows(gate2_ref) * shared


def _merge(attn_p, attn_s, sgu_n, xp3, xs3, cond_tbl, woa, wos, g2, wrh, wrl, wsgu, wsd, batch, n_prompt_tiles):
    t = sgu_n.shape[0]
    n_tiles = t // DENSE_TILE
    last_p = n_prompt_tiles - 1
    x_blk = (DENSE_TILE // ROW_GROUP, ROW_GROUP, D_MODEL)
    const2 = lambda i: (0, 0)
    const3 = lambda i: (0, 0, 0)
    tile2 = lambda i: (i, 0)
    cond = functools.partial(_cond_spec, tile_rows=DENSE_TILE, tiles_per_batch=n_prompt_tiles // batch,
                             batch=batch, n_prompt_tiles=n_prompt_tiles)
    return pl.pallas_call(
        functools.partial(_merge_kernel, n_prompt_tiles=n_prompt_tiles),
        grid=(n_tiles,),
        in_specs=[pl.BlockSpec((DENSE_TILE, ATTN_WIDTH), lambda i: (jnp.minimum(i, last_p), 0)),
                  pl.BlockSpec((DENSE_TILE, ATTN_WIDTH), const2),
                  pl.BlockSpec((DENSE_TILE, SGU_WIDTH), tile2),
                  pl.BlockSpec(x_blk, lambda i: (jnp.minimum(i, last_p), 0, 0)),
                  pl.BlockSpec(x_blk, const3)]
                 + [cond(2), cond(3), cond(4), cond(5)]
                 + [pl.BlockSpec((ATTN_WIDTH, D_MODEL), const2),
                    pl.BlockSpec((SGU_WIDTH, D_MODEL), const2),
                    pl.BlockSpec((1, D_MODEL), const2),
                    pl.BlockSpec((D_MODEL, N_EXPERTS), const2),
                    pl.BlockSpec((D_MODEL, N_EXPERTS), const2),
                    pl.BlockSpec((D_MODEL, 2 * EXPERT_DIM), const2),
                    pl.BlockSpec((EXPERT_DIM, D_MODEL), const2)],
        out_specs=[pl.BlockSpec((DENSE_TILE, D_MODEL), tile2),
                   pl.BlockSpec((DENSE_TILE * ROW_TILES, LANES), tile2),
                   pl.BlockSpec((DENSE_TILE, N_EXPERTS), tile2)],
        out_shape=[jax.ShapeDtypeStruct((t, D_MODEL), F32),
                   jax.ShapeDtypeStruct((t * ROW_TILES, LANES), ROW_DTYPE),
                   jax.ShapeDtypeStruct((t, N_EXPERTS), F32)],
        compiler_params=_params(1),
        name="merge",
    )(attn_p, attn_s, sgu_n, xp3, xs3, *([cond_tbl] * 4), woa, wos, g2, wrh, wrl, wsgu, wsd)


def _first_argmax(x, iota, size, axis):
    m = jnp.max(x, axis=axis, keepdims=True)
    idx = jnp.min(jnp.where(x == m, iota, size), axis=axis, keepdims=True)
    return m, idx


def _route_kernel(sc_ref, bias_ref, tri_ref, eidx_ref, wts_ref, rank_ref, cnt_ref, base_ref):
    tt = sc_ref.shape[0]

    @pl.when(pl.program_id(0) == 0)
    def _():
        base_ref[...] = jnp.zeros_like(base_ref)

    s_t = sc_ref[...].T
    sel = s_t + bias_ref[...]
    sel3 = sel.reshape(N_EXPERT_GROUPS, GROUP_SIZE, tt)
    io3 = lax.broadcasted_iota(I32, sel3.shape, 1)
    m1, i1 = _first_argmax(sel3, io3, GROUP_SIZE, 1)
    m2 = jnp.max(jnp.where(io3 == i1, -jnp.inf, sel3), axis=1, keepdims=True)
    gs = (m1 + m2).reshape(N_EXPERT_GROUPS, tt)
    io8 = lax.broadcasted_iota(I32, gs.shape, 0)
    gmask = jnp.zeros(gs.shape, jnp.bool_)
    for _ in range(TOPK_GROUPS):
        _, gi = _first_argmax(gs, io8, N_EXPERT_GROUPS, 0)
        hit = io8 == gi
        gmask = jnp.logical_or(gmask, hit)
        gs = jnp.where(hit, -jnp.inf, gs)
    emask = jnp.broadcast_to(gmask.reshape(N_EXPERT_GROUPS, 1, tt), sel3.shape).reshape(N_EXPERTS, tt)
    cand = jnp.where(emask, sel, -jnp.inf)
    io = lax.broadcasted_iota(I32, cand.shape, 0)
    chosen = jnp.zeros(cand.shape, jnp.bool_)
    eidx, wts = [], []
    for _ in range(TOP_K):
        _, ei = _first_argmax(cand, io, N_EXPERTS, 0)
        hit = io == ei
        eidx.append(ei)
        wts.append(jnp.sum(jnp.where(hit, s_t, 0.0), axis=0, keepdims=True))
        chosen = jnp.logical_or(chosen, hit)
        cand = jnp.where(hit, -jnp.inf, cand)
    w = jnp.concatenate(wts, axis=0)
    wts_ref[...] = w / jnp.sum(w, axis=0, keepdims=True) * ROUTED_SCALE
    eidx_ref[...] = jnp.concatenate(eidx, axis=0)
    cf = jnp.where(chosen, 1.0, 0.0)
    ahead = jnp.dot(cf.astype(BF16), tri_ref[...], preferred_element_type=F32) + base_ref[...]
    ranks = [jnp.sum(jnp.where(io == ei, ahead, 0.0), axis=0, keepdims=True) for ei in eidx]
    rank_ref[...] = jnp.concatenate(ranks, axis=0).astype(I32)
    total = base_ref[...] + jnp.sum(cf, axis=1, keepdims=True)
    base_ref[...] = total
    cnt_ref[...] = total.astype(I32)


def _route(scores, bias_col, tri, t):
    n_tiles = t // TOKEN_TILE
    col = lambda i: (0, i)
    return pl.pallas_call(
        _route_kernel,
        grid=(n_tiles,),
        in_specs=[pl.BlockSpec((TOKEN_TILE, N_EXPERTS), lambda i: (i, 0)),
                  pl.BlockSpec((N_EXPERTS, 1), lambda i: (0, 0)),
                  pl.BlockSpec((TOKEN_TILE, TOKEN_TILE), lambda i: (0, 0))],
        out_specs=[pl.BlockSpec((TOP_K, TOKEN_TILE), col),
                   pl.BlockSpec((TOP_K, TOKEN_TILE), col),
                   pl.BlockSpec((TOP_K, TOKEN_TILE), col),
                   pl.BlockSpec((N_EXPERTS, 1), lambda i: (0, 0))],
        out_shape=[jax.ShapeDtypeStruct((TOP_K, t), I32),
                   jax.ShapeDtypeStruct((TOP_K, t), F32),
                   jax.ShapeDtypeStruct((TOP_K, t), I32),
                   jax.ShapeDtypeStruct((N_EXPERTS, 1), I32)],
        scratch_shapes=[pltpu.VMEM((N_EXPERTS, 1), F32)],
        compiler_params=_params(1),
        name="route",
    )(scores, bias_col, tri)


def _pos_kernel(eidx_ref, rank_ref, start_ref, pos_ref):
    e = eidx_ref[...]
    tt = e.shape[1]
    io = lax.broadcasted_iota(I32, (N_EXPERTS, tt), 0)
    st = start_ref[...]
    rows = [jnp.sum(jnp.where(io == e[k:k + 1, :], st, 0), axis=0, keepdims=True) for k in range(TOP_K)]
    pos_ref[...] = jnp.concatenate(rows, axis=0) + rank_ref[...]


def _positions(eidx, rank, start_col):
    t = eidx.shape[1]
    lane_tiles = t // LANES
    width = max(d for d in range(1, POS_TILE_MAX // LANES + 1) if lane_tiles % d == 0) * LANES
    col = lambda i: (0, i)
    blk = pl.BlockSpec((TOP_K, width), col)
    return pl.pallas_call(
        _pos_kernel,
        grid=(t // width,),
        in_specs=[blk, blk, pl.BlockSpec((N_EXPERTS, 1), lambda i: (0, 0))],
        out_specs=blk,
        out_shape=jax.ShapeDtypeStruct((TOP_K, t), I32),
        compiler_params=_params(1),
        name="positions",
    )(eidx, rank, start_col)


def _dispatch_kernel(pos_ref, h_ref, xs_ref, zero_ref, sem, zsem, *, n_rows):
    tt = h_ref.shape[0] // ROW_TILES
    first = pl.program_id(0) == 0

    @pl.when(first)
    def _():
        zero_ref[...] = jnp.zeros_like(zero_ref)
        _rows_copy(zero_ref, 0, xs_ref, n_rows, EXPERT_BLOCK, zsem).start()

    for t in range(tt):
        for k in range(TOP_K):
            _rows_copy(h_ref, t, xs_ref, pos_ref[k, t], 1, sem).start(priority=k % 2)
    for k in range(TOP_K):
        _rows_copy(h_ref, 0, xs_ref, 0, tt, sem).wait()

    @pl.when(first)
    def _():
        _rows_copy(zero_ref, 0, xs_ref, n_rows, EXPERT_BLOCK, zsem).wait()


def _dispatch(pos, h2_tiles, t, n_rows):
    return pl.pallas_call(
        functools.partial(_dispatch_kernel, n_rows=n_rows),
        grid=(t // TOKEN_TILE,),
        in_specs=[pl.BlockSpec((TOP_K, TOKEN_TILE), lambda i: (0, i), memory_space=pltpu.SMEM),
                  pl.BlockSpec((TOKEN_TILE * ROW_TILES, LANES), lambda i: (i, 0))],
        out_specs=pl.BlockSpec(memory_space=pl.ANY),
        out_shape=jax.ShapeDtypeStruct(((n_rows + EXPERT_BLOCK) * ROW_TILES, LANES), ROW_DTYPE),
        scratch_shapes=[pltpu.VMEM((EXPERT_BLOCK * ROW_TILES, LANES), ROW_DTYPE),
                        pltpu.SemaphoreType.DMA, pltpu.SemaphoreType.DMA],
        compiler_params=_params(1),
        name="dispatch",
    )(pos, h2_tiles)


def _out_pieces(nvalid, fn):
    s = 1 << (EXPERT_BLOCK.bit_length() - 1)
    while s >= 1:
        @pl.when((nvalid & s) != 0)
        def _(s=s):
            fn(nvalid & ~(2 * s - 1), s)
        s //= 2


def _for_block_size(nvalid, fn):
    lo = 0
    for rows in BLOCK_SIZES:
        @pl.when(jnp.logical_and(nvalid > lo, nvalid <= rows))
        def _(rows=rows):
            fn(rows)
        lo = rows


def _expert_kernel(count_ref, blk0_ref, row_ref, cnt_ref, xs_ref, wg_ref, wu_ref, wd_ref, ys_ref,
                   xbuf, ybuf, wg_buf, wu_buf, wd_buf, wgu_s, wd_s, state, xsem, ysem, wsem):
    e = pl.program_id(0)
    n_experts = pl.num_programs(0)
    cnt = count_ref[e]
    g0 = blk0_ref[e]
    nblk = (cnt + EXPERT_BLOCK - 1) // EXPERT_BLOCK
    wslot = lax.rem(e, W_SLOTS)

    def w_copies(expert, slot):
        return [pltpu.make_async_copy(src.at[expert], dst.at[slot], wsem.at[slot])
                for src, dst in ((wg_ref, wg_buf), (wu_ref, wu_buf), (wd_ref, wd_buf))]

    def w_fetch(expert):
        @pl.when(expert < n_experts)
        def _():
            for c in w_copies(expert, lax.rem(expert, W_SLOTS)):
                c.start()

    def x_copy(row, slot, rows):
        return _rows_copy(xs_ref, row, xbuf.at[slot], 0, rows, xsem.at[slot])

    def x_fetch(g, slot):
        _for_block_size(cnt_ref[g], lambda rows: x_copy(row_ref[g], slot, rows).start())

    def y_wait(nvalid, slot):
        def wait(off, size):
            _rows_copy(ybuf.at[slot], 0, ys_ref, 0, size, ysem.at[slot]).wait()
        _out_pieces(nvalid, wait)

    @pl.when(e == 0)
    def _():
        state[0] = 0
        state[1] = 0
        for g in range(X_SLOTS - 1):
            x_fetch(g, g)
        for ahead in range(W_SLOTS - 1):
            w_fetch(ahead)

    w_fetch(e + W_SLOTS - 1)
    for c in w_copies(e, wslot):
        c.wait()
    wgu_s[:, :EXPERT_DIM] = wg_buf[wslot].astype(BF16)
    wgu_s[:, EXPERT_DIM:] = wu_buf[wslot].astype(BF16)
    wd_s[...] = wd_buf[wslot].astype(BF16)

    def compute(rows, xslot, yslot, row0, nvalid):
        x_copy(row0, xslot, rows).wait()
        x = _load_rows(xbuf.at[xslot], rows).astype(BF16)
        gu = jnp.dot(x, wgu_s[...], preferred_element_type=F32)
        g = gu[:, :EXPERT_DIM]
        act = (g * jax.nn.sigmoid(g)) * gu[:, EXPERT_DIM:]
        y = jnp.dot(act.astype(BF16), wd_s[...], preferred_element_type=F32)
        y_wait(state[0], 1 - yslot)
        _store_rows(ybuf.at[yslot], y)

        def put(off, size):
            _rows_copy(ybuf.at[yslot], off, ys_ref, row0 + off, size, ysem.at[yslot]).start()
        _out_pieces(nvalid, put)

    def block(j, carry):
        g = g0 + j
        xslot = state[1]
        yslot = g & 1
        row0 = row_ref[g]
        nvalid = cnt_ref[g]
        x_fetch(g + X_SLOTS - 1, jnp.where(xslot == 0, X_SLOTS - 1, xslot - 1))
        _for_block_size(nvalid, lambda rows: compute(rows, xslot, yslot, row0, nvalid))
        state[0] = nvalid
        state[1] = jnp.where(xslot == X_SLOTS - 1, 0, xslot + 1)
        return carry

    lax.fori_loop(0, nblk, block, 0)

    @pl.when(e == pl.num_programs(0) - 1)
    def _():
        y_wait(state[0], (g0 + nblk - 1) & 1)


def _experts(count, blk0, blk_row, blk_cnt, xs, w_gate, w_up, w_down, n_rows):
    blk_rows = EXPERT_BLOCK * ROW_TILES
    any_spec = pl.BlockSpec(memory_space=pl.ANY)
    return pl.pallas_call(
        _expert_kernel,
        grid_spec=pltpu.PrefetchScalarGridSpec(
            num_scalar_prefetch=4,
            grid=(N_EXPERTS,),
            in_specs=[any_spec, any_spec, any_spec, any_spec],
            out_specs=any_spec,
            scratch_shapes=[pltpu.VMEM((X_SLOTS, blk_rows, LANES), ROW_DTYPE),
                            pltpu.VMEM((2, blk_rows, LANES), ROW_DTYPE),
                            pltpu.VMEM((W_SLOTS, D_MODEL, EXPERT_DIM), F32),
                            pltpu.VMEM((W_SLOTS, D_MODEL, EXPERT_DIM), F32),
                            pltpu.VMEM((W_SLOTS, EXPERT_DIM, D_MODEL), F32),
                            pltpu.VMEM((D_MODEL, 2 * EXPERT_DIM), BF16),
                            pltpu.VMEM((EXPERT_DIM, D_MODEL), BF16),
                            pltpu.SMEM((2,), I32),
                            pltpu.SemaphoreType.DMA((X_SLOTS,)),
                            pltpu.SemaphoreType.DMA((2,)),
                            pltpu.SemaphoreType.DMA((W_SLOTS,))]),
        out_shape=jax.ShapeDtypeStruct((n_rows * ROW_TILES, LANES), ROW_DTYPE),
        compiler_params=_params(1),
        name="experts",
    )(count, blk0, blk_row, blk_cnt, xs, w_gate, w_up, w_down)


def _combine_kernel(pos_ref, posn_ref, ys_ref, xb_ref, wts_ref, gate2_ref, gf_ref, yp_ref, ysm_ref,
                    buf, ytile, sem, *, n_prompt_tiles, n_tiles):
    i = pl.program_id(0)
    tt = xb_ref.shape[0]
    slot = i & 1

    def fetch(p_ref, s, t):
        for k in range(TOP_K):
            _rows_copy(ys_ref, p_ref[k, t], buf.at[s, k], t, 1, sem.at[s]).start(priority=k % 2)

    @pl.when(i == 0)
    def _():
        for t in range(tt):
            fetch(pos_ref, 0, t)

    for k in range(TOP_K):
        _rows_copy(ys_ref, 0, buf.at[slot, k], 0, tt, sem.at[slot]).wait()

    def finish(s, g):
        r0 = g * COMBINE_GROUP
        w = wts_ref[r0:r0 + COMBINE_GROUP, :]
        routed = None
        for k in range(TOP_K):
            rows = _load_rows(buf.at[s, k, pl.ds(r0 * ROW_TILES, COMBINE_GROUP * ROW_TILES)], COMBINE_GROUP)
            term = w[:, k:k + 1] * rows
            routed = term if routed is None else routed + term
        gate = gate2_ref[g * COMBINE_GROUP // ROW_GROUP]
        x = xb_ref[r0:r0 + COMBINE_GROUP, :] + gate * routed
        return _rms(x) * gf_ref[...]

    def tile_work(s, fetch_next):
        for g in range(tt // COMBINE_GROUP):
            y = finish(s, g)
            if fetch_next:
                for r in range(COMBINE_GROUP):
                    fetch(posn_ref, 1 - s, g * COMBINE_GROUP + r)
            ytile[g * COMBINE_GROUP:(g + 1) * COMBINE_GROUP, :] = y

    for s in range(2):
        @pl.when(jnp.logical_and(slot == s, i + 1 < n_tiles))
        def _(s=s):
            tile_work(s, True)

    @pl.when(i + 1 == n_tiles)
    def _():
        tile_work((n_tiles - 1) % 2, False)

    @pl.when(i < n_prompt_tiles)
    def _():
        yp_ref[...] = ytile[...]

    @pl.when(i >= n_prompt_tiles)
    def _():
        ysm_ref[...] = ytile[...]


def _combine(pos, ys, xb, wts_t, cond_tbl, gf, batch, n_prompt_rows, t):
    tt = COMBINE_TILE
    n_tiles = t // tt
    n_p = n_prompt_rows // tt
    pos_spec = lambda f: pl.BlockSpec((TOP_K, tt), f, memory_space=pltpu.SMEM)
    return pl.pallas_call(
        functools.partial(_combine_kernel, n_prompt_tiles=n_p, n_tiles=n_tiles),
        grid=(n_tiles,),
        in_specs=[pos_spec(lambda i: (0, i)),
                  pos_spec(lambda i: (0, jnp.minimum(i + 1, n_tiles - 1))),
                  pl.BlockSpec(memory_space=pl.ANY),
                  pl.BlockSpec((tt, D_MODEL), lambda i: (i, 0)),
                  pl.BlockSpec((tt, TOP_K), lambda i: (i, 0))]
                 + [_cond_spec(5, tt, n_p // batch, batch, n_p)]
                 + [pl.BlockSpec((1, D_MODEL), lambda i: (0, 0))],
        out_specs=[pl.BlockSpec((tt, D_MODEL), lambda i: (jnp.minimum(i, n_p - 1), 0)),
                   pl.BlockSpec((tt, D_MODEL), lambda i: (jnp.maximum(i - n_p, 0), 0))],
        out_shape=[jax.ShapeDtypeStruct((n_prompt_rows, D_MODEL), F32),
                   jax.ShapeDtypeStruct((t - n_prompt_rows, D_MODEL), F32)],
        scratch_shapes=[pltpu.VMEM((2, TOP_K, tt * ROW_TILES, LANES), ROW_DTYPE),
                        pltpu.VMEM((tt, D_MODEL), F32),
                        pltpu.SemaphoreType.DMA((2,))],
        compiler_params=_params(1),
        name="combine",
    )(pos, pos, ys, xb, wts_t, cond_tbl, gf)


def _sgu_mask():
    pos = np.arange(SGU_CHUNK) // CHUNK
    return jnp.asarray(pos[None, :] <= pos[:, None])


def _layer(xp3, xs3, cache_k, cache_v, mod, final_g, batch, seq, dec_batch, n, p):
    tp = batch * seq
    t = tp + dec_batch * n
    n_prompt_tiles = tp // DENSE_TILE
    pad_groups = (DENSE_TILE - dec_batch * n) // ROW_GROUP
    xs3 = jnp.pad(xs3, ((0, pad_groups), (0, 0), (0, 0)))
    cond_tbl = jnp.concatenate([jnp.repeat(mod[:batch], COND_REPEAT, axis=0), mod[batch:]], axis=0)
    cond_tbl = jnp.pad(cond_tbl, ((0, pad_groups), (0, 0)))
    cond_tbl = cond_tbl.reshape(batch * COND_REPEAT + dec_batch + pad_groups, 1, N_MOD * D_MODEL)

    wm = jnp.where(_sgu_mask(), p["sgu_w"], 0.0)
    reps = SGU_CHUNK // n
    eye = jnp.eye(reps, dtype=F32)
    wms = jnp.stack([jnp.kron(eye, wm[g, :n, :n]) for g in range(SGU_GROUPS)])
    bs = jnp.broadcast_to(p["sgu_b"][:, :, None], (SGU_GROUPS, SGU_CHUNK, SGU_CHUNK))
    bss = jnp.broadcast_to(jnp.tile(p["sgu_b"][:, :n], (1, reps))[:, :, None], (SGU_GROUPS, SGU_CHUNK, SGU_CHUNK))

    q, k, v, sgu_n, vn_s = _stage1(
        xp3, xs3, cond_tbl, p["norm1_g"].reshape(1, -1), p["w_in"].astype(BF16),
        p["sgu_ln_g"].reshape(1, -1), p["sgu_ln_b"].reshape(1, -1),
        wm.astype(BF16), wms.astype(BF16), bs, bss, p["sgu_out_g"].reshape(1, -1), batch, n_prompt_tiles)

    gattn = p["attn_out_g"].reshape(1, -1)
    attn_p = _attn_prompt(q, k, v, p["attn_sinks"], gattn, batch, seq)
    w = cache_k.shape[1]
    attn_s = _attn_sample(q, k, v, cache_k.reshape(dec_batch, w, KV_WIDTH), cache_v.reshape(dec_batch, w, KV_WIDTH),
                          p["attn_sinks"], gattn, dec_batch, n, tp)

    wo = p["w_out"].astype(BF16)
    wsgu = jnp.concatenate([p["ws_gate"], p["ws_up"]], axis=1).astype(BF16)
    wrh = p["w_router"].astype(BF16)
    wrl = (p["w_router"] - wrh.astype(F32)).astype(BF16)
    attn_s = jnp.pad(attn_s, ((0, DENSE_TILE - dec_batch * n), (0, 0)))
    xb, h2_tiles, scores = _merge(attn_p, attn_s, sgu_n, xp3, xs3, cond_tbl,
                                  wo[:ATTN_WIDTH], wo[ATTN_WIDTH:], p["norm2_g"].reshape(1, -1), wrh, wrl,
                                  wsgu, p["ws_down"].astype(BF16), batch, n_prompt_tiles)

    ti = np.arange(TOKEN_TILE)
    tri = jnp.asarray(ti[:, None] < ti[None, :], dtype=BF16)
    eidx, wts, rank, counts = _route(scores, p["router_bias"].reshape(N_EXPERTS, 1), tri, t)

    counts = counts[:, 0]
    ends = jnp.cumsum(counts)
    start = (ends - counts).astype(I32)
    nblk = (counts + EXPERT_BLOCK - 1) // EXPERT_BLOCK
    blk_end = jnp.cumsum(nblk)
    blk0 = (blk_end - nblk).astype(I32)
    pos = _positions(eidx, rank, start.reshape(N_EXPERTS, 1))

    n_rows = t * TOP_K
    g = jnp.arange(n_rows // EXPERT_BLOCK + N_EXPERTS + X_SLOTS, dtype=I32)[:, None]
    mine = jnp.logical_and(g >= blk0[None, :], g < blk_end[None, :])
    off = (g - blk0[None, :]) * EXPERT_BLOCK
    blk_row = jnp.sum(jnp.where(mine, start[None, :] + off, 0), axis=1).astype(I32)
    blk_cnt = jnp.sum(jnp.where(mine, jnp.minimum(counts[None, :] - off, EXPERT_BLOCK), 0), axis=1).astype(I32)

    xs_sorted = _dispatch(pos, h2_tiles, t, n_rows)
    ys_sorted = _experts(counts.astype(I32), blk0, blk_row, blk_cnt, xs_sorted,
                         p["w_gate"], p["w_up"], p["w_down"], n_rows)
    y_p, y_s = _combine(pos, ys_sorted, xb, wts.T, cond_tbl, final_g.reshape(1, -1), batch, tp, t)
    return y_p, y_s, k, v, vn_s[:dec_batch * n]


def kernel(x_prompt, x_sample, cache_k, cache_v, c_prompt, c_sample, norm1_g, w_ada, b_ada, w_in, sgu_ln_g, sgu_ln_b, sgu_w, sgu_b, attn_sinks, attn_out_g, sgu_out_g, w_out, norm2_g, w_router, router_bias, w_gate, w_up, w_down, ws_gate, ws_up, ws_down, final_g):
    batch, seq, d = x_prompt.shape
    dec_batch, n, _ = x_sample.shape
    depth = norm1_g.shape[0]
    assert depth == 1 and d == D_MODEL
    assert seq % DENSE_TILE == 0 and dec_batch * n == TOKEN_TILE and n == ROW_GROUP
    assert seq % ATTN_TILE == 0 and seq >= WINDOW and cache_k.shape[2] == WINDOW
    tp = batch * seq

    xp3 = x_prompt.reshape(tp // ROW_GROUP, ROW_GROUP, d)
    xs3 = x_sample.reshape(dec_batch * n // ROW_GROUP, ROW_GROUP, d)
    c_all = jnp.concatenate([c_prompt, c_sample], axis=0)
    l = 0
    mod = _adaln(c_all, w_ada[l], b_ada[l])
    p = dict(norm1_g=norm1_g[l], w_in=w_in[l], sgu_ln_g=sgu_ln_g[l], sgu_ln_b=sgu_ln_b[l], sgu_w=sgu_w[l],
             sgu_b=sgu_b[l], attn_sinks=attn_sinks[l], attn_out_g=attn_out_g[l], sgu_out_g=sgu_out_g[l],
             w_out=w_out[l], norm2_g=norm2_g[l], w_router=w_router[l], router_bias=router_bias[l],
             w_gate=w_gate[l], w_up=w_up[l], w_down=w_down[l], ws_gate=ws_gate[l], ws_up=ws_up[l],
             ws_down=ws_down[l])
    y_p, y_s, k, v, vn_s = _layer(xp3, xs3, cache_k[l], cache_v[l], mod, final_g, batch, seq, dec_batch, n, p)

    keep = min(WINDOW, seq)
    kv_shape = (batch, keep, N_KV_HEADS, HEAD_DIM)
    k_p = jnp.stack([k[(b + 1) * seq - keep:(b + 1) * seq] for b in range(batch)]).reshape(kv_shape)
    v_p = jnp.stack([v[(b + 1) * seq - keep:(b + 1) * seq] for b in range(batch)]).reshape(kv_shape)
    ts = dec_batch * n
    k_s = k[tp:tp + ts].reshape(dec_batch, n, N_KV_HEADS, HEAD_DIM)
    v_s = v[tp:tp + ts].reshape(dec_batch, n, N_KV_HEADS, HEAD_DIM)
    return (y_p.reshape(batch, seq, d), y_s.reshape(dec_batch, n, d),
            k_p[None], v_p[None], k_s[None], v_s[None], vn_s.reshape(dec_batch, n, SGU_WIDTH)[None])
```

```python
import functools

import numpy as np
import jax
import jax.numpy as jnp
from jax import lax
from jax.experimental import pallas as pl
from jax.experimental.pallas import tpu as pltpu

F32 = jnp.float32
BF16 = jnp.bfloat16
I32 = jnp.int32

D_MODEL = 1024
CHUNK = 64
HEAD_DIM = 64
ATTN_WIDTH = 512
N_HEADS = 8
N_KV_HEADS = 2
GQA_GROUP = 4
KV_WIDTH = 128
WINDOW = 128
SGU_CHUNK = 128
SGU_WIDTH = 512
SGU_GROUPS = 4
IN_COLS = ATTN_WIDTH + 2 * KV_WIDTH + 2 * SGU_WIDTH
N_MOD = 6
N_EXPERTS = 256
TOP_K = 8
N_EXPERT_GROUPS = 8
GROUP_SIZE = N_EXPERTS // N_EXPERT_GROUPS
TOPK_GROUPS = 4
EXPERT_DIM = 256
ROUTED_SCALE = 2.5
EPS = 1e-6

LANES = 128
ROW_WORDS = D_MODEL // 2
ROW_TILES = ROW_WORDS // LANES
ROW_DTYPE = jnp.uint32
ROW_GROUP = 16
TOKEN_TILE = 256
DENSE_TILE = 512
COND_REPEAT = DENSE_TILE // ROW_GROUP
ATTN_TILE = 512
EXPERT_BLOCK = 640
BLOCK_SIZES = (128, 256, 384, 512, 576, EXPERT_BLOCK)
X_SLOTS = 4
W_SLOTS = 4
POS_TILE_MAX = 4096
COMBINE_TILE = 256
COMBINE_GROUP = 8
VMEM_LIMIT_BYTES = 56 * 1024 * 1024


def _params(n_axes=1):
    return pltpu.CompilerParams(dimension_semantics=("arbitrary",) * n_axes,
                                vmem_limit_bytes=VMEM_LIMIT_BYTES)


def _rms(x):
    return x * lax.rsqrt(jnp.mean(x * x, axis=-1, keepdims=True) + EPS)


def _cond_rows(m_ref):
    m = m_ref[...]
    g, _, d = m.shape
    return jnp.broadcast_to(m, (g, ROW_GROUP, d)).reshape(g * ROW_GROUP, d)


def _cond_spec(piece, tile_rows, tiles_per_batch, batch, n_prompt_tiles):
    g = tile_rows // ROW_GROUP
    per = COND_REPEAT // g
    return pl.BlockSpec(
        (g, 1, D_MODEL),
        lambda i: (jnp.where(i < n_prompt_tiles, (i // tiles_per_batch) * per, batch * per + i - n_prompt_tiles),
                   0, piece))


def _pick_tile(is_sample, prompt_ref, sample_ref):
    v = jnp.where(is_sample, sample_ref[...], prompt_ref[...])
    return v.reshape(v.shape[0] * v.shape[1], v.shape[-1])


def _pack_rows(x):
    x = x.astype(F32)
    return pltpu.pack_elementwise([x[:, :ROW_WORDS], x[:, ROW_WORDS:]], packed_dtype=BF16)


def _unpack_rows(w):
    return jnp.concatenate([pltpu.unpack_elementwise(w, index=i, packed_dtype=BF16, unpacked_dtype=F32)
                            for i in range(2)], axis=1)


def _load_rows(ref, n_rows):
    w = jnp.concatenate([ref[pl.ds(j, n_rows, stride=ROW_TILES), :] for j in range(ROW_TILES)], axis=1)
    return _unpack_rows(w)


def _store_rows(ref, x):
    n_rows = x.shape[0]
    w = _pack_rows(x)
    for j in range(ROW_TILES):
        ref[pl.ds(j, n_rows, stride=ROW_TILES), :] = w[:, j * LANES:(j + 1) * LANES]


def _tile_offset(row):
    off = row * ROW_TILES
    return off if isinstance(off, int) else pl.multiple_of(off, ROW_TILES)


def _rows_copy(src_ref, src_row, dst_ref, dst_row, n_rows, sem):
    s = _tile_offset(src_row)
    d = _tile_offset(dst_row)
    return pltpu.make_async_copy(src_ref.at[pl.ds(s, n_rows * ROW_TILES)],
                                 dst_ref.at[pl.ds(d, n_rows * ROW_TILES)], sem)


def _adaln_kernel(c_ref, w_ref, b_ref, o_ref):
    c = c_ref[...]
    s = c * jax.nn.sigmoid(c)
    w = w_ref[...]
    sh = s.astype(BF16)
    sl = (s - sh.astype(F32)).astype(BF16)
    wh = w.astype(BF16)
    wl = (w - wh.astype(F32)).astype(BF16)
    o_ref[...] = (jnp.dot(sh, wh, preferred_element_type=F32)
                  + (jnp.dot(sl, wh, preferred_element_type=F32)
                     + jnp.dot(sh, wl, preferred_element_type=F32))) + b_ref[...]


def _adaln(c, w_ada, b_ada):
    n, d = c.shape
    cols = w_ada.shape[1]
    tn = 1536
    return pl.pallas_call(
        _adaln_kernel,
        grid=(cols // tn,),
        in_specs=[pl.BlockSpec((n, d), lambda j: (0, 0)),
                  pl.BlockSpec((d, tn), lambda j: (0, j)),
                  pl.BlockSpec((1, tn), lambda j: (0, j))],
        out_specs=pl.BlockSpec((n, tn), lambda j: (0, j)),
        out_shape=jax.ShapeDtypeStruct((n, cols), F32),
        compiler_params=_params(1),
        name="adaln",
    )(c, w_ada, b_ada.reshape(1, cols))


def _stage1_kernel(xp_ref, xs_ref, sh_ref, sc_ref, g1_ref, win_ref, lng_ref, lnb_ref,
                   wm_ref, wms_ref, bs_ref, bss_ref, gsgu_ref,
                   q_ref, k_ref, v_ref, sgu_ref, vn_ref, *, n_prompt_tiles):
    is_sample = pl.program_id(0) >= n_prompt_tiles
    x = _pick_tile(is_sample, xp_ref, xs_ref)
    h = _rms(x) * g1_ref[...] * (1.0 + _cond_rows(sc_ref)) + _cond_rows(sh_ref)
    proj = jnp.dot(h.astype(BF16), win_ref[...], preferred_element_type=F32)
    q_ref[...] = (proj[:, :ATTN_WIDTH] * (HEAD_DIM ** -0.5)).astype(BF16)
    k_ref[...] = proj[:, ATTN_WIDTH:ATTN_WIDTH + KV_WIDTH]
    v_ref[...] = proj[:, ATTN_WIDTH + KV_WIDTH:ATTN_WIDTH + 2 * KV_WIDTH]
    c0 = ATTN_WIDTH + 2 * KV_WIDTH
    u = jax.nn.gelu(proj[:, c0:c0 + SGU_WIDTH])
    vg = jax.nn.gelu(proj[:, c0 + SGU_WIDTH:])
    mu = jnp.mean(vg, axis=-1, keepdims=True)
    xc = vg - mu
    var = jnp.mean(xc * xc, axis=-1, keepdims=True)
    vn = xc * lax.rsqrt(var + EPS) * lng_ref[...] + lnb_ref[...]

    @pl.when(is_sample)
    def _():
        vn_ref[...] = vn

    vnb = vn.astype(BF16)
    wm = jnp.where(is_sample, wms_ref[...], wm_ref[...])
    bs = jnp.where(is_sample, bss_ref[...], bs_ref[...])
    rows = []
    for c in range(DENSE_TILE // SGU_CHUNK):
        r0 = c * SGU_CHUNK
        cols = []
        for g in range(SGU_GROUPS):
            l0 = g * SGU_CHUNK
            sv = jnp.dot(wm[g], vnb[r0:r0 + SGU_CHUNK, l0:l0 + SGU_CHUNK],
                         preferred_element_type=F32) + bs[g]
            cols.append(sv)
        rows.append(jnp.concatenate(cols, axis=1))
    sgu = u * jnp.concatenate(rows, axis=0)
    sgu_ref[...] = (_rms(sgu) * gsgu_ref[...]).astype(BF16)


def _stage1(xp3, xs3, cond_tbl, g1, win_b, lng, lnb, wm, wms, bs, bss, gsgu, batch, n_prompt_tiles):
    t = (xp3.shape[0] + xs3.shape[0]) * ROW_GROUP
    n_tiles = t // DENSE_TILE
    last_p = n_prompt_tiles - 1
    x_blk = (DENSE_TILE // ROW_GROUP, ROW_GROUP, D_MODEL)
    const2 = lambda i: (0, 0)
    const3 = lambda i: (0, 0, 0)
    tile2 = lambda i: (i, 0)
    cond = functools.partial(_cond_spec, tile_rows=DENSE_TILE, tiles_per_batch=n_prompt_tiles // batch,
                             batch=batch, n_prompt_tiles=n_prompt_tiles)
    sgu_w_spec = pl.BlockSpec((SGU_GROUPS, SGU_CHUNK, SGU_CHUNK), const3)
    return pl.pallas_call(
        functools.partial(_stage1_kernel, n_prompt_tiles=n_prompt_tiles),
        grid=(n_tiles,),
        in_specs=[pl.BlockSpec(x_blk, lambda i: (jnp.minimum(i, last_p), 0, 0)),
                  pl.BlockSpec(x_blk, const3)]
                 + [cond(0), cond(1)]
                 + [pl.BlockSpec((1, D_MODEL), const2),
                    pl.BlockSpec((D_MODEL, IN_COLS), const2),
                    pl.BlockSpec((1, SGU_WIDTH), const2),
                    pl.BlockSpec((1, SGU_WIDTH), const2),
                    sgu_w_spec, sgu_w_spec, sgu_w_spec, sgu_w_spec,
                    pl.BlockSpec((1, SGU_WIDTH), const2)],
        out_specs=[pl.BlockSpec((DENSE_TILE, ATTN_WIDTH), tile2),
                   pl.BlockSpec((DENSE_TILE, KV_WIDTH), tile2),
                   pl.BlockSpec((DENSE_TILE, KV_WIDTH), tile2),
                   pl.BlockSpec((DENSE_TILE, SGU_WIDTH), tile2),
                   pl.BlockSpec((DENSE_TILE, SGU_WIDTH), const2)],
        out_shape=[jax.ShapeDtypeStruct((t, ATTN_WIDTH), BF16),
                   jax.ShapeDtypeStruct((t, KV_WIDTH), F32),
                   jax.ShapeDtypeStruct((t, KV_WIDTH), F32),
                   jax.ShapeDtypeStruct((t, SGU_WIDTH), BF16),
                   jax.ShapeDtypeStruct((DENSE_TILE, SGU_WIDTH), F32)],
        compiler_params=_params(1),
        name="stage1",
    )(xp3, xs3, cond_tbl, cond_tbl, g1, win_b, lng, lnb, wm, wms, bs, bss, gsgu)


def _alibi_slope(h):
    return float(np.float32(2.0) ** np.float32(-8.0 * (h + 1.0) / N_HEADS))


def _sink_attention(q, k, v, back, valid_fn, sinks_ref, gout):
    nq = q.shape[0]
    nk = k.shape[0]
    q_t = q.astype(F32).T.astype(BF16)
    k = k.astype(BF16)
    v_t = v.T.astype(BF16)
    c = lax.broadcasted_iota(I32, (nk, nq), 0)
    r = lax.broadcasted_iota(I32, (nk, nq), 1)
    dist = jnp.abs(r - c + back).astype(F32)
    valid = None if valid_fn is None else valid_fn(c, r)
    outs = []
    for kvh in range(N_KV_HEADS):
        kv0 = kvh * HEAD_DIM
        heads = range(kvh * GQA_GROUP, (kvh + 1) * GQA_GROUP)
        qs = jnp.concatenate([q_t[h * HEAD_DIM:(h + 1) * HEAD_DIM, :] for h in heads], axis=1)
        l = jnp.dot(k[:, kv0:kv0 + HEAD_DIM], qs, preferred_element_type=F32)
        biased = [l[:, g * nq:(g + 1) * nq] - _alibi_slope(h) * dist for g, h in enumerate(heads)]
        if valid is not None:
            biased = [jnp.where(valid, b, -jnp.inf) for b in biased]
        l = jnp.concatenate(biased, axis=1)
        sink = jnp.concatenate([jnp.full((1, nq), sinks_ref[h], F32) for h in heads], axis=1)
        m = jnp.maximum(jnp.max(l, axis=0, keepdims=True), sink)
        e = jnp.exp(l - m)
        denom = jnp.sum(e, axis=0, keepdims=True) + jnp.exp(sink - m)
        pv = jnp.dot(v_t[kv0:kv0 + HEAD_DIM, :], e.astype(BF16), preferred_element_type=F32)
        o_t = pv / denom
        outs.extend(o_t[:, g * nq:(g + 1) * nq] for g in range(GQA_GROUP))
    o = jnp.concatenate(outs, axis=0).T
    return (_rms(o) * gout).astype(BF16)


def _attn_prompt_kernel(sinks_ref, q_ref, kp_ref, kc_ref, vp_ref, vc_ref, g_ref, o_ref):
    i = pl.program_id(1)
    back = kp_ref.shape[0]

    k = jnp.concatenate([kp_ref[...], kc_ref[...]], axis=0)
    v = jnp.concatenate([vp_ref[...], vc_ref[...]], axis=0)
    for s in range(q_ref.shape[0] // WINDOW):
        def valid_fn(c, r, s=s):
            kc = c // CHUNK
            qc = r // CHUNK + (back // CHUNK)
            in_band = jnp.logical_and(kc <= qc, kc >= qc - WINDOW // CHUNK)
            if s > 0:
                return in_band
            return jnp.logical_and(in_band, jnp.logical_or(i > 0, kc >= back // CHUNK))

        rows = slice(s * WINDOW, (s + 1) * WINDOW)
        keys = slice(s * WINDOW, s * WINDOW + back + WINDOW)
        o_ref[rows, :] = _sink_attention(q_ref[rows, :], k[keys], v[keys], back, valid_fn, sinks_ref, g_ref[...])


def _attn_prompt(q, k, v, sinks, gattn, batch, seq):
    nt = seq // ATTN_TILE
    per_tile = ATTN_TILE // WINDOW
    cur = lambda b, i, s: (b * nt + i, 0)
    prev = lambda b, i, s: (jnp.maximum((b * nt + i) * per_tile - 1, 0), 0)
    kv_blk = (ATTN_TILE, KV_WIDTH)
    back_blk = (WINDOW, KV_WIDTH)
    return pl.pallas_call(
        _attn_prompt_kernel,
        grid_spec=pltpu.PrefetchScalarGridSpec(
            num_scalar_prefetch=1,
            grid=(batch, nt),
            in_specs=[pl.BlockSpec((ATTN_TILE, ATTN_WIDTH), cur),
                      pl.BlockSpec(back_blk, prev), pl.BlockSpec(kv_blk, cur),
                      pl.BlockSpec(back_blk, prev), pl.BlockSpec(kv_blk, cur),
                      pl.BlockSpec((1, ATTN_WIDTH), lambda b, i, s: (0, 0))],
            out_specs=pl.BlockSpec((ATTN_TILE, ATTN_WIDTH), cur)),
        out_shape=jax.ShapeDtypeStruct((batch * seq, ATTN_WIDTH), BF16),
        compiler_params=_params(2),
        name="attn_prompt",
    )(sinks, q, k, k, v, v, gattn)


def _attn_sample_kernel(sinks_ref, q_ref, ck_ref, cv_ref, k_ref, v_ref, g_ref, o_ref):
    k = jnp.concatenate([ck_ref[0], k_ref[...]], axis=0)
    v = jnp.concatenate([cv_ref[0], v_ref[...]], axis=0)
    o_ref[...] = _sink_attention(q_ref[...], k, v, ck_ref.shape[1], None, sinks_ref, g_ref[...])


def _attn_sample(q, k, v, cache_k, cache_v, sinks, gattn, dec_batch, n, row0):
    w = cache_k.shape[1]
    blk0 = row0 // n
    new = lambda b, s: (blk0 + b, 0)
    return pl.pallas_call(
        _attn_sample_kernel,
        grid_spec=pltpu.PrefetchScalarGridSpec(
            num_scalar_prefetch=1,
            grid=(dec_batch,),
            in_specs=[pl.BlockSpec((n, ATTN_WIDTH), new),
                      pl.BlockSpec((1, w, KV_WIDTH), lambda b, s: (b, 0, 0)),
                      pl.BlockSpec((1, w, KV_WIDTH), lambda b, s: (b, 0, 0)),
                      pl.BlockSpec((n, KV_WIDTH), new),
                      pl.BlockSpec((n, KV_WIDTH), new),
                      pl.BlockSpec((1, ATTN_WIDTH), lambda b, s: (0, 0))],
            out_specs=pl.BlockSpec((n, ATTN_WIDTH), lambda b, s: (b, 0))),
        out_shape=jax.ShapeDtypeStruct((dec_batch * n, ATTN_WIDTH), BF16),
        compiler_params=_params(1),
        name="attn_sample",
    )(sinks, q, cache_k, cache_v, k, v, gattn)


def _merge_kernel(ap_ref, as_ref, sgu_ref, xp_ref, xs_ref,
                  gate1_ref, sh2_ref, sc2_ref, gate2_ref,
                  woa_ref, wos_ref, g2_ref, wrh_ref, wrl_ref, wsgu_ref, wsd_ref,
                  xb_ref, h2_ref, sc_ref, *, n_prompt_tiles):
    is_sample = pl.program_id(0) >= n_prompt_tiles
    x = _pick_tile(is_sample, xp_ref, xs_ref)
    a = jnp.where(is_sample, as_ref[...], ap_ref[...])
    mix = (jnp.dot(a, woa_ref[...], preferred_element_type=F32)
           + jnp.dot(sgu_ref[...], wos_ref[...], preferred_element_type=F32))
    x1 = x + _cond_rows(gate1_ref) * mix
    h2 = _rms(x1) * g2_ref[...] * (1.0 + _cond_rows(sc2_ref)) + _cond_rows(sh2_ref)
    hh = h2.astype(BF16)
    _store_rows(h2_ref, hh)
    hl = (h2 - hh.astype(F32)).astype(BF16)
    logits = (jnp.dot(hh, wrh_ref[...], preferred_element_type=F32)
              + (jnp.dot(hl, wrh_ref[...], preferred_element_type=F32)
                 + jnp.dot(hh, wrl_ref[...], preferred_element_type=F32)))
    sc_ref[...] = jax.nn.sigmoid(logits)
    gu = jnp.dot(hh, wsgu_ref[...], preferred_element_type=F32)
    g = gu[:, :EXPERT_DIM]
    act = (g * jax.nn.sigmoid(g)) * gu[:, EXPERT_DIM:]
    shared = jnp.dot(act.astype(BF16), wsd_ref[...], preferred_element_type=F32)
    xb_ref[...] = x1 + _cond_rows(gate2_ref) * shared


def _merge(attn_p, attn_s, sgu_n, xp3, xs3, cond_tbl, woa, wos, g2, wrh, wrl, wsgu, wsd, batch, n_prompt_tiles):
    t = sgu_n.shape[0]
    n_tiles = t // DENSE_TILE
    last_p = n_prompt_tiles - 1
    x_blk = (DENSE_TILE // ROW_GROUP, ROW_GROUP, D_MODEL)
    const2 = lambda i: (0, 0)
    const3 = lambda i: (0, 0, 0)
    tile2 = lambda i: (i, 0)
    cond = functools.partial(_cond_spec, tile_rows=DENSE_TILE, tiles_per_batch=n_prompt_tiles // batch,
                             batch=batch, n_prompt_tiles=n_prompt_tiles)
    return pl.pallas_call(
        functools.partial(_merge_kernel, n_prompt_tiles=n_prompt_tiles),
        grid=(n_tiles,),
        in_specs=[pl.BlockSpec((DENSE_TILE, ATTN_WIDTH), lambda i: (jnp.minimum(i, last_p), 0)),
                  pl.BlockSpec((DENSE_TILE, ATTN_WIDTH), const2),
                  pl.BlockSpec((DENSE_TILE, SGU_WIDTH), tile2),
                  pl.BlockSpec(x_blk, lambda i: (jnp.minimum(i, last_p), 0, 0)),
                  pl.BlockSpec(x_blk, const3)]
                 + [cond(2), cond(3), cond(4), cond(5)]
                 + [pl.BlockSpec((ATTN_WIDTH, D_MODEL), const2),
                    pl.BlockSpec((SGU_WIDTH, D_MODEL), const2),
                    pl.BlockSpec((1, D_MODEL), const2),
                    pl.BlockSpec((D_MODEL, N_EXPERTS), const2),
                    pl.BlockSpec((D_MODEL, N_EXPERTS), const2),
                    pl.BlockSpec((D_MODEL, 2 * EXPERT_DIM), const2),
                    pl.BlockSpec((EXPERT_DIM, D_MODEL), const2)],
        out_specs=[pl.BlockSpec((DENSE_TILE, D_MODEL), tile2),
                   pl.BlockSpec((DENSE_TILE * ROW_TILES, LANES), tile2),
                   pl.BlockSpec((DENSE_TILE, N_EXPERTS), tile2)],
        out_shape=[jax.ShapeDtypeStruct((t, D_MODEL), F32),
                   jax.ShapeDtypeStruct((t * ROW_TILES, LANES), ROW_DTYPE),
                   jax.ShapeDtypeStruct((t, N_EXPERTS), F32)],
        compiler_params=_params(1),
        name="merge",
    )(attn_p, attn_s, sgu_n, xp3, xs3, *([cond_tbl] * 4), woa, wos, g2, wrh, wrl, wsgu, wsd)


def _first_argmax(x, iota, size, axis):
    m = jnp.max(x, axis=axis, keepdims=True)
    idx = jnp.min(jnp.where(x == m, iota, size), axis=axis, keepdims=True)
    return m, idx


def _route_kernel(sc_ref, bias_ref, tri_ref, eidx_ref, wts_ref, rank_ref, cnt_ref, base_ref):
    tt = sc_ref.shape[0]

    @pl.when(pl.program_id(0) == 0)
    def _():
        base_ref[...] = jnp.zeros_like(base_ref)

    s_t = sc_ref[...].T
    sel = s_t + bias_ref[...]
    sel3 = sel.reshape(N_EXPERT_GROUPS, GROUP_SIZE, tt)
    io3 = lax.broadcasted_iota(I32, sel3.shape, 1)
    m1, i1 = _first_argmax(sel3, io3, GROUP_SIZE, 1)
    m2 = jnp.max(jnp.where(io3 == i1, -jnp.inf, sel3), axis=1, keepdims=True)
    gs = (m1 + m2).reshape(N_EXPERT_GROUPS, tt)
    io8 = lax.broadcasted_iota(I32, gs.shape, 0)
    gmask = jnp.zeros(gs.shape, jnp.bool_)
    for _ in range(TOPK_GROUPS):
        _, gi = _first_argmax(gs, io8, N_EXPERT_GROUPS, 0)
        hit = io8 == gi
        gmask = jnp.logical_or(gmask, hit)
        gs = jnp.where(hit, -jnp.inf, gs)
    emask = jnp.broadcast_to(gmask.reshape(N_EXPERT_GROUPS, 1, tt), sel3.shape).reshape(N_EXPERTS, tt)
    cand = jnp.where(emask, sel, -jnp.inf)
    io = lax.broadcasted_iota(I32, cand.shape, 0)
    chosen = jnp.zeros(cand.shape, jnp.bool_)
    eidx, wts = [], []
    for _ in range(TOP_K):
        _, ei = _first_argmax(cand, io, N_EXPERTS, 0)
        hit = io == ei
        eidx.append(ei)
        wts.append(jnp.sum(jnp.where(hit, s_t, 0.0), axis=0, keepdims=True))
        chosen = jnp.logical_or(chosen, hit)
        cand = jnp.where(hit, -jnp.inf, cand)
    w = jnp.concatenate(wts, axis=0)
    wts_ref[...] = w / jnp.sum(w, axis=0, keepdims=True) * ROUTED_SCALE
    eidx_ref[...] = jnp.concatenate(eidx, axis=0)
    cf = jnp.where(chosen, 1.0, 0.0)
    ahead = jnp.dot(cf.astype(BF16), tri_ref[...], preferred_element_type=F32) + base_ref[...]
    ranks = [jnp.sum(jnp.where(io == ei, ahead, 0.0), axis=0, keepdims=True) for ei in eidx]
    rank_ref[...] = jnp.concatenate(ranks, axis=0).astype(I32)
    total = base_ref[...] + jnp.sum(cf, axis=1, keepdims=True)
    base_ref[...] = total
    cnt_ref[...] = total.astype(I32)


def _route(scores, bias_col, tri, t):
    n_tiles = t // TOKEN_TILE
    col = lambda i: (0, i)
    return pl.pallas_call(
        _route_kernel,
        grid=(n_tiles,),
        in_specs=[pl.BlockSpec((TOKEN_TILE, N_EXPERTS), lambda i: (i, 0)),
                  pl.BlockSpec((N_EXPERTS, 1), lambda i: (0, 0)),
                  pl.BlockSpec((TOKEN_TILE, TOKEN_TILE), lambda i: (0, 0))],
        out_specs=[pl.BlockSpec((TOP_K, TOKEN_TILE), col),
                   pl.BlockSpec((TOP_K, TOKEN_TILE), col),
                   pl.BlockSpec((TOP_K, TOKEN_TILE), col),
                   pl.BlockSpec((N_EXPERTS, 1), lambda i: (0, 0))],
        out_shape=[jax.ShapeDtypeStruct((TOP_K, t), I32),
                   jax.ShapeDtypeStruct((TOP_K, t), F32),
                   jax.ShapeDtypeStruct((TOP_K, t), I32),
                   jax.ShapeDtypeStruct((N_EXPERTS, 1), I32)],
        scratch_shapes=[pltpu.VMEM((N_EXPERTS, 1), F32)],
        compiler_params=_params(1),
        name="route",
    )(scores, bias_col, tri)


def _pos_kernel(eidx_ref, rank_ref, start_ref, pos_ref):
    e = eidx_ref[...]
    tt = e.shape[1]
    io = lax.broadcasted_iota(I32, (N_EXPERTS, tt), 0)
    st = start_ref[...]
    rows = [jnp.sum(jnp.where(io == e[k:k + 1, :], st, 0), axis=0, keepdims=True) for k in range(TOP_K)]
    pos_ref[...] = jnp.concatenate(rows, axis=0) + rank_ref[...]


def _positions(eidx, rank, start_col):
    t = eidx.shape[1]
    lane_tiles = t // LANES
    width = max(d for d in range(1, POS_TILE_MAX // LANES + 1) if lane_tiles % d == 0) * LANES
    col = lambda i: (0, i)
    blk = pl.BlockSpec((TOP_K, width), col)
    return pl.pallas_call(
        _pos_kernel,
        grid=(t // width,),
        in_specs=[blk, blk, pl.BlockSpec((N_EXPERTS, 1), lambda i: (0, 0))],
        out_specs=blk,
        out_shape=jax.ShapeDtypeStruct((TOP_K, t), I32),
        compiler_params=_params(1),
        name="positions",
    )(eidx, rank, start_col)


def _dispatch_kernel(pos_ref, h_ref, xs_ref, zero_ref, sem, zsem, *, n_rows):
    tt = h_ref.shape[0] // ROW_TILES
    first = pl.program_id(0) == 0

    @pl.when(first)
    def _():
        zero_ref[...] = jnp.zeros_like(zero_ref)
        _rows_copy(zero_ref, 0, xs_ref, n_rows, EXPERT_BLOCK, zsem).start()

    for t in range(tt):
        for k in range(TOP_K):
            _rows_copy(h_ref, t, xs_ref, pos_ref[k, t], 1, sem).start(priority=k % 2)
    for k in range(TOP_K):
        _rows_copy(h_ref, 0, xs_ref, 0, tt, sem).wait()

    @pl.when(first)
    def _():
        _rows_copy(zero_ref, 0, xs_ref, n_rows, EXPERT_BLOCK, zsem).wait()


def _dispatch(pos, h2_tiles, t, n_rows):
    return pl.pallas_call(
        functools.partial(_dispatch_kernel, n_rows=n_rows),
        grid=(t // TOKEN_TILE,),
        in_specs=[pl.BlockSpec((TOP_K, TOKEN_TILE), lambda i: (0, i), memory_space=pltpu.SMEM),
                  pl.BlockSpec((TOKEN_TILE * ROW_TILES, LANES), lambda i: (i, 0))],
        out_specs=pl.BlockSpec(memory_space=pl.ANY),
        out_shape=jax.ShapeDtypeStruct(((n_rows + EXPERT_BLOCK) * ROW_TILES, LANES), ROW_DTYPE),
        scratch_shapes=[pltpu.VMEM((EXPERT_BLOCK * ROW_TILES, LANES), ROW_DTYPE),
                        pltpu.SemaphoreType.DMA, pltpu.SemaphoreType.DMA],
        compiler_params=_params(1),
        name="dispatch",
    )(pos, h2_tiles)


def _out_pieces(nvalid, fn):
    s = 1 << (EXPERT_BLOCK.bit_length() - 1)
    while s >= 1:
        @pl.when((nvalid & s) != 0)
        def _(s=s):
            fn(nvalid & ~(2 * s - 1), s)
        s //= 2


def _for_block_size(nvalid, fn):
    lo = 0
    for rows in BLOCK_SIZES:
        @pl.when(jnp.logical_and(nvalid > lo, nvalid <= rows))
        def _(rows=rows):
            fn(rows)
        lo = rows


def _expert_kernel(count_ref, blk0_ref, row_ref, cnt_ref, xs_ref, wg_ref, wu_ref, wd_ref, ys_ref,
                   xbuf, ybuf, wg_buf, wu_buf, wd_buf, wgu_s, wd_s, state, xsem, ysem, wsem):
    e = pl.program_id(0)
    n_experts = pl.num_programs(0)
    cnt = count_ref[e]
    g0 = blk0_ref[e]
    nblk = (cnt + EXPERT_BLOCK - 1) // EXPERT_BLOCK
    wslot = lax.rem(e, W_SLOTS)

    def w_copies(expert, slot):
        return [pltpu.make_async_copy(src.at[expert], dst.at[slot], wsem.at[slot])
                for src, dst in ((wg_ref, wg_buf), (wu_ref, wu_buf), (wd_ref, wd_buf))]

    def w_fetch(expert):
        @pl.when(expert < n_experts)
        def _():
            for c in w_copies(expert, lax.rem(expert, W_SLOTS)):
                c.start()

    def x_copy(row, slot, rows):
        return _rows_copy(xs_ref, row, xbuf.at[slot], 0, rows, xsem.at[slot])

    def x_fetch(g, slot):
        _for_block_size(cnt_ref[g], lambda rows: x_copy(row_ref[g], slot, rows).start())

    def y_wait(nvalid, slot):
        def wait(off, size):
            _rows_copy(ybuf.at[slot], 0, ys_ref, 0, size, ysem.at[slot]).wait()
        _out_pieces(nvalid, wait)

    @pl.when(e == 0)
    def _():
        state[0] = 0
        state[1] = 0
        for g in range(X_SLOTS - 1):
            x_fetch(g, g)
        for ahead in range(W_SLOTS - 1):
            w_fetch(ahead)

    w_fetch(e + W_SLOTS - 1)
    for c in w_copies(e, wslot):
        c.wait()
    wgu_s[:, :EXPERT_DIM] = wg_buf[wslot].astype(BF16)
    wgu_s[:, EXPERT_DIM:] = wu_buf[wslot].astype(BF16)
    wd_s[...] = wd_buf[wslot].astype(BF16)

    def compute(rows, xslot, yslot, row0, nvalid):
        x_copy(row0, xslot, rows).wait()
        x = _load_rows(xbuf.at[xslot], rows).astype(BF16)
        gu = jnp.dot(x, wgu_s[...], preferred_element_type=F32)
        g = gu[:, :EXPERT_DIM]
        act = (g * jax.nn.sigmoid(g)) * gu[:, EXPERT_DIM:]
        y = jnp.dot(act.astype(BF16), wd_s[...], preferred_element_type=F32)
        y_wait(state[0], 1 - yslot)
        _store_rows(ybuf.at[yslot], y)

        def put(off, size):
            _rows_copy(ybuf.at[yslot], off, ys_ref, row0 + off, size, ysem.at[yslot]).start()
        _out_pieces(nvalid, put)

    def block(j, carry):
        g = g0 + j
        xslot = state[1]
        yslot = g & 1
        row0 = row_ref[g]
        nvalid = cnt_ref[g]
        x_fetch(g + X_SLOTS - 1, jnp.where(xslot == 0, X_SLOTS - 1, xslot - 1))
        _for_block_size(nvalid, lambda rows: compute(rows, xslot, yslot, row0, nvalid))
        state[0] = nvalid
        state[1] = jnp.where(xslot == X_SLOTS - 1, 0, xslot + 1)
        return carry

    lax.fori_loop(0, nblk, block, 0)

    @pl.when(e == pl.num_programs(0) - 1)
    def _():
        y_wait(state[0], (g0 + nblk - 1) & 1)


def _experts(count, blk0, blk_row, blk_cnt, xs, w_gate, w_up, w_down, n_rows):
    blk_rows = EXPERT_BLOCK * ROW_TILES
    any_spec = pl.BlockSpec(memory_space=pl.ANY)
    return pl.pallas_call(
        _expert_kernel,
        grid_spec=pltpu.PrefetchScalarGridSpec(
            num_scalar_prefetch=4,
            grid=(N_EXPERTS,),
            in_specs=[any_spec, any_spec, any_spec, any_spec],
            out_specs=any_spec,
            scratch_shapes=[pltpu.VMEM((X_SLOTS, blk_rows, LANES), ROW_DTYPE),
                            pltpu.VMEM((2, blk_rows, LANES), ROW_DTYPE),
                            pltpu.VMEM((W_SLOTS, D_MODEL, EXPERT_DIM), F32),
                            pltpu.VMEM((W_SLOTS, D_MODEL, EXPERT_DIM), F32),
                            pltpu.VMEM((W_SLOTS, EXPERT_DIM, D_MODEL), F32),
                            pltpu.VMEM((D_MODEL, 2 * EXPERT_DIM), BF16),
                            pltpu.VMEM((EXPERT_DIM, D_MODEL), BF16),
                            pltpu.SMEM((2,), I32),
                            pltpu.SemaphoreType.DMA((X_SLOTS,)),
                            pltpu.SemaphoreType.DMA((2,)),
                            pltpu.SemaphoreType.DMA((W_SLOTS,))]),
        out_shape=jax.ShapeDtypeStruct((n_rows * ROW_TILES, LANES), ROW_DTYPE),
        compiler_params=_params(1),
        name="experts",
    )(count, blk0, blk_row, blk_cnt, xs, w_gate, w_up, w_down)


def _combine_kernel(pos_ref, posn_ref, ys_ref, xb_ref, wts_ref, gate2_ref, gf_ref, yp_ref, ysm_ref,
                    buf, ytile, sem, *, n_prompt_tiles, n_tiles):
    i = pl.program_id(0)
    tt = xb_ref.shape[0]
    slot = i & 1

    def fetch(p_ref, s, t):
        for k in range(TOP_K):
            _rows_copy(ys_ref, p_ref[k, t], buf.at[s, k], t, 1, sem.at[s]).start(priority=k % 2)

    @pl.when(i == 0)
    def _():
        for t in range(tt):
            fetch(pos_ref, 0, t)

    for k in range(TOP_K):
        _rows_copy(ys_ref, 0, buf.at[slot, k], 0, tt, sem.at[slot]).wait()

    def finish(s, g):
        r0 = g * COMBINE_GROUP
        w = wts_ref[r0:r0 + COMBINE_GROUP, :]
        routed = None
        for k in range(TOP_K):
            rows = _load_rows(buf.at[s, k, pl.ds(r0 * ROW_TILES, COMBINE_GROUP * ROW_TILES)], COMBINE_GROUP)
            term = w[:, k:k + 1] * rows
            routed = term if routed is None else routed + term
        gate = gate2_ref[g * COMBINE_GROUP // ROW_GROUP]
        x = xb_ref[r0:r0 + COMBINE_GROUP, :] + gate * routed
        return _rms(x) * gf_ref[...]

    def tile_work(s, fetch_next):
        for g in range(tt // COMBINE_GROUP):
            y = finish(s, g)
            if fetch_next:
                for r in range(COMBINE_GROUP):
                    fetch(posn_ref, 1 - s, g * COMBINE_GROUP + r)
            ytile[g * COMBINE_GROUP:(g + 1) * COMBINE_GROUP, :] = y

    for s in range(2):
        @pl.when(jnp.logical_and(slot == s, i + 1 < n_tiles))
        def _(s=s):
            tile_work(s, True)

    @pl.when(i + 1 == n_tiles)
    def _():
        tile_work((n_tiles - 1) % 2, False)

    @pl.when(i < n_prompt_tiles)
    def _():
        yp_ref[...] = ytile[...]

    @pl.when(i >= n_prompt_tiles)
    def _():
        ysm_ref[...] = ytile[...]


def _combine(pos, ys, xb, wts_t, cond_tbl, gf, batch, n_prompt_rows, t):
    tt = COMBINE_TILE
    n_tiles = t // tt
    n_p = n_prompt_rows // tt
    pos_spec = lambda f: pl.BlockSpec((TOP_K, tt), f, memory_space=pltpu.SMEM)
    return pl.pallas_call(
        functools.partial(_combine_kernel, n_prompt_tiles=n_p, n_tiles=n_tiles),
        grid=(n_tiles,),
        in_specs=[pos_spec(lambda i: (0, i)),
                  pos_spec(lambda i: (0, jnp.minimum(i + 1, n_tiles - 1))),
                  pl.BlockSpec(memory_space=pl.ANY),
                  pl.BlockSpec((tt, D_MODEL), lambda i: (i, 0)),
                  pl.BlockSpec((tt, TOP_K), lambda i: (i, 0))]
                 + [_cond_spec(5, tt, n_p // batch, batch, n_p)]
                 + [pl.BlockSpec((1, D_MODEL), lambda i: (0, 0))],
        out_specs=[pl.BlockSpec((tt, D_MODEL), lambda i: (jnp.minimum(i, n_p - 1), 0)),
                   pl.BlockSpec((tt, D_MODEL), lambda i: (jnp.maximum(i - n_p, 0), 0))],
        out_shape=[jax.ShapeDtypeStruct((n_prompt_rows, D_MODEL), F32),
                   jax.ShapeDtypeStruct((t - n_prompt_rows, D_MODEL), F32)],
        scratch_shapes=[pltpu.VMEM((2, TOP_K, tt * ROW_TILES, LANES), ROW_DTYPE),
                        pltpu.VMEM((tt, D_MODEL), F32),
                        pltpu.SemaphoreType.DMA((2,))],
        compiler_params=_params(1),
        name="combine",
    )(pos, pos, ys, xb, wts_t, cond_tbl, gf)


def _sgu_mask():
    pos = np.arange(SGU_CHUNK) // CHUNK
    return jnp.asarray(pos[None, :] <= pos[:, None])


def _layer(xp3, xs3, cache_k, cache_v, mod, final_g, batch, seq, dec_batch, n, p):
    tp = batch * seq
    t = tp + dec_batch * n
    n_prompt_tiles = tp // DENSE_TILE
    pad_groups = (DENSE_TILE - dec_batch * n) // ROW_GROUP
    xs3 = jnp.pad(xs3, ((0, pad_groups), (0, 0), (0, 0)))
    cond_tbl = jnp.concatenate([jnp.repeat(mod[:batch], COND_REPEAT, axis=0), mod[batch:]], axis=0)
    cond_tbl = jnp.pad(cond_tbl, ((0, pad_groups), (0, 0)))
    cond_tbl = cond_tbl.reshape(batch * COND_REPEAT + dec_batch + pad_groups, 1, N_MOD * D_MODEL)

    wm = jnp.where(_sgu_mask(), p["sgu_w"], 0.0)
    reps = SGU_CHUNK // n
    eye = jnp.eye(reps, dtype=F32)
    wms = jnp.stack([jnp.kron(eye, wm[g, :n, :n]) for g in range(SGU_GROUPS)])
    bs = jnp.broadcast_to(p["sgu_b"][:, :, None], (SGU_GROUPS, SGU_CHUNK, SGU_CHUNK))
    bss = jnp.broadcast_to(jnp.tile(p["sgu_b"][:, :n], (1, reps))[:, :, None], (SGU_GROUPS, SGU_CHUNK, SGU_CHUNK))

    q, k, v, sgu_n, vn_s = _stage1(
        xp3, xs3, cond_tbl, p["norm1_g"].reshape(1, -1), p["w_in"].astype(BF16),
        p["sgu_ln_g"].reshape(1, -1), p["sgu_ln_b"].reshape(1, -1),
        wm.astype(BF16), wms.astype(BF16), bs, bss, p["sgu_out_g"].reshape(1, -1), batch, n_prompt_tiles)

    gattn = p["attn_out_g"].reshape(1, -1)
    attn_p = _attn_prompt(q, k, v, p["attn_sinks"], gattn, batch, seq)
    w = cache_k.shape[1]
    attn_s = _attn_sample(q, k, v, cache_k.reshape(dec_batch, w, KV_WIDTH), cache_v.reshape(dec_batch, w, KV_WIDTH),
                          p["attn_sinks"], gattn, dec_batch, n, tp)

    wo = p["w_out"].astype(BF16)
    wsgu = jnp.concatenate([p["ws_gate"], p["ws_up"]], axis=1).astype(BF16)
    wrh = p["w_router"].astype(BF16)
    wrl = (p["w_router"] - wrh.astype(F32)).astype(BF16)
    attn_s = jnp.pad(attn_s, ((0, DENSE_TILE - dec_batch * n), (0, 0)))
    xb, h2_tiles, scores = _merge(attn_p, attn_s, sgu_n, xp3, xs3, cond_tbl,
                                  wo[:ATTN_WIDTH], wo[ATTN_WIDTH:], p["norm2_g"].reshape(1, -1), wrh, wrl,
                                  wsgu, p["ws_down"].astype(BF16), batch, n_prompt_tiles)

    ti = np.arange(TOKEN_TILE)
    tri = jnp.asarray(ti[:, None] < ti[None, :], dtype=BF16)
    eidx, wts, rank, counts = _route(scores, p["router_bias"].reshape(N_EXPERTS, 1), tri, t)

    counts = counts[:, 0]
    ends = jnp.cumsum(counts)
    start = (ends - counts).astype(I32)
    nblk = (counts + EXPERT_BLOCK - 1) // EXPERT_BLOCK
    blk_end = jnp.cumsum(nblk)
    blk0 = (blk_end - nblk).astype(I32)
    pos = _positions(eidx, rank, start.reshape(N_EXPERTS, 1))

    n_rows = t * TOP_K
    g = jnp.arange(n_rows // EXPERT_BLOCK + N_EXPERTS + X_SLOTS, dtype=I32)[:, None]
    mine = jnp.logical_and(g >= blk0[None, :], g < blk_end[None, :])
    off = (g - blk0[None, :]) * EXPERT_BLOCK
    blk_row = jnp.sum(jnp.where(mine, start[None, :] + off, 0), axis=1).astype(I32)
    blk_cnt = jnp.sum(jnp.where(mine, jnp.minimum(counts[None, :] - off, EXPERT_BLOCK), 0), axis=1).astype(I32)

    xs_sorted = _dispatch(pos, h2_tiles, t, n_rows)
    ys_sorted = _experts(counts.astype(I32), blk0, blk_row, blk_cnt, xs_sorted,
                         p["w_gate"], p["w_up"], p["w_down"], n_rows)
    y_p, y_s = _combine(pos, ys_sorted, xb, wts.T, cond_tbl, final_g.reshape(1, -1), batch, tp, t)
    return y_p, y_s, k, v, vn_s[:dec_batch * n]


def kernel(x_prompt, x_sample, cache_k, cache_v, c_prompt, c_sample, norm1_g, w_ada, b_ada, w_in, sgu_ln_g, sgu_ln_b, sgu_w, sgu_b, attn_sinks, attn_out_g, sgu_out_g, w_out, norm2_g, w_router, router_bias, w_gate, w_up, w_down, ws_gate, ws_up, ws_down, final_g):
    batch, seq, d = x_prompt.shape
    dec_batch, n, _ = x_sample.shape
    depth = norm1_g.shape[0]
    assert depth == 1 and d == D_MODEL
    assert seq % DENSE_TILE == 0 and dec_batch * n == TOKEN_TILE and n == ROW_GROUP
    assert seq % ATTN_TILE == 0 and seq >= WINDOW and cache_k.shape[2] == WINDOW
    tp = batch * seq

    xp3 = x_prompt.reshape(tp // ROW_GROUP, ROW_GROUP, d)
    xs3 = x_sample.reshape(dec_batch * n // ROW_GROUP, ROW_GROUP, d)
    c_all = jnp.concatenate([c_prompt, c_sample], axis=0)
    l = 0
    mod = _adaln(c_all, w_ada[l], b_ada[l])
    p = dict(norm1_g=norm1_g[l], w_in=w_in[l], sgu_ln_g=sgu_ln_g[l], sgu_ln_b=sgu_ln_b[l], sgu_w=sgu_w[l],
             sgu_b=sgu_b[l], attn_sinks=attn_sinks[l], attn_out_g=attn_out_g[l], sgu_out_g=sgu_out_g[l],
             w_out=w_out[l], norm2_g=norm2_g[l], w_router=w_router[l], router_bias=router_bias[l],
             w_gate=w_gate[l], w_up=w_up[l], w_down=w_down[l], ws_gate=ws_gate[l], ws_up=ws_up[l],
             ws_down=ws_down[l])
    y_p, y_s, k, v, vn_s = _layer(xp3, xs3, cache_k[l], cache_v[l], mod, final_g, batch, seq, dec_batch, n, p)

    keep = min(WINDOW, seq)
    kv_shape = (batch, keep, N_KV_HEADS, HEAD_DIM)
    k_p = jnp.stack([k[(b + 1) * seq - keep:(b + 1) * seq] for b in range(batch)]).reshape(kv_shape)
    v_p = jnp.stack([v[(b + 1) * seq - keep:(b + 1) * seq] for b in range(batch)]).reshape(kv_shape)
    ts = dec_batch * n
    k_s = k[tp:tp + ts].reshape(dec_batch, n, N_KV_HEADS, HEAD_DIM)
    v_s = v[tp:tp + ts].reshape(dec_batch, n, N_KV_HEADS, HEAD_DIM)
    return (y_p.reshape(batch, seq, d), y_s.reshape(dec_batch, n, d),
            k_p[None], v_p[None], k_s[None], v_s[None], vn_s.reshape(dec_batch, n, SGU_WIDTH)[None])
```
